```python
import math
import jax, jax.numpy as jnp
from jax import lax
import numpy as np

D_MODEL = 1024
BATCH = 32
SEQ = 2048
DEPTH = 1

CHUNK = 64
QBLK = 128
MEM_LEN = 256
EPS = 1e-6
DN_HEADS = 4
DN_HEAD_DIM = D_MODEL // 8
DN_WIDTH = DN_HEADS * DN_HEAD_DIM
CONV_WIDTH = 4
SB_HEADS = 8
SB_HEAD_DIM = D_MODEL // 16
SB_WIDTH = SB_HEADS * SB_HEAD_DIM
MIX_WIDTH = DN_WIDTH + SB_WIDTH
IN_SPLITS = (DN_WIDTH, DN_WIDTH, DN_WIDTH, DN_WIDTH, DN_HEADS, DN_HEADS, SB_WIDTH, SB_WIDTH, SB_WIDTH)
IN_WIDTH = 4 * DN_WIDTH + 2 * DN_HEADS + 3 * SB_WIDTH
X_HEADS = 4
X_HEAD_DIM = D_MODEL // X_HEADS
N_EXPERTS = 32
TOP_K = 4
D_FF = D_MODEL
SWIGLU_ALPHA = 1.702
SWIGLU_LIMIT = 7.0
EXPERT_BLOCK = 512

kernel_name = "hybrid_deltanet_stickbreak_memxattn_moe"


def rms_norm(x, w):
    x32 = x.astype(jnp.float32)
    y = x32 * lax.rsqrt(jnp.mean(x32 * x32, axis=-1, keepdims=True) + EPS)
    return (y * w.astype(jnp.float32)).astype(x.dtype)


def l2_norm(x):
    x32 = x.astype(jnp.float32)
    return x32 * lax.rsqrt(jnp.sum(x32 * x32, axis=-1, keepdims=True) + EPS)


def causal_depthwise_conv(u, w):
    c = u.shape[-1]
    return lax.conv_general_dilated(
        u, w[:, None, :].astype(u.dtype), window_strides=(1,),
        padding=((CONV_WIDTH - 1, 0),), dimension_numbers=("NWC", "WIO", "NWC"),
        feature_group_count=c)


def chunk_gated_delta_rule(q, k, v, g, beta):
    bsz, s, h, dk = q.shape
    dv = v.shape[-1]
    n = s // CHUNK
    q = q * (dk ** -0.5)

    def chunks(t):
        return t.reshape(bsz, n, CHUNK, h, -1).transpose(0, 3, 1, 2, 4)

    q, k, v = chunks(q), chunks(k), chunks(v)
    g = jnp.cumsum(g.reshape(bsz, n, CHUNK, h).transpose(0, 3, 1, 2), axis=-1)
    beta = beta.reshape(bsz, n, CHUNK, h).transpose(0, 3, 1, 2)
    tri = jnp.tril(jnp.ones((CHUNK, CHUNK), bool))
    strict = jnp.tril(jnp.ones((CHUNK, CHUNK), bool), -1)
    gdiff = g[..., :, None] - g[..., None, :]
    decay = jnp.where(tri, jnp.exp(jnp.where(tri, gdiff, 0.0)), 0.0)
    k_beta = k * beta[..., None]
    v_beta = v * beta[..., None]
    lower = jnp.where(strict, jnp.einsum("bhncd,bhnsd->bhncs", k_beta, k) * decay, 0.0)
    rhs = jnp.concatenate([v_beta, k_beta * jnp.exp(g)[..., None]], axis=-1)
    sol = lax.linalg.triangular_solve(lower + jnp.eye(CHUNK, dtype=lower.dtype), rhs,
                                      left_side=True, lower=True, unit_diagonal=True)
    u, w = sol[..., :dv], sol[..., dv:]
    attn_intra = jnp.where(tri, jnp.einsum("bhncd,bhnsd->bhncs", q, k) * decay, 0.0)
    g_last = g[..., -1]
    k_tail = k * jnp.exp(g_last[..., None] - g)[..., None]
    q_g = q * jnp.exp(g)[..., None]

    def step(state, inp):
        q_i, kt_i, u_i, w_i, a_i, gl_i = inp
        v_new = u_i - jnp.einsum("bhck,bhkv->bhcv", w_i, state)
        o = jnp.einsum("bhck,bhkv->bhcv", q_i, state) + jnp.einsum("bhcs,bhsv->bhcv", a_i, v_new)
        state = state * jnp.exp(gl_i)[..., None, None] + jnp.einsum("bhck,bhcv->bhkv", kt_i, v_new)
        return state, o

    xs = tuple(jnp.moveaxis(t, 2, 0) for t in (q_g, k_tail, u, w, attn_intra, g_last))
    state0 = jnp.zeros((bsz, h, dk, dv), jnp.float32)
    _, o = lax.scan(step, state0, xs)
    return o.transpose(1, 0, 3, 2, 4).reshape(bsz, s, h, dv)


def gated_deltanet_group(q, k, v, z, a, b, conv_w, a_log, dt_bias, norm_w):
    bsz, s, _ = q.shape
    dtype = q.dtype
    qkv = jax.nn.silu(causal_depthwise_conv(jnp.concatenate([q, k, v], axis=-1), conv_w))
    q, k, v = jnp.split(qkv, 3, axis=-1)
    shp = (bsz, s, DN_HEADS, DN_HEAD_DIM)
    q = l2_norm(q.reshape(shp))
    k = l2_norm(k.reshape(shp))
    v = v.reshape(shp).astype(jnp.float32)
    beta = jax.nn.sigmoid(b.astype(jnp.float32))
    g = -jnp.exp(a_log.astype(jnp.float32)) * jax.nn.softplus(
        a.astype(jnp.float32) + dt_bias.astype(jnp.float32))
    o = chunk_gated_delta_rule(q, k, v, g, beta)
    o = o * lax.rsqrt(jnp.mean(o * o, axis=-1, keepdims=True) + EPS)
    o = o * norm_w.astype(jnp.float32) * jax.nn.silu(z.reshape(shp).astype(jnp.float32))
    return o.reshape(bsz, s, DN_WIDTH).astype(dtype)


def stick_breaking_group(q, k, v, q_norm_w, k_norm_w):
    bsz, s, _ = q.shape
    dtype = q.dtype
    shp = (bsz, s, SB_HEADS, SB_HEAD_DIM)
    q = rms_norm(q.reshape(shp), q_norm_w).astype(jnp.float32).transpose(0, 2, 1, 3)
    k = rms_norm(k.reshape(shp), k_norm_w).astype(jnp.float32).transpose(0, 2, 1, 3)
    v = v.reshape(shp).astype(jnp.float32).transpose(0, 2, 1, 3)
    scale = SB_HEAD_DIM ** -0.5
    outs = []
    for blk in range(s // QBLK):
        qs, qe = blk * QBLK, (blk + 1) * QBLK
        z = jnp.einsum("bhtd,bhsd->bhts", q[:, :, qs:qe], k[:, :, :qe]) * scale
        causal = jnp.arange(qe)[None, :] < jnp.arange(qs, qe)[:, None]
        log_fail = jnp.where(causal, jax.nn.log_sigmoid(-z), 0.0)
        after = lax.cumsum(log_fail, axis=3, reverse=True) - log_fail
        attn = jnp.where(causal, jnp.exp(jax.nn.log_sigmoid(z) + after), 0.0)
        outs.append(jnp.einsum("bhts,bhsd->bhtd", attn, v[:, :, :qe]))
    o = jnp.concatenate(outs, axis=2)
    return o.transpose(0, 2, 1, 3).reshape(bsz, s, SB_WIDTH).astype(dtype)


def memory_cross_attention(h, mem, wq, wk, wv, q_norm_w, k_norm_w, wo):
    bsz, s, _ = h.shape
    m = mem.shape[1]
    q = rms_norm((h @ wq).reshape(bsz, s, X_HEADS, X_HEAD_DIM), q_norm_w)
    k = rms_norm((mem @ wk).reshape(bsz, m, X_HEADS, X_HEAD_DIM), k_norm_w)
    v = (mem @ wv).reshape(bsz, m, X_HEADS, X_HEAD_DIM)
    sc = jnp.einsum("bshd,bmhd->bhsm", q, k).astype(jnp.float32) * (X_HEAD_DIM ** -0.5)
    p = jax.nn.softmax(sc, axis=-1).astype(v.dtype)
    o = jnp.einsum("bhsm,bmhd->bshd", p, v).reshape(bsz, s, X_HEADS * X_HEAD_DIM)
    return o @ wo


def moe_ffn(h, router_w, router_b, w_gate, b_gate, w_up, b_up, w_down, b_down):
    bsz, s, d = h.shape
    n = bsz * s
    nk = n * TOP_K
    t = h.reshape(n, d)
    logits = (t @ router_w + router_b).astype(jnp.float32)
    top_val, top_idx = lax.top_k(logits, TOP_K)
    gates = jax.nn.softmax(top_val, axis=-1)
    flat_e = top_idx.reshape(-1).astype(jnp.int32)
    flat_tok = jnp.arange(nk, dtype=jnp.int32) // TOP_K
    order = jnp.argsort(flat_e)
    sorted_e = flat_e[order]
    sorted_tok = flat_tok[order]
    counts = jnp.bincount(flat_e, length=N_EXPERTS)
    starts = jnp.cumsum(counts) - counts
    padded = (counts + EXPERT_BLOCK - 1) // EXPERT_BLOCK * EXPERT_BLOCK
    pad_ends = jnp.cumsum(padded)
    pad_starts = pad_ends - padded
    dest = pad_starts[sorted_e] + (jnp.arange(nk, dtype=jnp.int32) - starts[sorted_e])
    n_blocks = -(-nk // EXPERT_BLOCK) + N_EXPERTS
    rows = n_blocks * EXPERT_BLOCK
    x_pad = jnp.zeros((rows, d), t.dtype).at[dest].set(t[sorted_tok])
    block_e = jnp.minimum(
        jnp.searchsorted(pad_ends, jnp.arange(n_blocks, dtype=pad_ends.dtype) * EXPERT_BLOCK, side="right"),
        N_EXPERTS - 1)

    def expert_block(args):
        xb, e = args
        gate = jnp.minimum(xb @ w_gate[e] + b_gate[e], SWIGLU_LIMIT)
        up = jnp.clip(xb @ w_up[e] + b_up[e], -SWIGLU_LIMIT, SWIGLU_LIMIT)
        act = (up + 1.0) * gate * jax.nn.sigmoid(gate * SWIGLU_ALPHA)
        return act @ w_down[e] + b_down[e]

    y_pad = lax.map(expert_block, (x_pad.reshape(n_blocks, EXPERT_BLOCK, d), block_e))
    y = y_pad.reshape(rows, d)[dest] * gates.reshape(-1)[order][:, None].astype(t.dtype)
    out = jax.ops.segment_sum(y, sorted_tok, num_segments=n)
    return out.reshape(bsz, s, d)


def setup_inputs(seed: int = 0) -> dict:
    key = jax.random.key(seed)
    ks = iter(jax.random.split(key, 40))
    f32 = jnp.float32
    L = DEPTH

    def nrm(shape, scale):
        return jax.random.normal(next(ks), shape, f32) * scale

    def gain(shape):
        return 1.0 + nrm(shape, 0.02)

    x = nrm((BATCH, SEQ, D_MODEL), 1.0)
    mem = nrm((BATCH, MEM_LEN, D_MODEL), 1.0)
    norm1_w = gain((L, D_MODEL))
    w_in = nrm((L, D_MODEL, IN_WIDTH), D_MODEL ** -0.5)
    conv_w = nrm((L, CONV_WIDTH, 3 * DN_WIDTH), CONV_WIDTH ** -0.5)
    a_log = jnp.log(jax.random.uniform(next(ks), (L, DN_HEADS), f32, 1.0, 16.0))
    dt = jnp.exp(jax.random.uniform(next(ks), (L, DN_HEADS), f32, math.log(1e-3), math.log(1e-1)))
    dt_bias = dt + jnp.log(-jnp.expm1(-dt))
    dn_norm_w = gain((L, DN_HEAD_DIM))
    sb_q_norm_w = gain((L, SB_HEAD_DIM))
    sb_k_norm_w = gain((L, SB_HEAD_DIM))
    w_out = nrm((L, MIX_WIDTH, D_MODEL), MIX_WIDTH ** -0.5)
    norm2_w = gain((L, D_MODEL))
    mem_norm_w = gain((L, D_MODEL))
    xq_w = nrm((L, D_MODEL, X_HEADS * X_HEAD_DIM), D_MODEL ** -0.5)
    xk_w = nrm((L, D_MODEL, X_HEADS * X_HEAD_DIM), D_MODEL ** -0.5)
    xv_w = nrm((L, D_MODEL, X_HEADS * X_HEAD_DIM), D_MODEL ** -0.5)
    xq_norm_w = gain((L, X_HEAD_DIM))
    xk_norm_w = gain((L, X_HEAD_DIM))
    xo_w = nrm((L, X_HEADS * X_HEAD_DIM, D_MODEL), D_MODEL ** -0.5)
    norm3_w = gain((L, D_MODEL))
    router_w = nrm((L, D_MODEL, N_EXPERTS), D_MODEL ** -0.5)
    router_b = nrm((L, N_EXPERTS), 0.01)
    w_gate = nrm((L, N_EXPERTS, D_MODEL, D_FF), D_MODEL ** -0.5)
    b_gate = nrm((L, N_EXPERTS, D_FF), 0.01)
    w_up = nrm((L, N_EXPERTS, D_MODEL, D_FF), D_MODEL ** -0.5)
    b_up = nrm((L, N_EXPERTS, D_FF), 0.01)
    w_down = nrm((L, N_EXPERTS, D_FF, D_MODEL), D_FF ** -0.5)
    b_down = nrm((L, N_EXPERTS, D_MODEL), 0.01)
    return {"x": x, "mem": mem, "norm1_w": norm1_w, "w_in": w_in, "conv_w": conv_w,
            "a_log": a_log, "dt_bias": dt_bias, "dn_norm_w": dn_norm_w,
            "sb_q_norm_w": sb_q_norm_w, "sb_k_norm_w": sb_k_norm_w, "w_out": w_out,
            "norm2_w": norm2_w, "mem_norm_w": mem_norm_w, "xq_w": xq_w, "xk_w": xk_w,
            "xv_w": xv_w, "xq_norm_w": xq_norm_w, "xk_norm_w": xk_norm_w, "xo_w": xo_w,
            "norm3_w": norm3_w, "router_w": router_w, "router_b": router_b,
            "w_gate": w_gate, "b_gate": b_gate, "w_up": w_up, "b_up": b_up,
            "w_down": w_down, "b_down": b_down}


def reference(x, mem, norm1_w, w_in, conv_w, a_log, dt_bias, dn_norm_w, sb_q_norm_w,
              sb_k_norm_w, w_out, norm2_w, mem_norm_w, xq_w, xk_w, xv_w, xq_norm_w,
              xk_norm_w, xo_w, norm3_w, router_w, router_b, w_gate, b_gate, w_up, b_up,
              w_down, b_down):
    offsets = np.cumsum(IN_SPLITS)[:-1].tolist()
    for l in range(DEPTH):
        n = rms_norm(x, norm1_w[l])
        proj = n @ w_in[l]
        dq, dk, dv, dz, da, db, sq, sk, sv = jnp.split(proj, offsets, axis=-1)
        y_dn = gated_deltanet_group(dq, dk, dv, dz, da, db, conv_w[l], a_log[l], dt_bias[l], dn_norm_w[l])
        y_sb = stick_breaking_group(sq, sk, sv, sb_q_norm_w[l], sb_k_norm_w[l])
        x = x + jnp.concatenate([y_dn, y_sb], axis=-1) @ w_out[l]
        x = x + memory_cross_attention(rms_norm(x, norm2_w[l]), rms_norm(mem, mem_norm_w[l]),
                                       xq_w[l], xk_w[l], xv_w[l], xq_norm_w[l], xk_norm_w[l], xo_w[l])
        x = x + moe_ffn(rms_norm(x, norm3_w[l]), router_w[l], router_b[l], w_gate[l], b_gate[l],
                        w_up[l], b_up[l], w_down[l], b_down[l])
    return x
```

```python
import functools

import jax
import jax.numpy as jnp
from jax import lax
from jax.experimental import pallas as pl
from jax.experimental.pallas import tpu as pltpu

F32 = jnp.float32
BF16 = jnp.bfloat16

EPS = 1e-6
CHUNK = 64
DN_HEADS = 4
DN_DIM = 128
DN_WIDTH = DN_HEADS * DN_DIM
CONV_WIDTH = 4
HALO = 16
SB_HEADS = 8
SB_DIM = 64
SB_WIDTH = SB_HEADS * SB_DIM
SB_BLOCK = 128
X_HEADS = 4
X_DIM = 256
N_EXPERTS = 32
TOP_K = 4
EXPERT_BLOCK = 512
SWIGLU_ALPHA = 1.702
SWIGLU_LIMIT = 7.0
LANES = 128
SB_UNDERFLOW = -88.0
V7X_VMEM_LIMIT = 48 * 1024 * 1024


def _dot(a, b):
    return jnp.dot(a, b, preferred_element_type=F32)


def _dot_nt(a, b):
    return lax.dot_general(a, b, (((1,), (1,)), ((), ())), preferred_element_type=F32)


def _dot_tn(a, b):
    return lax.dot_general(a, b, (((0,), (0,)), ((), ())), preferred_element_type=F32)


def _dot_f32(a, b):
    return jnp.dot(a, b, preferred_element_type=F32, precision=lax.Precision.HIGHEST)


def _softplus(x):
    return jnp.maximum(x, 0.0) + jnp.log(1.0 + jnp.exp(-jnp.abs(x)))


def _sigmoid(x):
    return 1.0 / (1.0 + jnp.exp(-x))


def _rms(x, w):
    return x * lax.rsqrt(jnp.mean(x * x, axis=-1, keepdims=True) + EPS) * w


def _params(*sem):
    return pltpu.CompilerParams(dimension_semantics=sem, vmem_limit_bytes=V7X_VMEM_LIMIT)


def _const_spec(shape):
    nd = len(shape)
    return pl.BlockSpec(shape, lambda *_: (0,) * nd)


def _in_proj_kernel(x_ref, nw_ref, wdn_ref, wsb_ref, wab_ref, wabt_ref,
                    dn_ref, sb_ref, ab_ref, abt_ref):
    n = _rms(x_ref[...], nw_ref[...]).astype(BF16)
    dn_ref[...] = _dot(n, wdn_ref[...]).astype(BF16)
    sb_ref[...] = _dot(n, wsb_ref[...]).astype(BF16)
    ab_ref[...] = _dot(n, wab_ref[...])
    abt_ref[...] = _dot_nt(wabt_ref[...], n)


def _in_proj(x2d, norm_w, w_dn, w_sb, w_ab, w_abt, tm):
    n, d = x2d.shape
    return pl.pallas_call(
        _in_proj_kernel,
        grid=(n // tm,),
        in_specs=[
            pl.BlockSpec((tm, d), lambda i: (i, 0)),
            _const_spec((1, d)),
            _const_spec(w_dn.shape),
            _const_spec(w_sb.shape),
            _const_spec(w_ab.shape),
            _const_spec(w_abt.shape),
        ],
        out_specs=[
            pl.BlockSpec((tm, w_dn.shape[1]), lambda i: (i, 0)),
            pl.BlockSpec((tm, w_sb.shape[1]), lambda i: (i, 0)),
            pl.BlockSpec((tm, LANES), lambda i: (i, 0)),
            pl.BlockSpec((8, tm), lambda i: (0, i)),
        ],
        out_shape=[
            jax.ShapeDtypeStruct((n, w_dn.shape[1]), BF16),
            jax.ShapeDtypeStruct((n, w_sb.shape[1]), BF16),
            jax.ShapeDtypeStruct((n, LANES), F32),
            jax.ShapeDtypeStruct((8, n), F32),
        ],
        compiler_params=_params("arbitrary"),
        name="in_proj",
    )(x2d, norm_w, w_dn, w_sb, w_ab, w_abt)


def _deltanet_kernel(dn_ref, ab_ref, abt_ref, convw_ref, alog_ref, dtb_ref, alogt_ref, dtbt_ref,
                     normw_ref, o_ref, state_s):
    s = dn_ref.shape[0]
    state_s[...] = jnp.zeros(state_s.shape, F32)

    def conv_silu(r0, c, part, h, l2):
        cs = slice(part * DN_WIDTH + h * LANES, part * DN_WIDTH + (h + 1) * LANES)
        prev0 = pl.multiple_of(jnp.maximum(r0 - HALO, 0), HALO)
        prev = dn_ref[pl.ds(prev0, HALO), cs].astype(F32)
        prev = jnp.where(c > 0, prev, 0.0)
        win = jnp.concatenate([prev, dn_ref[pl.ds(r0, CHUNK), cs].astype(F32)], axis=0)
        w = convw_ref[:, cs]
        y = w[0:1, :] * win[HALO - 3:HALO - 3 + CHUNK, :]
        for i in range(1, CONV_WIDTH):
            y = y + w[i:i + 1, :] * win[HALO - 3 + i:HALO - 3 + i + CHUNK, :]
        y = y * _sigmoid(y)
        if l2:
            y = y * lax.rsqrt(jnp.sum(y * y, axis=-1, keepdims=True) + EPS)
        return y

    row = lax.broadcasted_iota(jnp.int32, (CHUNK, CHUNK), 0)
    col = lax.broadcasted_iota(jnp.int32, (CHUNK, CHUNK), 1)
    tri = row >= col
    strict = row > col
    tril_f = tri.astype(F32)
    triu_f = (row <= col).astype(F32)
    neg_a_col = -jnp.exp(alog_ref[...])
    neg_a_row = -jnp.exp(alogt_ref[...])
    scale = DN_DIM ** -0.5

    def chunk_body(c, carry):
        r0 = pl.multiple_of(c * CHUNK, CHUNK)
        ab = ab_ref[pl.ds(r0, CHUNK), :]
        g_col = neg_a_col * _softplus(ab + dtb_ref[...])
        gc_col_all = _dot_f32(tril_f, g_col)
        beta_all = _sigmoid(ab)
        abt = abt_ref[c]
        g_row = neg_a_row * _softplus(abt + dtbt_ref[...])
        gc_row_all = _dot_f32(g_row, triu_f)
        for h in range(DN_HEADS):
            cs = slice(h * LANES, (h + 1) * LANES)
            q = conv_silu(r0, c, 0, h, True) * scale
            k = conv_silu(r0, c, 1, h, True)
            v = conv_silu(r0, c, 2, h, False)
            gcol = gc_col_all[:, h:h + 1]
            grow = gc_row_all[h:h + 1, :]
            beta = beta_all[:, DN_HEADS + h:DN_HEADS + h + 1]
            glast = gcol[CHUNK - 1:CHUNK, :]
            decay = jnp.where(tri, jnp.exp(jnp.where(tri, gcol - grow, 0.0)), 0.0)
            kb = k * beta
            kb16 = kb.astype(BF16)
            k16 = k.astype(BF16)
            q16 = q.astype(BF16)
            x = -jnp.where(strict, _dot_nt(kb16, k16) * decay, 0.0)
            sol = jnp.concatenate([v * beta, kb * jnp.exp(gcol)], axis=1)
            p16 = x.astype(BF16)
            sol = sol + _dot(p16, sol.astype(BF16))
            for _ in range(5):
                p16 = _dot(p16, p16).astype(BF16)
                sol = sol + _dot(p16, sol.astype(BF16))
            u = sol[:, :DN_DIM]
            w16 = sol[:, DN_DIM:].astype(BF16)
            attn16 = jnp.where(tri, _dot_nt(q16, k16) * decay, 0.0).astype(BF16)
            kt16 = (k * jnp.exp(glast - gcol)).astype(BF16)
            qg16 = (q * jnp.exp(gcol)).astype(BF16)
            st = state_s[h]
            st16 = st.astype(BF16)
            v_new = u - _dot(w16, st16)
            v_new16 = v_new.astype(BF16)
            o = _dot(qg16, st16) + _dot(attn16, v_new16)
            state_s[h] = st * jnp.exp(glast) + _dot_tn(kt16, v_new16)
            o = o * lax.rsqrt(jnp.mean(o * o, axis=-1, keepdims=True) + EPS)
            z = dn_ref[pl.ds(r0, CHUNK), 3 * DN_WIDTH + h * LANES:3 * DN_WIDTH + (h + 1) * LANES].astype(F32)
            o_ref[pl.ds(r0, CHUNK), cs] = (o * normw_ref[...] * (z * _sigmoid(z))).astype(BF16)
        return carry

    lax.fori_loop(0, s // CHUNK, chunk_body, 0)


def _deltanet(dn3, ab3, abt4, conv_w, alog, dtb, alogt, dtbt, norm_w):
    b, s, _ = dn3.shape
    nch = s // CHUNK
    return pl.pallas_call(
        _deltanet_kernel,
        grid=(b,),
        in_specs=[
            pl.BlockSpec((None, s, 4 * DN_WIDTH), lambda i: (i, 0, 0)),
            pl.BlockSpec((None, s, LANES), lambda i: (i, 0, 0)),
            pl.BlockSpec((None, nch, 8, CHUNK), lambda i: (i, 0, 0, 0)),
            _const_spec(conv_w.shape),
            _const_spec((1, LANES)),
            _const_spec((1, LANES)),
            _const_spec((8, 1)),
            _const_spec((8, 1)),
            _const_spec((1, DN_DIM)),
        ],
        out_specs=pl.BlockSpec((None, s, DN_WIDTH), lambda i: (i, 0, 0)),
        out_shape=jax.ShapeDtypeStruct((b, s, DN_WIDTH), BF16),
        scratch_shapes=[pltpu.VMEM((DN_HEADS, DN_DIM, DN_DIM), F32)],
        compiler_params=_params("arbitrary"),
        name="deltanet",
    )(dn3, ab3, abt4, conv_w, alog, dtb, alogt, dtbt, norm_w)


def _stickbreak_kernel(q_ref, k_ref, v_ref, qw_ref, kw_ref, m2_ref, o_ref,
                       qn_s, kn_s, carry_s, acc_s):
    s = q_ref.shape[0]
    lane = lax.broadcasted_iota(jnp.int32, (1, LANES), 1)
    first = lane < SB_DIM

    def head_norm(x, w):
        sq = x * x
        sa = jnp.sum(jnp.where(first, sq, 0.0), axis=-1, keepdims=True)
        sb = jnp.sum(jnp.where(first, 0.0, sq), axis=-1, keepdims=True)
        ra = lax.rsqrt(sa * (1.0 / SB_DIM) + EPS)
        rb = lax.rsqrt(sb * (1.0 / SB_DIM) + EPS)
        return x * jnp.where(first, ra, rb) * w

    qn = head_norm(q_ref[...].astype(F32), qw_ref[...]) * (SB_DIM ** -0.5)
    qn_s[0] = jnp.where(first, qn, 0.0).astype(BF16)
    qn_s[1] = jnp.where(first, 0.0, qn).astype(BF16)
    kn_s[...] = head_norm(k_ref[...].astype(F32), kw_ref[...]).astype(BF16)

    row = lax.broadcasted_iota(jnp.int32, (SB_BLOCK, SB_BLOCK), 0)
    col = lax.broadcasted_iota(jnp.int32, (SB_BLOCK, SB_BLOCK), 1)
    causal = col < row

    def tile(h, q16, k16, v16, masked):
        z = _dot_nt(q16, k16)
        sp = _softplus(z)
        log_fail = -sp
        if masked:
            log_fail = jnp.where(causal, log_fail, 0.0)
        hi = log_fail.astype(BF16)
        lo = (log_fail - hi.astype(F32)).astype(BF16)
        cs = _dot(hi, m2_ref[...]) + _dot(lo, m2_ref[...])
        carry = carry_s[h]
        p = jnp.exp(z - sp + carry + cs[:, :SB_BLOCK])
        if masked:
            p = jnp.where(causal, p, 0.0)
        acc_s[h] = acc_s[h] + _dot(p.astype(BF16), v16)
        carry_s[h] = carry + cs[:, SB_BLOCK:]

    def q_block(qb, carry):
        r0 = pl.multiple_of(qb * SB_BLOCK, SB_BLOCK)
        carry_s[...] = jnp.zeros(carry_s.shape, F32)
        acc_s[...] = jnp.zeros(acc_s.shape, F32)
        k16 = kn_s[pl.ds(r0, SB_BLOCK), :]
        v16 = v_ref[pl.ds(r0, SB_BLOCK), :]
        for h in range(2):
            tile(h, qn_s[h, pl.ds(r0, SB_BLOCK), :], k16, v16, True)

        def cond(st):
            kb, alive = st
            return jnp.logical_and(kb >= 0, alive)

        def body(st):
            kb, _ = st
            alive = jnp.max(carry_s[...]) > SB_UNDERFLOW
            c0 = pl.multiple_of(kb * SB_BLOCK, SB_BLOCK)
            kk = kn_s[pl.ds(c0, SB_BLOCK), :]
            vv = v_ref[pl.ds(c0, SB_BLOCK), :]
            for h in range(2):
                tile(h, qn_s[h, pl.ds(r0, SB_BLOCK), :], kk, vv, False)
            return kb - 1, alive

        lax.while_loop(cond, body, (qb - 1, jnp.bool_(True)))
        o_ref[pl.ds(r0, SB_BLOCK), :] = jnp.where(first, acc_s[0], acc_s[1]).astype(BF16)
        return carry

    lax.fori_loop(0, s // SB_BLOCK, q_block, 0)


def _stickbreak(sb3, qw, kw, m2):
    b, s, _ = sb3.shape
    pairs = SB_HEADS // 2
    return pl.pallas_call(
        _stickbreak_kernel,
        grid=(b, pairs),
        in_specs=[
            pl.BlockSpec((None, s, LANES), lambda i, j: (i, 0, j)),
            pl.BlockSpec((None, s, LANES), lambda i, j: (i, 0, pairs + j)),
            pl.BlockSpec((None, s, LANES), lambda i, j: (i, 0, 2 * pairs + j)),
            _const_spec((1, LANES)),
            _const_spec((1, LANES)),
            _const_spec(m2.shape),
        ],
        out_specs=pl.BlockSpec((None, s, LANES), lambda i, j: (i, 0, j)),
        out_shape=jax.ShapeDtypeStruct((b, s, SB_WIDTH), BF16),
        scratch_shapes=[
            pltpu.VMEM((2, s, LANES), BF16),
            pltpu.VMEM((s, LANES), BF16),
            pltpu.VMEM((2, SB_BLOCK, SB_BLOCK), F32),
            pltpu.VMEM((2, SB_BLOCK, LANES), F32),
        ],
        compiler_params=_params("arbitrary", "arbitrary"),
        name="stickbreak",
    )(sb3, sb3, sb3, qw, kw, m2)


def _memkv_kernel(mem_ref, nw_ref, wk_ref, wv_ref, knw_ref, k_ref, v_ref):
    n = _rms(mem_ref[...], nw_ref[...]).astype(BF16)
    k = _dot(n, wk_ref[...])
    for h in range(X_HEADS):
        cs = slice(h * X_DIM, (h + 1) * X_DIM)
        k_ref[:, cs] = _rms(k[:, cs], knw_ref[...]).astype(BF16)
    v_ref[...] = _dot(n, wv_ref[...]).astype(BF16)


def _memkv(mem, norm_w, wk, wv, k_norm_w):
    b, m, d = mem.shape
    return pl.pallas_call(
        _memkv_kernel,
        grid=(b,),
        in_specs=[
            pl.BlockSpec((None, m, d), lambda i: (i, 0, 0)),
            _const_spec((1, d)),
            _const_spec(wk.shape),
            _const_spec(wv.shape),
            _const_spec((1, X_DIM)),
        ],
        out_specs=[
            pl.BlockSpec((None, m, d), lambda i: (i, 0, 0)),
            pl.BlockSpec((None, m, d), lambda i: (i, 0, 0)),
        ],
        out_shape=[jax.ShapeDtypeStruct((b, m, d), BF16)] * 2,
        compiler_params=_params("arbitrary"),
        name="memkv",
    )(mem, norm_w, wk, wv, k_norm_w)


def _mid_kernel(x_ref, ydn_ref, ysb_ref, wout_ref, n2w_ref, wq_ref, qnw_ref, km_ref, vm_ref, wo_ref,
                n3w_ref, wrh_ref, wrl_ref, rb_ref, upper_ref,
                x2_ref, h3_ref, idx_ref, gate_ref, rank_ref, cnt_ref, count_s):
    tm = x_ref.shape[0]
    first_step = jnp.logical_and(pl.program_id(0) == 0, pl.program_id(1) == 0)

    @pl.when(first_step)
    def _():
        count_s[...] = jnp.zeros(count_s.shape, F32)

    x1 = (x_ref[...] + _dot(ydn_ref[...], wout_ref[0:DN_WIDTH, :])
          + _dot(ysb_ref[...], wout_ref[DN_WIDTH:, :]))
    n2 = _rms(x1, n2w_ref[...]).astype(BF16)
    q = _dot(n2, wq_ref[...])
    heads = []
    for h in range(X_HEADS):
        cs = slice(h * X_DIM, (h + 1) * X_DIM)
        qh = (_rms(q[:, cs], qnw_ref[...]) * (X_DIM ** -0.5)).astype(BF16)
        sc = _dot_nt(qh, km_ref[:, cs])
        sc = sc - jnp.max(sc, axis=-1, keepdims=True)
        e = jnp.exp(sc)
        p = e / jnp.sum(e, axis=-1, keepdims=True)
        heads.append(_dot(p.astype(BF16), vm_ref[:, cs]).astype(BF16))
    o = jnp.concatenate(heads, axis=1)
    x2 = x1 + _dot(o, wo_ref[...])
    x2_ref[...] = x2
    h3 = _rms(x2, n3w_ref[...])
    h3_ref[...] = h3
    hi = h3.astype(BF16)
    lo = (h3 - hi.astype(F32)).astype(BF16)
    logits = (_dot_nt(wrh_ref[...], hi) + _dot_nt(wrh_ref[...], lo) + _dot_nt(wrl_ref[...], hi)
              + rb_ref[...])
    eid = lax.broadcasted_iota(jnp.int32, (N_EXPERTS, tm), 0).astype(F32)
    vals, ids = [], []
    cur = logits
    for _ in range(TOP_K):
        m = jnp.max(cur, axis=0, keepdims=True)
        i = jnp.min(jnp.where(cur == m, eid, float(N_EXPERTS)), axis=0, keepdims=True)
        vals.append(m)
        ids.append(i)
        cur = jnp.where(eid == i, -jnp.inf, cur)
    exps = [jnp.exp(v - vals[0]) for v in vals]
    denom = exps[0] + exps[1] + exps[2] + exps[3]
    onehot = jnp.zeros((N_EXPERTS, tm), F32)
    for i in ids:
        onehot = onehot + jnp.where(eid == i, 1.0, 0.0)
    before = count_s[...] + _dot(onehot.astype(BF16), upper_ref[...])
    for k in range(TOP_K):
        idx_ref[k:k + 1, :] = ids[k].astype(jnp.int32)
        gate_ref[k:k + 1, :] = exps[k] / denom
        rank_ref[k:k + 1, :] = jnp.sum(jnp.where(eid == ids[k], before, 0.0), axis=0,
                                       keepdims=True).astype(jnp.int32)
    count_s[...] = count_s[...] + jnp.sum(onehot, axis=1, keepdims=True)
    cnt_ref[...] = jnp.broadcast_to(count_s[...], cnt_ref.shape)


def _mid(x3, ydn3, ysb3, w_out, n2w, wq, qnw, k_mem, v_mem, wo, n3w, wr_hi, wr_lo, rb, upper, tm):
    b, s, d = x3.shape
    n = b * s
    nt = s // tm
    m = k_mem.shape[1]
    tok = lambda i, j: (0, i * nt + j)
    return pl.pallas_call(
        _mid_kernel,
        grid=(b, nt),
        in_specs=[
            pl.BlockSpec((None, tm, d), lambda i, j: (i, j, 0)),
            pl.BlockSpec((None, tm, DN_WIDTH), lambda i, j: (i, j, 0)),
            pl.BlockSpec((None, tm, SB_WIDTH), lambda i, j: (i, j, 0)),
            _const_spec(w_out.shape),
            _const_spec((1, d)),
            _const_spec(wq.shape),
            _const_spec((1, X_DIM)),
            pl.BlockSpec((None, m, d), lambda i, j: (i, 0, 0)),
            pl.BlockSpec((None, m, d), lambda i, j: (i, 0, 0)),
            _const_spec(wo.shape),
            _const_spec((1, d)),
            _const_spec(wr_hi.shape),
            _const_spec(wr_lo.shape),
            _const_spec((N_EXPERTS, 1)),
            _const_spec(upper.shape),
        ],
        out_specs=[
            pl.BlockSpec((None, tm, d), lambda i, j: (i, j, 0)),
            pl.BlockSpec((None, tm, d), lambda i, j: (i, j, 0)),
            pl.BlockSpec((TOP_K, tm), tok),
            pl.BlockSpec((TOP_K, tm), tok),
            pl.BlockSpec((TOP_K, tm), tok),
            _const_spec((N_EXPERTS, LANES)),
        ],
        out_shape=[
            jax.ShapeDtypeStruct((b, s, d), F32),
            jax.ShapeDtypeStruct((b, s, d), F32),
            jax.ShapeDtypeStruct((TOP_K, n), jnp.int32),
            jax.ShapeDtypeStruct((TOP_K, n), F32),
            jax.ShapeDtypeStruct((TOP_K, n), jnp.int32),
            jax.ShapeDtypeStruct((N_EXPERTS, LANES), F32),
        ],
        scratch_shapes=[pltpu.VMEM((N_EXPERTS, 1), F32)],
        compiler_params=_params("arbitrary", "arbitrary"),
        name="mid",
    )(x3, ydn3, ysb3, w_out, n2w, wq, qnw, k_mem, v_mem, wo, n3w, wr_hi, wr_lo, rb, upper)


def _row_copy(src_ref, src_row, dst_ref, dst_row, sem):
    return pltpu.make_async_copy(src_ref.at[pl.ds(src_row, 1), :], dst_ref.at[pl.ds(dst_row, 1), :], sem)


def _dispatch_kernel(dest_ref, h_ref, zero_ref, xpad_ref, sem):
    del zero_ref
    tf = h_ref.shape[0]

    def issue(i, carry):
        _row_copy(h_ref, i % tf, xpad_ref, dest_ref[0, i], sem).start()
        return carry

    lax.fori_loop(0, TOP_K * tf, issue, 0)

    def drain(i, carry):
        _row_copy(h_ref, 0, xpad_ref, 0, sem).wait()
        return carry

    lax.fori_loop(0, TOP_K * tf, drain, 0)


def _dispatch(dest_tiles, h2d, zeros, tf):
    n, d = h2d.shape
    return pl.pallas_call(
        _dispatch_kernel,
        grid=(n // tf,),
        in_specs=[
            pl.BlockSpec((None, 1, TOP_K * tf), lambda i: (i, 0, 0), memory_space=pltpu.SMEM),
            pl.BlockSpec((tf, d), lambda i: (i, 0)),
            pl.BlockSpec(memory_space=pl.ANY),
        ],
        out_specs=pl.BlockSpec(memory_space=pl.ANY),
        out_shape=jax.ShapeDtypeStruct(zeros.shape, F32),
        scratch_shapes=[pltpu.SemaphoreType.DMA(())],
        input_output_aliases={2: 0},
        compiler_params=_params("arbitrary"),
        name="dispatch",
    )(dest_tiles, h2d, zeros)


def _experts_kernel(be_ref, nb_ref, x_ref, wg_ref, bg_ref, wu_ref, bu_ref, wd_ref, bd_ref, y_ref):
    del be_ref

    @pl.when(pl.program_id(0) < nb_ref[0])
    def _():
        x = x_ref[...].astype(BF16)
        gate = jnp.minimum(_dot(x, wg_ref[...]) + bg_ref[...], SWIGLU_LIMIT)
        up = jnp.clip(_dot(x, wu_ref[...]) + bu_ref[...], -SWIGLU_LIMIT, SWIGLU_LIMIT)
        act = (up + 1.0) * gate * _sigmoid(gate * SWIGLU_ALPHA)
        y_ref[...] = _dot(act.astype(BF16), wd_ref[...]) + bd_ref[...]


def _experts(block_e, nb_used, x_pad, wg, bg, wu, bu, wd, bd):
    rows, d = x_pad.shape
    nblk = rows // EXPERT_BLOCK
    dff = wg.shape[2]
    row_blk = lambda j, be, nb: (jnp.minimum(j, nb[0] - 1), 0)
    w_blk = lambda j, be, nb: (be[j], 0, 0)
    return pl.pallas_call(
        _experts_kernel,
        grid_spec=pltpu.PrefetchScalarGridSpec(
            num_scalar_prefetch=2,
            grid=(nblk,),
            in_specs=[
                pl.BlockSpec((EXPERT_BLOCK, d), row_blk),
                pl.BlockSpec((None, d, dff), w_blk),
                pl.BlockSpec((None, 1, dff), w_blk),
                pl.BlockSpec((None, d, dff), w_blk),
                pl.BlockSpec((None, 1, dff), w_blk),
                pl.BlockSpec((None, dff, d), w_blk),
                pl.BlockSpec((None, 1, d), w_blk),
            ],
            out_specs=pl.BlockSpec((EXPERT_BLOCK, d), row_blk),
        ),
        out_shape=jax.ShapeDtypeStruct((rows, d), F32),
        compiler_params=_params("arbitrary"),
        name="experts",
    )(block_e, nb_used, x_pad, wg, bg, wu, bu, wd, bd)


def _combine_kernel(dest_ref, x2_ref, gate_ref, ypad_ref, o_ref, buf, sem):
    th = x2_ref.shape[0]

    def issue(i, carry):
        _row_copy(ypad_ref, dest_ref[0, i], buf.at[i // th], i % th, sem).start()
        return carry

    lax.fori_loop(0, TOP_K * th, issue, 0)

    def drain(i, carry):
        _row_copy(ypad_ref, 0, buf.at[0], 0, sem).wait()
        return carry

    lax.fori_loop(0, TOP_K * th, drain, 0)
    g = gate_ref[...]
    out = x2_ref[...]
    for k in range(TOP_K):
        out = out + buf[k] * g[:, k:k + 1]
    o_ref[...] = out


def _combine(dest_tiles, x2d, gates_nk, y_pad, th):
    n, d = x2d.shape
    return pl.pallas_call(
        _combine_kernel,
        grid=(n // th,),
        in_specs=[
            pl.BlockSpec((None, 1, TOP_K * th), lambda i: (i, 0, 0), memory_space=pltpu.SMEM),
            pl.BlockSpec((th, d), lambda i: (i, 0)),
            pl.BlockSpec((th, TOP_K), lambda i: (i, 0)),
            pl.BlockSpec(memory_space=pl.ANY),
        ],
        out_specs=pl.BlockSpec((th, d), lambda i: (i, 0)),
        out_shape=jax.ShapeDtypeStruct((n, d), F32),
        scratch_shapes=[pltpu.VMEM((TOP_K, th, d), F32), pltpu.SemaphoreType.DMA(())],
        compiler_params=_params("arbitrary"),
        name="combine",
    )(dest_tiles, x2d, gates_nk, y_pad)


def _tile_dest(dest, t):
    k, n = dest.shape
    return dest.reshape(k, n // t, t).transpose(1, 0, 2).reshape(n // t, 1, k * t)


def _layer(x, mem, norm1_w, w_in, conv_w, a_log, dt_bias, dn_norm_w, sb_q_norm_w, sb_k_norm_w, w_out,
           norm2_w, mem_norm_w, xq_w, xk_w, xv_w, xq_norm_w, xk_norm_w, xo_w, norm3_w, router_w,
           router_b, w_gate, b_gate, w_up, b_up, w_down, b_down):
    b, s, d = x.shape
    n = b * s
    tm_proj = min(512, s)
    tm_mid = min(256, s)
    t_moe = min(256, s)

    o_dn, o_ab, o_sb = 4 * DN_WIDTH, 4 * DN_WIDTH + 2 * DN_HEADS, 4 * DN_WIDTH + 2 * DN_HEADS
    w_dn = w_in[:, :o_dn].astype(BF16)
    w_ab_f = w_in[:, o_dn:o_ab]
    w_ab = jnp.pad(w_ab_f, ((0, 0), (0, LANES - 2 * DN_HEADS))).astype(BF16)
    w_abt = w_ab_f.T.astype(BF16)
    w_sb = w_in[:, o_sb:].astype(BF16)
    row = lambda v: v.reshape(1, -1).astype(F32)

    dn, sb, ab, abt = _in_proj(x.reshape(n, d), row(norm1_w), w_dn, w_sb, w_ab, w_abt, tm_proj)

    abt4 = abt.reshape(8, n // CHUNK, CHUNK).transpose(1, 0, 2).reshape(b, s // CHUNK, 8, CHUNK)
    pad_lane = lambda v: jnp.pad(v.astype(F32), (0, LANES - v.shape[0])).reshape(1, LANES)
    pad_col = lambda v: jnp.pad(v.astype(F32), (0, 8 - v.shape[0])).reshape(8, 1)
    y_dn = _deltanet(dn.reshape(b, s, -1), ab.reshape(b, s, LANES), abt4, conv_w.astype(F32),
                     pad_lane(a_log), pad_lane(dt_bias), pad_col(a_log), pad_col(dt_bias), row(dn_norm_w))

    ii = jnp.arange(SB_BLOCK)
    m2 = jnp.concatenate([(ii[:, None] > ii[None, :]).astype(BF16),
                          jnp.ones((SB_BLOCK, SB_BLOCK), BF16)], axis=1)
    y_sb = _stickbreak(sb.reshape(b, s, -1), row(jnp.tile(sb_q_norm_w, 2)), row(jnp.tile(sb_k_norm_w, 2)), m2)

    k_mem, v_mem = _memkv(mem, row(mem_norm_w), xk_w.astype(BF16), xv_w.astype(BF16), row(xk_norm_w))

    wr_t = router_w.T.astype(F32)
    wr_hi = wr_t.astype(BF16)
    wr_lo = (wr_t - wr_hi.astype(F32)).astype(BF16)
    jj = jnp.arange(tm_mid)
    upper = (jj[:, None] < jj[None, :]).astype(BF16)
    x2, h3, idx, gates, rank, cnt = _mid(
        x, y_dn, y_sb, w_out.astype(BF16), row(norm2_w), xq_w.astype(BF16), row(xq_norm_w), k_mem, v_mem,
        xo_w.astype(BF16), row(norm3_w), wr_hi, wr_lo, router_b.reshape(N_EXPERTS, 1).astype(F32), upper,
        tm_mid)

    counts = cnt[:, 0].astype(jnp.int32)
    padded = (counts + EXPERT_BLOCK - 1) // EXPERT_BLOCK * EXPERT_BLOCK
    pad_ends = jnp.cumsum(padded)
    pad_starts = pad_ends - padded
    sel = idx[:, :, None] == jnp.arange(N_EXPERTS, dtype=jnp.int32)[None, None, :]
    dest = rank + jnp.sum(jnp.where(sel, pad_starts[None, None, :], 0), axis=-1)
    n_blocks = -(-n * TOP_K // EXPERT_BLOCK) + N_EXPERTS
    nb_used = (pad_ends[-1] // EXPERT_BLOCK).astype(jnp.int32)
    blk = jnp.minimum(jnp.arange(n_blocks, dtype=jnp.int32), nb_used - 1) * EXPERT_BLOCK
    block_e = jnp.minimum(jnp.sum(pad_ends[None, :] <= blk[:, None], axis=1), N_EXPERTS - 1).astype(jnp.int32)
    dest_tiles = _tile_dest(dest, t_moe)

    x_pad = _dispatch(dest_tiles, h3.reshape(n, d), jnp.zeros((n_blocks * EXPERT_BLOCK, d), F32), t_moe)
    bias = lambda v: v.reshape(N_EXPERTS, 1, -1).astype(F32)
    y_pad = _experts(block_e, nb_used.reshape(1), x_pad, w_gate.astype(BF16), bias(b_gate),
                     w_up.astype(BF16), bias(b_up), w_down.astype(BF16), bias(b_down))
    out = _combine(dest_tiles, x2.reshape(n, d), gates.T, y_pad, t_moe)
    return out.reshape(b, s, d)


def kernel(x, mem, norm1_w, w_in, conv_w, a_log, dt_bias, dn_norm_w, sb_q_norm_w, sb_k_norm_w, w_out,
           norm2_w, mem_norm_w, xq_w, xk_w, xv_w, xq_norm_w, xk_norm_w, xo_w, norm3_w, router_w,
           router_b, w_gate, b_gate, w_up, b_up, w_down, b_down):
    depth = w_in.shape[0]
    for l in range(depth):
        x = _layer(x, mem, norm1_w[l], w_in[l], conv_w[l], a_log[l], dt_bias[l], dn_norm_w[l],
                   sb_q_norm_w[l], sb_k_norm_w[l], w_out[l], norm2_w[l], mem_norm_w[l], xq_w[l], xk_w[l],
                   xv_w[l], xq_norm_w[l], xk_norm_w[l], xo_w[l], norm3_w[l], router_w[l], router_b[l],
                   w_gate[l], b_gate[l], w_up[l], b_up[l], w_down[l], b_down[l])
    return x
```

```python
import functools

import jax
import jax.numpy as jnp
from jax import lax
from jax.experimental import pallas as pl
from jax.experimental.pallas import tpu as pltpu

F32 = jnp.float32
BF16 = jnp.bfloat16

EPS = 1e-6
CHUNK = 64
DN_HEADS = 4
DN_DIM = 128
DN_WIDTH = DN_HEADS * DN_DIM
CONV_WIDTH = 4
DN_PAIR = 2
HALO = 16
SB_HEADS = 8
SB_DIM = 64
SB_WIDTH = SB_HEADS * SB_DIM
SB_BLOCK = 128
SB_GROUP = 4
X_HEADS = 4
X_DIM = 256
N_EXPERTS = 32
TOP_K = 4
EXPERT_BLOCK = 512
SWIGLU_ALPHA = 1.702
SWIGLU_LIMIT = 7.0
LANES = 128
SB_UNDERFLOW = -88.0
V7X_VMEM_LIMIT = 48 * 1024 * 1024


def _dot(a, b):
    return jnp.dot(a, b, preferred_element_type=F32)


def _dot_nt(a, b):
    return lax.dot_general(a, b, (((1,), (1,)), ((), ())), preferred_element_type=F32)


def _dot_tn(a, b):
    return lax.dot_general(a, b, (((0,), (0,)), ((), ())), preferred_element_type=F32)


def _dot_f32(a, b):
    return jnp.dot(a, b, preferred_element_type=F32, precision=lax.Precision.HIGHEST)


def _softplus(x):
    return jnp.maximum(x, 0.0) + jnp.log(1.0 + jnp.exp(-jnp.abs(x)))


def _sigmoid(x):
    return 1.0 / (1.0 + jnp.exp(-x))


def _rms(x, w):
    return x * lax.rsqrt(jnp.mean(x * x, axis=-1, keepdims=True) + EPS) * w


def _params(*sem):
    return pltpu.CompilerParams(dimension_semantics=sem, vmem_limit_bytes=V7X_VMEM_LIMIT)


def _const_spec(shape):
    nd = len(shape)
    return pl.BlockSpec(shape, lambda *_: (0,) * nd)


def _in_proj_kernel(x_ref, nw_ref, wdn_ref, wsb_ref, wab_ref, wabt_ref,
                    dn_ref, sb_ref, ab_ref, abt_ref):
    n = _rms(x_ref[...], nw_ref[...]).astype(BF16)
    dn_ref[...] = _dot(n, wdn_ref[...]).astype(BF16)
    sb_ref[...] = _dot(n, wsb_ref[...]).astype(BF16)
    ab_ref[...] = _dot(n, wab_ref[...])
    abt_ref[...] = _dot_nt(wabt_ref[...], n)


def _in_proj(x2d, norm_w, w_dn, w_sb, w_ab, w_abt, tm):
    n, d = x2d.shape
    return pl.pallas_call(
        _in_proj_kernel,
        grid=(n // tm,),
        in_specs=[
            pl.BlockSpec((tm, d), lambda i: (i, 0)),
            _const_spec((1, d)),
            _const_spec(w_dn.shape),
            _const_spec(w_sb.shape),
            _const_spec(w_ab.shape),
            _const_spec(w_abt.shape),
        ],
        out_specs=[
            pl.BlockSpec((tm, w_dn.shape[1]), lambda i: (i, 0)),
            pl.BlockSpec((tm, w_sb.shape[1]), lambda i: (i, 0)),
            pl.BlockSpec((tm, LANES), lambda i: (i, 0)),
            pl.BlockSpec((8, tm), lambda i: (0, i)),
        ],
        out_shape=[
            jax.ShapeDtypeStruct((n, w_dn.shape[1]), BF16),
            jax.ShapeDtypeStruct((n, w_sb.shape[1]), BF16),
            jax.ShapeDtypeStruct((n, LANES), F32),
            jax.ShapeDtypeStruct((8, n), F32),
        ],
        compiler_params=_params("arbitrary"),
        name="in_proj",
    )(x2d, norm_w, w_dn, w_sb, w_ab, w_abt)


def _deltanet_kernel(dn_ref, ab_ref, abt_ref, convw_ref, alog_ref, dtb_ref, alogt_ref, dtbt_ref,
                     normw_ref, o_ref, state_s, u_s, w_s, kt_s, qg_s, attn_s, egl_s):
    s = dn_ref.shape[0]
    state_s[...] = jnp.zeros(state_s.shape, F32)

    def conv_silu(r0, c, part, h, l2):
        cs = slice(part * DN_WIDTH + h * LANES, part * DN_WIDTH + (h + 1) * LANES)
        prev0 = pl.multiple_of(jnp.maximum(r0 - HALO, 0), HALO)
        prev = dn_ref[pl.ds(prev0, HALO), cs].astype(F32)
        prev = jnp.where(c > 0, prev, 0.0)
        win = jnp.concatenate([prev, dn_ref[pl.ds(r0, CHUNK), cs].astype(F32)], axis=0)
        w = convw_ref[:, cs]
        y = w[0:1, :] * win[HALO - 3:HALO - 3 + CHUNK, :]
        for i in range(1, CONV_WIDTH):
            y = y + w[i:i + 1, :] * win[HALO - 3 + i:HALO - 3 + i + CHUNK, :]
        y = y * _sigmoid(y)
        if l2:
            y = y * lax.rsqrt(jnp.sum(y * y, axis=-1, keepdims=True) + EPS)
        return y

    row = lax.broadcasted_iota(jnp.int32, (CHUNK, CHUNK), 0)
    col = lax.broadcasted_iota(jnp.int32, (CHUNK, CHUNK), 1)
    tri = row >= col
    strict = row > col
    tril_f = tri.astype(F32)
    triu_f = (row <= col).astype(F32)
    neg_a_col = -jnp.exp(alog_ref[...])
    neg_a_row = -jnp.exp(alogt_ref[...])
    scale = DN_DIM ** -0.5

    def precompute(it, carry):
        items = []
        for ci in range(DN_PAIR):
            c = it * DN_PAIR + ci
            r0 = pl.multiple_of(c * CHUNK, CHUNK)
            ab = ab_ref[pl.ds(r0, CHUNK), :]
            g_col = neg_a_col * _softplus(ab + dtb_ref[...])
            gc_col_all = _dot_f32(tril_f, g_col)
            beta_all = _sigmoid(ab)
            abt = abt_ref[c]
            g_row = neg_a_row * _softplus(abt + dtbt_ref[...])
            gc_row_all = _dot_f32(g_row, triu_f)
            for h in range(DN_HEADS):
                q = conv_silu(r0, c, 0, h, True) * scale
                k = conv_silu(r0, c, 1, h, True)
                v = conv_silu(r0, c, 2, h, False)
                gcol = gc_col_all[:, h:h + 1]
                grow = gc_row_all[h:h + 1, :]
                beta = beta_all[:, DN_HEADS + h:DN_HEADS + h + 1]
                glast = gcol[CHUNK - 1:CHUNK, :]
                decay = jnp.where(tri, jnp.exp(jnp.where(tri, gcol - grow, 0.0)), 0.0)
                kb = k * beta
                cs = slice(h * LANES, (h + 1) * LANES)
                kt_s[pl.ds(r0, CHUNK), cs] = (k * jnp.exp(glast - gcol)).astype(BF16)
                qg_s[pl.ds(r0, CHUNK), cs] = (q * jnp.exp(gcol)).astype(BF16)
                egl_s[c, h:h + 1, :] = jnp.broadcast_to(jnp.exp(glast), (1, LANES))
                items.append(dict(c=c, r0=r0, h=h, cs=cs, decay=decay, kb16=kb.astype(BF16),
                                  k16=k.astype(BF16), q16=q.astype(BF16),
                                  sol=jnp.concatenate([v * beta, kb * jnp.exp(gcol)], axis=1)))
        kk = [_dot_nt(t["kb16"], t["k16"]) for t in items]
        qk = [_dot_nt(t["q16"], t["k16"]) for t in items]
        for t, kk_i, qk_i in zip(items, kk, qk):
            attn_s[t["c"], t["h"]] = jnp.where(tri, qk_i * t["decay"], 0.0).astype(BF16)
            t["p16"] = (-jnp.where(strict, kk_i * t["decay"], 0.0)).astype(BF16)
        for level in range(6):
            app = [_dot(t["p16"], t["sol"].astype(BF16)) for t in items]
            if level < 5:
                sq = [_dot(t["p16"], t["p16"]) for t in items]
            for i, t in enumerate(items):
                t["sol"] = t["sol"] + app[i]
                if level < 5:
                    t["p16"] = sq[i].astype(BF16)
        for t in items:
            u_s[pl.ds(t["r0"], CHUNK), t["cs"]] = t["sol"][:, :DN_DIM]
            w_s[pl.ds(t["r0"], CHUNK), t["cs"]] = t["sol"][:, DN_DIM:].astype(BF16)
        return carry

    lax.fori_loop(0, s // (CHUNK * DN_PAIR), precompute, 0)

    def recurrence(c, carry):
        r0 = pl.multiple_of(c * CHUNK, CHUNK)
        heads = range(DN_HEADS)
        cols = [slice(h * LANES, (h + 1) * LANES) for h in heads]
        st = [state_s[h] for h in heads]
        st16 = [x.astype(BF16) for x in st]
        ws = [_dot(w_s[pl.ds(r0, CHUNK), cols[h]], st16[h]) for h in heads]
        qs = [_dot(qg_s[pl.ds(r0, CHUNK), cols[h]], st16[h]) for h in heads]
        vn16 = [(u_s[pl.ds(r0, CHUNK), cols[h]] - ws[h]).astype(BF16) for h in heads]
        av = [_dot(attn_s[c, h], vn16[h]) for h in heads]
        ks = [_dot_tn(kt_s[pl.ds(r0, CHUNK), cols[h]], vn16[h]) for h in heads]
        for h in heads:
            state_s[h] = st[h] * egl_s[c, h:h + 1, :] + ks[h]
            o = qs[h] + av[h]
            o = o * lax.rsqrt(jnp.mean(o * o, axis=-1, keepdims=True) + EPS)
            z = dn_ref[pl.ds(r0, CHUNK), 3 * DN_WIDTH + h * LANES:3 * DN_WIDTH + (h + 1) * LANES].astype(F32)
            o_ref[pl.ds(r0, CHUNK), cols[h]] = (o * normw_ref[...] * (z * _sigmoid(z))).astype(BF16)
        return carry

    lax.fori_loop(0, s // CHUNK, recurrence, 0)


def _deltanet(dn3, ab3, abt4, conv_w, alog, dtb, alogt, dtbt, norm_w):
    b, s, _ = dn3.shape
    nch = s // CHUNK
    return pl.pallas_call(
        _deltanet_kernel,
        grid=(b,),
        in_specs=[
            pl.BlockSpec((None, s, 4 * DN_WIDTH), lambda i: (i, 0, 0)),
            pl.BlockSpec((None, s, LANES), lambda i: (i, 0, 0)),
            pl.BlockSpec((None, nch, 8, CHUNK), lambda i: (i, 0, 0, 0)),
            _const_spec(conv_w.shape),
            _const_spec((1, LANES)),
            _const_spec((1, LANES)),
            _const_spec((8, 1)),
            _const_spec((8, 1)),
            _const_spec((1, DN_DIM)),
        ],
        out_specs=pl.BlockSpec((None, s, DN_WIDTH), lambda i: (i, 0, 0)),
        out_shape=jax.ShapeDtypeStruct((b, s, DN_WIDTH), BF16),
        scratch_shapes=[
            pltpu.VMEM((DN_HEADS, DN_DIM, DN_DIM), F32),
            pltpu.VMEM((s, DN_WIDTH), F32),
            pltpu.VMEM((s, DN_WIDTH), BF16),
            pltpu.VMEM((s, DN_WIDTH), BF16),
            pltpu.VMEM((s, DN_WIDTH), BF16),
            pltpu.VMEM((nch, DN_HEADS, CHUNK, CHUNK), BF16),
            pltpu.VMEM((nch, 8, LANES), F32),
        ],
        compiler_params=_params("arbitrary"),
        name="deltanet",
    )(dn3, ab3, abt4, conv_w, alog, dtb, alogt, dtbt, norm_w)


def _stickbreak_kernel(q_ref, k_ref, v_ref, qw_ref, kw_ref, m2_ref, o_ref,
                       qn_s, kn_s, carry_s, acc_s):
    s = q_ref.shape[0]
    group = carry_s.shape[0]
    lane = lax.broadcasted_iota(jnp.int32, (1, LANES), 1)
    first = lane < SB_DIM

    def head_norm(r0, x_ref, w):
        x = x_ref[pl.ds(r0, SB_BLOCK), :].astype(F32)
        sq = x * x
        sa = jnp.sum(jnp.where(first, sq, 0.0), axis=-1, keepdims=True)
        sb = jnp.sum(jnp.where(first, 0.0, sq), axis=-1, keepdims=True)
        ra = lax.rsqrt(sa * (1.0 / SB_DIM) + EPS)
        rb = lax.rsqrt(sb * (1.0 / SB_DIM) + EPS)
        return x * jnp.where(first, ra, rb) * w

    def norm_block(i, carry):
        r0 = pl.multiple_of(i * SB_BLOCK, SB_BLOCK)
        qn = head_norm(r0, q_ref, qw_ref[...]) * (SB_DIM ** -0.5)
        qn_s[0, pl.ds(r0, SB_BLOCK), :] = jnp.where(first, qn, 0.0).astype(BF16)
        qn_s[1, pl.ds(r0, SB_BLOCK), :] = jnp.where(first, 0.0, qn).astype(BF16)
        kn_s[pl.ds(r0, SB_BLOCK), :] = head_norm(r0, k_ref, kw_ref[...]).astype(BF16)
        return carry

    lax.fori_loop(0, s // SB_BLOCK, norm_block, 0)

    row = lax.broadcasted_iota(jnp.int32, (SB_BLOCK, SB_BLOCK), 0)
    col = lax.broadcasted_iota(jnp.int32, (SB_BLOCK, SB_BLOCK), 1)
    causal = col < row

    def step(rows, keys, valid):
        tiles = [(g, h) for g in range(group) for h in range(2)]
        k16 = [kn_s[pl.ds(c0, SB_BLOCK), :] for c0 in keys]
        v16 = [v_ref[pl.ds(c0, SB_BLOCK), :] for c0 in keys]
        z = [_dot_nt(qn_s[h, pl.ds(rows[g], SB_BLOCK), :], k16[g]) for g, h in tiles]
        sp = [_softplus(x) for x in z]
        if valid is None:
            log_fail = [jnp.where(causal, -x, 0.0) for x in sp]
        else:
            log_fail = [-x for x in sp]
        cs = [_dot(x.astype(BF16), m2_ref[...]) for x in log_fail]
        p = []
        for i, (g, h) in enumerate(tiles):
            w = jnp.exp(z[i] - sp[i] + carry_s[g, h] + cs[i][:, :SB_BLOCK])
            p.append(jnp.where(causal, w, 0.0) if valid is None else w)
        pv = [_dot(p[i].astype(BF16), v16[g]) for i, (g, h) in enumerate(tiles)]
        for i, (g, h) in enumerate(tiles):
            acc = acc_s[g, h] + pv[i]
            carry = carry_s[g, h] + cs[i][:, SB_BLOCK:]
            if valid is not None:
                acc = jnp.where(valid[g], acc, acc_s[g, h])
                carry = jnp.where(valid[g], carry, carry_s[g, h])
            acc_s[g, h] = acc
            carry_s[g, h] = carry

    def q_group(qg, carry):
        blocks = [qg * group + g for g in range(group)]
        rows = [pl.multiple_of(qb * SB_BLOCK, SB_BLOCK) for qb in blocks]
        carry_s[...] = jnp.zeros(carry_s.shape, F32)
        acc_s[...] = jnp.zeros(acc_s.shape, F32)
        step(rows, rows, None)

        def cond(st):
            d, alive = st
            return jnp.logical_and(d <= blocks[-1], alive)

        def body(st):
            d, _ = st
            alive = jnp.max(carry_s[...]) > SB_UNDERFLOW
            keys = [pl.multiple_of(jnp.maximum(qb - d, 0) * SB_BLOCK, SB_BLOCK) for qb in blocks]
            step(rows, keys, [qb >= d for qb in blocks])
            return d + 1, alive

        lax.while_loop(cond, body, (jnp.int32(1), jnp.bool_(True)))
        for g in range(group):
            o_ref[pl.ds(rows[g], SB_BLOCK), :] = jnp.where(first, acc_s[g, 0], acc_s[g, 1]).astype(BF16)
        return carry

    lax.fori_loop(0, s // (SB_BLOCK * group), q_group, 0)


def _stickbreak(sb3, qw, kw, m2):
    b, s, _ = sb3.shape
    pairs = SB_HEADS // 2
    group = min(SB_GROUP, s // SB_BLOCK)
    return pl.pallas_call(
        _stickbreak_kernel,
        grid=(b, pairs),
        in_specs=[
            pl.BlockSpec((None, s, LANES), lambda i, j: (i, 0, j)),
            pl.BlockSpec((None, s, LANES), lambda i, j: (i, 0, pairs + j)),
            pl.BlockSpec((None, s, LANES), lambda i, j: (i, 0, 2 * pairs + j)),
            _const_spec((1, LANES)),
            _const_spec((1, LANES)),
            _const_spec(m2.shape),
        ],
        out_specs=pl.BlockSpec((None, s, LANES), lambda i, j: (i, 0, j)),
        out_shape=jax.ShapeDtypeStruct((b, s, SB_WIDTH), BF16),
        scratch_shapes=[
            pltpu.VMEM((2, s, LANES), BF16),
            pltpu.VMEM((s, LANES), BF16),
            pltpu.VMEM((group, 2, SB_BLOCK, SB_BLOCK), F32),
            pltpu.VMEM((group, 2, SB_BLOCK, LANES), F32),
        ],
        compiler_params=_params("arbitrary", "arbitrary"),
        name="stickbreak",
    )(sb3, sb3, sb3, qw, kw, m2)


def _memkv_kernel(mem_ref, nw_ref, wk_ref, wv_ref, knw_ref, k_ref, v_ref):
    n = _rms(mem_ref[...], nw_ref[...]).astype(BF16)
    k = _dot(n, wk_ref[...])
    for h in range(X_HEADS):
        cs = slice(h * X_DIM, (h + 1) * X_DIM)
        k_ref[:, cs] = _rms(k[:, cs], knw_ref[...]).astype(BF16)
    v_ref[...] = _dot(n, wv_ref[...]).astype(BF16)


def _memkv(mem, norm_w, wk, wv, k_norm_w):
    b, m, d = mem.shape
    return pl.pallas_call(
        _memkv_kernel,
        grid=(b,),
        in_specs=[
            pl.BlockSpec((None, m, d), lambda i: (i, 0, 0)),
            _const_spec((1, d)),
            _const_spec(wk.shape),
            _const_spec(wv.shape),
            _const_spec((1, X_DIM)),
        ],
        out_specs=[
            pl.BlockSpec((None, m, d), lambda i: (i, 0, 0)),
            pl.BlockSpec((None, m, d), lambda i: (i, 0, 0)),
        ],
        out_shape=[jax.ShapeDtypeStruct((b, m, d), BF16)] * 2,
        compiler_params=_params("arbitrary"),
        name="memkv",
    )(mem, norm_w, wk, wv, k_norm_w)


def _mid_kernel(x_ref, ydn_ref, ysb_ref, wout_ref, n2w_ref, wq_ref, qnw_ref, km_ref, vm_ref, wo_ref,
                n3w_ref, wrh_ref, wrl_ref, rb_ref, upper_ref,
                x2_ref, h3_ref, idx_ref, gate_ref, rank_ref, cnt_ref, count_s):
    tm = x_ref.shape[0]
    first_step = jnp.logical_and(pl.program_id(0) == 0, pl.program_id(1) == 0)

    @pl.when(first_step)
    def _():
        count_s[...] = jnp.zeros(count_s.shape, F32)

    x1 = (x_ref[...] + _dot(ydn_ref[...], wout_ref[0:DN_WIDTH, :])
          + _dot(ysb_ref[...], wout_ref[DN_WIDTH:, :]))
    n2 = _rms(x1, n2w_ref[...]).astype(BF16)
    q = _dot(n2, wq_ref[...])
    heads = []
    for h in range(X_HEADS):
        cs = slice(h * X_DIM, (h + 1) * X_DIM)
        qh = (_rms(q[:, cs], qnw_ref[...]) * (X_DIM ** -0.5)).astype(BF16)
        sc = _dot_nt(qh, km_ref[:, cs])
        sc = sc - jnp.max(sc, axis=-1, keepdims=True)
        e = jnp.exp(sc)
        p = e / jnp.sum(e, axis=-1, keepdims=True)
        heads.append(_dot(p.astype(BF16), vm_ref[:, cs]).astype(BF16))
    o = jnp.concatenate(heads, axis=1)
    x2 = x1 + _dot(o, wo_ref[...])
    x2_ref[...] = x2
    h3 = _rms(x2, n3w_ref[...])
    h3_ref[...] = h3
    hi = h3.astype(BF16)
    lo = (h3 - hi.astype(F32)).astype(BF16)
    logits = (_dot_nt(wrh_ref[...], hi) + _dot_nt(wrh_ref[...], lo) + _dot_nt(wrl_ref[...], hi)
              + rb_ref[...])
    eid = lax.broadcasted_iota(jnp.int32, (N_EXPERTS, tm), 0).astype(F32)
    vals, ids = [], []
    cur = logits
    for _ in range(TOP_K):
        m = jnp.max(cur, axis=0, keepdims=True)
        i = jnp.min(jnp.where(cur == m, eid, float(N_EXPERTS)), axis=0, keepdims=True)
        vals.append(m)
        ids.append(i)
        cur = jnp.where(eid == i, -jnp.inf, cur)
    exps = [jnp.exp(v - vals[0]) for v in vals]
    denom = exps[0] + exps[1] + exps[2] + exps[3]
    onehot = jnp.zeros((N_EXPERTS, tm), F32)
    for i in ids:
        onehot = onehot + jnp.where(eid == i, 1.0, 0.0)
    before = count_s[...] + _dot(onehot.astype(BF16), upper_ref[...])
    for k in range(TOP_K):
        idx_ref[k:k + 1, :] = ids[k].astype(jnp.int32)
        gate_ref[k:k + 1, :] = exps[k] / denom
        rank_ref[k:k + 1, :] = jnp.sum(jnp.where(eid == ids[k], before, 0.0), axis=0,
                                       keepdims=True).astype(jnp.int32)
    count_s[...] = count_s[...] + jnp.sum(onehot, axis=1, keepdims=True)
    cnt_ref[...] = jnp.broadcast_to(count_s[...], cnt_ref.shape)


def _mid(x3, ydn3, ysb3, w_out, n2w, wq, qnw, k_mem, v_mem, wo, n3w, wr_hi, wr_lo, rb, upper, tm):
    b, s, d = x3.shape
    n = b * s
    nt = s // tm
    m = k_mem.shape[1]
    tok = lambda i, j: (0, i * nt + j)
    return pl.pallas_call(
        _mid_kernel,
        grid=(b, nt),
        in_specs=[
            pl.BlockSpec((None, tm, d), lambda i, j: (i, j, 0)),
            pl.BlockSpec((None, tm, DN_WIDTH), lambda i, j: (i, j, 0)),
            pl.BlockSpec((None, tm, SB_WIDTH), lambda i, j: (i, j, 0)),
            _const_spec(w_out.shape),
            _const_spec((1, d)),
            _const_spec(wq.shape),
            _const_spec((1, X_DIM)),
            pl.BlockSpec((None, m, d), lambda i, j: (i, 0, 0)),
            pl.BlockSpec((None, m, d), lambda i, j: (i, 0, 0)),
            _const_spec(wo.shape),
            _const_spec((1, d)),
            _const_spec(wr_hi.shape),
            _const_spec(wr_lo.shape),
            _const_spec((N_EXPERTS, 1)),
            _const_spec(upper.shape),
        ],
        out_specs=[
            pl.BlockSpec((None, tm, d), lambda i, j: (i, j, 0)),
            pl.BlockSpec((None, tm, d), lambda i, j: (i, j, 0)),
            pl.BlockSpec((TOP_K, tm), tok),
            pl.BlockSpec((TOP_K, tm), tok),
            pl.BlockSpec((TOP_K, tm), tok),
            _const_spec((N_EXPERTS, LANES)),
        ],
        out_shape=[
            jax.ShapeDtypeStruct((b, s, d), F32),
            jax.ShapeDtypeStruct((b, s, d), F32),
            jax.ShapeDtypeStruct((TOP_K, n), jnp.int32),
            jax.ShapeDtypeStruct((TOP_K, n), F32),
            jax.ShapeDtypeStruct((TOP_K, n), jnp.int32),
            jax.ShapeDtypeStruct((N_EXPERTS, LANES), F32),
        ],
        scratch_shapes=[pltpu.VMEM((N_EXPERTS, 1), F32)],
        compiler_params=_params("arbitrary", "arbitrary"),
        name="mid",
    )(x3, ydn3, ysb3, w_out, n2w, wq, qnw, k_mem, v_mem, wo, n3w, wr_hi, wr_lo, rb, upper)


def _row_copy(src_ref, src_row, dst_ref, dst_row, sem):
    return pltpu.make_async_copy(src_ref.at[pl.ds(src_row, 1), :], dst_ref.at[pl.ds(dst_row, 1), :], sem)


def _dispatch_kernel(dest_ref, h_ref, zero_ref, xpad_ref, sem):
    del zero_ref
    tf = h_ref.shape[0]
    for k in range(TOP_K):
        def issue(t, carry, k=k):
            _row_copy(h_ref, t, xpad_ref, dest_ref[0, k * tf + t], sem).start()
            return carry

        lax.fori_loop(0, tf, issue, 0, unroll=8)
    for k in range(TOP_K):
        pltpu.make_async_copy(h_ref, xpad_ref.at[pl.ds(0, tf), :], sem).wait()


def _dispatch(dest_tiles, h2d, zeros, tf):
    n, d = h2d.shape
    return pl.pallas_call(
        _dispatch_kernel,
        grid=(n // tf,),
        in_specs=[
            pl.BlockSpec((None, 1, TOP_K * tf), lambda i: (i, 0, 0), memory_space=pltpu.SMEM),
            pl.BlockSpec((tf, d), lambda i: (i, 0)),
            pl.BlockSpec(memory_space=pl.ANY),
        ],
        out_specs=pl.BlockSpec(memory_space=pl.ANY),
        out_shape=jax.ShapeDtypeStruct(zeros.shape, F32),
        scratch_shapes=[pltpu.SemaphoreType.DMA(())],
        input_output_aliases={2: 0},
        compiler_params=_params("arbitrary"),
        name="dispatch",
    )(dest_tiles, h2d, zeros)


def _experts_kernel(be_ref, nb_ref, x_ref, wg_ref, bg_ref, wu_ref, bu_ref, wd_ref, bd_ref, y_ref):
    del be_ref

    @pl.when(pl.program_id(0) < nb_ref[0])
    def _():
        x = x_ref[...].astype(BF16)
        gate = jnp.minimum(_dot(x, wg_ref[...]) + bg_ref[...], SWIGLU_LIMIT)
        up = jnp.clip(_dot(x, wu_ref[...]) + bu_ref[...], -SWIGLU_LIMIT, SWIGLU_LIMIT)
        act = (up + 1.0) * gate * _sigmoid(gate * SWIGLU_ALPHA)
        y_ref[...] = _dot(act.astype(BF16), wd_ref[...]) + bd_ref[...]


def _experts(block_e, nb_used, x_pad, wg, bg, wu, bu, wd, bd):
    rows, d = x_pad.shape
    nblk = rows // EXPERT_BLOCK
    dff = wg.shape[2]
    row_blk = lambda j, be, nb: (jnp.minimum(j, nb[0] - 1), 0)
    w_blk = lambda j, be, nb: (be[j], 0, 0)
    return pl.pallas_call(
        _experts_kernel,
        grid_spec=pltpu.PrefetchScalarGridSpec(
            num_scalar_prefetch=2,
            grid=(nblk,),
            in_specs=[
                pl.BlockSpec((EXPERT_BLOCK, d), row_blk),
                pl.BlockSpec((None, d, dff), w_blk),
                pl.BlockSpec((None, 1, dff), w_blk),
                pl.BlockSpec((None, d, dff), w_blk),
                pl.BlockSpec((None, 1, dff), w_blk),
                pl.BlockSpec((None, dff, d), w_blk),
                pl.BlockSpec((None, 1, d), w_blk),
            ],
            out_specs=pl.BlockSpec((EXPERT_BLOCK, d), row_blk),
        ),
        out_shape=jax.ShapeDtypeStruct((rows, d), F32),
        compiler_params=_params("arbitrary"),
        name="experts",
    )(block_e, nb_used, x_pad, wg, bg, wu, bu, wd, bd)


def _combine_kernel(dest_ref, x2_ref, gate_ref, ypad_ref, o_ref, buf, sem):
    th = x2_ref.shape[0]
    for k in range(TOP_K):
        def issue(t, carry, k=k):
            _row_copy(ypad_ref, dest_ref[0, k * th + t], buf.at[k], t, sem).start()
            return carry

        lax.fori_loop(0, th, issue, 0, unroll=8)
    for k in range(TOP_K):
        pltpu.make_async_copy(ypad_ref.at[pl.ds(0, th), :], buf.at[k], sem).wait()
    g = gate_ref[...]
    out = x2_ref[...]
    for k in range(TOP_K):
        out = out + buf[k] * g[:, k:k + 1]
    o_ref[...] = out


def _combine(dest_tiles, x2d, gates_nk, y_pad, th):
    n, d = x2d.shape
    return pl.pallas_call(
        _combine_kernel,
        grid=(n // th,),
        in_specs=[
            pl.BlockSpec((None, 1, TOP_K * th), lambda i: (i, 0, 0), memory_space=pltpu.SMEM),
            pl.BlockSpec((th, d), lambda i: (i, 0)),
            pl.BlockSpec((th, TOP_K), lambda i: (i, 0)),
            pl.BlockSpec(memory_space=pl.ANY),
        ],
        out_specs=pl.BlockSpec((th, d), lambda i: (i, 0)),
        out_shape=jax.ShapeDtypeStruct((n, d), F32),
        scratch_shapes=[pltpu.VMEM((TOP_K, th, d), F32), pltpu.SemaphoreType.DMA(())],
        compiler_params=_params("arbitrary"),
        name="combine",
    )(dest_tiles, x2d, gates_nk, y_pad)


def _tile_dest(dest, t):
    k, n = dest.shape
    return dest.reshape(k, n // t, t).transpose(1, 0, 2).reshape(n // t, 1, k * t)


def _layer(x, mem, norm1_w, w_in, conv_w, a_log, dt_bias, dn_norm_w, sb_q_norm_w, sb_k_norm_w, w_out,
           norm2_w, mem_norm_w, xq_w, xk_w, xv_w, xq_norm_w, xk_norm_w, xo_w, norm3_w, router_w,
           router_b, w_gate, b_gate, w_up, b_up, w_down, b_down):
    b, s, d = x.shape
    n = b * s
    tm_proj = min(512, s)
    tm_mid = min(256, s)
    t_moe = min(256, s)

    o_dn, o_ab, o_sb = 4 * DN_WIDTH, 4 * DN_WIDTH + 2 * DN_HEADS, 4 * DN_WIDTH + 2 * DN_HEADS
    w_dn = w_in[:, :o_dn].astype(BF16)
    w_ab_f = w_in[:, o_dn:o_ab]
    w_ab = jnp.pad(w_ab_f, ((0, 0), (0, LANES - 2 * DN_HEADS))).astype(BF16)
    w_abt = w_ab_f.T.astype(BF16)
    w_sb = w_in[:, o_sb:].astype(BF16)
    row = lambda v: v.reshape(1, -1).astype(F32)

    dn, sb, ab, abt = _in_proj(x.reshape(n, d), row(norm1_w), w_dn, w_sb, w_ab, w_abt, tm_proj)

    abt4 = abt.reshape(8, n // CHUNK, CHUNK).transpose(1, 0, 2).reshape(b, s // CHUNK, 8, CHUNK)
    pad_lane = lambda v: jnp.pad(v.astype(F32), (0, LANES - v.shape[0])).reshape(1, LANES)
    pad_col = lambda v: jnp.pad(v.astype(F32), (0, 8 - v.shape[0])).reshape(8, 1)
    y_dn = _deltanet(dn.reshape(b, s, -1), ab.reshape(b, s, LANES), abt4, conv_w.astype(F32),
                     pad_lane(a_log), pad_lane(dt_bias), pad_col(a_log), pad_col(dt_bias), row(dn_norm_w))

    ii = jnp.arange(SB_BLOCK)
    m2 = jnp.concatenate([(ii[:, None] > ii[None, :]).astype(BF16),
                          jnp.ones((SB_BLOCK, SB_BLOCK), BF16)], axis=1)
    y_sb = _stickbreak(sb.reshape(b, s, -1), row(jnp.tile(sb_q_norm_w, 2)), row(jnp.tile(sb_k_norm_w, 2)), m2)

    k_mem, v_mem = _memkv(mem, row(mem_norm_w), xk_w.astype(BF16), xv_w.astype(BF16), row(xk_norm_w))

    wr_t = router_w.T.astype(F32)
    wr_hi = wr_t.astype(BF16)
    wr_lo = (wr_t - wr_hi.astype(F32)).astype(BF16)
    jj = jnp.arange(tm_mid)
    upper = (jj[:, None] < jj[None, :]).astype(BF16)
    x2, h3, idx, gates, rank, cnt = _mid(
        x, y_dn, y_sb, w_out.astype(BF16), row(norm2_w), xq_w.astype(BF16), row(xq_norm_w), k_mem, v_mem,
        xo_w.astype(BF16), row(norm3_w), wr_hi, wr_lo, router_b.reshape(N_EXPERTS, 1).astype(F32), upper,
        tm_mid)

    counts = cnt[:, 0].astype(jnp.int32)
    padded = (counts + EXPERT_BLOCK - 1) // EXPERT_BLOCK * EXPERT_BLOCK
    pad_ends = jnp.cumsum(padded)
    pad_starts = pad_ends - padded
    sel = idx[:, :, None] == jnp.arange(N_EXPERTS, dtype=jnp.int32)[None, None, :]
    dest = rank + jnp.sum(jnp.where(sel, pad_starts[None, None, :], 0), axis=-1)
    n_blocks = -(-n * TOP_K // EXPERT_BLOCK) + N_EXPERTS
    nb_used = (pad_ends[-1] // EXPERT_BLOCK).astype(jnp.int32)
    blk = jnp.minimum(jnp.arange(n_blocks, dtype=jnp.int32), nb_used - 1) * EXPERT_BLOCK
    block_e = jnp.minimum(jnp.sum(pad_ends[None, :] <= blk[:, None], axis=1), N_EXPERTS - 1).astype(jnp.int32)
    dest_tiles = _tile_dest(dest, t_moe)

    x_pad = _dispatch(dest_tiles, h3.reshape(n, d), jnp.zeros((n_blocks * EXPERT_BLOCK, d), F32), t_moe)
    bias = lambda v: v.reshape(N_EXPERTS, 1, -1).astype(F32)
    y_pad = _experts(block_e, nb_used.reshape(1), x_pad, w_gate.astype(BF16), bias(b_gate),
                     w_up.astype(BF16), bias(b_up), w_down.astype(BF16), bias(b_down))
    out = _combine(dest_tiles, x2.reshape(n, d), gates.T, y_pad, t_moe)
    return out.reshape(b, s, d)


def kernel(x, mem, norm1_w, w_in, conv_w, a_log, dt_bias, dn_norm_w, sb_q_norm_w, sb_k_norm_w, w_out,
           norm2_w, mem_norm_w, xq_w, xk_w, xv_w, xq_norm_w, xk_norm_w, xo_w, norm3_w, router_w,
           router_b, w_gate, b_gate, w_up, b_up, w_down, b_down):
    depth = w_in.shape[0]
    for l in range(depth):
        x = _layer(x, mem, norm1_w[l], w_in[l], conv_w[l], a_log[l], dt_bias[l], dn_norm_w[l],
                   sb_q_norm_w[l], sb_k_norm_w[l], w_out[l], norm2_w[l], mem_norm_w[l], xq_w[l], xk_w[l],
                   xv_w[l], xq_norm_w[l], xk_norm_w[l], xo_w[l], norm3_w[l], router_w[l], router_b[l],
                   w_gate[l], b_gate[l], w_up[l], b_up[l], w_down[l], b_down[l])
    return x
```

```python
import functools

import jax
import jax.numpy as jnp
from jax import lax
from jax.experimental import pallas as pl
from jax.experimental.pallas import tpu as pltpu

F32 = jnp.float32
BF16 = jnp.bfloat16

EPS = 1e-6
CHUNK = 64
DN_HEADS = 4
DN_DIM = 128
DN_WIDTH = DN_HEADS * DN_DIM
CONV_WIDTH = 4
DN_PAIR = 2
HALO = 16
SB_HEADS = 8
SB_DIM = 64
SB_WIDTH = SB_HEADS * SB_DIM
SB_BLOCK = 128
SB_GROUP = 4
X_HEADS = 4
X_DIM = 256
N_EXPERTS = 32
TOP_K = 4
EXPERT_BLOCK = 512
SWIGLU_ALPHA = 1.702
SWIGLU_LIMIT = 7.0
LANES = 128
SB_UNDERFLOW = -88.0
V7X_VMEM_LIMIT = 48 * 1024 * 1024
V7X_EXPERTS_VMEM_LIMIT = 58 * 1024 * 1024


def _dot(a, b):
    return jnp.dot(a, b, preferred_element_type=F32)


def _dot_nt(a, b):
    return lax.dot_general(a, b, (((1,), (1,)), ((), ())), preferred_element_type=F32)


def _dot_tn(a, b):
    return lax.dot_general(a, b, (((0,), (0,)), ((), ())), preferred_element_type=F32)


def _dot_f32(a, b):
    return jnp.dot(a, b, preferred_element_type=F32, precision=lax.Precision.HIGHEST)


def _softplus(x):
    return jnp.maximum(x, 0.0) + jnp.log(1.0 + jnp.exp(-jnp.abs(x)))


def _sigmoid(x):
    return 1.0 / (1.0 + jnp.exp(-x))


def _rms(x, w):
    return x * lax.rsqrt(jnp.mean(x * x, axis=-1, keepdims=True) + EPS) * w


def _params(*sem):
    return pltpu.CompilerParams(dimension_semantics=sem, vmem_limit_bytes=V7X_VMEM_LIMIT)


def _const_spec(shape):
    nd = len(shape)
    return pl.BlockSpec(shape, lambda *_: (0,) * nd)


def _in_proj_kernel(x_ref, nw_ref, wdn_ref, wsb_ref, wab_ref, wabt_ref,
                    dn_ref, sb_ref, ab_ref, abt_ref):
    n = _rms(x_ref[...], nw_ref[...]).astype(BF16)
    dn_ref[...] = _dot(n, wdn_ref[...]).astype(BF16)
    sb_ref[...] = _dot(n, wsb_ref[...]).astype(BF16)
    ab_ref[...] = _dot(n, wab_ref[...])
    abt_ref[...] = _dot_nt(wabt_ref[...], n)


def _in_proj(x2d, norm_w, w_dn, w_sb, w_ab, w_abt, tm):
    n, d = x2d.shape
    return pl.pallas_call(
        _in_proj_kernel,
        grid=(n // tm,),
        in_specs=[
            pl.BlockSpec((tm, d), lambda i: (i, 0)),
            _const_spec((1, d)),
            _const_spec(w_dn.shape),
            _const_spec(w_sb.shape),
            _const_spec(w_ab.shape),
            _const_spec(w_abt.shape),
        ],
        out_specs=[
            pl.BlockSpec((tm, w_dn.shape[1]), lambda i: (i, 0)),
            pl.BlockSpec((tm, w_sb.shape[1]), lambda i: (i, 0)),
            pl.BlockSpec((tm, LANES), lambda i: (i, 0)),
            pl.BlockSpec((8, tm), lambda i: (0, i)),
        ],
        out_shape=[
            jax.ShapeDtypeStruct((n, w_dn.shape[1]), BF16),
            jax.ShapeDtypeStruct((n, w_sb.shape[1]), BF16),
            jax.ShapeDtypeStruct((n, LANES), F32),
            jax.ShapeDtypeStruct((8, n), F32),
        ],
        compiler_params=_params("arbitrary"),
        name="in_proj",
    )(x2d, norm_w, w_dn, w_sb, w_ab, w_abt)


def _deltanet_kernel(dn_ref, ab_ref, abt_ref, convw_ref, alog_ref, dtb_ref, alogt_ref, dtbt_ref,
                     normw_ref, o_ref, state_s, u_s, w_s, kt_s, qg_s, attn_s, egl_s):
    s = dn_ref.shape[0]
    state_s[...] = jnp.zeros(state_s.shape, F32)

    def conv_silu(r0, c, part, h, l2):
        cs = slice(part * DN_WIDTH + h * LANES, part * DN_WIDTH + (h + 1) * LANES)
        prev0 = pl.multiple_of(jnp.maximum(r0 - HALO, 0), HALO)
        prev = dn_ref[pl.ds(prev0, HALO), cs].astype(F32)
        prev = jnp.where(c > 0, prev, 0.0)
        win = jnp.concatenate([prev, dn_ref[pl.ds(r0, CHUNK), cs].astype(F32)], axis=0)
        w = convw_ref[:, cs]
        y = w[0:1, :] * win[HALO - 3:HALO - 3 + CHUNK, :]
        for i in range(1, CONV_WIDTH):
            y = y + w[i:i + 1, :] * win[HALO - 3 + i:HALO - 3 + i + CHUNK, :]
        y = y * _sigmoid(y)
        if l2:
            y = y * lax.rsqrt(jnp.sum(y * y, axis=-1, keepdims=True) + EPS)
        return y

    row = lax.broadcasted_iota(jnp.int32, (CHUNK, CHUNK), 0)
    col = lax.broadcasted_iota(jnp.int32, (CHUNK, CHUNK), 1)
    tri = row >= col
    strict = row > col
    tril_f = tri.astype(F32)
    triu_f = (row <= col).astype(F32)
    neg_a_col = -jnp.exp(alog_ref[...])
    neg_a_row = -jnp.exp(alogt_ref[...])
    scale = DN_DIM ** -0.5

    def precompute(it, carry):
        items = []
        for ci in range(DN_PAIR):
            c = it * DN_PAIR + ci
            r0 = pl.multiple_of(c * CHUNK, CHUNK)
            ab = ab_ref[pl.ds(r0, CHUNK), :]
            g_col = neg_a_col * _softplus(ab + dtb_ref[...])
            gc_col_all = _dot_f32(tril_f, g_col)
            beta_all = _sigmoid(ab)
            abt = abt_ref[c]
            g_row = neg_a_row * _softplus(abt + dtbt_ref[...])
            gc_row_all = _dot_f32(g_row, triu_f)
            for h in range(DN_HEADS):
                q = conv_silu(r0, c, 0, h, True) * scale
                k = conv_silu(r0, c, 1, h, True)
                v = conv_silu(r0, c, 2, h, False)
                gcol = gc_col_all[:, h:h + 1]
                grow = gc_row_all[h:h + 1, :]
                beta = beta_all[:, DN_HEADS + h:DN_HEADS + h + 1]
                glast = gcol[CHUNK - 1:CHUNK, :]
                decay = jnp.where(tri, jnp.exp(jnp.where(tri, gcol - grow, 0.0)), 0.0)
                kb = k * beta
                cs = slice(h * LANES, (h + 1) * LANES)
                kt_s[pl.ds(r0, CHUNK), cs] = (k * jnp.exp(glast - gcol)).astype(BF16)
                qg_s[pl.ds(r0, CHUNK), cs] = (q * jnp.exp(gcol)).astype(BF16)
                egl_s[c, h:h + 1, :] = jnp.broadcast_to(jnp.exp(glast), (1, LANES))
                items.append(dict(c=c, r0=r0, h=h, cs=cs, decay=decay, kb16=kb.astype(BF16),
                                  k16=k.astype(BF16), q16=q.astype(BF16),
                                  sol=jnp.concatenate([v * beta, kb * jnp.exp(gcol)], axis=1)))
        kk = [_dot_nt(t["kb16"], t["k16"]) for t in items]
        qk = [_dot_nt(t["q16"], t["k16"]) for t in items]
        for t, kk_i, qk_i in zip(items, kk, qk):
            attn_s[t["c"], t["h"]] = jnp.where(tri, qk_i * t["decay"], 0.0).astype(BF16)
            t["p16"] = (-jnp.where(strict, kk_i * t["decay"], 0.0)).astype(BF16)
        for level in range(6):
            app = [_dot(t["p16"], t["sol"].astype(BF16)) for t in items]
            if level < 5:
                sq = [_dot(t["p16"], t["p16"]) for t in items]
            for i, t in enumerate(items):
                t["sol"] = t["sol"] + app[i]
                if level < 5:
                    t["p16"] = sq[i].astype(BF16)
        for t in items:
            u_s[pl.ds(t["r0"], CHUNK), t["cs"]] = t["sol"][:, :DN_DIM]
            w_s[pl.ds(t["r0"], CHUNK), t["cs"]] = t["sol"][:, DN_DIM:].astype(BF16)
        return carry

    lax.fori_loop(0, s // (CHUNK * DN_PAIR), precompute, 0)

    def recurrence(c, carry):
        r0 = pl.multiple_of(c * CHUNK, CHUNK)
        heads = range(DN_HEADS)
        cols = [slice(h * LANES, (h + 1) * LANES) for h in heads]
        st = [state_s[h] for h in heads]
        st16 = [x.astype(BF16) for x in st]
        ws = [_dot(w_s[pl.ds(r0, CHUNK), cols[h]], st16[h]) for h in heads]
        qs = [_dot(qg_s[pl.ds(r0, CHUNK), cols[h]], st16[h]) for h in heads]
        vn16 = [(u_s[pl.ds(r0, CHUNK), cols[h]] - ws[h]).astype(BF16) for h in heads]
        av = [_dot(attn_s[c, h], vn16[h]) for h in heads]
        ks = [_dot_tn(kt_s[pl.ds(r0, CHUNK), cols[h]], vn16[h]) for h in heads]
        for h in heads:
            state_s[h] = st[h] * egl_s[c, h:h + 1, :] + ks[h]
            o = qs[h] + av[h]
            o = o * lax.rsqrt(jnp.mean(o * o, axis=-1, keepdims=True) + EPS)
            z = dn_ref[pl.ds(r0, CHUNK), 3 * DN_WIDTH + h * LANES:3 * DN_WIDTH + (h + 1) * LANES].astype(F32)
            o_ref[pl.ds(r0, CHUNK), cols[h]] = (o * normw_ref[...] * (z * _sigmoid(z))).astype(BF16)
        return carry

    lax.fori_loop(0, s // CHUNK, recurrence, 0)


def _deltanet(dn3, ab3, abt4, conv_w, alog, dtb, alogt, dtbt, norm_w):
    b, s, _ = dn3.shape
    nch = s // CHUNK
    return pl.pallas_call(
        _deltanet_kernel,
        grid=(b,),
        in_specs=[
            pl.BlockSpec((None, s, 4 * DN_WIDTH), lambda i: (i, 0, 0)),
            pl.BlockSpec((None, s, LANES), lambda i: (i, 0, 0)),
            pl.BlockSpec((None, nch, 8, CHUNK), lambda i: (i, 0, 0, 0)),
            _const_spec(conv_w.shape),
            _const_spec((1, LANES)),
            _const_spec((1, LANES)),
            _const_spec((8, 1)),
            _const_spec((8, 1)),
            _const_spec((1, DN_DIM)),
        ],
        out_specs=pl.BlockSpec((None, s, DN_WIDTH), lambda i: (i, 0, 0)),
        out_shape=jax.ShapeDtypeStruct((b, s, DN_WIDTH), BF16),
        scratch_shapes=[
            pltpu.VMEM((DN_HEADS, DN_DIM, DN_DIM), F32),
            pltpu.VMEM((s, DN_WIDTH), F32),
            pltpu.VMEM((s, DN_WIDTH), BF16),
            pltpu.VMEM((s, DN_WIDTH), BF16),
            pltpu.VMEM((s, DN_WIDTH), BF16),
            pltpu.VMEM((nch, DN_HEADS, CHUNK, CHUNK), BF16),
            pltpu.VMEM((nch, 8, LANES), F32),
        ],
        compiler_params=_params("arbitrary"),
        name="deltanet",
    )(dn3, ab3, abt4, conv_w, alog, dtb, alogt, dtbt, norm_w)


def _stickbreak_kernel(q_ref, k_ref, v_ref, qw_ref, kw_ref, m2_ref, o_ref,
                       qn_s, kn_s, carry_s, acc_s):
    s = q_ref.shape[0]
    group = carry_s.shape[0]
    lane = lax.broadcasted_iota(jnp.int32, (1, LANES), 1)
    first = lane < SB_DIM

    def head_norm(r0, x_ref, w):
        x = x_ref[pl.ds(r0, SB_BLOCK), :].astype(F32)
        sq = x * x
        sa = jnp.sum(jnp.where(first, sq, 0.0), axis=-1, keepdims=True)
        sb = jnp.sum(jnp.where(first, 0.0, sq), axis=-1, keepdims=True)
        ra = lax.rsqrt(sa * (1.0 / SB_DIM) + EPS)
        rb = lax.rsqrt(sb * (1.0 / SB_DIM) + EPS)
        return x * jnp.where(first, ra, rb) * w

    def norm_block(i, carry):
        r0 = pl.multiple_of(i * SB_BLOCK, SB_BLOCK)
        qn = head_norm(r0, q_ref, qw_ref[...]) * (SB_DIM ** -0.5)
        qn_s[0, pl.ds(r0, SB_BLOCK), :] = jnp.where(first, qn, 0.0).astype(BF16)
        qn_s[1, pl.ds(r0, SB_BLOCK), :] = jnp.where(first, 0.0, qn).astype(BF16)
        kn_s[pl.ds(r0, SB_BLOCK), :] = head_norm(r0, k_ref, kw_ref[...]).astype(BF16)
        return carry

    lax.fori_loop(0, s // SB_BLOCK, norm_block, 0)

    row = lax.broadcasted_iota(jnp.int32, (SB_BLOCK, SB_BLOCK), 0)
    col = lax.broadcasted_iota(jnp.int32, (SB_BLOCK, SB_BLOCK), 1)
    causal = col < row

    def step(rows, keys, valid):
        tiles = [(g, h) for g in range(group) for h in range(2)]
        k16 = [kn_s[pl.ds(c0, SB_BLOCK), :] for c0 in keys]
        v16 = [v_ref[pl.ds(c0, SB_BLOCK), :] for c0 in keys]
        z = [_dot_nt(qn_s[h, pl.ds(rows[g], SB_BLOCK), :], k16[g]) for g, h in tiles]
        sp = [_softplus(x) for x in z]
        if valid is None:
            log_fail = [jnp.where(causal, -x, 0.0) for x in sp]
        else:
            log_fail = [-x for x in sp]
        cs = [_dot(x.astype(BF16), m2_ref[...]) for x in log_fail]
        p = []
        for i, (g, h) in enumerate(tiles):
            w = jnp.exp(z[i] - sp[i] + carry_s[g, h] + cs[i][:, :SB_BLOCK])
            p.append(jnp.where(causal, w, 0.0) if valid is None else w)
        pv = [_dot(p[i].astype(BF16), v16[g]) for i, (g, h) in enumerate(tiles)]
        for i, (g, h) in enumerate(tiles):
            acc = acc_s[g, h] + pv[i]
            carry = carry_s[g, h] + cs[i][:, SB_BLOCK:]
            if valid is not None:
                acc = jnp.where(valid[g], acc, acc_s[g, h])
                carry = jnp.where(valid[g], carry, carry_s[g, h])
            acc_s[g, h] = acc
            carry_s[g, h] = carry

    def q_group(qg, carry):
        blocks = [qg * group + g for g in range(group)]
        rows = [pl.multiple_of(qb * SB_BLOCK, SB_BLOCK) for qb in blocks]
        carry_s[...] = jnp.zeros(carry_s.shape, F32)
        acc_s[...] = jnp.zeros(acc_s.shape, F32)
        step(rows, rows, None)

        def cond(st):
            d, alive = st
            return jnp.logical_and(d <= blocks[-1], alive)

        def body(st):
            d, _ = st
            alive = jnp.max(carry_s[...]) > SB_UNDERFLOW
            keys = [pl.multiple_of(jnp.maximum(qb - d, 0) * SB_BLOCK, SB_BLOCK) for qb in blocks]
            step(rows, keys, [qb >= d for qb in blocks])
            return d + 1, alive

        lax.while_loop(cond, body, (jnp.int32(1), jnp.bool_(True)))
        for g in range(group):
            o_ref[pl.ds(rows[g], SB_BLOCK), :] = jnp.where(first, acc_s[g, 0], acc_s[g, 1]).astype(BF16)
        return carry

    lax.fori_loop(0, s // (SB_BLOCK * group), q_group, 0)


def _stickbreak(sb3, qw, kw, m2):
    b, s, _ = sb3.shape
    pairs = SB_HEADS // 2
    group = min(SB_GROUP, s // SB_BLOCK)
    return pl.pallas_call(
        _stickbreak_kernel,
        grid=(b, pairs),
        in_specs=[
            pl.BlockSpec((None, s, LANES), lambda i, j: (i, 0, j)),
            pl.BlockSpec((None, s, LANES), lambda i, j: (i, 0, pairs + j)),
            pl.BlockSpec((None, s, LANES), lambda i, j: (i, 0, 2 * pairs + j)),
            _const_spec((1, LANES)),
            _const_spec((1, LANES)),
            _const_spec(m2.shape),
        ],
        out_specs=pl.BlockSpec((None, s, LANES), lambda i, j: (i, 0, j)),
        out_shape=jax.ShapeDtypeStruct((b, s, SB_WIDTH), BF16),
        scratch_shapes=[
            pltpu.VMEM((2, s, LANES), BF16),
            pltpu.VMEM((s, LANES), BF16),
            pltpu.VMEM((group, 2, SB_BLOCK, SB_BLOCK), F32),
            pltpu.VMEM((group, 2, SB_BLOCK, LANES), F32),
        ],
        compiler_params=_params("arbitrary", "arbitrary"),
        name="stickbreak",
    )(sb3, sb3, sb3, qw, kw, m2)


def _memkv_kernel(mem_ref, nw_ref, wk_ref, wv_ref, knw_ref, k_ref, v_ref):
    n = _rms(mem_ref[...], nw_ref[...]).astype(BF16)
    k = _dot(n, wk_ref[...])
    for h in range(X_HEADS):
        cs = slice(h * X_DIM, (h + 1) * X_DIM)
        k_ref[:, cs] = _rms(k[:, cs], knw_ref[...]).astype(BF16)
    v_ref[...] = _dot(n, wv_ref[...]).astype(BF16)


def _memkv(mem, norm_w, wk, wv, k_norm_w):
    b, m, d = mem.shape
    return pl.pallas_call(
        _memkv_kernel,
        grid=(b,),
        in_specs=[
            pl.BlockSpec((None, m, d), lambda i: (i, 0, 0)),
            _const_spec((1, d)),
            _const_spec(wk.shape),
            _const_spec(wv.shape),
            _const_spec((1, X_DIM)),
        ],
        out_specs=[
            pl.BlockSpec((None, m, d), lambda i: (i, 0, 0)),
            pl.BlockSpec((None, m, d), lambda i: (i, 0, 0)),
        ],
        out_shape=[jax.ShapeDtypeStruct((b, m, d), BF16)] * 2,
        compiler_params=_params("arbitrary"),
        name="memkv",
    )(mem, norm_w, wk, wv, k_norm_w)


def _mid_kernel(x_ref, ydn_ref, ysb_ref, wout_ref, n2w_ref, wq_ref, qnw_ref, km_ref, vm_ref, wo_ref,
                n3w_ref, wrh_ref, wrl_ref, rb_ref, upper_ref,
                x2_ref, h3_ref, idx_ref, gate_ref, rank_ref, cnt_ref, count_s):
    tm = x_ref.shape[0]
    first_step = jnp.logical_and(pl.program_id(0) == 0, pl.program_id(1) == 0)

    @pl.when(first_step)
    def _():
        count_s[...] = jnp.zeros(count_s.shape, F32)

    sub = upper_ref.shape[0]
    subs = range(tm // sub)
    rows = [slice(i * sub, (i + 1) * sub) for i in subs]
    heads = range(X_HEADS)
    cols = [slice(h * X_DIM, (h + 1) * X_DIM) for h in heads]
    x1 = [x_ref[r, :] + _dot(ydn_ref[r, :], wout_ref[0:DN_WIDTH, :])
          + _dot(ysb_ref[r, :], wout_ref[DN_WIDTH:, :]) for r in rows]
    n2 = [_rms(v, n2w_ref[...]).astype(BF16) for v in x1]
    q = [_dot(v, wq_ref[...]) for v in n2]
    qh = [[(_rms(q[i][:, c], qnw_ref[...]) * (X_DIM ** -0.5)).astype(BF16) for c in cols] for i in subs]
    sc = [[_dot_nt(qh[i][h], km_ref[:, cols[h]]) for h in heads] for i in subs]
    ex = [[jnp.exp(sc[i][h] - jnp.max(sc[i][h], axis=-1, keepdims=True)) for h in heads] for i in subs]
    pr = [[(ex[i][h] / jnp.sum(ex[i][h], axis=-1, keepdims=True)).astype(BF16) for h in heads] for i in subs]
    oh = [[_dot(pr[i][h], vm_ref[:, cols[h]]).astype(BF16) for h in heads] for i in subs]
    x2 = [x1[i] + _dot(jnp.concatenate(oh[i], axis=1), wo_ref[...]) for i in subs]
    h3 = [_rms(v, n3w_ref[...]) for v in x2]
    hi = [v.astype(BF16) for v in h3]
    lo = [(h3[i] - hi[i].astype(F32)).astype(BF16) for i in subs]
    logits = [_dot_nt(wrh_ref[...], hi[i]) + _dot_nt(wrh_ref[...], lo[i]) + _dot_nt(wrl_ref[...], hi[i])
              + rb_ref[...] for i in subs]
    eid = lax.broadcasted_iota(jnp.int32, (N_EXPERTS, sub), 0).astype(F32)
    count = count_s[...]
    for i in subs:
        x2_ref[rows[i], :] = x2[i]
        h3_ref[rows[i], :] = h3[i]
        vals, ids = [], []
        cur = logits[i]
        for _ in range(TOP_K):
            m = jnp.max(cur, axis=0, keepdims=True)
            j = jnp.min(jnp.where(cur == m, eid, float(N_EXPERTS)), axis=0, keepdims=True)
            vals.append(m)
            ids.append(j)
            cur = jnp.where(eid == j, -jnp.inf, cur)
        exps = [jnp.exp(v - vals[0]) for v in vals]
        denom = exps[0] + exps[1] + exps[2] + exps[3]
        onehot = jnp.zeros((N_EXPERTS, sub), F32)
        for j in ids:
            onehot = onehot + jnp.where(eid == j, 1.0, 0.0)
        before = count + _dot(onehot.astype(BF16), upper_ref[...])
        for k in range(TOP_K):
            idx_ref[k:k + 1, rows[i]] = ids[k].astype(jnp.int32)
            gate_ref[k:k + 1, rows[i]] = exps[k] / denom
            rank_ref[k:k + 1, rows[i]] = jnp.sum(jnp.where(eid == ids[k], before, 0.0), axis=0,
                                                 keepdims=True).astype(jnp.int32)
        count = count + jnp.sum(onehot, axis=1, keepdims=True)
    count_s[...] = count
    cnt_ref[...] = jnp.broadcast_to(count, cnt_ref.shape)


def _mid(x3, ydn3, ysb3, w_out, n2w, wq, qnw, k_mem, v_mem, wo, n3w, wr_hi, wr_lo, rb, upper, tm):
    b, s, d = x3.shape
    n = b * s
    nt = s // tm
    m = k_mem.shape[1]
    tok = lambda i, j: (0, i * nt + j)
    return pl.pallas_call(
        _mid_kernel,
        grid=(b, nt),
        in_specs=[
            pl.BlockSpec((None, tm, d), lambda i, j: (i, j, 0)),
            pl.BlockSpec((None, tm, DN_WIDTH), lambda i, j: (i, j, 0)),
            pl.BlockSpec((None, tm, SB_WIDTH), lambda i, j: (i, j, 0)),
            _const_spec(w_out.shape),
            _const_spec((1, d)),
            _const_spec(wq.shape),
            _const_spec((1, X_DIM)),
            pl.BlockSpec((None, m, d), lambda i, j: (i, 0, 0)),
            pl.BlockSpec((None, m, d), lambda i, j: (i, 0, 0)),
            _const_spec(wo.shape),
            _const_spec((1, d)),
            _const_spec(wr_hi.shape),
            _const_spec(wr_lo.shape),
            _const_spec((N_EXPERTS, 1)),
            _const_spec(upper.shape),
        ],
        out_specs=[
            pl.BlockSpec((None, tm, d), lambda i, j: (i, j, 0)),
            pl.BlockSpec((None, tm, d), lambda i, j: (i, j, 0)),
            pl.BlockSpec((TOP_K, tm), tok),
            pl.BlockSpec((TOP_K, tm), tok),
            pl.BlockSpec((TOP_K, tm), tok),
            _const_spec((N_EXPERTS, LANES)),
        ],
        out_shape=[
            jax.ShapeDtypeStruct((b, s, d), F32),
            jax.ShapeDtypeStruct((b, s, d), F32),
            jax.ShapeDtypeStruct((TOP_K, n), jnp.int32),
            jax.ShapeDtypeStruct((TOP_K, n), F32),
            jax.ShapeDtypeStruct((TOP_K, n), jnp.int32),
            jax.ShapeDtypeStruct((N_EXPERTS, LANES), F32),
        ],
        scratch_shapes=[pltpu.VMEM((N_EXPERTS, 1), F32)],
        compiler_params=_params("arbitrary", "arbitrary"),
        name="mid",
    )(x3, ydn3, ysb3, w_out, n2w, wq, qnw, k_mem, v_mem, wo, n3w, wr_hi, wr_lo, rb, upper)


def _row_copy(src_ref, src_row, dst_ref, dst_row, sem):
    return pltpu.make_async_copy(src_ref.at[pl.ds(src_row, 1), :], dst_ref.at[pl.ds(dst_row, 1), :], sem)


def _dispatch_kernel(pend_ref, padded_ref, dest_ref, h_ref, xpad_ref, zero_s, sem, zsem):
    tf = h_ref.shape[0]

    @pl.when(pl.program_id(0) == 0)
    def _():
        zero_s[...] = jnp.zeros(zero_s.shape, F32)

        def last_block(e):
            start = pl.multiple_of(pend_ref[e] - EXPERT_BLOCK, EXPERT_BLOCK)
            return pltpu.make_async_copy(zero_s, xpad_ref.at[pl.ds(start, EXPERT_BLOCK), :], zsem)

        for e in range(N_EXPERTS):
            @pl.when(padded_ref[e] > 0)
            def _(e=e):
                last_block(e).start()
        for e in range(N_EXPERTS):
            @pl.when(padded_ref[e] > 0)
            def _(e=e):
                last_block(e).wait()

    for k in range(TOP_K):
        def issue(t, carry, k=k):
            _row_copy(h_ref, t, xpad_ref, dest_ref[0, k * tf + t], sem).start()
            return carry

        lax.fori_loop(0, tf, issue, 0, unroll=8)
    for k in range(TOP_K):
        pltpu.make_async_copy(h_ref, xpad_ref.at[pl.ds(0, tf), :], sem).wait()


def _dispatch(pad_ends, padded, dest_tiles, h2d, rows, tf):
    n, d = h2d.shape
    return pl.pallas_call(
        _dispatch_kernel,
        grid_spec=pltpu.PrefetchScalarGridSpec(
            num_scalar_prefetch=2,
            grid=(n // tf,),
            in_specs=[
                pl.BlockSpec((None, 1, TOP_K * tf), lambda i, *_: (i, 0, 0), memory_space=pltpu.SMEM),
                pl.BlockSpec((tf, d), lambda i, *_: (i, 0)),
            ],
            out_specs=pl.BlockSpec(memory_space=pl.ANY),
            scratch_shapes=[pltpu.VMEM((EXPERT_BLOCK, d), F32), pltpu.SemaphoreType.DMA(()),
                            pltpu.SemaphoreType.DMA(())],
        ),
        out_shape=jax.ShapeDtypeStruct((rows, d), F32),
        compiler_params=_params("arbitrary"),
        name="dispatch",
    )(pad_ends, padded, dest_tiles, h2d)


def _experts_kernel(be_ref, nb_ref, x_ref, wg_ref, bg_ref, wu_ref, bu_ref, wd_ref, bd_ref, y_ref,
                    wg_s, wu_s, wd_s):
    j = pl.program_id(0)
    used = j < nb_ref[0]
    new_expert = jnp.logical_or(j == 0, be_ref[j] != be_ref[jnp.maximum(j - 1, 0)])

    @pl.when(jnp.logical_and(used, new_expert))
    def _():
        wg_s[...] = wg_ref[...].astype(BF16)
        wu_s[...] = wu_ref[...].astype(BF16)
        wd_s[...] = wd_ref[...].astype(BF16)

    @pl.when(used)
    def _():
        x = x_ref[...].astype(BF16)
        gate = jnp.minimum(_dot(x, wg_s[...]) + bg_ref[...], SWIGLU_LIMIT)
        up = jnp.clip(_dot(x, wu_s[...]) + bu_ref[...], -SWIGLU_LIMIT, SWIGLU_LIMIT)
        act = (up + 1.0) * gate * _sigmoid(gate * SWIGLU_ALPHA)
        y_ref[...] = _dot(act.astype(BF16), wd_s[...]) + bd_ref[...]


def _experts(block_e, nb_used, x_pad, wg, bg, wu, bu, wd, bd):
    rows, d = x_pad.shape
    nblk = rows // EXPERT_BLOCK
    dff = wg.shape[2]
    row_blk = lambda j, be, nb: (jnp.minimum(j, nb[0] - 1), 0)
    w_blk = lambda j, be, nb: (be[j], 0, 0)
    return pl.pallas_call(
        _experts_kernel,
        grid_spec=pltpu.PrefetchScalarGridSpec(
            num_scalar_prefetch=2,
            grid=(nblk,),
            in_specs=[
                pl.BlockSpec((EXPERT_BLOCK, d), row_blk),
                pl.BlockSpec((None, d, dff), w_blk),
                pl.BlockSpec((None, 1, dff), w_blk),
                pl.BlockSpec((None, d, dff), w_blk),
                pl.BlockSpec((None, 1, dff), w_blk),
                pl.BlockSpec((None, dff, d), w_blk),
                pl.BlockSpec((None, 1, d), w_blk),
            ],
            out_specs=pl.BlockSpec((EXPERT_BLOCK, d), row_blk),
            scratch_shapes=[pltpu.VMEM((d, dff), BF16), pltpu.VMEM((d, dff), BF16),
                            pltpu.VMEM((dff, d), BF16)],
        ),
        out_shape=jax.ShapeDtypeStruct((rows, d), F32),
        compiler_params=pltpu.CompilerParams(dimension_semantics=("arbitrary",),
                                             vmem_limit_bytes=V7X_EXPERTS_VMEM_LIMIT),
        name="experts",
    )(block_e, nb_used, x_pad, wg, bg, wu, bu, wd, bd)


def _combine_kernel(dest_ref, x2_ref, gate_ref, ypad_ref, o_ref, buf, sem):
    th = x2_ref.shape[0]
    for k in range(TOP_K):
        def issue(t, carry, k=k):
            _row_copy(ypad_ref, dest_ref[0, k * th + t], buf.at[k], t, sem).start()
            return carry

        lax.fori_loop(0, th, issue, 0, unroll=8)
    for k in range(TOP_K):
        pltpu.make_async_copy(ypad_ref.at[pl.ds(0, th), :], buf.at[k], sem).wait()
    g = gate_ref[...]
    out = x2_ref[...]
    for k in range(TOP_K):
        out = out + buf[k] * g[:, k:k + 1]
    o_ref[...] = out


def _combine(dest_tiles, x2d, gates_nk, y_pad, th):
    n, d = x2d.shape
    return pl.pallas_call(
        _combine_kernel,
        grid=(n // th,),
        in_specs=[
            pl.BlockSpec((None, 1, TOP_K * th), lambda i: (i, 0, 0), memory_space=pltpu.SMEM),
            pl.BlockSpec((th, d), lambda i: (i, 0)),
            pl.BlockSpec((th, TOP_K), lambda i: (i, 0)),
            pl.BlockSpec(memory_space=pl.ANY),
        ],
        out_specs=pl.BlockSpec((th, d), lambda i: (i, 0)),
        out_shape=jax.ShapeDtypeStruct((n, d), F32),
        scratch_shapes=[pltpu.VMEM((TOP_K, th, d), F32), pltpu.SemaphoreType.DMA(())],
        compiler_params=_params("arbitrary"),
        name="combine",
    )(dest_tiles, x2d, gates_nk, y_pad)


def _tile_dest(dest, t):
    k, n = dest.shape
    return dest.reshape(k, n // t, t).transpose(1, 0, 2).reshape(n // t, 1, k * t)


def _layer(x, mem, norm1_w, w_in, conv_w, a_log, dt_bias, dn_norm_w, sb_q_norm_w, sb_k_norm_w, w_out,
           norm2_w, mem_norm_w, xq_w, xk_w, xv_w, xq_norm_w, xk_norm_w, xo_w, norm3_w, router_w,
           router_b, w_gate, b_gate, w_up, b_up, w_down, b_down):
    b, s, d = x.shape
    n = b * s
    tm_proj = min(512, s)
    tm_mid = min(512, s)
    sub_mid = min(256, s)
    t_moe = min(256, s)

    o_dn, o_ab, o_sb = 4 * DN_WIDTH, 4 * DN_WIDTH + 2 * DN_HEADS, 4 * DN_WIDTH + 2 * DN_HEADS
    w_dn = w_in[:, :o_dn].astype(BF16)
    w_ab_f = w_in[:, o_dn:o_ab]
    w_ab = jnp.pad(w_ab_f, ((0, 0), (0, LANES - 2 * DN_HEADS))).astype(BF16)
    w_abt = w_ab_f.T.astype(BF16)
    w_sb = w_in[:, o_sb:].astype(BF16)
    row = lambda v: v.reshape(1, -1).astype(F32)

    dn, sb, ab, abt = _in_proj(x.reshape(n, d), row(norm1_w), w_dn, w_sb, w_ab, w_abt, tm_proj)

    abt4 = abt.reshape(8, n // CHUNK, CHUNK).transpose(1, 0, 2).reshape(b, s // CHUNK, 8, CHUNK)
    pad_lane = lambda v: jnp.pad(v.astype(F32), (0, LANES - v.shape[0])).reshape(1, LANES)
    pad_col = lambda v: jnp.pad(v.astype(F32), (0, 8 - v.shape[0])).reshape(8, 1)
    y_dn = _deltanet(dn.reshape(b, s, -1), ab.reshape(b, s, LANES), abt4, conv_w.astype(F32),
                     pad_lane(a_log), pad_lane(dt_bias), pad_col(a_log), pad_col(dt_bias), row(dn_norm_w))

    ii = jnp.arange(SB_BLOCK)
    m2 = jnp.concatenate([(ii[:, None] > ii[None, :]).astype(BF16),
                          jnp.ones((SB_BLOCK, SB_BLOCK), BF16)], axis=1)
    y_sb = _stickbreak(sb.reshape(b, s, -1), row(jnp.tile(sb_q_norm_w, 2)), row(jnp.tile(sb_k_norm_w, 2)), m2)

    k_mem, v_mem = _memkv(mem, row(mem_norm_w), xk_w.astype(BF16), xv_w.astype(BF16), row(xk_norm_w))

    wr_t = router_w.T.astype(F32)
    wr_hi = wr_t.astype(BF16)
    wr_lo = (wr_t - wr_hi.astype(F32)).astype(BF16)
    jj = jnp.arange(sub_mid)
    upper = (jj[:, None] < jj[None, :]).astype(BF16)
    x2, h3, idx, gates, rank, cnt = _mid(
        x, y_dn, y_sb, w_out.astype(BF16), row(norm2_w), xq_w.astype(BF16), row(xq_norm_w), k_mem, v_mem,
        xo_w.astype(BF16), row(norm3_w), wr_hi, wr_lo, router_b.reshape(N_EXPERTS, 1).astype(F32), upper,
        tm_mid)

    counts = cnt[:, 0].astype(jnp.int32)
    padded = (counts + EXPERT_BLOCK - 1) // EXPERT_BLOCK * EXPERT_BLOCK
    pad_ends = jnp.cumsum(padded)
    pad_starts = pad_ends - padded
    sel = idx[:, :, None] == jnp.arange(N_EXPERTS, dtype=jnp.int32)[None, None, :]
    dest = rank + jnp.sum(jnp.where(sel, pad_starts[None, None, :], 0), axis=-1)
    n_blocks = -(-n * TOP_K // EXPERT_BLOCK) + N_EXPERTS
    nb_used = (pad_ends[-1] // EXPERT_BLOCK).astype(jnp.int32)
    blk = jnp.minimum(jnp.arange(n_blocks, dtype=jnp.int32), nb_used - 1) * EXPERT_BLOCK
    block_e = jnp.minimum(jnp.sum(pad_ends[None, :] <= blk[:, None], axis=1), N_EXPERTS - 1).astype(jnp.int32)
    dest_tiles = _tile_dest(dest, t_moe)

    x_pad = _dispatch(pad_ends.astype(jnp.int32), padded.astype(jnp.int32), dest_tiles, h3.reshape(n, d),
                      n_blocks * EXPERT_BLOCK, t_moe)
    bias = lambda v: v.reshape(N_EXPERTS, 1, -1).astype(F32)
    y_pad = _experts(block_e, nb_used.reshape(1), x_pad, w_gate.astype(F32), bias(b_gate),
                     w_up.astype(F32), bias(b_up), w_down.astype(F32), bias(b_down))
    out = _combine(dest_tiles, x2.reshape(n, d), gates.T, y_pad, t_moe)
    return out.reshape(b, s, d)


def kernel(x, mem, norm1_w, w_in, conv_w, a_log, dt_bias, dn_norm_w, sb_q_norm_w, sb_k_norm_w, w_out,
           norm2_w, mem_norm_w, xq_w, xk_w, xv_w, xq_norm_w, xk_norm_w, xo_w, norm3_w, router_w,
           router_b, w_gate, b_gate, w_up, b_up, w_down, b_down):
    depth = w_in.shape[0]
    for l in range(depth):
        x = _layer(x, mem, norm1_w[l], w_in[l], conv_w[l], a_log[l], dt_bias[l], dn_norm_w[l],
                   sb_q_norm_w[l], sb_k_norm_w[l], w_out[l], norm2_w[l], mem_norm_w[l], xq_w[l], xk_w[l],
                   xv_w[l], xq_norm_w[l], xk_norm_w[l], xo_w[l], norm3_w[l], router_w[l], router_b[l],
                   w_gate[l], b_gate[l], w_up[l], b_up[l], w_down[l], b_down[l])
    return x
```

```python
import functools

import jax
import jax.numpy as jnp
from jax import lax
from jax.experimental import pallas as pl
from jax.experimental.pallas import tpu as pltpu

F32 = jnp.float32
BF16 = jnp.bfloat16

EPS = 1e-6
CHUNK = 64
DN_HEADS = 4
DN_DIM = 128
DN_WIDTH = DN_HEADS * DN_DIM
CONV_WIDTH = 4
DN_PAIR = 2
HALO = 16
SB_HEADS = 8
SB_DIM = 64
SB_WIDTH = SB_HEADS * SB_DIM
SB_BLOCK = 128
SB_GROUP = 4
X_HEADS = 4
X_DIM = 256
N_EXPERTS = 32
TOP_K = 4
EXPERT_BLOCK = 512
SWIGLU_ALPHA = 1.702
SWIGLU_LIMIT = 7.0
LANES = 128
SB_UNDERFLOW = -88.0
V7X_VMEM_LIMIT = 48 * 1024 * 1024
V7X_EXPERTS_VMEM_LIMIT = 58 * 1024 * 1024


def _dot(a, b):
    return jnp.dot(a, b, preferred_element_type=F32)


def _dot_nt(a, b):
    return lax.dot_general(a, b, (((1,), (1,)), ((), ())), preferred_element_type=F32)


def _dot_tn(a, b):
    return lax.dot_general(a, b, (((0,), (0,)), ((), ())), preferred_element_type=F32)


def _split3(x):
    hi = x.astype(BF16)
    r = x - hi.astype(F32)
    mid = r.astype(BF16)
    return hi, mid, (r - mid.astype(F32)).astype(BF16)


def _softplus(x):
    return jnp.maximum(x, 0.0) + jnp.log(1.0 + jnp.exp(-jnp.abs(x)))


def _sigmoid(x):
    return 1.0 / (1.0 + jnp.exp(-x))


def _rms(x, w):
    return x * lax.rsqrt(jnp.mean(x * x, axis=-1, keepdims=True) + EPS) * w


def _params(*sem):
    return pltpu.CompilerParams(dimension_semantics=sem, vmem_limit_bytes=V7X_VMEM_LIMIT)


def _const_spec(shape):
    nd = len(shape)
    return pl.BlockSpec(shape, lambda *_: (0,) * nd)


def _in_proj_kernel(x_ref, nw_ref, wdn_ref, wsb_ref, wab_ref, wabt_ref,
                    dn_ref, sb_ref, ab_ref, abt_ref):
    n = _rms(x_ref[...], nw_ref[...]).astype(BF16)
    dn_ref[...] = _dot(n, wdn_ref[...]).astype(BF16)
    sb_ref[...] = _dot(n, wsb_ref[...]).astype(BF16)
    ab_ref[...] = _dot(n, wab_ref[...])
    abt_ref[...] = _dot_nt(wabt_ref[...], n)


def _in_proj(x2d, norm_w, w_dn, w_sb, w_ab, w_abt, tm):
    n, d = x2d.shape
    return pl.pallas_call(
        _in_proj_kernel,
        grid=(n // tm,),
        in_specs=[
            pl.BlockSpec((tm, d), lambda i: (i, 0)),
            _const_spec((1, d)),
            _const_spec(w_dn.shape),
            _const_spec(w_sb.shape),
            _const_spec(w_ab.shape),
            _const_spec(w_abt.shape),
        ],
        out_specs=[
            pl.BlockSpec((tm, w_dn.shape[1]), lambda i: (i, 0)),
            pl.BlockSpec((tm, w_sb.shape[1]), lambda i: (i, 0)),
            pl.BlockSpec((tm, LANES), lambda i: (i, 0)),
            pl.BlockSpec((8, tm), lambda i: (0, i)),
        ],
        out_shape=[
            jax.ShapeDtypeStruct((n, w_dn.shape[1]), BF16),
            jax.ShapeDtypeStruct((n, w_sb.shape[1]), BF16),
            jax.ShapeDtypeStruct((n, LANES), F32),
            jax.ShapeDtypeStruct((8, n), F32),
        ],
        compiler_params=_params("arbitrary"),
        name="in_proj",
    )(x2d, norm_w, w_dn, w_sb, w_ab, w_abt)


def _deltanet_kernel(dn_ref, ab_ref, abt_ref, convw_ref, alog_ref, dtb_ref, alogt_ref, dtbt_ref,
                     normw_ref, o_ref, state_s, u_s, w_s, kt_s, qg_s, attn_s, egl_s):
    s = dn_ref.shape[0]
    state_s[...] = jnp.zeros(state_s.shape, F32)

    def conv_silu(r0, c, part, h, l2):
        cs = slice(part * DN_WIDTH + h * LANES, part * DN_WIDTH + (h + 1) * LANES)
        prev0 = pl.multiple_of(jnp.maximum(r0 - HALO, 0), HALO)
        prev = dn_ref[pl.ds(prev0, HALO), cs].astype(F32)
        prev = jnp.where(c > 0, prev, 0.0)
        win = jnp.concatenate([prev, dn_ref[pl.ds(r0, CHUNK), cs].astype(F32)], axis=0)
        w = convw_ref[:, cs]
        y = w[0:1, :] * win[HALO - 3:HALO - 3 + CHUNK, :]
        for i in range(1, CONV_WIDTH):
            y = y + w[i:i + 1, :] * win[HALO - 3 + i:HALO - 3 + i + CHUNK, :]
        y = y * _sigmoid(y)
        if l2:
            y = y * lax.rsqrt(jnp.sum(y * y, axis=-1, keepdims=True) + EPS)
        return y

    row = lax.broadcasted_iota(jnp.int32, (CHUNK, CHUNK), 0)
    col = lax.broadcasted_iota(jnp.int32, (CHUNK, CHUNK), 1)
    tri = row >= col
    strict = row > col
    tril16 = jnp.where(tri, 1.0, 0.0).astype(BF16)
    triu16 = jnp.where(row <= col, 1.0, 0.0).astype(BF16)
    neg_a_col = -jnp.exp(alog_ref[...])
    neg_a_row = -jnp.exp(alogt_ref[...])
    scale = DN_DIM ** -0.5

    def precompute(it, carry):
        items = []
        for ci in range(DN_PAIR):
            c = it * DN_PAIR + ci
            r0 = pl.multiple_of(c * CHUNK, CHUNK)
            ab = ab_ref[pl.ds(r0, CHUNK), :]
            g_col = neg_a_col * _softplus(ab + dtb_ref[...])
            gc_col_all = sum(_dot(tril16, p) for p in _split3(g_col))
            beta_all = _sigmoid(ab)
            abt = abt_ref[c]
            g_row = neg_a_row * _softplus(abt + dtbt_ref[...])
            gc_row_all = sum(_dot(p, triu16) for p in _split3(g_row))
            for h in range(DN_HEADS):
                q = conv_silu(r0, c, 0, h, True) * scale
                k = conv_silu(r0, c, 1, h, True)
                v = conv_silu(r0, c, 2, h, False)
                gcol = gc_col_all[:, h:h + 1]
                grow = gc_row_all[h:h + 1, :]
                beta = beta_all[:, DN_HEADS + h:DN_HEADS + h + 1]
                glast = gcol[CHUNK - 1:CHUNK, :]
                decay = jnp.where(tri, jnp.exp(jnp.where(tri, gcol - grow, 0.0)), 0.0)
                kb = k * beta
                cs = slice(h * LANES, (h + 1) * LANES)
                kt_s[pl.ds(r0, CHUNK), cs] = (k * jnp.exp(glast - gcol)).astype(BF16)
                qg_s[pl.ds(r0, CHUNK), cs] = (q * jnp.exp(gcol)).astype(BF16)
                egl_s[c, h:h + 1, :] = jnp.broadcast_to(jnp.exp(glast), (1, LANES))
                items.append(dict(c=c, r0=r0, h=h, cs=cs, decay=decay, kb16=kb.astype(BF16),
                                  k16=k.astype(BF16), q16=q.astype(BF16),
                                  sol=jnp.concatenate([v * beta, kb * jnp.exp(gcol)], axis=1)))
        kk = [_dot_nt(t["kb16"], t["k16"]) for t in items]
        qk = [_dot_nt(t["q16"], t["k16"]) for t in items]
        for t, kk_i, qk_i in zip(items, kk, qk):
            attn_s[t["c"], t["h"]] = jnp.where(tri, qk_i * t["decay"], 0.0).astype(BF16)
            t["tm"] = -jnp.where(strict, kk_i * t["decay"], 0.0)
            t["p16"] = t["tm"].astype(BF16)
        sq = [_dot(t["p16"], t["p16"]) for t in items]
        for level in range(1, 6):
            for i, t in enumerate(items):
                t["pw"] = sq[i]
                t["p16"] = sq[i].astype(BF16)
            app = [_dot(t["p16"], t["tm"].astype(BF16)) for t in items]
            if level < 5:
                sq = [_dot(t["p16"], t["p16"]) for t in items]
            for i, t in enumerate(items):
                t["tm"] = t["tm"] + t["pw"] + app[i]
        corr = [_dot(t["tm"].astype(BF16), t["sol"].astype(BF16)) for t in items]
        for i, t in enumerate(items):
            sol = t["sol"] + corr[i]
            u_s[pl.ds(t["r0"], CHUNK), t["cs"]] = sol[:, :DN_DIM]
            w_s[pl.ds(t["r0"], CHUNK), t["cs"]] = sol[:, DN_DIM:].astype(BF16)
        return carry

    lax.fori_loop(0, s // (CHUNK * DN_PAIR), precompute, 0)

    def recurrence(c, carry):
        r0 = pl.multiple_of(c * CHUNK, CHUNK)
        heads = range(DN_HEADS)
        cols = [slice(h * LANES, (h + 1) * LANES) for h in heads]
        st = [state_s[h] for h in heads]
        st16 = [x.astype(BF16) for x in st]
        ws = [_dot(w_s[pl.ds(r0, CHUNK), cols[h]], st16[h]) for h in heads]
        qs = [_dot(qg_s[pl.ds(r0, CHUNK), cols[h]], st16[h]) for h in heads]
        vn16 = [(u_s[pl.ds(r0, CHUNK), cols[h]] - ws[h]).astype(BF16) for h in heads]
        av = [_dot(attn_s[c, h], vn16[h]) for h in heads]
        ks = [_dot_tn(kt_s[pl.ds(r0, CHUNK), cols[h]], vn16[h]) for h in heads]
        for h in heads:
            state_s[h] = st[h] * egl_s[c, h:h + 1, :] + ks[h]
            o = qs[h] + av[h]
            o = o * lax.rsqrt(jnp.mean(o * o, axis=-1, keepdims=True) + EPS)
            z = dn_ref[pl.ds(r0, CHUNK), 3 * DN_WIDTH + h * LANES:3 * DN_WIDTH + (h + 1) * LANES].astype(F32)
            o_ref[pl.ds(r0, CHUNK), cols[h]] = (o * normw_ref[...] * (z * _sigmoid(z))).astype(BF16)
        return carry

    lax.fori_loop(0, s // CHUNK, recurrence, 0)


def _deltanet(dn3, ab3, abt4, conv_w, alog, dtb, alogt, dtbt, norm_w):
    b, s, _ = dn3.shape
    nch = s // CHUNK
    return pl.pallas_call(
        _deltanet_kernel,
        grid=(b,),
        in_specs=[
            pl.BlockSpec((None, s, 4 * DN_WIDTH), lambda i: (i, 0, 0)),
            pl.BlockSpec((None, s, LANES), lambda i: (i, 0, 0)),
            pl.BlockSpec((None, nch, 8, CHUNK), lambda i: (i, 0, 0, 0)),
            _const_spec(conv_w.shape),
            _const_spec((1, LANES)),
            _const_spec((1, LANES)),
            _const_spec((8, 1)),
            _const_spec((8, 1)),
            _const_spec((1, DN_DIM)),
        ],
        out_specs=pl.BlockSpec((None, s, DN_WIDTH), lambda i: (i, 0, 0)),
        out_shape=jax.ShapeDtypeStruct((b, s, DN_WIDTH), BF16),
        scratch_shapes=[
            pltpu.VMEM((DN_HEADS, DN_DIM, DN_DIM), F32),
            pltpu.VMEM((s, DN_WIDTH), F32),
            pltpu.VMEM((s, DN_WIDTH), BF16),
            pltpu.VMEM((s, DN_WIDTH), BF16),
            pltpu.VMEM((s, DN_WIDTH), BF16),
            pltpu.VMEM((nch, DN_HEADS, CHUNK, CHUNK), BF16),
            pltpu.VMEM((nch, 8, LANES), F32),
        ],
        compiler_params=_params("arbitrary"),
        name="deltanet",
    )(dn3, ab3, abt4, conv_w, alog, dtb, alogt, dtbt, norm_w)


def _stickbreak_kernel(q_ref, k_ref, v_ref, qw_ref, kw_ref, m2_ref, o_ref,
                       qn_s, kn_s, carry_s, acc_s):
    s = q_ref.shape[0]
    group = carry_s.shape[0]
    lane = lax.broadcasted_iota(jnp.int32, (1, LANES), 1)
    first = lane < SB_DIM

    def head_norm(r0, x_ref, w):
        x = x_ref[pl.ds(r0, SB_BLOCK), :].astype(F32)
        sq = x * x
        sa = jnp.sum(jnp.where(first, sq, 0.0), axis=-1, keepdims=True)
        sb = jnp.sum(jnp.where(first, 0.0, sq), axis=-1, keepdims=True)
        ra = lax.rsqrt(sa * (1.0 / SB_DIM) + EPS)
        rb = lax.rsqrt(sb * (1.0 / SB_DIM) + EPS)
        return x * jnp.where(first, ra, rb) * w

    def norm_block(i, carry):
        r0 = pl.multiple_of(i * SB_BLOCK, SB_BLOCK)
        qn = head_norm(r0, q_ref, qw_ref[...]) * (SB_DIM ** -0.5)
        qn_s[0, pl.ds(r0, SB_BLOCK), :] = jnp.where(first, qn, 0.0).astype(BF16)
        qn_s[1, pl.ds(r0, SB_BLOCK), :] = jnp.where(first, 0.0, qn).astype(BF16)
        kn_s[pl.ds(r0, SB_BLOCK), :] = head_norm(r0, k_ref, kw_ref[...]).astype(BF16)
        return carry

    lax.fori_loop(0, s // SB_BLOCK, norm_block, 0)

    row = lax.broadcasted_iota(jnp.int32, (SB_BLOCK, SB_BLOCK), 0)
    col = lax.broadcasted_iota(jnp.int32, (SB_BLOCK, SB_BLOCK), 1)
    causal = col < row

    def step(rows, keys, valid):
        tiles = [(g, h) for g in range(group) for h in range(2)]
        k16 = [kn_s[pl.ds(c0, SB_BLOCK), :] for c0 in keys]
        v16 = [v_ref[pl.ds(c0, SB_BLOCK), :] for c0 in keys]
        z = [_dot_nt(qn_s[h, pl.ds(rows[g], SB_BLOCK), :], k16[g]) for g, h in tiles]
        sp = [_softplus(x) for x in z]
        if valid is None:
            log_fail = [jnp.where(causal, -x, 0.0) for x in sp]
        else:
            log_fail = [-x for x in sp]
        cs = [_dot(x.astype(BF16), m2_ref[...]) for x in log_fail]
        p = []
        for i, (g, h) in enumerate(tiles):
            w = jnp.exp(z[i] - sp[i] + carry_s[g, h] + cs[i][:, :SB_BLOCK])
            p.append(jnp.where(causal, w, 0.0) if valid is None else w)
        pv = [_dot(p[i].astype(BF16), v16[g]) for i, (g, h) in enumerate(tiles)]
        for i, (g, h) in enumerate(tiles):
            acc = acc_s[g, h] + pv[i]
            carry = carry_s[g, h] + cs[i][:, SB_BLOCK:]
            if valid is not None:
                acc = jnp.where(valid[g], acc, acc_s[g, h])
                carry = jnp.where(valid[g], carry, carry_s[g, h])
            acc_s[g, h] = acc
            carry_s[g, h] = carry

    def q_group(qg, carry):
        blocks = [qg * group + g for g in range(group)]
        rows = [pl.multiple_of(qb * SB_BLOCK, SB_BLOCK) for qb in blocks]
        carry_s[...] = jnp.zeros(carry_s.shape, F32)
        acc_s[...] = jnp.zeros(acc_s.shape, F32)
        step(rows, rows, None)

        def cond(st):
            d, alive = st
            return jnp.logical_and(d <= blocks[-1], alive)

        def body(st):
            d, _ = st
            alive = jnp.max(carry_s[...]) > SB_UNDERFLOW
            keys = [pl.multiple_of(jnp.maximum(qb - d, 0) * SB_BLOCK, SB_BLOCK) for qb in blocks]
            step(rows, keys, [qb >= d for qb in blocks])
            return d + 1, alive

        lax.while_loop(cond, body, (jnp.int32(1), jnp.bool_(True)))
        for g in range(group):
            o_ref[pl.ds(rows[g], SB_BLOCK), :] = jnp.where(first, acc_s[g, 0], acc_s[g, 1]).astype(BF16)
        return carry

    lax.fori_loop(0, s // (SB_BLOCK * group), q_group, 0)


def _stickbreak(sb3, qw, kw, m2):
    b, s, _ = sb3.shape
    pairs = SB_HEADS // 2
    group = min(SB_GROUP, s // SB_BLOCK)
    return pl.pallas_call(
        _stickbreak_kernel,
        grid=(b, pairs),
        in_specs=[
            pl.BlockSpec((None, s, LANES), lambda i, j: (i, 0, j)),
            pl.BlockSpec((None, s, LANES), lambda i, j: (i, 0, pairs + j)),
            pl.BlockSpec((None, s, LANES), lambda i, j: (i, 0, 2 * pairs + j)),
            _const_spec((1, LANES)),
            _const_spec((1, LANES)),
            _const_spec(m2.shape),
        ],
        out_specs=pl.BlockSpec((None, s, LANES), lambda i, j: (i, 0, j)),
        out_shape=jax.ShapeDtypeStruct((b, s, SB_WIDTH), BF16),
        scratch_shapes=[
            pltpu.VMEM((2, s, LANES), BF16),
            pltpu.VMEM((s, LANES), BF16),
            pltpu.VMEM((group, 2, SB_BLOCK, SB_BLOCK), F32),
            pltpu.VMEM((group, 2, SB_BLOCK, LANES), F32),
        ],
        compiler_params=_params("arbitrary", "arbitrary"),
        name="stickbreak",
    )(sb3, sb3, sb3, qw, kw, m2)


def _memkv_kernel(mem_ref, nw_ref, wk_ref, wv_ref, knw_ref, k_ref, v_ref):
    n = _rms(mem_ref[...], nw_ref[...]).astype(BF16)
    k = _dot(n, wk_ref[...])
    for h in range(X_HEADS):
        cs = slice(h * X_DIM, (h + 1) * X_DIM)
        k_ref[:, cs] = _rms(k[:, cs], knw_ref[...]).astype(BF16)
    v_ref[...] = _dot(n, wv_ref[...]).astype(BF16)


def _memkv(mem, norm_w, wk, wv, k_norm_w):
    b, m, d = mem.shape
    return pl.pallas_call(
        _memkv_kernel,
        grid=(b,),
        in_specs=[
            pl.BlockSpec((None, m, d), lambda i: (i, 0, 0)),
            _const_spec((1, d)),
            _const_spec(wk.shape),
            _const_spec(wv.shape),
            _const_spec((1, X_DIM)),
        ],
        out_specs=[
            pl.BlockSpec((None, m, d), lambda i: (i, 0, 0)),
            pl.BlockSpec((None, m, d), lambda i: (i, 0, 0)),
        ],
        out_shape=[jax.ShapeDtypeStruct((b, m, d), BF16)] * 2,
        compiler_params=_params("arbitrary"),
        name="memkv",
    )(mem, norm_w, wk, wv, k_norm_w)


def _mid_kernel(x_ref, ydn_ref, ysb_ref, wout_ref, n2w_ref, wq_ref, qnw_ref, km_ref, vm_ref, wo_ref,
                n3w_ref, wrh_ref, wrl_ref, rb_ref, upper_ref,
                x2_ref, h3_ref, idx_ref, gate_ref, rank_ref, cnt_ref, count_s):
    tm = x_ref.shape[0]
    first_step = jnp.logical_and(pl.program_id(0) == 0, pl.program_id(1) == 0)

    @pl.when(first_step)
    def _():
        count_s[...] = jnp.zeros(count_s.shape, F32)

    sub = upper_ref.shape[0]
    subs = range(tm // sub)
    rows = [slice(i * sub, (i + 1) * sub) for i in subs]
    heads = range(X_HEADS)
    cols = [slice(h * X_DIM, (h + 1) * X_DIM) for h in heads]
    x1 = [x_ref[r, :] + _dot(ydn_ref[r, :], wout_ref[0:DN_WIDTH, :])
          + _dot(ysb_ref[r, :], wout_ref[DN_WIDTH:, :]) for r in rows]
    n2 = [_rms(v, n2w_ref[...]).astype(BF16) for v in x1]
    q = [_dot(v, wq_ref[...]) for v in n2]
    qh = [[(_rms(q[i][:, c], qnw_ref[...]) * (X_DIM ** -0.5)).astype(BF16) for c in cols] for i in subs]
    sc = [[_dot_nt(qh[i][h], km_ref[:, cols[h]]) for h in heads] for i in subs]
    ex = [[jnp.exp(sc[i][h] - jnp.max(sc[i][h], axis=-1, keepdims=True)) for h in heads] for i in subs]
    pr = [[(ex[i][h] / jnp.sum(ex[i][h], axis=-1, keepdims=True)).astype(BF16) for h in heads] for i in subs]
    oh = [[_dot(pr[i][h], vm_ref[:, cols[h]]).astype(BF16) for h in heads] for i in subs]
    x2 = [x1[i] + _dot(jnp.concatenate(oh[i], axis=1), wo_ref[...]) for i in subs]
    h3 = [_rms(v, n3w_ref[...]) for v in x2]
    hi = [v.astype(BF16) for v in h3]
    lo = [(h3[i] - hi[i].astype(F32)).astype(BF16) for i in subs]
    logits = [_dot_nt(wrh_ref[...], hi[i]) + _dot_nt(wrh_ref[...], lo[i]) + _dot_nt(wrl_ref[...], hi[i])
              + rb_ref[...] for i in subs]
    eid = lax.broadcasted_iota(jnp.int32, (N_EXPERTS, sub), 0).astype(F32)
    count = count_s[...]
    for i in subs:
        x2_ref[rows[i], :] = x2[i]
        h3_ref[rows[i], :] = h3[i]
        vals, ids = [], []
        cur = logits[i]
        for _ in range(TOP_K):
            m = jnp.max(cur, axis=0, keepdims=True)
            j = jnp.min(jnp.where(cur == m, eid, float(N_EXPERTS)), axis=0, keepdims=True)
            vals.append(m)
            ids.append(j)
            cur = jnp.where(eid == j, -jnp.inf, cur)
        exps = [jnp.exp(v - vals[0]) for v in vals]
        denom = exps[0] + exps[1] + exps[2] + exps[3]
        onehot = jnp.zeros((N_EXPERTS, sub), F32)
        for j in ids:
            onehot = onehot + jnp.where(eid == j, 1.0, 0.0)
        before = count + _dot(onehot.astype(BF16), upper_ref[...])
        for k in range(TOP_K):
            idx_ref[k:k + 1, rows[i]] = ids[k].astype(jnp.int32)
            gate_ref[k:k + 1, rows[i]] = exps[k] / denom
            rank_ref[k:k + 1, rows[i]] = jnp.sum(jnp.where(eid == ids[k], before, 0.0), axis=0,
                                                 keepdims=True).astype(jnp.int32)
        count = count + jnp.sum(onehot, axis=1, keepdims=True)
    count_s[...] = count
    cnt_ref[...] = jnp.broadcast_to(count, cnt_ref.shape)


def _mid(x3, ydn3, ysb3, w_out, n2w, wq, qnw, k_mem, v_mem, wo, n3w, wr_hi, wr_lo, rb, upper, tm):
    b, s, d = x3.shape
    n = b * s
    nt = s // tm
    m = k_mem.shape[1]
    tok = lambda i, j: (0, i * nt + j)
    return pl.pallas_call(
        _mid_kernel,
        grid=(b, nt),
        in_specs=[
            pl.BlockSpec((None, tm, d), lambda i, j: (i, j, 0)),
            pl.BlockSpec((None, tm, DN_WIDTH), lambda i, j: (i, j, 0)),
            pl.BlockSpec((None, tm, SB_WIDTH), lambda i, j: (i, j, 0)),
            _const_spec(w_out.shape),
            _const_spec((1, d)),
            _const_spec(wq.shape),
            _const_spec((1, X_DIM)),
            pl.BlockSpec((None, m, d), lambda i, j: (i, 0, 0)),
            pl.BlockSpec((None, m, d), lambda i, j: (i, 0, 0)),
            _const_spec(wo.shape),
            _const_spec((1, d)),
            _const_spec(wr_hi.shape),
            _const_spec(wr_lo.shape),
            _const_spec((N_EXPERTS, 1)),
            _const_spec(upper.shape),
        ],
        out_specs=[
            pl.BlockSpec((None, tm, d), lambda i, j: (i, j, 0)),
            pl.BlockSpec((None, tm, d), lambda i, j: (i, j, 0)),
            pl.BlockSpec((TOP_K, tm), tok),
            pl.BlockSpec((TOP_K, tm), tok),
            pl.BlockSpec((TOP_K, tm), tok),
            _const_spec((N_EXPERTS, LANES)),
        ],
        out_shape=[
            jax.ShapeDtypeStruct((b, s, d), F32),
            jax.ShapeDtypeStruct((b, s, d), F32),
            jax.ShapeDtypeStruct((TOP_K, n), jnp.int32),
            jax.ShapeDtypeStruct((TOP_K, n), F32),
            jax.ShapeDtypeStruct((TOP_K, n), jnp.int32),
            jax.ShapeDtypeStruct((N_EXPERTS, LANES), F32),
        ],
        scratch_shapes=[pltpu.VMEM((N_EXPERTS, 1), F32)],
        compiler_params=_params("arbitrary", "arbitrary"),
        name="mid",
    )(x3, ydn3, ysb3, w_out, n2w, wq, qnw, k_mem, v_mem, wo, n3w, wr_hi, wr_lo, rb, upper)


def _row_copy(src_ref, src_row, dst_ref, dst_row, sem):
    return pltpu.make_async_copy(src_ref.at[pl.ds(src_row, 1), :], dst_ref.at[pl.ds(dst_row, 1), :], sem)


def _dispatch_kernel(pend_ref, padded_ref, dest_ref, h_ref, xpad_ref, zero_s, sem, zsem):
    tf = h_ref.shape[0]

    @pl.when(pl.program_id(0) == 0)
    def _():
        zero_s[...] = jnp.zeros(zero_s.shape, F32)

        def last_block(e):
            start = pl.multiple_of(pend_ref[e] - EXPERT_BLOCK, EXPERT_BLOCK)
            return pltpu.make_async_copy(zero_s, xpad_ref.at[pl.ds(start, EXPERT_BLOCK), :], zsem)

        for e in range(N_EXPERTS):
            @pl.when(padded_ref[e] > 0)
            def _(e=e):
                last_block(e).start()
        for e in range(N_EXPERTS):
            @pl.when(padded_ref[e] > 0)
            def _(e=e):
                last_block(e).wait()

    for k in range(TOP_K):
        for t in range(tf):
            _row_copy(h_ref, t, xpad_ref, dest_ref[0, k * tf + t], sem).start()
    for k in range(TOP_K):
        pltpu.make_async_copy(h_ref, xpad_ref.at[pl.ds(0, tf), :], sem).wait()


def _dispatch(pad_ends, padded, dest_tiles, h2d, rows, tf):
    n, d = h2d.shape
    return pl.pallas_call(
        _dispatch_kernel,
        grid_spec=pltpu.PrefetchScalarGridSpec(
            num_scalar_prefetch=2,
            grid=(n // tf,),
            in_specs=[
                pl.BlockSpec((None, 1, TOP_K * tf), lambda i, *_: (i, 0, 0), memory_space=pltpu.SMEM),
                pl.BlockSpec((tf, d), lambda i, *_: (i, 0)),
            ],
            out_specs=pl.BlockSpec(memory_space=pl.ANY),
            scratch_shapes=[pltpu.VMEM((EXPERT_BLOCK, d), F32), pltpu.SemaphoreType.DMA(()),
                            pltpu.SemaphoreType.DMA(())],
        ),
        out_shape=jax.ShapeDtypeStruct((rows, d), F32),
        compiler_params=_params("arbitrary"),
        name="dispatch",
    )(pad_ends, padded, dest_tiles, h2d)


def _experts_kernel(be_ref, nb_ref, x_ref, wg_ref, bg_ref, wu_ref, bu_ref, wd_ref, bd_ref, y_ref,
                    wg_s, wu_s, wd_s):
    j = pl.program_id(0)
    used = j < nb_ref[0]
    new_expert = jnp.logical_or(j == 0, be_ref[j] != be_ref[jnp.maximum(j - 1, 0)])

    @pl.when(jnp.logical_and(used, new_expert))
    def _():
        wg_s[...] = wg_ref[...].astype(BF16)
        wu_s[...] = wu_ref[...].astype(BF16)
        wd_s[...] = wd_ref[...].astype(BF16)

    @pl.when(used)
    def _():
        x = x_ref[...].astype(BF16)
        gate = jnp.minimum(_dot(x, wg_s[...]) + bg_ref[...], SWIGLU_LIMIT)
        up = jnp.clip(_dot(x, wu_s[...]) + bu_ref[...], -SWIGLU_LIMIT, SWIGLU_LIMIT)
        act = (up + 1.0) * gate * _sigmoid(gate * SWIGLU_ALPHA)
        y_ref[...] = _dot(act.astype(BF16), wd_s[...]) + bd_ref[...]


def _experts(block_e, nb_used, x_pad, wg, bg, wu, bu, wd, bd):
    rows, d = x_pad.shape
    nblk = rows // EXPERT_BLOCK
    dff = wg.shape[2]
    row_blk = lambda j, be, nb: (jnp.minimum(j, nb[0] - 1), 0)
    w_blk = lambda j, be, nb: (be[j], 0, 0)
    return pl.pallas_call(
        _experts_kernel,
        grid_spec=pltpu.PrefetchScalarGridSpec(
            num_scalar_prefetch=2,
            grid=(nblk,),
            in_specs=[
                pl.BlockSpec((EXPERT_BLOCK, d), row_blk),
                pl.BlockSpec((None, d, dff), w_blk),
                pl.BlockSpec((None, 1, dff), w_blk),
                pl.BlockSpec((None, d, dff), w_blk),
                pl.BlockSpec((None, 1, dff), w_blk),
                pl.BlockSpec((None, dff, d), w_blk),
                pl.BlockSpec((None, 1, d), w_blk),
            ],
            out_specs=pl.BlockSpec((EXPERT_BLOCK, d), row_blk),
            scratch_shapes=[pltpu.VMEM((d, dff), BF16), pltpu.VMEM((d, dff), BF16),
                            pltpu.VMEM((dff, d), BF16)],
        ),
        out_shape=jax.ShapeDtypeStruct((rows, d), F32),
        compiler_params=pltpu.CompilerParams(dimension_semantics=("arbitrary",),
                                             vmem_limit_bytes=V7X_EXPERTS_VMEM_LIMIT),
        name="experts",
    )(block_e, nb_used, x_pad, wg, bg, wu, bu, wd, bd)


def _combine_kernel(dest_ref, x2_ref, gate_ref, ypad_ref, o_ref, buf, sem):
    th = x2_ref.shape[0]
    for k in range(TOP_K):
        for t in range(th):
            _row_copy(ypad_ref, dest_ref[0, k * th + t], buf.at[k], t, sem).start()
    for k in range(TOP_K):
        pltpu.make_async_copy(ypad_ref.at[pl.ds(0, th), :], buf.at[k], sem).wait()
    g = gate_ref[...]
    out = x2_ref[...]
    for k in range(TOP_K):
        out = out + buf[k] * g[:, k:k + 1]
    o_ref[...] = out


def _combine(dest_tiles, x2d, gates_nk, y_pad, th):
    n, d = x2d.shape
    return pl.pallas_call(
        _combine_kernel,
        grid=(n // th,),
        in_specs=[
            pl.BlockSpec((None, 1, TOP_K * th), lambda i: (i, 0, 0), memory_space=pltpu.SMEM),
            pl.BlockSpec((th, d), lambda i: (i, 0)),
            pl.BlockSpec((th, TOP_K), lambda i: (i, 0)),
            pl.BlockSpec(memory_space=pl.ANY),
        ],
        out_specs=pl.BlockSpec((th, d), lambda i: (i, 0)),
        out_shape=jax.ShapeDtypeStruct((n, d), F32),
        scratch_shapes=[pltpu.VMEM((TOP_K, th, d), F32), pltpu.SemaphoreType.DMA(())],
        compiler_params=_params("arbitrary"),
        name="combine",
    )(dest_tiles, x2d, gates_nk, y_pad)


def _tile_dest(dest, t):
    k, n = dest.shape
    return dest.reshape(k, n // t, t).transpose(1, 0, 2).reshape(n // t, 1, k * t)


def _layer(x, mem, norm1_w, w_in, conv_w, a_log, dt_bias, dn_norm_w, sb_q_norm_w, sb_k_norm_w, w_out,
           norm2_w, mem_norm_w, xq_w, xk_w, xv_w, xq_norm_w, xk_norm_w, xo_w, norm3_w, router_w,
           router_b, w_gate, b_gate, w_up, b_up, w_down, b_down):
    b, s, d = x.shape
    n = b * s
    tm_proj = min(512, s)
    tm_mid = min(512, s)
    sub_mid = min(256, s)
    t_moe = min(256, s)

    o_dn, o_ab, o_sb = 4 * DN_WIDTH, 4 * DN_WIDTH + 2 * DN_HEADS, 4 * DN_WIDTH + 2 * DN_HEADS
    w_dn = w_in[:, :o_dn].astype(BF16)
    w_ab_f = w_in[:, o_dn:o_ab]
    w_ab = jnp.pad(w_ab_f, ((0, 0), (0, LANES - 2 * DN_HEADS))).astype(BF16)
    w_abt = w_ab_f.T.astype(BF16)
    w_sb = w_in[:, o_sb:].astype(BF16)
    row = lambda v: v.reshape(1, -1).astype(F32)

    dn, sb, ab, abt = _in_proj(x.reshape(n, d), row(norm1_w), w_dn, w_sb, w_ab, w_abt, tm_proj)

    abt4 = abt.reshape(8, n // CHUNK, CHUNK).transpose(1, 0, 2).reshape(b, s // CHUNK, 8, CHUNK)
    pad_lane = lambda v: jnp.pad(v.astype(F32), (0, LANES - v.shape[0])).reshape(1, LANES)
    pad_col = lambda v: jnp.pad(v.astype(F32), (0, 8 - v.shape[0])).reshape(8, 1)
    y_dn = _deltanet(dn.reshape(b, s, -1), ab.reshape(b, s, LANES), abt4, conv_w.astype(F32),
                     pad_lane(a_log), pad_lane(dt_bias), pad_col(a_log), pad_col(dt_bias), row(dn_norm_w))

    ii = jnp.arange(SB_BLOCK)
    m2 = jnp.concatenate([(ii[:, None] > ii[None, :]).astype(BF16),
                          jnp.ones((SB_BLOCK, SB_BLOCK), BF16)], axis=1)
    y_sb = _stickbreak(sb.reshape(b, s, -1), row(jnp.tile(sb_q_norm_w, 2)), row(jnp.tile(sb_k_norm_w, 2)), m2)

    k_mem, v_mem = _memkv(mem, row(mem_norm_w), xk_w.astype(BF16), xv_w.astype(BF16), row(xk_norm_w))

    wr_t = router_w.T.astype(F32)
    wr_hi = wr_t.astype(BF16)
    wr_lo = (wr_t - wr_hi.astype(F32)).astype(BF16)
    jj = jnp.arange(sub_mid)
    upper = (jj[:, None] < jj[None, :]).astype(BF16)
    x2, h3, idx, gates, rank, cnt = _mid(
        x, y_dn, y_sb, w_out.astype(BF16), row(norm2_w), xq_w.astype(BF16), row(xq_norm_w), k_mem, v_mem,
        xo_w.astype(BF16), row(norm3_w), wr_hi, wr_lo, router_b.reshape(N_EXPERTS, 1).astype(F32), upper,
        tm_mid)

    counts = cnt[:, 0].astype(jnp.int32)
    padded = (counts + EXPERT_BLOCK - 1) // EXPERT_BLOCK * EXPERT_BLOCK
    pad_ends = jnp.cumsum(padded)
    pad_starts = pad_ends - padded
    sel = idx[:, :, None] == jnp.arange(N_EXPERTS, dtype=jnp.int32)[None, None, :]
    dest = rank + jnp.sum(jnp.where(sel, pad_starts[None, None, :], 0), axis=-1)
    n_blocks = -(-n * TOP_K // EXPERT_BLOCK) + N_EXPERTS
    nb_used = (pad_ends[-1] // EXPERT_BLOCK).astype(jnp.int32)
    blk = jnp.minimum(jnp.arange(n_blocks, dtype=jnp.int32), nb_used - 1) * EXPERT_BLOCK
    block_e = jnp.minimum(jnp.sum(pad_ends[None, :] <= blk[:, None], axis=1), N_EXPERTS - 1).astype(jnp.int32)
    dest_tiles = _tile_dest(dest, t_moe)

    x_pad = _dispatch(pad_ends.astype(jnp.int32), padded.astype(jnp.int32), dest_tiles, h3.reshape(n, d),
                      n_blocks * EXPERT_BLOCK, t_moe)
    bias = lambda v: v.reshape(N_EXPERTS, 1, -1).astype(F32)
    y_pad = _experts(block_e, nb_used.reshape(1), x_pad, w_gate.astype(F32), bias(b_gate),
                     w_up.astype(F32), bias(b_up), w_down.astype(F32), bias(b_down))
    out = _combine(dest_tiles, x2.reshape(n, d), gates.T, y_pad, t_moe)
    return out.reshape(b, s, d)


def kernel(x, mem, norm1_w, w_in, conv_w, a_log, dt_bias, dn_norm_w, sb_q_norm_w, sb_k_norm_w, w_out,
           norm2_w, mem_norm_w, xq_w, xk_w, xv_w, xq_norm_w, xk_norm_w, xo_w, norm3_w, router_w,
           router_b, w_gate, b_gate, w_up, b_up, w_down, b_down):
    depth = w_in.shape[0]
    for l in range(depth):
        x = _layer(x, mem, norm1_w[l], w_in[l], conv_w[l], a_log[l], dt_bias[l], dn_norm_w[l],
                   sb_q_norm_w[l], sb_k_norm_w[l], w_out[l], norm2_w[l], mem_norm_w[l], xq_w[l], xk_w[l],
                   xv_w[l], xq_norm_w[l], xk_norm_w[l], xo_w[l], norm3_w[l], router_w[l], router_b[l],
                   w_gate[l], b_gate[l], w_up[l], b_up[l], w_down[l], b_down[l])
    return x
```

```python
import functools

import jax
import jax.numpy as jnp
from jax import lax
from jax.experimental import pallas as pl
from jax.experimental.pallas import tpu as pltpu

F32 = jnp.float32
BF16 = jnp.bfloat16

EPS = 1e-6
CHUNK = 64
DN_HEADS = 4
DN_DIM = 128
DN_WIDTH = DN_HEADS * DN_DIM
CONV_WIDTH = 4
DN_PAIR = 2
HALO = 16
SB_HEADS = 8
SB_DIM = 64
SB_WIDTH = SB_HEADS * SB_DIM
SB_BLOCK = 128
SB_GROUP = 4
SB_LANES = 256
X_HEADS = 4
X_DIM = 256
N_EXPERTS = 32
TOP_K = 4
EXPERT_BLOCK = 512
SWIGLU_ALPHA = 1.702
SWIGLU_LIMIT = 7.0
LANES = 128
SB_UNDERFLOW = -88.0
V7X_VMEM_LIMIT = 48 * 1024 * 1024
V7X_EXPERTS_VMEM_LIMIT = 58 * 1024 * 1024


def _dot(a, b):
    return jnp.dot(a, b, preferred_element_type=F32)


def _dot_nt(a, b):
    return lax.dot_general(a, b, (((1,), (1,)), ((), ())), preferred_element_type=F32)


def _dot_tn(a, b):
    return lax.dot_general(a, b, (((0,), (0,)), ((), ())), preferred_element_type=F32)


def _split3(x):
    hi = x.astype(BF16)
    r = x - hi.astype(F32)
    mid = r.astype(BF16)
    return hi, mid, (r - mid.astype(F32)).astype(BF16)


def _softplus(x):
    return jnp.maximum(x, 0.0) + jnp.log(1.0 + jnp.exp(-jnp.abs(x)))


def _sigmoid(x):
    return 1.0 / (1.0 + jnp.exp(-x))


def _rms(x, w):
    return x * lax.rsqrt(jnp.mean(x * x, axis=-1, keepdims=True) + EPS) * w


def _params(*sem):
    return pltpu.CompilerParams(dimension_semantics=sem, vmem_limit_bytes=V7X_VMEM_LIMIT)


def _const_spec(shape):
    nd = len(shape)
    return pl.BlockSpec(shape, lambda *_: (0,) * nd)


def _in_proj_kernel(x_ref, nw_ref, wdn_ref, wsb_ref, wab_ref, wabt_ref,
                    dn_ref, sb_ref, ab_ref, abt_ref):
    n = _rms(x_ref[...], nw_ref[...]).astype(BF16)
    dn_ref[...] = _dot(n, wdn_ref[...]).astype(BF16)
    sb_ref[...] = _dot(n, wsb_ref[...]).astype(BF16)
    ab_ref[...] = _dot(n, wab_ref[...])
    abt_ref[...] = _dot_nt(wabt_ref[...], n)


def _in_proj(x2d, norm_w, w_dn, w_sb, w_ab, w_abt, tm):
    n, d = x2d.shape
    return pl.pallas_call(
        _in_proj_kernel,
        grid=(n // tm,),
        in_specs=[
            pl.BlockSpec((tm, d), lambda i: (i, 0)),
            _const_spec((1, d)),
            _const_spec(w_dn.shape),
            _const_spec(w_sb.shape),
            _const_spec(w_ab.shape),
            _const_spec(w_abt.shape),
        ],
        out_specs=[
            pl.BlockSpec((tm, w_dn.shape[1]), lambda i: (i, 0)),
            pl.BlockSpec((tm, w_sb.shape[1]), lambda i: (i, 0)),
            pl.BlockSpec((tm, LANES), lambda i: (i, 0)),
            pl.BlockSpec((8, tm), lambda i: (0, i)),
        ],
        out_shape=[
            jax.ShapeDtypeStruct((n, w_dn.shape[1]), BF16),
            jax.ShapeDtypeStruct((n, w_sb.shape[1]), BF16),
            jax.ShapeDtypeStruct((n, LANES), F32),
            jax.ShapeDtypeStruct((8, n), F32),
        ],
        compiler_params=_params("arbitrary"),
        name="in_proj",
    )(x2d, norm_w, w_dn, w_sb, w_ab, w_abt)


def _deltanet_kernel(dn_ref, ab_ref, abt_ref, convw_ref, alog_ref, dtb_ref, alogt_ref, dtbt_ref,
                     normw_ref, o_ref, state_s, u_s, w_s, kt_s, qg_s, attn_s, egl_s):
    s = dn_ref.shape[0]
    state_s[...] = jnp.zeros(state_s.shape, F32)

    def conv_silu(r0, c, part, h, l2):
        cs = slice(part * DN_WIDTH + h * LANES, part * DN_WIDTH + (h + 1) * LANES)
        prev0 = pl.multiple_of(jnp.maximum(r0 - HALO, 0), HALO)
        prev = dn_ref[pl.ds(prev0, HALO), cs].astype(F32)
        prev = jnp.where(c > 0, prev, 0.0)
        win = jnp.concatenate([prev, dn_ref[pl.ds(r0, CHUNK), cs].astype(F32)], axis=0)
        w = convw_ref[:, cs]
        y = w[0:1, :] * win[HALO - 3:HALO - 3 + CHUNK, :]
        for i in range(1, CONV_WIDTH):
            y = y + w[i:i + 1, :] * win[HALO - 3 + i:HALO - 3 + i + CHUNK, :]
        y = y * _sigmoid(y)
        if l2:
            y = y * lax.rsqrt(jnp.sum(y * y, axis=-1, keepdims=True) + EPS)
        return y

    row = lax.broadcasted_iota(jnp.int32, (CHUNK, CHUNK), 0)
    col = lax.broadcasted_iota(jnp.int32, (CHUNK, CHUNK), 1)
    tri = row >= col
    strict = row > col
    tril16 = jnp.where(tri, 1.0, 0.0).astype(BF16)
    triu16 = jnp.where(row <= col, 1.0, 0.0).astype(BF16)
    neg_a_col = -jnp.exp(alog_ref[...])
    neg_a_row = -jnp.exp(alogt_ref[...])
    scale = DN_DIM ** -0.5

    def precompute(it, carry):
        items = []
        for ci in range(DN_PAIR):
            c = it * DN_PAIR + ci
            r0 = pl.multiple_of(c * CHUNK, CHUNK)
            ab = ab_ref[pl.ds(r0, CHUNK), :]
            g_col = neg_a_col * _softplus(ab + dtb_ref[...])
            gc_col_all = sum(_dot(tril16, p) for p in _split3(g_col))
            beta_all = _sigmoid(ab)
            abt = abt_ref[c]
            g_row = neg_a_row * _softplus(abt + dtbt_ref[...])
            gc_row_all = sum(_dot(p, triu16) for p in _split3(g_row))
            for h in range(DN_HEADS):
                q = conv_silu(r0, c, 0, h, True) * scale
                k = conv_silu(r0, c, 1, h, True)
                v = conv_silu(r0, c, 2, h, False)
                gcol = gc_col_all[:, h:h + 1]
                grow = gc_row_all[h:h + 1, :]
                beta = beta_all[:, DN_HEADS + h:DN_HEADS + h + 1]
                glast = gcol[CHUNK - 1:CHUNK, :]
                decay = jnp.where(tri, jnp.exp(jnp.where(tri, gcol - grow, 0.0)), 0.0)
                kb = k * beta
                cs = slice(h * LANES, (h + 1) * LANES)
                kt_s[pl.ds(r0, CHUNK), cs] = (k * jnp.exp(glast - gcol)).astype(BF16)
                qg_s[pl.ds(r0, CHUNK), cs] = (q * jnp.exp(gcol)).astype(BF16)
                egl_s[c, h:h + 1, :] = jnp.broadcast_to(jnp.exp(glast), (1, LANES))
                items.append(dict(c=c, r0=r0, h=h, cs=cs, decay=decay, kb16=kb.astype(BF16),
                                  k16=k.astype(BF16), q16=q.astype(BF16),
                                  sol=jnp.concatenate([v * beta, kb * jnp.exp(gcol)], axis=1)))
        kk = [_dot_nt(t["kb16"], t["k16"]) for t in items]
        qk = [_dot_nt(t["q16"], t["k16"]) for t in items]
        for t, kk_i, qk_i in zip(items, kk, qk):
            attn_s[t["c"], t["h"]] = jnp.where(tri, qk_i * t["decay"], 0.0).astype(BF16)
            t["tm"] = -jnp.where(strict, kk_i * t["decay"], 0.0)
            t["p16"] = t["tm"].astype(BF16)
        sq = [_dot(t["p16"], t["p16"]) for t in items]
        for level in range(1, 6):
            for i, t in enumerate(items):
                t["pw"] = sq[i]
                t["p16"] = sq[i].astype(BF16)
            app = [_dot(t["p16"], t["tm"].astype(BF16)) for t in items]
            if level < 5:
                sq = [_dot(t["p16"], t["p16"]) for t in items]
            for i, t in enumerate(items):
                t["tm"] = t["tm"] + t["pw"] + app[i]
        corr = [_dot(t["tm"].astype(BF16), t["sol"].astype(BF16)) for t in items]
        for i, t in enumerate(items):
            sol = t["sol"] + corr[i]
            u_s[pl.ds(t["r0"], CHUNK), t["cs"]] = sol[:, :DN_DIM]
            w_s[pl.ds(t["r0"], CHUNK), t["cs"]] = sol[:, DN_DIM:].astype(BF16)
        return carry

    lax.fori_loop(0, s // (CHUNK * DN_PAIR), precompute, 0)

    def recurrence(c, carry):
        r0 = pl.multiple_of(c * CHUNK, CHUNK)
        heads = range(DN_HEADS)
        cols = [slice(h * LANES, (h + 1) * LANES) for h in heads]
        st = [state_s[h] for h in heads]
        st16 = [x.astype(BF16) for x in st]
        ws = [_dot(w_s[pl.ds(r0, CHUNK), cols[h]], st16[h]) for h in heads]
        qs = [_dot(qg_s[pl.ds(r0, CHUNK), cols[h]], st16[h]) for h in heads]
        vn16 = [(u_s[pl.ds(r0, CHUNK), cols[h]] - ws[h]).astype(BF16) for h in heads]
        av = [_dot(attn_s[c, h], vn16[h]) for h in heads]
        ks = [_dot_tn(kt_s[pl.ds(r0, CHUNK), cols[h]], vn16[h]) for h in heads]
        for h in heads:
            state_s[h] = st[h] * egl_s[c, h:h + 1, :] + ks[h]
            o = qs[h] + av[h]
            o = o * lax.rsqrt(jnp.mean(o * o, axis=-1, keepdims=True) + EPS)
            z = dn_ref[pl.ds(r0, CHUNK), 3 * DN_WIDTH + h * LANES:3 * DN_WIDTH + (h + 1) * LANES].astype(F32)
            o_ref[pl.ds(r0, CHUNK), cols[h]] = (o * normw_ref[...] * (z * _sigmoid(z))).astype(BF16)
        return carry

    lax.fori_loop(0, s // CHUNK, recurrence, 0)


def _deltanet(dn3, ab3, abt4, conv_w, alog, dtb, alogt, dtbt, norm_w):
    b, s, _ = dn3.shape
    nch = s // CHUNK
    return pl.pallas_call(
        _deltanet_kernel,
        grid=(b,),
        in_specs=[
            pl.BlockSpec((None, s, 4 * DN_WIDTH), lambda i: (i, 0, 0)),
            pl.BlockSpec((None, s, LANES), lambda i: (i, 0, 0)),
            pl.BlockSpec((None, nch, 8, CHUNK), lambda i: (i, 0, 0, 0)),
            _const_spec(conv_w.shape),
            _const_spec((1, LANES)),
            _const_spec((1, LANES)),
            _const_spec((8, 1)),
            _const_spec((8, 1)),
            _const_spec((1, DN_DIM)),
        ],
        out_specs=pl.BlockSpec((None, s, DN_WIDTH), lambda i: (i, 0, 0)),
        out_shape=jax.ShapeDtypeStruct((b, s, DN_WIDTH), BF16),
        scratch_shapes=[
            pltpu.VMEM((DN_HEADS, DN_DIM, DN_DIM), F32),
            pltpu.VMEM((s, DN_WIDTH), F32),
            pltpu.VMEM((s, DN_WIDTH), BF16),
            pltpu.VMEM((s, DN_WIDTH), BF16),
            pltpu.VMEM((s, DN_WIDTH), BF16),
            pltpu.VMEM((nch, DN_HEADS, CHUNK, CHUNK), BF16),
            pltpu.VMEM((nch, 8, LANES), F32),
        ],
        compiler_params=_params("arbitrary"),
        name="deltanet",
    )(dn3, ab3, abt4, conv_w, alog, dtb, alogt, dtbt, norm_w)


def _stickbreak_kernel(q_ref, k_ref, v_ref, qw_ref, kw_ref, m2_ref, hsum_ref, o_ref,
                       qn_s, kn_s, vm_s, carry_s, acc_s):
    s, width = q_ref.shape
    nh = width // SB_DIM
    group = carry_s.shape[0]
    lane = lax.broadcasted_iota(jnp.int32, (1, width), 1)
    head_lanes = [jnp.logical_and(lane >= h * SB_DIM, lane < (h + 1) * SB_DIM) for h in range(nh)]

    def head_norm(r0, x_ref, w):
        x = x_ref[pl.ds(r0, SB_BLOCK), :].astype(F32)
        sq = x * x
        hi = sq.astype(BF16)
        lo = (sq - hi.astype(F32)).astype(BF16)
        ms = (_dot(hi, hsum_ref[...]) + _dot(lo, hsum_ref[...])) * (1.0 / SB_DIM)
        return x * lax.rsqrt(ms + EPS) * w

    def norm_block(i, carry):
        r0 = pl.multiple_of(i * SB_BLOCK, SB_BLOCK)
        qn = head_norm(r0, q_ref, qw_ref[...]) * (SB_DIM ** -0.5)
        v = v_ref[pl.ds(r0, SB_BLOCK), :]
        for h in range(nh):
            qn_s[h, pl.ds(r0, SB_BLOCK), :] = jnp.where(head_lanes[h], qn, 0.0).astype(BF16)
            vm_s[i, h * SB_BLOCK:(h + 1) * SB_BLOCK, :] = jnp.where(head_lanes[h], v, jnp.zeros_like(v))
        kn_s[pl.ds(r0, SB_BLOCK), :] = head_norm(r0, k_ref, kw_ref[...]).astype(BF16)
        return carry

    lax.fori_loop(0, s // SB_BLOCK, norm_block, 0)

    row = lax.broadcasted_iota(jnp.int32, (SB_BLOCK, SB_BLOCK), 0)
    col = lax.broadcasted_iota(jnp.int32, (SB_BLOCK, SB_BLOCK), 1)
    causal = col < row

    def step(rows, key_blocks, valid):
        tiles = [(g, h) for g in range(group) for h in range(nh)]
        k16 = [kn_s[pl.ds(pl.multiple_of(kb * SB_BLOCK, SB_BLOCK), SB_BLOCK), :] for kb in key_blocks]
        z = [_dot_nt(qn_s[h, pl.ds(rows[g], SB_BLOCK), :], k16[g]) for g, h in tiles]
        sp = [_softplus(x) for x in z]
        if valid is None:
            log_fail = [jnp.where(causal, -x, 0.0) for x in sp]
        else:
            log_fail = [-x for x in sp]
        cs = [_dot(x.astype(BF16), m2_ref[...]) for x in log_fail]
        p = []
        for i, (g, h) in enumerate(tiles):
            w = jnp.exp(z[i] - sp[i] + carry_s[g, h] + cs[i][:, :SB_BLOCK])
            p.append((jnp.where(causal, w, 0.0) if valid is None else w).astype(BF16))
        pv = [_dot(jnp.concatenate(p[g * nh:(g + 1) * nh], axis=1), vm_s[key_blocks[g]])
              for g in range(group)]
        for g in range(group):
            acc = acc_s[g] + pv[g]
            acc_s[g] = acc if valid is None else jnp.where(valid[g], acc, acc_s[g])
        for i, (g, h) in enumerate(tiles):
            carry = carry_s[g, h] + cs[i][:, SB_BLOCK:]
            carry_s[g, h] = carry if valid is None else jnp.where(valid[g], carry, carry_s[g, h])

    def q_group(qg, carry):
        blocks = [qg * group + g for g in range(group)]
        rows = [pl.multiple_of(qb * SB_BLOCK, SB_BLOCK) for qb in blocks]
        carry_s[...] = jnp.zeros(carry_s.shape, F32)
        acc_s[...] = jnp.zeros(acc_s.shape, F32)
        step(rows, blocks, None)

        def cond(st):
            d, alive = st
            return jnp.logical_and(d <= blocks[-1], alive)

        def body(st):
            d, _ = st
            alive = jnp.max(carry_s[...]) > SB_UNDERFLOW
            step(rows, [jnp.maximum(qb - d, 0) for qb in blocks], [qb >= d for qb in blocks])
            return d + 1, alive

        lax.while_loop(cond, body, (jnp.int32(1), jnp.bool_(True)))
        for g in range(group):
            o_ref[pl.ds(rows[g], SB_BLOCK), :] = acc_s[g].astype(BF16)
        return carry

    lax.fori_loop(0, s // (SB_BLOCK * group), q_group, 0)


def _stickbreak(sb3, qw, kw, m2, hsum):
    b, s, _ = sb3.shape
    width = hsum.shape[0]
    nh = width // SB_DIM
    parts = SB_WIDTH // width
    group = min(SB_GROUP, s // SB_BLOCK)
    return pl.pallas_call(
        _stickbreak_kernel,
        grid=(b, parts),
        in_specs=[
            pl.BlockSpec((None, s, width), lambda i, j: (i, 0, j)),
            pl.BlockSpec((None, s, width), lambda i, j: (i, 0, parts + j)),
            pl.BlockSpec((None, s, width), lambda i, j: (i, 0, 2 * parts + j)),
            _const_spec((1, width)),
            _const_spec((1, width)),
            _const_spec(m2.shape),
            _const_spec(hsum.shape),
        ],
        out_specs=pl.BlockSpec((None, s, width), lambda i, j: (i, 0, j)),
        out_shape=jax.ShapeDtypeStruct((b, s, SB_WIDTH), BF16),
        scratch_shapes=[
            pltpu.VMEM((nh, s, width), BF16),
            pltpu.VMEM((s, width), BF16),
            pltpu.VMEM((s // SB_BLOCK, nh * SB_BLOCK, width), BF16),
            pltpu.VMEM((group, nh, SB_BLOCK, SB_BLOCK), F32),
            pltpu.VMEM((group, SB_BLOCK, width), F32),
        ],
        compiler_params=_params("arbitrary", "arbitrary"),
        name="stickbreak",
    )(sb3, sb3, sb3, qw, kw, m2, hsum)


def _memkv_kernel(mem_ref, nw_ref, wk_ref, wv_ref, knw_ref, k_ref, v_ref):
    n = _rms(mem_ref[...], nw_ref[...]).astype(BF16)
    k = _dot(n, wk_ref[...])
    for h in range(X_HEADS):
        cs = slice(h * X_DIM, (h + 1) * X_DIM)
        k_ref[:, cs] = _rms(k[:, cs], knw_ref[...]).astype(BF16)
    v_ref[...] = _dot(n, wv_ref[...]).astype(BF16)


def _memkv(mem, norm_w, wk, wv, k_norm_w):
    b, m, d = mem.shape
    return pl.pallas_call(
        _memkv_kernel,
        grid=(b,),
        in_specs=[
            pl.BlockSpec((None, m, d), lambda i: (i, 0, 0)),
            _const_spec((1, d)),
            _const_spec(wk.shape),
            _const_spec(wv.shape),
            _const_spec((1, X_DIM)),
        ],
        out_specs=[
            pl.BlockSpec((None, m, d), lambda i: (i, 0, 0)),
            pl.BlockSpec((None, m, d), lambda i: (i, 0, 0)),
        ],
        out_shape=[jax.ShapeDtypeStruct((b, m, d), BF16)] * 2,
        compiler_params=_params("arbitrary"),
        name="memkv",
    )(mem, norm_w, wk, wv, k_norm_w)


def _mid_kernel(x_ref, ydn_ref, ysb_ref, wout_ref, n2w_ref, wq_ref, qnw_ref, km_ref, vm_ref, wo_ref,
                n3w_ref, wrh_ref, wrl_ref, rb_ref, upper_ref,
                x2_ref, h3_ref, idx_ref, gate_ref, rank_ref, cnt_ref, count_s):
    tm = x_ref.shape[0]
    first_step = jnp.logical_and(pl.program_id(0) == 0, pl.program_id(1) == 0)

    @pl.when(first_step)
    def _():
        count_s[...] = jnp.zeros(count_s.shape, F32)

    sub = upper_ref.shape[0]
    subs = range(tm // sub)
    rows = [slice(i * sub, (i + 1) * sub) for i in subs]
    heads = range(X_HEADS)
    cols = [slice(h * X_DIM, (h + 1) * X_DIM) for h in heads]
    x1 = [x_ref[r, :] + _dot(ydn_ref[r, :], wout_ref[0:DN_WIDTH, :])
          + _dot(ysb_ref[r, :], wout_ref[DN_WIDTH:, :]) for r in rows]
    n2 = [_rms(v, n2w_ref[...]).astype(BF16) for v in x1]
    q = [_dot(v, wq_ref[...]) for v in n2]
    qh = [[(_rms(q[i][:, c], qnw_ref[...]) * (X_DIM ** -0.5)).astype(BF16) for c in cols] for i in subs]
    sc = [[_dot_nt(qh[i][h], km_ref[:, cols[h]]) for h in heads] for i in subs]
    ex = [[jnp.exp(sc[i][h] - jnp.max(sc[i][h], axis=-1, keepdims=True)) for h in heads] for i in subs]
    pr = [[(ex[i][h] / jnp.sum(ex[i][h], axis=-1, keepdims=True)).astype(BF16) for h in heads] for i in subs]
    oh = [[_dot(pr[i][h], vm_ref[:, cols[h]]).astype(BF16) for h in heads] for i in subs]
    x2 = [x1[i] + _dot(jnp.concatenate(oh[i], axis=1), wo_ref[...]) for i in subs]
    h3 = [_rms(v, n3w_ref[...]) for v in x2]
    hi = [v.astype(BF16) for v in h3]
    lo = [(h3[i] - hi[i].astype(F32)).astype(BF16) for i in subs]
    logits = [_dot_nt(wrh_ref[...], hi[i]) + _dot_nt(wrh_ref[...], lo[i]) + _dot_nt(wrl_ref[...], hi[i])
              + rb_ref[...] for i in subs]
    eid = lax.broadcasted_iota(jnp.int32, (N_EXPERTS, sub), 0).astype(F32)
    count = count_s[...]
    for i in subs:
        x2_ref[rows[i], :] = x2[i]
        h3_ref[rows[i], :] = h3[i]
        vals, ids = [], []
        cur = logits[i]
        for _ in range(TOP_K):
            m = jnp.max(cur, axis=0, keepdims=True)
            j = jnp.min(jnp.where(cur == m, eid, float(N_EXPERTS)), axis=0, keepdims=True)
            vals.append(m)
            ids.append(j)
            cur = jnp.where(eid == j, -jnp.inf, cur)
        exps = [jnp.exp(v - vals[0]) for v in vals]
        denom = exps[0] + exps[1] + exps[2] + exps[3]
        onehot = jnp.zeros((N_EXPERTS, sub), F32)
        for j in ids:
            onehot = onehot + jnp.where(eid == j, 1.0, 0.0)
        before = count + _dot(onehot.astype(BF16), upper_ref[...])
        for k in range(TOP_K):
            idx_ref[k:k + 1, rows[i]] = ids[k].astype(jnp.int32)
            gate_ref[k:k + 1, rows[i]] = exps[k] / denom
            rank_ref[k:k + 1, rows[i]] = jnp.sum(jnp.where(eid == ids[k], before, 0.0), axis=0,
                                                 keepdims=True).astype(jnp.int32)
        count = count + jnp.sum(onehot, axis=1, keepdims=True)
    count_s[...] = count
    cnt_ref[...] = jnp.broadcast_to(count, cnt_ref.shape)


def _mid(x3, ydn3, ysb3, w_out, n2w, wq, qnw, k_mem, v_mem, wo, n3w, wr_hi, wr_lo, rb, upper, tm):
    b, s, d = x3.shape
    n = b * s
    nt = s // tm
    m = k_mem.shape[1]
    tok = lambda i, j: (0, i * nt + j)
    return pl.pallas_call(
        _mid_kernel,
        grid=(b, nt),
        in_specs=[
            pl.BlockSpec((None, tm, d), lambda i, j: (i, j, 0)),
            pl.BlockSpec((None, tm, DN_WIDTH), lambda i, j: (i, j, 0)),
            pl.BlockSpec((None, tm, SB_WIDTH), lambda i, j: (i, j, 0)),
            _const_spec(w_out.shape),
            _const_spec((1, d)),
            _const_spec(wq.shape),
            _const_spec((1, X_DIM)),
            pl.BlockSpec((None, m, d), lambda i, j: (i, 0, 0)),
            pl.BlockSpec((None, m, d), lambda i, j: (i, 0, 0)),
            _const_spec(wo.shape),
            _const_spec((1, d)),
            _const_spec(wr_hi.shape),
            _const_spec(wr_lo.shape),
            _const_spec((N_EXPERTS, 1)),
            _const_spec(upper.shape),
        ],
        out_specs=[
            pl.BlockSpec((None, tm, d), lambda i, j: (i, j, 0)),
            pl.BlockSpec((None, tm, d), lambda i, j: (i, j, 0)),
            pl.BlockSpec((TOP_K, tm), tok),
            pl.BlockSpec((TOP_K, tm), tok),
            pl.BlockSpec((TOP_K, tm), tok),
            _const_spec((N_EXPERTS, LANES)),
        ],
        out_shape=[
            jax.ShapeDtypeStruct((b, s, d), F32),
            jax.ShapeDtypeStruct((b, s, d), F32),
            jax.ShapeDtypeStruct((TOP_K, n), jnp.int32),
            jax.ShapeDtypeStruct((TOP_K, n), F32),
            jax.ShapeDtypeStruct((TOP_K, n), jnp.int32),
            jax.ShapeDtypeStruct((N_EXPERTS, LANES), F32),
        ],
        scratch_shapes=[pltpu.VMEM((N_EXPERTS, 1), F32)],
        compiler_params=_params("arbitrary", "arbitrary"),
        name="mid",
    )(x3, ydn3, ysb3, w_out, n2w, wq, qnw, k_mem, v_mem, wo, n3w, wr_hi, wr_lo, rb, upper)


def _row_copy(src_ref, src_row, dst_ref, dst_row, sem):
    return pltpu.make_async_copy(src_ref.at[pl.ds(src_row, 1), :], dst_ref.at[pl.ds(dst_row, 1), :], sem)


def _dispatch_kernel(pend_ref, padded_ref, dest_ref, h_ref, xpad_ref, zero_s, sem, zsem):
    tf = h_ref.shape[0]

    @pl.when(pl.program_id(0) == 0)
    def _():
        zero_s[...] = jnp.zeros(zero_s.shape, F32)

        def last_block(e):
            start = pl.multiple_of(pend_ref[e] - EXPERT_BLOCK, EXPERT_BLOCK)
            return pltpu.make_async_copy(zero_s, xpad_ref.at[pl.ds(start, EXPERT_BLOCK), :], zsem)

        for e in range(N_EXPERTS):
            @pl.when(padded_ref[e] > 0)
            def _(e=e):
                last_block(e).start()
        for e in range(N_EXPERTS):
            @pl.when(padded_ref[e] > 0)
            def _(e=e):
                last_block(e).wait()

    for k in range(TOP_K):
        for t in range(tf):
            _row_copy(h_ref, t, xpad_ref, dest_ref[0, k * tf + t], sem).start(priority=t % 2)
    for k in range(TOP_K):
        pltpu.make_async_copy(h_ref, xpad_ref.at[pl.ds(0, tf), :], sem).wait()


def _dispatch(pad_ends, padded, dest_tiles, h2d, rows, tf):
    n, d = h2d.shape
    return pl.pallas_call(
        _dispatch_kernel,
        grid_spec=pltpu.PrefetchScalarGridSpec(
            num_scalar_prefetch=2,
            grid=(n // tf,),
            in_specs=[
                pl.BlockSpec((None, 1, TOP_K * tf), lambda i, *_: (i, 0, 0), memory_space=pltpu.SMEM),
                pl.BlockSpec((tf, d), lambda i, *_: (i, 0)),
            ],
            out_specs=pl.BlockSpec(memory_space=pl.ANY),
            scratch_shapes=[pltpu.VMEM((EXPERT_BLOCK, d), F32), pltpu.SemaphoreType.DMA(()),
                            pltpu.SemaphoreType.DMA(())],
        ),
        out_shape=jax.ShapeDtypeStruct((rows, d), F32),
        compiler_params=_params("arbitrary"),
        name="dispatch",
    )(pad_ends, padded, dest_tiles, h2d)


def _experts_kernel(be_ref, nb_ref, x_ref, wg_ref, bg_ref, wu_ref, bu_ref, wd_ref, bd_ref, y_ref,
                    wg_s, wu_s, wd_s):
    j = pl.program_id(0)
    used = j < nb_ref[0]
    new_expert = jnp.logical_or(j == 0, be_ref[j] != be_ref[jnp.maximum(j - 1, 0)])

    @pl.when(jnp.logical_and(used, new_expert))
    def _():
        wg_s[...] = wg_ref[...].astype(BF16)
        wu_s[...] = wu_ref[...].astype(BF16)
        wd_s[...] = wd_ref[...].astype(BF16)

    @pl.when(used)
    def _():
        x = x_ref[...].astype(BF16)
        gate = jnp.minimum(_dot(x, wg_s[...]) + bg_ref[...], SWIGLU_LIMIT)
        up = jnp.clip(_dot(x, wu_s[...]) + bu_ref[...], -SWIGLU_LIMIT, SWIGLU_LIMIT)
        act = (up + 1.0) * gate * _sigmoid(gate * SWIGLU_ALPHA)
        y_ref[...] = _dot(act.astype(BF16), wd_s[...]) + bd_ref[...]


def _experts(block_e, nb_used, x_pad, wg, bg, wu, bu, wd, bd):
    rows, d = x_pad.shape
    nblk = rows // EXPERT_BLOCK
    dff = wg.shape[2]
    row_blk = lambda j, be, nb: (jnp.minimum(j, nb[0] - 1), 0)
    w_blk = lambda j, be, nb: (be[j], 0, 0)
    return pl.pallas_call(
        _experts_kernel,
        grid_spec=pltpu.PrefetchScalarGridSpec(
            num_scalar_prefetch=2,
            grid=(nblk,),
            in_specs=[
                pl.BlockSpec((EXPERT_BLOCK, d), row_blk),
                pl.BlockSpec((None, d, dff), w_blk),
                pl.BlockSpec((None, 1, dff), w_blk),
                pl.BlockSpec((None, d, dff), w_blk),
                pl.BlockSpec((None, 1, dff), w_blk),
                pl.BlockSpec((None, dff, d), w_blk),
                pl.BlockSpec((None, 1, d), w_blk),
            ],
            out_specs=pl.BlockSpec((EXPERT_BLOCK, d), row_blk),
            scratch_shapes=[pltpu.VMEM((d, dff), BF16), pltpu.VMEM((d, dff), BF16),
                            pltpu.VMEM((dff, d), BF16)],
        ),
        out_shape=jax.ShapeDtypeStruct((rows, d), F32),
        compiler_params=pltpu.CompilerParams(dimension_semantics=("arbitrary",),
                                             vmem_limit_bytes=V7X_EXPERTS_VMEM_LIMIT),
        name="experts",
    )(block_e, nb_used, x_pad, wg, bg, wu, bu, wd, bd)


def _combine_kernel(dest_ref, dnext_ref, x2_ref, gate_ref, ypad_ref, o_ref, buf, sem):
    th = x2_ref.shape[0]
    i = pl.program_id(0)
    slot = i % 2

    def gather(dref, s):
        for k in range(TOP_K):
            for t in range(th):
                _row_copy(ypad_ref, dref[0, k * th + t], buf.at[s, k], t, sem.at[s]).start(priority=t % 2)

    @pl.when(i == 0)
    def _():
        gather(dest_ref, 0)

    @pl.when(i + 1 < pl.num_programs(0))
    def _():
        gather(dnext_ref, 1 - slot)

    for k in range(TOP_K):
        pltpu.make_async_copy(ypad_ref.at[pl.ds(0, th), :], buf.at[slot, k], sem.at[slot]).wait()
    g = gate_ref[...]
    out = x2_ref[...]
    for k in range(TOP_K):
        out = out + buf[slot, k] * g[:, k:k + 1]
    o_ref[...] = out


def _combine(dest_tiles, x2d, gates_nk, y_pad, th):
    n, d = x2d.shape
    nt = n // th
    return pl.pallas_call(
        _combine_kernel,
        grid=(nt,),
        in_specs=[
            pl.BlockSpec((None, 1, TOP_K * th), lambda i: (i, 0, 0), memory_space=pltpu.SMEM),
            pl.BlockSpec((None, 1, TOP_K * th), lambda i: (jnp.minimum(i + 1, nt - 1), 0, 0),
                         memory_space=pltpu.SMEM),
            pl.BlockSpec((th, d), lambda i: (i, 0)),
            pl.BlockSpec((th, TOP_K), lambda i: (i, 0)),
            pl.BlockSpec(memory_space=pl.ANY),
        ],
        out_specs=pl.BlockSpec((th, d), lambda i: (i, 0)),
        out_shape=jax.ShapeDtypeStruct((n, d), F32),
        scratch_shapes=[pltpu.VMEM((2, TOP_K, th, d), F32), pltpu.SemaphoreType.DMA((2,))],
        compiler_params=_params("arbitrary"),
        name="combine",
    )(dest_tiles, dest_tiles, x2d, gates_nk, y_pad)


def _tile_dest(dest, t):
    k, n = dest.shape
    return dest.reshape(k, n // t, t).transpose(1, 0, 2).reshape(n // t, 1, k * t)


def _layer(x, mem, norm1_w, w_in, conv_w, a_log, dt_bias, dn_norm_w, sb_q_norm_w, sb_k_norm_w, w_out,
           norm2_w, mem_norm_w, xq_w, xk_w, xv_w, xq_norm_w, xk_norm_w, xo_w, norm3_w, router_w,
           router_b, w_gate, b_gate, w_up, b_up, w_down, b_down):
    b, s, d = x.shape
    n = b * s
    tm_proj = min(512, s)
    tm_mid = min(512, s)
    sub_mid = min(256, s)
    t_moe = min(256, s)

    o_dn, o_ab, o_sb = 4 * DN_WIDTH, 4 * DN_WIDTH + 2 * DN_HEADS, 4 * DN_WIDTH + 2 * DN_HEADS
    w_dn = w_in[:, :o_dn].astype(BF16)
    w_ab_f = w_in[:, o_dn:o_ab]
    w_ab = jnp.pad(w_ab_f, ((0, 0), (0, LANES - 2 * DN_HEADS))).astype(BF16)
    w_abt = w_ab_f.T.astype(BF16)
    w_sb = w_in[:, o_sb:].astype(BF16)
    row = lambda v: v.reshape(1, -1).astype(F32)

    dn, sb, ab, abt = _in_proj(x.reshape(n, d), row(norm1_w), w_dn, w_sb, w_ab, w_abt, tm_proj)

    abt4 = abt.reshape(8, n // CHUNK, CHUNK).transpose(1, 0, 2).reshape(b, s // CHUNK, 8, CHUNK)
    pad_lane = lambda v: jnp.pad(v.astype(F32), (0, LANES - v.shape[0])).reshape(1, LANES)
    pad_col = lambda v: jnp.pad(v.astype(F32), (0, 8 - v.shape[0])).reshape(8, 1)
    y_dn = _deltanet(dn.reshape(b, s, -1), ab.reshape(b, s, LANES), abt4, conv_w.astype(F32),
                     pad_lane(a_log), pad_lane(dt_bias), pad_col(a_log), pad_col(dt_bias), row(dn_norm_w))

    ii = jnp.arange(SB_BLOCK)
    m2 = jnp.concatenate([(ii[:, None] > ii[None, :]).astype(BF16),
                          jnp.ones((SB_BLOCK, SB_BLOCK), BF16)], axis=1)
    sb_heads = SB_LANES // SB_DIM
    hh = jnp.arange(SB_LANES) // SB_DIM
    hsum = (hh[:, None] == hh[None, :]).astype(BF16)
    y_sb = _stickbreak(sb.reshape(b, s, -1), row(jnp.tile(sb_q_norm_w, sb_heads)),
                       row(jnp.tile(sb_k_norm_w, sb_heads)), m2, hsum)

    k_mem, v_mem = _memkv(mem, row(mem_norm_w), xk_w.astype(BF16), xv_w.astype(BF16), row(xk_norm_w))

    wr_t = router_w.T.astype(F32)
    wr_hi = wr_t.astype(BF16)
    wr_lo = (wr_t - wr_hi.astype(F32)).astype(BF16)
    jj = jnp.arange(sub_mid)
    upper = (jj[:, None] < jj[None, :]).astype(BF16)
    x2, h3, idx, gates, rank, cnt = _mid(
        x, y_dn, y_sb, w_out.astype(BF16), row(norm2_w), xq_w.astype(BF16), row(xq_norm_w), k_mem, v_mem,
        xo_w.astype(BF16), row(norm3_w), wr_hi, wr_lo, router_b.reshape(N_EXPERTS, 1).astype(F32), upper,
        tm_mid)

    counts = cnt[:, 0].astype(jnp.int32)
    padded = (counts + EXPERT_BLOCK - 1) // EXPERT_BLOCK * EXPERT_BLOCK
    pad_ends = jnp.cumsum(padded)
    pad_starts = pad_ends - padded
    sel = idx[:, :, None] == jnp.arange(N_EXPERTS, dtype=jnp.int32)[None, None, :]
    dest = rank + jnp.sum(jnp.where(sel, pad_starts[None, None, :], 0), axis=-1)
    n_blocks = -(-n * TOP_K // EXPERT_BLOCK) + N_EXPERTS
    nb_used = (pad_ends[-1] // EXPERT_BLOCK).astype(jnp.int32)
    blk = jnp.minimum(jnp.arange(n_blocks, dtype=jnp.int32), nb_used - 1) * EXPERT_BLOCK
    block_e = jnp.minimum(jnp.sum(pad_ends[None, :] <= blk[:, None], axis=1), N_EXPERTS - 1).astype(jnp.int32)
    dest_tiles = _tile_dest(dest, t_moe)

    x_pad = _dispatch(pad_ends.astype(jnp.int32), padded.astype(jnp.int32), dest_tiles, h3.reshape(n, d),
                      n_blocks * EXPERT_BLOCK, t_moe)
    bias = lambda v: v.reshape(N_EXPERTS, 1, -1).astype(F32)
    y_pad = _experts(block_e, nb_used.reshape(1), x_pad, w_gate.astype(F32), bias(b_gate),
                     w_up.astype(F32), bias(b_up), w_down.astype(F32), bias(b_down))
    out = _combine(dest_tiles, x2.reshape(n, d), gates.T, y_pad, t_moe)
    return out.reshape(b, s, d)


def kernel(x, mem, norm1_w, w_in, conv_w, a_log, dt_bias, dn_norm_w, sb_q_norm_w, sb_k_norm_w, w_out,
           norm2_w, mem_norm_w, xq_w, xk_w, xv_w, xq_norm_w, xk_norm_w, xo_w, norm3_w, router_w,
           router_b, w_gate, b_gate, w_up, b_up, w_down, b_down):
    depth = w_in.shape[0]
    for l in range(depth):
        x = _layer(x, mem, norm1_w[l], w_in[l], conv_w[l], a_log[l], dt_bias[l], dn_norm_w[l],
                   sb_q_norm_w[l], sb_k_norm_w[l], w_out[l], norm2_w[l], mem_norm_w[l], xq_w[l], xk_w[l],
                   xv_w[l], xq_norm_w[l], xk_norm_w[l], xo_w[l], norm3_w[l], router_w[l], router_b[l],
                   w_gate[l], b_gate[l], w_up[l], b_up[l], w_down[l], b_down[l])
    return x
```

```python
import functools

import jax
import jax.numpy as jnp
from jax import lax
from jax.experimental import pallas as pl
from jax.experimental.pallas import tpu as pltpu

F32 = jnp.float32
BF16 = jnp.bfloat16

EPS = 1e-6
CHUNK = 64
DN_HEADS = 4
DN_DIM = 128
DN_WIDTH = DN_HEADS * DN_DIM
CONV_WIDTH = 4
DN_PAIR = 2
HALO = 16
SB_HEADS = 8
SB_DIM = 64
SB_WIDTH = SB_HEADS * SB_DIM
SB_BLOCK = 128
SB_GROUP = 4
SB_LANES = 256
X_HEADS = 4
X_DIM = 256
N_EXPERTS = 32
TOP_K = 4
EXPERT_BLOCK = 512
PROJ_SUBS = 2
EXPERT_SUBS = 2
SWIGLU_ALPHA = 1.702
SWIGLU_LIMIT = 7.0
LANES = 128
SB_UNDERFLOW = -88.0
V7X_VMEM_LIMIT = 48 * 1024 * 1024
V7X_EXPERTS_VMEM_LIMIT = 58 * 1024 * 1024


def _dot(a, b):
    return jnp.dot(a, b, preferred_element_type=F32)


def _dot_nt(a, b):
    return lax.dot_general(a, b, (((1,), (1,)), ((), ())), preferred_element_type=F32)


def _dot_tn(a, b):
    return lax.dot_general(a, b, (((0,), (0,)), ((), ())), preferred_element_type=F32)


def _split3(x):
    hi = x.astype(BF16)
    r = x - hi.astype(F32)
    mid = r.astype(BF16)
    return hi, mid, (r - mid.astype(F32)).astype(BF16)


def _softplus(x):
    return jnp.maximum(x, 0.0) + jnp.log(1.0 + jnp.exp(-jnp.abs(x)))


def _sigmoid(x):
    return 1.0 / (1.0 + jnp.exp(-x))


def _rms(x, w):
    return x * lax.rsqrt(jnp.mean(x * x, axis=-1, keepdims=True) + EPS) * w


def _params(*sem):
    return pltpu.CompilerParams(dimension_semantics=sem, vmem_limit_bytes=V7X_VMEM_LIMIT)


def _const_spec(shape):
    nd = len(shape)
    return pl.BlockSpec(shape, lambda *_: (0,) * nd)


def _in_proj_kernel(x_ref, nw_ref, wdn_ref, wsb_ref, wab_ref, wabt_ref,
                    dn_ref, sb_ref, ab_ref, abt_ref):
    tm = x_ref.shape[0]
    sub = tm // PROJ_SUBS
    rows = [slice(i * sub, (i + 1) * sub) for i in range(PROJ_SUBS)]
    n = [_rms(x_ref[r, :], nw_ref[...]).astype(BF16) for r in rows]
    for r, v in zip(rows, n):
        dn_ref[r, :] = _dot(v, wdn_ref[...]).astype(BF16)
    for r, v in zip(rows, n):
        sb_ref[r, :] = _dot(v, wsb_ref[...]).astype(BF16)
    for r, v in zip(rows, n):
        ab_ref[r, :] = _dot(v, wab_ref[...])
        abt_ref[:, r] = _dot_nt(wabt_ref[...], v)


def _in_proj(x2d, norm_w, w_dn, w_sb, w_ab, w_abt, tm):
    n, d = x2d.shape
    return pl.pallas_call(
        _in_proj_kernel,
        grid=(n // tm,),
        in_specs=[
            pl.BlockSpec((tm, d), lambda i: (i, 0)),
            _const_spec((1, d)),
            _const_spec(w_dn.shape),
            _const_spec(w_sb.shape),
            _const_spec(w_ab.shape),
            _const_spec(w_abt.shape),
        ],
        out_specs=[
            pl.BlockSpec((tm, w_dn.shape[1]), lambda i: (i, 0)),
            pl.BlockSpec((tm, w_sb.shape[1]), lambda i: (i, 0)),
            pl.BlockSpec((tm, LANES), lambda i: (i, 0)),
            pl.BlockSpec((8, tm), lambda i: (0, i)),
        ],
        out_shape=[
            jax.ShapeDtypeStruct((n, w_dn.shape[1]), BF16),
            jax.ShapeDtypeStruct((n, w_sb.shape[1]), BF16),
            jax.ShapeDtypeStruct((n, LANES), F32),
            jax.ShapeDtypeStruct((8, n), F32),
        ],
        compiler_params=_params("arbitrary"),
        name="in_proj",
    )(x2d, norm_w, w_dn, w_sb, w_ab, w_abt)


def _deltanet_kernel(dn_ref, ab_ref, abt_ref, convw_ref, alog_ref, dtb_ref, alogt_ref, dtbt_ref,
                     normw_ref, o_ref, state_s, u_s, w_s, kt_s, qg_s, attn_s, egl_s):
    s = dn_ref.shape[0]
    state_s[...] = jnp.zeros(state_s.shape, F32)

    def conv_silu(r0, c, part, h, l2):
        cs = slice(part * DN_WIDTH + h * LANES, part * DN_WIDTH + (h + 1) * LANES)
        prev0 = pl.multiple_of(jnp.maximum(r0 - HALO, 0), HALO)
        prev = dn_ref[pl.ds(prev0, HALO), cs].astype(F32)
        prev = jnp.where(c > 0, prev, 0.0)
        win = jnp.concatenate([prev, dn_ref[pl.ds(r0, CHUNK), cs].astype(F32)], axis=0)
        w = convw_ref[:, cs]
        y = w[0:1, :] * win[HALO - 3:HALO - 3 + CHUNK, :]
        for i in range(1, CONV_WIDTH):
            y = y + w[i:i + 1, :] * win[HALO - 3 + i:HALO - 3 + i + CHUNK, :]
        y = y * _sigmoid(y)
        if l2:
            y = y * lax.rsqrt(jnp.sum(y * y, axis=-1, keepdims=True) + EPS)
        return y

    row = lax.broadcasted_iota(jnp.int32, (CHUNK, CHUNK), 0)
    col = lax.broadcasted_iota(jnp.int32, (CHUNK, CHUNK), 1)
    tri = row >= col
    strict = row > col
    tril16 = jnp.where(tri, 1.0, 0.0).astype(BF16)
    triu16 = jnp.where(row <= col, 1.0, 0.0).astype(BF16)
    neg_a_col = -jnp.exp(alog_ref[...])
    neg_a_row = -jnp.exp(alogt_ref[...])
    scale = DN_DIM ** -0.5

    def precompute(it, carry):
        items = []
        for ci in range(DN_PAIR):
            c = it * DN_PAIR + ci
            r0 = pl.multiple_of(c * CHUNK, CHUNK)
            ab = ab_ref[pl.ds(r0, CHUNK), :]
            g_col = neg_a_col * _softplus(ab + dtb_ref[...])
            gc_col_all = sum(_dot(tril16, p) for p in _split3(g_col))
            beta_all = _sigmoid(ab)
            abt = abt_ref[c]
            g_row = neg_a_row * _softplus(abt + dtbt_ref[...])
            gc_row_all = sum(_dot(p, triu16) for p in _split3(g_row))
            for h in range(DN_HEADS):
                q = conv_silu(r0, c, 0, h, True) * scale
                k = conv_silu(r0, c, 1, h, True)
                v = conv_silu(r0, c, 2, h, False)
                gcol = gc_col_all[:, h:h + 1]
                grow = gc_row_all[h:h + 1, :]
                beta = beta_all[:, DN_HEADS + h:DN_HEADS + h + 1]
                glast = gcol[CHUNK - 1:CHUNK, :]
                decay = jnp.where(tri, jnp.exp(jnp.where(tri, gcol - grow, 0.0)), 0.0)
                kb = k * beta
                cs = slice(h * LANES, (h + 1) * LANES)
                kt_s[pl.ds(r0, CHUNK), cs] = (k * jnp.exp(glast - gcol)).astype(BF16)
                qg_s[pl.ds(r0, CHUNK), cs] = (q * jnp.exp(gcol)).astype(BF16)
                egl_s[c, h:h + 1, :] = jnp.broadcast_to(jnp.exp(glast), (1, LANES))
                items.append(dict(c=c, r0=r0, h=h, cs=cs, decay=decay, kb16=kb.astype(BF16),
                                  k16=k.astype(BF16), q16=q.astype(BF16),
                                  sol=jnp.concatenate([v * beta, kb * jnp.exp(gcol)], axis=1)))
        kk = [_dot_nt(t["kb16"], t["k16"]) for t in items]
        qk = [_dot_nt(t["q16"], t["k16"]) for t in items]
        for t, kk_i, qk_i in zip(items, kk, qk):
            attn_s[t["c"], t["h"]] = jnp.where(tri, qk_i * t["decay"], 0.0).astype(BF16)
            t["tm"] = -jnp.where(strict, kk_i * t["decay"], 0.0)
            t["p16"] = t["tm"].astype(BF16)
        sq = [_dot(t["p16"], t["p16"]) for t in items]
        for level in range(1, 6):
            for i, t in enumerate(items):
                t["pw"] = sq[i]
                t["p16"] = sq[i].astype(BF16)
            app = [_dot(t["p16"], t["tm"].astype(BF16)) for t in items]
            if level < 5:
                sq = [_dot(t["p16"], t["p16"]) for t in items]
            for i, t in enumerate(items):
                t["tm"] = t["tm"] + t["pw"] + app[i]
        corr = [_dot(t["tm"].astype(BF16), t["sol"].astype(BF16)) for t in items]
        for i, t in enumerate(items):
            sol = t["sol"] + corr[i]
            u_s[pl.ds(t["r0"], CHUNK), t["cs"]] = sol[:, :DN_DIM]
            w_s[pl.ds(t["r0"], CHUNK), t["cs"]] = sol[:, DN_DIM:].astype(BF16)
        return carry

    lax.fori_loop(0, s // (CHUNK * DN_PAIR), precompute, 0)

    def recurrence(c, carry):
        r0 = pl.multiple_of(c * CHUNK, CHUNK)
        heads = range(DN_HEADS)
        cols = [slice(h * LANES, (h + 1) * LANES) for h in heads]
        st = [state_s[h] for h in heads]
        st16 = [x.astype(BF16) for x in st]
        ws = [_dot(w_s[pl.ds(r0, CHUNK), cols[h]], st16[h]) for h in heads]
        qs = [_dot(qg_s[pl.ds(r0, CHUNK), cols[h]], st16[h]) for h in heads]
        vn16 = [(u_s[pl.ds(r0, CHUNK), cols[h]] - ws[h]).astype(BF16) for h in heads]
        av = [_dot(attn_s[c, h], vn16[h]) for h in heads]
        ks = [_dot_tn(kt_s[pl.ds(r0, CHUNK), cols[h]], vn16[h]) for h in heads]
        for h in heads:
            state_s[h] = st[h] * egl_s[c, h:h + 1, :] + ks[h]
            o = qs[h] + av[h]
            o = o * lax.rsqrt(jnp.mean(o * o, axis=-1, keepdims=True) + EPS)
            z = dn_ref[pl.ds(r0, CHUNK), 3 * DN_WIDTH + h * LANES:3 * DN_WIDTH + (h + 1) * LANES].astype(F32)
            o_ref[pl.ds(r0, CHUNK), cols[h]] = (o * normw_ref[...] * (z * _sigmoid(z))).astype(BF16)
        return carry

    lax.fori_loop(0, s // CHUNK, recurrence, 0)


def _deltanet(dn3, ab3, abt4, conv_w, alog, dtb, alogt, dtbt, norm_w):
    b, s, _ = dn3.shape
    nch = s // CHUNK
    return pl.pallas_call(
        _deltanet_kernel,
        grid=(b,),
        in_specs=[
            pl.BlockSpec((None, s, 4 * DN_WIDTH), lambda i: (i, 0, 0)),
            pl.BlockSpec((None, s, LANES), lambda i: (i, 0, 0)),
            pl.BlockSpec((None, nch, 8, CHUNK), lambda i: (i, 0, 0, 0)),
            _const_spec(conv_w.shape),
            _const_spec((1, LANES)),
            _const_spec((1, LANES)),
            _const_spec((8, 1)),
            _const_spec((8, 1)),
            _const_spec((1, DN_DIM)),
        ],
        out_specs=pl.BlockSpec((None, s, DN_WIDTH), lambda i: (i, 0, 0)),
        out_shape=jax.ShapeDtypeStruct((b, s, DN_WIDTH), BF16),
        scratch_shapes=[
            pltpu.VMEM((DN_HEADS, DN_DIM, DN_DIM), F32),
            pltpu.VMEM((s, DN_WIDTH), F32),
            pltpu.VMEM((s, DN_WIDTH), BF16),
            pltpu.VMEM((s, DN_WIDTH), BF16),
            pltpu.VMEM((s, DN_WIDTH), BF16),
            pltpu.VMEM((nch, DN_HEADS, CHUNK, CHUNK), BF16),
            pltpu.VMEM((nch, 8, LANES), F32),
        ],
        compiler_params=_params("arbitrary"),
        name="deltanet",
    )(dn3, ab3, abt4, conv_w, alog, dtb, alogt, dtbt, norm_w)


def _stickbreak_kernel(q_ref, k_ref, v_ref, qw_ref, kw_ref, m2_ref, hsum_ref, o_ref,
                       qn_s, kn_s, vm_s, carry_s, acc_s):
    s, width = q_ref.shape
    nh = width // SB_DIM
    group = carry_s.shape[0]
    lane = lax.broadcasted_iota(jnp.int32, (1, width), 1)
    head_lanes = [jnp.logical_and(lane >= h * SB_DIM, lane < (h + 1) * SB_DIM) for h in range(nh)]

    def head_norm(r0, x_ref, w):
        x = x_ref[pl.ds(r0, SB_BLOCK), :].astype(F32)
        sq = x * x
        hi = sq.astype(BF16)
        lo = (sq - hi.astype(F32)).astype(BF16)
        ms = (_dot(hi, hsum_ref[...]) + _dot(lo, hsum_ref[...])) * (1.0 / SB_DIM)
        return x * lax.rsqrt(ms + EPS) * w

    def norm_block(i, carry):
        r0 = pl.multiple_of(i * SB_BLOCK, SB_BLOCK)
        qn = head_norm(r0, q_ref, qw_ref[...]) * (SB_DIM ** -0.5)
        v = v_ref[pl.ds(r0, SB_BLOCK), :]
        for h in range(nh):
            qn_s[h, pl.ds(r0, SB_BLOCK), :] = jnp.where(head_lanes[h], qn, 0.0).astype(BF16)
            vm_s[i, h * SB_BLOCK:(h + 1) * SB_BLOCK, :] = jnp.where(head_lanes[h], v, jnp.zeros_like(v))
        kn_s[pl.ds(r0, SB_BLOCK), :] = head_norm(r0, k_ref, kw_ref[...]).astype(BF16)
        return carry

    lax.fori_loop(0, s // SB_BLOCK, norm_block, 0)

    row = lax.broadcasted_iota(jnp.int32, (SB_BLOCK, SB_BLOCK), 0)
    col = lax.broadcasted_iota(jnp.int32, (SB_BLOCK, SB_BLOCK), 1)
    causal = col < row

    def step(rows, key_blocks, valid):
        tiles = [(g, h) for g in range(group) for h in range(nh)]
        k16 = [kn_s[pl.ds(pl.multiple_of(kb * SB_BLOCK, SB_BLOCK), SB_BLOCK), :] for kb in key_blocks]
        z = [_dot_nt(qn_s[h, pl.ds(rows[g], SB_BLOCK), :], k16[g]) for g, h in tiles]
        sp = [_softplus(x) for x in z]
        if valid is None:
            log_fail = [jnp.where(causal, -x, 0.0) for x in sp]
        else:
            log_fail = [-x for x in sp]
        cs = [_dot(x.astype(BF16), m2_ref[...]) for x in log_fail]
        p = []
        for i, (g, h) in enumerate(tiles):
            w = jnp.exp(z[i] - sp[i] + carry_s[g, h] + cs[i][:, :SB_BLOCK])
            p.append((jnp.where(causal, w, 0.0) if valid is None else w).astype(BF16))
        pv = [_dot(jnp.concatenate(p[g * nh:(g + 1) * nh], axis=1), vm_s[key_blocks[g]])
              for g in range(group)]
        for g in range(group):
            acc = acc_s[g] + pv[g]
            acc_s[g] = acc if valid is None else jnp.where(valid[g], acc, acc_s[g])
        for i, (g, h) in enumerate(tiles):
            carry = carry_s[g, h] + cs[i][:, SB_BLOCK:]
            carry_s[g, h] = carry if valid is None else jnp.where(valid[g], carry, carry_s[g, h])

    def q_group(qg, carry):
        blocks = [qg * group + g for g in range(group)]
        rows = [pl.multiple_of(qb * SB_BLOCK, SB_BLOCK) for qb in blocks]
        carry_s[...] = jnp.zeros(carry_s.shape, F32)
        acc_s[...] = jnp.zeros(acc_s.shape, F32)
        step(rows, blocks, None)

        def cond(st):
            d, alive = st
            return jnp.logical_and(d <= blocks[-1], alive)

        def body(st):
            d, _ = st
            alive = jnp.max(carry_s[...]) > SB_UNDERFLOW
            step(rows, [jnp.maximum(qb - d, 0) for qb in blocks], [qb >= d for qb in blocks])
            return d + 1, alive

        lax.while_loop(cond, body, (jnp.int32(1), jnp.bool_(True)))
        for g in range(group):
            o_ref[pl.ds(rows[g], SB_BLOCK), :] = acc_s[g].astype(BF16)
        return carry

    lax.fori_loop(0, s // (SB_BLOCK * group), q_group, 0)


def _stickbreak(sb3, qw, kw, m2, hsum):
    b, s, _ = sb3.shape
    width = hsum.shape[0]
    nh = width // SB_DIM
    parts = SB_WIDTH // width
    group = min(SB_GROUP, s // SB_BLOCK)
    return pl.pallas_call(
        _stickbreak_kernel,
        grid=(b, parts),
        in_specs=[
            pl.BlockSpec((None, s, width), lambda i, j: (i, 0, j)),
            pl.BlockSpec((None, s, width), lambda i, j: (i, 0, parts + j)),
            pl.BlockSpec((None, s, width), lambda i, j: (i, 0, 2 * parts + j)),
            _const_spec((1, width)),
            _const_spec((1, width)),
            _const_spec(m2.shape),
            _const_spec(hsum.shape),
        ],
        out_specs=pl.BlockSpec((None, s, width), lambda i, j: (i, 0, j)),
        out_shape=jax.ShapeDtypeStruct((b, s, SB_WIDTH), BF16),
        scratch_shapes=[
            pltpu.VMEM((nh, s, width), BF16),
            pltpu.VMEM((s, width), BF16),
            pltpu.VMEM((s // SB_BLOCK, nh * SB_BLOCK, width), BF16),
            pltpu.VMEM((group, nh, SB_BLOCK, SB_BLOCK), F32),
            pltpu.VMEM((group, SB_BLOCK, width), F32),
        ],
        compiler_params=_params("arbitrary", "arbitrary"),
        name="stickbreak",
    )(sb3, sb3, sb3, qw, kw, m2, hsum)


def _memkv_kernel(mem_ref, nw_ref, wk_ref, wv_ref, knw_ref, k_ref, v_ref):
    n = _rms(mem_ref[...], nw_ref[...]).astype(BF16)
    k = _dot(n, wk_ref[...])
    for h in range(X_HEADS):
        cs = slice(h * X_DIM, (h + 1) * X_DIM)
        k_ref[:, cs] = _rms(k[:, cs], knw_ref[...]).astype(BF16)
    v_ref[...] = _dot(n, wv_ref[...]).astype(BF16)


def _memkv(mem, norm_w, wk, wv, k_norm_w):
    b, m, d = mem.shape
    return pl.pallas_call(
        _memkv_kernel,
        grid=(b,),
        in_specs=[
            pl.BlockSpec((None, m, d), lambda i: (i, 0, 0)),
            _const_spec((1, d)),
            _const_spec(wk.shape),
            _const_spec(wv.shape),
            _const_spec((1, X_DIM)),
        ],
        out_specs=[
            pl.BlockSpec((None, m, d), lambda i: (i, 0, 0)),
            pl.BlockSpec((None, m, d), lambda i: (i, 0, 0)),
        ],
        out_shape=[jax.ShapeDtypeStruct((b, m, d), BF16)] * 2,
        compiler_params=_params("arbitrary"),
        name="memkv",
    )(mem, norm_w, wk, wv, k_norm_w)


def _mid_kernel(x_ref, ydn_ref, ysb_ref, wout_ref, n2w_ref, wq_ref, qnw_ref, km_ref, vm_ref, wo_ref,
                n3w_ref, wrh_ref, wrl_ref, rb_ref, upper_ref,
                x2_ref, h3_ref, idx_ref, gate_ref, rank_ref, cnt_ref, count_s):
    tm = x_ref.shape[0]
    first_step = jnp.logical_and(pl.program_id(0) == 0, pl.program_id(1) == 0)

    @pl.when(first_step)
    def _():
        count_s[...] = jnp.zeros(count_s.shape, F32)

    sub = upper_ref.shape[0]
    subs = range(tm // sub)
    rows = [slice(i * sub, (i + 1) * sub) for i in subs]
    heads = range(X_HEADS)
    cols = [slice(h * X_DIM, (h + 1) * X_DIM) for h in heads]
    x1 = [x_ref[r, :] + _dot(ydn_ref[r, :], wout_ref[0:DN_WIDTH, :])
          + _dot(ysb_ref[r, :], wout_ref[DN_WIDTH:, :]) for r in rows]
    n2 = [_rms(v, n2w_ref[...]).astype(BF16) for v in x1]
    q = [_dot(v, wq_ref[...]) for v in n2]
    qh = [[(_rms(q[i][:, c], qnw_ref[...]) * (X_DIM ** -0.5)).astype(BF16) for c in cols] for i in subs]
    sc = [[_dot_nt(qh[i][h], km_ref[:, cols[h]]) for h in heads] for i in subs]
    ex = [[jnp.exp(sc[i][h] - jnp.max(sc[i][h], axis=-1, keepdims=True)) for h in heads] for i in subs]
    pr = [[(ex[i][h] / jnp.sum(ex[i][h], axis=-1, keepdims=True)).astype(BF16) for h in heads] for i in subs]
    oh = [[_dot(pr[i][h], vm_ref[:, cols[h]]).astype(BF16) for h in heads] for i in subs]
    x2 = [x1[i] + _dot(jnp.concatenate(oh[i], axis=1), wo_ref[...]) for i in subs]
    h3 = [_rms(v, n3w_ref[...]) for v in x2]
    hi = [v.astype(BF16) for v in h3]
    lo = [(h3[i] - hi[i].astype(F32)).astype(BF16) for i in subs]
    logits = [_dot_nt(wrh_ref[...], hi[i]) + _dot_nt(wrh_ref[...], lo[i]) + _dot_nt(wrl_ref[...], hi[i])
              + rb_ref[...] for i in subs]
    eid = lax.broadcasted_iota(jnp.int32, (N_EXPERTS, sub), 0).astype(F32)
    count = count_s[...]
    for i in subs:
        x2_ref[rows[i], :] = x2[i]
        h3_ref[rows[i], :] = h3[i]
        vals, ids = [], []
        cur = logits[i]
        for _ in range(TOP_K):
            m = jnp.max(cur, axis=0, keepdims=True)
            j = jnp.min(jnp.where(cur == m, eid, float(N_EXPERTS)), axis=0, keepdims=True)
            vals.append(m)
            ids.append(j)
            cur = jnp.where(eid == j, -jnp.inf, cur)
        exps = [jnp.exp(v - vals[0]) for v in vals]
        denom = exps[0] + exps[1] + exps[2] + exps[3]
        onehot = jnp.zeros((N_EXPERTS, sub), F32)
        for j in ids:
            onehot = onehot + jnp.where(eid == j, 1.0, 0.0)
        before = count + _dot(onehot.astype(BF16), upper_ref[...])
        for k in range(TOP_K):
            idx_ref[k:k + 1, rows[i]] = ids[k].astype(jnp.int32)
            gate_ref[k:k + 1, rows[i]] = exps[k] / denom
            rank_ref[k:k + 1, rows[i]] = jnp.sum(jnp.where(eid == ids[k], before, 0.0), axis=0,
                                                 keepdims=True).astype(jnp.int32)
        count = count + jnp.sum(onehot, axis=1, keepdims=True)
    count_s[...] = count
    cnt_ref[...] = jnp.broadcast_to(count, cnt_ref.shape)


def _mid(x3, ydn3, ysb3, w_out, n2w, wq, qnw, k_mem, v_mem, wo, n3w, wr_hi, wr_lo, rb, upper, tm):
    b, s, d = x3.shape
    n = b * s
    nt = s // tm
    m = k_mem.shape[1]
    tok = lambda i, j: (0, i * nt + j)
    return pl.pallas_call(
        _mid_kernel,
        grid=(b, nt),
        in_specs=[
            pl.BlockSpec((None, tm, d), lambda i, j: (i, j, 0)),
            pl.BlockSpec((None, tm, DN_WIDTH), lambda i, j: (i, j, 0)),
            pl.BlockSpec((None, tm, SB_WIDTH), lambda i, j: (i, j, 0)),
            _const_spec(w_out.shape),
            _const_spec((1, d)),
            _const_spec(wq.shape),
            _const_spec((1, X_DIM)),
            pl.BlockSpec((None, m, d), lambda i, j: (i, 0, 0)),
            pl.BlockSpec((None, m, d), lambda i, j: (i, 0, 0)),
            _const_spec(wo.shape),
            _const_spec((1, d)),
            _const_spec(wr_hi.shape),
            _const_spec(wr_lo.shape),
            _const_spec((N_EXPERTS, 1)),
            _const_spec(upper.shape),
        ],
        out_specs=[
            pl.BlockSpec((None, tm, d), lambda i, j: (i, j, 0)),
            pl.BlockSpec((None, tm, d), lambda i, j: (i, j, 0)),
            pl.BlockSpec((TOP_K, tm), tok),
            pl.BlockSpec((TOP_K, tm), tok),
            pl.BlockSpec((TOP_K, tm), tok),
            _const_spec((N_EXPERTS, LANES)),
        ],
        out_shape=[
            jax.ShapeDtypeStruct((b, s, d), F32),
            jax.ShapeDtypeStruct((b, s, d), F32),
            jax.ShapeDtypeStruct((TOP_K, n), jnp.int32),
            jax.ShapeDtypeStruct((TOP_K, n), F32),
            jax.ShapeDtypeStruct((TOP_K, n), jnp.int32),
            jax.ShapeDtypeStruct((N_EXPERTS, LANES), F32),
        ],
        scratch_shapes=[pltpu.VMEM((N_EXPERTS, 1), F32)],
        compiler_params=_params("arbitrary", "arbitrary"),
        name="mid",
    )(x3, ydn3, ysb3, w_out, n2w, wq, qnw, k_mem, v_mem, wo, n3w, wr_hi, wr_lo, rb, upper)


def _row_copy(src_ref, src_row, dst_ref, dst_row, sem):
    return pltpu.make_async_copy(src_ref.at[pl.ds(src_row, 1), :], dst_ref.at[pl.ds(dst_row, 1), :], sem)


def _dispatch_kernel(pend_ref, padded_ref, dest_ref, h_hbm, xpad_ref, zero_s, hbuf, sem, zsem, fsem):
    tf = hbuf.shape[1]
    i = pl.program_id(0)
    last = pl.num_programs(0) - 1

    def fetch(tile, slot):
        start = pl.multiple_of(tile * tf, tf)
        return pltpu.make_async_copy(h_hbm.at[pl.ds(start, tf), :], hbuf.at[slot], fsem.at[slot])

    def wait_rows(step):
        for _ in range(TOP_K):
            pltpu.make_async_copy(hbuf.at[0], xpad_ref.at[pl.ds(0, tf), :], sem.at[step % 2]).wait()

    @pl.when(i == 0)
    def _():
        fetch(0, 0).start()

        @pl.when(last >= 1)
        def _():
            fetch(1, 1).start()

        zero_s[...] = jnp.zeros(zero_s.shape, F32)

        def last_block(e):
            start = pl.multiple_of(pend_ref[e] - EXPERT_BLOCK, EXPERT_BLOCK)
            return pltpu.make_async_copy(zero_s, xpad_ref.at[pl.ds(start, EXPERT_BLOCK), :], zsem)

        for e in range(N_EXPERTS):
            @pl.when(padded_ref[e] > 0)
            def _(e=e):
                last_block(e).start()
        for e in range(N_EXPERTS):
            @pl.when(padded_ref[e] > 0)
            def _(e=e):
                last_block(e).wait()

    slot = i % 3
    fetch(i, slot).wait()
    src = hbuf.at[slot]
    for k in range(TOP_K):
        for t in range(tf):
            _row_copy(src, t, xpad_ref, dest_ref[0, k * tf + t], sem.at[i % 2]).start(priority=t % 2)

    @pl.when(i >= 1)
    def _():
        wait_rows(i - 1)

    @pl.when(i + 2 <= last)
    def _():
        fetch(i + 2, (i + 2) % 3).start()

    @pl.when(i == last)
    def _():
        wait_rows(i)


def _dispatch(pad_ends, padded, dest_tiles, h2d, rows, tf):
    n, d = h2d.shape
    return pl.pallas_call(
        _dispatch_kernel,
        grid_spec=pltpu.PrefetchScalarGridSpec(
            num_scalar_prefetch=2,
            grid=(n // tf,),
            in_specs=[
                pl.BlockSpec((None, 1, TOP_K * tf), lambda i, *_: (i, 0, 0), memory_space=pltpu.SMEM),
                pl.BlockSpec(memory_space=pl.ANY),
            ],
            out_specs=pl.BlockSpec(memory_space=pl.ANY),
            scratch_shapes=[pltpu.VMEM((EXPERT_BLOCK, d), F32), pltpu.VMEM((3, tf, d), F32),
                            pltpu.SemaphoreType.DMA((2,)), pltpu.SemaphoreType.DMA(()),
                            pltpu.SemaphoreType.DMA((3,))],
        ),
        out_shape=jax.ShapeDtypeStruct((rows, d), F32),
        compiler_params=_params("arbitrary"),
        name="dispatch",
    )(pad_ends, padded, dest_tiles, h2d)


def _experts_kernel(be_ref, nb_ref, x_ref, wg_ref, bg_ref, wu_ref, bu_ref, wd_ref, bd_ref, y_ref,
                    wg_s, wu_s, wd_s):
    j = pl.program_id(0)
    used = j < nb_ref[0]
    new_expert = jnp.logical_or(j == 0, be_ref[j] != be_ref[jnp.maximum(j - 1, 0)])

    @pl.when(jnp.logical_and(used, new_expert))
    def _():
        wg_s[...] = wg_ref[...].astype(BF16)
        wu_s[...] = wu_ref[...].astype(BF16)
        wd_s[...] = wd_ref[...].astype(BF16)

    @pl.when(used)
    def _():
        sub = EXPERT_BLOCK // EXPERT_SUBS
        rows = [slice(i * sub, (i + 1) * sub) for i in range(EXPERT_SUBS)]
        x = [x_ref[r, :].astype(BF16) for r in rows]
        gate = [jnp.minimum(_dot(v, wg_s[...]) + bg_ref[...], SWIGLU_LIMIT) for v in x]
        up = [jnp.clip(_dot(v, wu_s[...]) + bu_ref[...], -SWIGLU_LIMIT, SWIGLU_LIMIT) for v in x]
        act = [((u + 1.0) * g * _sigmoid(g * SWIGLU_ALPHA)).astype(BF16) for g, u in zip(gate, up)]
        for r, a in zip(rows, act):
            y_ref[r, :] = _dot(a, wd_s[...]) + bd_ref[...]


def _experts(block_e, nb_used, x_pad, wg, bg, wu, bu, wd, bd):
    rows, d = x_pad.shape
    nblk = rows // EXPERT_BLOCK
    dff = wg.shape[2]
    row_blk = lambda j, be, nb: (jnp.minimum(j, nb[0] - 1), 0)
    w_blk = lambda j, be, nb: (be[j], 0, 0)
    return pl.pallas_call(
        _experts_kernel,
        grid_spec=pltpu.PrefetchScalarGridSpec(
            num_scalar_prefetch=2,
            grid=(nblk,),
            in_specs=[
                pl.BlockSpec((EXPERT_BLOCK, d), row_blk),
                pl.BlockSpec((None, d, dff), w_blk),
                pl.BlockSpec((None, 1, dff), w_blk),
                pl.BlockSpec((None, d, dff), w_blk),
                pl.BlockSpec((None, 1, dff), w_blk),
                pl.BlockSpec((None, dff, d), w_blk),
                pl.BlockSpec((None, 1, d), w_blk),
            ],
            out_specs=pl.BlockSpec((EXPERT_BLOCK, d), row_blk),
            scratch_shapes=[pltpu.VMEM((d, dff), BF16), pltpu.VMEM((d, dff), BF16),
                            pltpu.VMEM((dff, d), BF16)],
        ),
        out_shape=jax.ShapeDtypeStruct((rows, d), F32),
        compiler_params=pltpu.CompilerParams(dimension_semantics=("arbitrary",),
                                             vmem_limit_bytes=V7X_EXPERTS_VMEM_LIMIT),
        name="experts",
    )(block_e, nb_used, x_pad, wg, bg, wu, bu, wd, bd)


def _combine_kernel(dest_ref, dnext_ref, x2_ref, gate_ref, ypad_ref, o_ref, buf, sem):
    th = x2_ref.shape[0]
    i = pl.program_id(0)
    slot = i % 2

    def gather(dref, s):
        for k in range(TOP_K):
            for t in range(th):
                _row_copy(ypad_ref, dref[0, k * th + t], buf.at[s, k], t, sem.at[s]).start(priority=t % 2)

    @pl.when(i == 0)
    def _():
        gather(dest_ref, 0)

    @pl.when(i + 1 < pl.num_programs(0))
    def _():
        gather(dnext_ref, 1 - slot)

    for k in range(TOP_K):
        pltpu.make_async_copy(ypad_ref.at[pl.ds(0, th), :], buf.at[slot, k], sem.at[slot]).wait()
    g = gate_ref[...]
    out = x2_ref[...]
    for k in range(TOP_K):
        out = out + buf[slot, k] * g[:, k:k + 1]
    o_ref[...] = out


def _combine(dest_tiles, x2d, gates_nk, y_pad, th):
    n, d = x2d.shape
    nt = n // th
    return pl.pallas_call(
        _combine_kernel,
        grid=(nt,),
        in_specs=[
            pl.BlockSpec((None, 1, TOP_K * th), lambda i: (i, 0, 0), memory_space=pltpu.SMEM),
            pl.BlockSpec((None, 1, TOP_K * th), lambda i: (jnp.minimum(i + 1, nt - 1), 0, 0),
                         memory_space=pltpu.SMEM),
            pl.BlockSpec((th, d), lambda i: (i, 0)),
            pl.BlockSpec((th, TOP_K), lambda i: (i, 0)),
            pl.BlockSpec(memory_space=pl.ANY),
        ],
        out_specs=pl.BlockSpec((th, d), lambda i: (i, 0)),
        out_shape=jax.ShapeDtypeStruct((n, d), F32),
        scratch_shapes=[pltpu.VMEM((2, TOP_K, th, d), F32), pltpu.SemaphoreType.DMA((2,))],
        compiler_params=_params("arbitrary"),
        name="combine",
    )(dest_tiles, dest_tiles, x2d, gates_nk, y_pad)


def _tile_dest(dest, t):
    k, n = dest.shape
    return dest.reshape(k, n // t, t).transpose(1, 0, 2).reshape(n // t, 1, k * t)


def _layer(x, mem, norm1_w, w_in, conv_w, a_log, dt_bias, dn_norm_w, sb_q_norm_w, sb_k_norm_w, w_out,
           norm2_w, mem_norm_w, xq_w, xk_w, xv_w, xq_norm_w, xk_norm_w, xo_w, norm3_w, router_w,
           router_b, w_gate, b_gate, w_up, b_up, w_down, b_down):
    b, s, d = x.shape
    n = b * s
    tm_proj = min(512, s)
    tm_mid = min(512, s)
    sub_mid = min(256, s)
    t_moe = min(256, s)

    o_dn, o_ab, o_sb = 4 * DN_WIDTH, 4 * DN_WIDTH + 2 * DN_HEADS, 4 * DN_WIDTH + 2 * DN_HEADS
    w_dn = w_in[:, :o_dn].astype(BF16)
    w_ab_f = w_in[:, o_dn:o_ab]
    w_ab = jnp.pad(w_ab_f, ((0, 0), (0, LANES - 2 * DN_HEADS))).astype(BF16)
    w_abt = w_ab_f.T.astype(BF16)
    w_sb = w_in[:, o_sb:].astype(BF16)
    row = lambda v: v.reshape(1, -1).astype(F32)

    dn, sb, ab, abt = _in_proj(x.reshape(n, d), row(norm1_w), w_dn, w_sb, w_ab, w_abt, tm_proj)

    abt4 = abt.reshape(8, n // CHUNK, CHUNK).transpose(1, 0, 2).reshape(b, s // CHUNK, 8, CHUNK)
    pad_lane = lambda v: jnp.pad(v.astype(F32), (0, LANES - v.shape[0])).reshape(1, LANES)
    pad_col = lambda v: jnp.pad(v.astype(F32), (0, 8 - v.shape[0])).reshape(8, 1)
    y_dn = _deltanet(dn.reshape(b, s, -1), ab.reshape(b, s, LANES), abt4, conv_w.astype(F32),
                     pad_lane(a_log), pad_lane(dt_bias), pad_col(a_log), pad_col(dt_bias), row(dn_norm_w))

    ii = jnp.arange(SB_BLOCK)
    m2 = jnp.concatenate([(ii[:, None] > ii[None, :]).astype(BF16),
                          jnp.ones((SB_BLOCK, SB_BLOCK), BF16)], axis=1)
    sb_heads = SB_LANES // SB_DIM
    hh = jnp.arange(SB_LANES) // SB_DIM
    hsum = (hh[:, None] == hh[None, :]).astype(BF16)
    y_sb = _stickbreak(sb.reshape(b, s, -1), row(jnp.tile(sb_q_norm_w, sb_heads)),
                       row(jnp.tile(sb_k_norm_w, sb_heads)), m2, hsum)

    k_mem, v_mem = _memkv(mem, row(mem_norm_w), xk_w.astype(BF16), xv_w.astype(BF16), row(xk_norm_w))

    wr_t = router_w.T.astype(F32)
    wr_hi = wr_t.astype(BF16)
    wr_lo = (wr_t - wr_hi.astype(F32)).astype(BF16)
    jj = jnp.arange(sub_mid)
    upper = (jj[:, None] < jj[None, :]).astype(BF16)
    x2, h3, idx, gates, rank, cnt = _mid(
        x, y_dn, y_sb, w_out.astype(BF16), row(norm2_w), xq_w.astype(BF16), row(xq_norm_w), k_mem, v_mem,
        xo_w.astype(BF16), row(norm3_w), wr_hi, wr_lo, router_b.reshape(N_EXPERTS, 1).astype(F32), upper,
        tm_mid)

    counts = cnt[:, 0].astype(jnp.int32)
    padded = (counts + EXPERT_BLOCK - 1) // EXPERT_BLOCK * EXPERT_BLOCK
    pad_ends = jnp.cumsum(padded)
    pad_starts = pad_ends - padded
    sel = idx[:, :, None] == jnp.arange(N_EXPERTS, dtype=jnp.int32)[None, None, :]
    dest = rank + jnp.sum(jnp.where(sel, pad_starts[None, None, :], 0), axis=-1)
    n_blocks = -(-n * TOP_K // EXPERT_BLOCK) + N_EXPERTS
    nb_used = (pad_ends[-1] // EXPERT_BLOCK).astype(jnp.int32)
    blk = jnp.minimum(jnp.arange(n_blocks, dtype=jnp.int32), nb_used - 1) * EXPERT_BLOCK
    block_e = jnp.minimum(jnp.sum(pad_ends[None, :] <= blk[:, None], axis=1), N_EXPERTS - 1).astype(jnp.int32)
    dest_tiles = _tile_dest(dest, t_moe)

    x_pad = _dispatch(pad_ends.astype(jnp.int32), padded.astype(jnp.int32), dest_tiles, h3.reshape(n, d),
                      n_blocks * EXPERT_BLOCK, t_moe)
    bias = lambda v: v.reshape(N_EXPERTS, 1, -1).astype(F32)
    y_pad = _experts(block_e, nb_used.reshape(1), x_pad, w_gate.astype(F32), bias(b_gate),
                     w_up.astype(F32), bias(b_up), w_down.astype(F32), bias(b_down))
    out = _combine(dest_tiles, x2.reshape(n, d), gates.T, y_pad, t_moe)
    return out.reshape(b, s, d)


def kernel(x, mem, norm1_w, w_in, conv_w, a_log, dt_bias, dn_norm_w, sb_q_norm_w, sb_k_norm_w, w_out,
           norm2_w, mem_norm_w, xq_w, xk_w, xv_w, xq_norm_w, xk_norm_w, xo_w, norm3_w, router_w,
           router_b, w_gate, b_gate, w_up, b_up, w_down, b_down):
    depth = w_in.shape[0]
    for l in range(depth):
        x = _layer(x, mem, norm1_w[l], w_in[l], conv_w[l], a_log[l], dt_bias[l], dn_norm_w[l],
                   sb_q_norm_w[l], sb_k_norm_w[l], w_out[l], norm2_w[l], mem_norm_w[l], xq_w[l], xk_w[l],
                   xv_w[l], xq_norm_w[l], xk_norm_w[l], xo_w[l], norm3_w[l], router_w[l], router_b[l],
                   w_gate[l], b_gate[l], w_up[l], b_up[l], w_down[l], b_down[l])
    return x
```

```python
import functools

import jax
import jax.numpy as jnp
from jax import lax
from jax.experimental import pallas as pl
from jax.experimental.pallas import tpu as pltpu

F32 = jnp.float32
BF16 = jnp.bfloat16

EPS = 1e-6
CHUNK = 64
DN_HEADS = 4
DN_DIM = 128
DN_WIDTH = DN_HEADS * DN_DIM
CONV_WIDTH = 4
DN_PAIR = 2
SB_HEADS = 8
SB_DIM = 64
SB_WIDTH = SB_HEADS * SB_DIM
SB_BLOCK = 128
SB_GROUP = 4
SB_LANES = 256
X_HEADS = 4
X_DIM = 256
N_EXPERTS = 32
TOP_K = 4
EXPERT_BLOCK = 512
SWIGLU_ALPHA = 1.702
SWIGLU_LIMIT = 7.0
LANES = 128
SB_UNDERFLOW = -88.0
V7X_VMEM_LIMIT = 48 * 1024 * 1024
V7X_EXPERTS_VMEM_LIMIT = 58 * 1024 * 1024


def _dot(a, b):
    return jnp.dot(a, b, preferred_element_type=F32)


def _dot_nt(a, b):
    return lax.dot_general(a, b, (((1,), (1,)), ((), ())), preferred_element_type=F32)


def _dot_tn(a, b):
    return lax.dot_general(a, b, (((0,), (0,)), ((), ())), preferred_element_type=F32)


def _split3(x):
    hi = x.astype(BF16)
    r = x - hi.astype(F32)
    mid = r.astype(BF16)
    return hi, mid, (r - mid.astype(F32)).astype(BF16)


def _softplus(x):
    return jnp.maximum(x, 0.0) + jnp.log(1.0 + jnp.exp(-jnp.abs(x)))


def _sigmoid(x):
    return 1.0 / (1.0 + jnp.exp(-x))


def _rms(x, w):
    return x * lax.rsqrt(jnp.mean(x * x, axis=-1, keepdims=True) + EPS) * w


def _params(*sem):
    return pltpu.CompilerParams(dimension_semantics=sem, vmem_limit_bytes=V7X_VMEM_LIMIT)


def _const_spec(shape):
    nd = len(shape)
    return pl.BlockSpec(shape, lambda *_: (0,) * nd)


def _in_proj_kernel(x_ref, nw_ref, wdn_ref, wsb_ref, wab_ref, wabt_ref,
                    dn_ref, sb_ref, ab_ref, abt_ref):
    n = _rms(x_ref[...], nw_ref[...]).astype(BF16)
    dn_ref[...] = _dot(n, wdn_ref[...]).astype(BF16)
    sb_ref[...] = _dot(n, wsb_ref[...]).astype(BF16)
    ab_ref[...] = _dot(n, wab_ref[...])
    abt_ref[...] = _dot_nt(wabt_ref[...], n)


def _in_proj(x2d, norm_w, w_dn, w_sb, w_ab, w_abt, tm):
    n, d = x2d.shape
    return pl.pallas_call(
        _in_proj_kernel,
        grid=(n // tm,),
        in_specs=[
            pl.BlockSpec((tm, d), lambda i: (i, 0)),
            _const_spec((1, d)),
            _const_spec(w_dn.shape),
            _const_spec(w_sb.shape),
            _const_spec(w_ab.shape),
            _const_spec(w_abt.shape),
        ],
        out_specs=[
            pl.BlockSpec((tm, w_dn.shape[1]), lambda i: (i, 0)),
            pl.BlockSpec((tm, w_sb.shape[1]), lambda i: (i, 0)),
            pl.BlockSpec((tm, LANES), lambda i: (i, 0)),
            pl.BlockSpec((8, tm), lambda i: (0, i)),
        ],
        out_shape=[
            jax.ShapeDtypeStruct((n, w_dn.shape[1]), BF16),
            jax.ShapeDtypeStruct((n, w_sb.shape[1]), BF16),
            jax.ShapeDtypeStruct((n, LANES), F32),
            jax.ShapeDtypeStruct((8, n), F32),
        ],
        compiler_params=_params("arbitrary"),
        name="in_proj",
    )(x2d, norm_w, w_dn, w_sb, w_ab, w_abt)


def _deltanet_kernel(dn_ref, ab_ref, abt_ref, convw_ref, shift_ref, alog_ref, dtb_ref, alogt_ref, dtbt_ref,
                     normw_ref, o_ref, state_s, u_s, w_s, kt_s, qg_s, attn_s, egl_s):
    s = dn_ref.shape[0]
    steps = s // (CHUNK * DN_PAIR)
    state_s[...] = jnp.zeros(state_s.shape, F32)

    def conv_silu(r0, c, part, h, l2):
        cs = slice(part * DN_WIDTH + h * LANES, part * DN_WIDTH + (h + 1) * LANES)
        prev0 = pl.multiple_of(jnp.maximum(r0 - CHUNK, 0), CHUNK)
        prev = dn_ref[pl.ds(prev0, CHUNK), cs]
        prev = jnp.where(c > 0, prev, jnp.zeros_like(prev))
        win = jnp.concatenate([prev, dn_ref[pl.ds(r0, CHUNK), cs]], axis=0)
        sh = _dot(shift_ref[...], win)
        w = convw_ref[:, cs]
        y = w[0:1, :] * sh[0:CHUNK, :]
        for i in range(1, CONV_WIDTH):
            y = y + w[i:i + 1, :] * sh[i * CHUNK:(i + 1) * CHUNK, :]
        y = y * _sigmoid(y)
        if l2:
            y = y * lax.rsqrt(jnp.sum(y * y, axis=-1, keepdims=True) + EPS)
        return y

    row = lax.broadcasted_iota(jnp.int32, (CHUNK, CHUNK), 0)
    col = lax.broadcasted_iota(jnp.int32, (CHUNK, CHUNK), 1)
    tri = row >= col
    strict = row > col
    tril16 = jnp.where(tri, 1.0, 0.0).astype(BF16)
    triu16 = jnp.where(row <= col, 1.0, 0.0).astype(BF16)
    neg_a_col = -jnp.exp(alog_ref[...])
    neg_a_row = -jnp.exp(alogt_ref[...])
    scale = DN_DIM ** -0.5
    heads = range(DN_HEADS)
    cols = [slice(h * LANES, (h + 1) * LANES) for h in heads]

    def pre_items(it):
        items = []
        for ci in range(DN_PAIR):
            c = it * DN_PAIR + ci
            r0 = pl.multiple_of(c * CHUNK, CHUNK)
            ab = ab_ref[pl.ds(r0, CHUNK), :]
            g_col = neg_a_col * _softplus(ab + dtb_ref[...])
            gc_col_all = sum(_dot(tril16, p) for p in _split3(g_col))
            beta_all = _sigmoid(ab)
            abt = abt_ref[c]
            g_row = neg_a_row * _softplus(abt + dtbt_ref[...])
            gc_row_all = sum(_dot(p, triu16) for p in _split3(g_row))
            for h in heads:
                q = conv_silu(r0, c, 0, h, True) * scale
                k = conv_silu(r0, c, 1, h, True)
                v = conv_silu(r0, c, 2, h, False)
                gcol = gc_col_all[:, h:h + 1]
                grow = gc_row_all[h:h + 1, :]
                beta = beta_all[:, DN_HEADS + h:DN_HEADS + h + 1]
                glast = gcol[CHUNK - 1:CHUNK, :]
                decay = jnp.where(tri, jnp.exp(jnp.where(tri, gcol - grow, 0.0)), 0.0)
                kb = k * beta
                kt_s[pl.ds(r0, CHUNK), cols[h]] = (k * jnp.exp(glast - gcol)).astype(BF16)
                qg_s[pl.ds(r0, CHUNK), cols[h]] = (q * jnp.exp(gcol)).astype(BF16)
                egl_s[c, h:h + 1, :] = jnp.broadcast_to(jnp.exp(glast), (1, LANES))
                items.append(dict(c=c, r0=r0, h=h, decay=decay, kb16=kb.astype(BF16),
                                  k16=k.astype(BF16), q16=q.astype(BF16),
                                  sol=jnp.concatenate([v * beta, kb * jnp.exp(gcol)], axis=1)))
        return items

    def pre_gram(items):
        kk = [_dot_nt(t["kb16"], t["k16"]) for t in items]
        qk = [_dot_nt(t["q16"], t["k16"]) for t in items]
        for t, kk_i, qk_i in zip(items, kk, qk):
            attn_s[t["c"], t["h"]] = jnp.where(tri, qk_i * t["decay"], 0.0).astype(BF16)
            t["tm"] = -jnp.where(strict, kk_i * t["decay"], 0.0)
            t["p16"] = t["tm"].astype(BF16)
        return [_dot(t["p16"], t["p16"]) for t in items]

    def pre_level(items, sq, last):
        for i, t in enumerate(items):
            t["pw"] = sq[i]
            t["p16"] = sq[i].astype(BF16)
        app = [_dot(t["p16"], t["tm"].astype(BF16)) for t in items]
        nxt = None if last else [_dot(t["p16"], t["p16"]) for t in items]
        for i, t in enumerate(items):
            t["tm"] = t["tm"] + t["pw"] + app[i]
        return nxt

    def pre_solve(items):
        corr = [_dot(t["tm"].astype(BF16), t["sol"].astype(BF16)) for t in items]
        for i, t in enumerate(items):
            sol = t["sol"] + corr[i]
            u_s[pl.ds(t["r0"], CHUNK), cols[t["h"]]] = sol[:, :DN_DIM]
            w_s[pl.ds(t["r0"], CHUNK), cols[t["h"]]] = sol[:, DN_DIM:].astype(BF16)

    def rec_read(c):
        r0 = pl.multiple_of(c * CHUNK, CHUNK)
        st = [state_s[h] for h in heads]
        st16 = [x.astype(BF16) for x in st]
        ws = [_dot(w_s[pl.ds(r0, CHUNK), cols[h]], st16[h]) for h in heads]
        qs = [_dot(qg_s[pl.ds(r0, CHUNK), cols[h]], st16[h]) for h in heads]
        return dict(c=c, r0=r0, st=st, ws=ws, qs=qs)

    def rec_update(rd):
        c, r0 = rd["c"], rd["r0"]
        vn16 = [(u_s[pl.ds(r0, CHUNK), cols[h]] - rd["ws"][h]).astype(BF16) for h in heads]
        av = [_dot(attn_s[c, h], vn16[h]) for h in heads]
        ks = [_dot_tn(kt_s[pl.ds(r0, CHUNK), cols[h]], vn16[h]) for h in heads]
        for h in heads:
            state_s[h] = rd["st"][h] * egl_s[c, h:h + 1, :] + ks[h]
            o = rd["qs"][h] + av[h]
            o = o * lax.rsqrt(jnp.mean(o * o, axis=-1, keepdims=True) + EPS)
            z = dn_ref[pl.ds(r0, CHUNK), 3 * DN_WIDTH + h * LANES:3 * DN_WIDTH + (h + 1) * LANES].astype(F32)
            o_ref[pl.ds(r0, CHUNK), cols[h]] = (o * normw_ref[...] * (z * _sigmoid(z))).astype(BF16)

    def step(it, with_pre, with_rec):
        hooks = []
        for ci in range(DN_PAIR if with_rec else 0):
            hooks += [("read", ci), ("update", ci)]
        pending = {}

        def run_hook():
            if hooks:
                kind, ci = hooks.pop(0)
                if kind == "read":
                    pending[ci] = rec_read((it - 1) * DN_PAIR + ci)
                else:
                    rec_update(pending.pop(ci))

        if not with_pre:
            while hooks:
                run_hook()
            return
        items = pre_items(it)
        run_hook()
        sq = pre_gram(items)
        run_hook()
        for level in range(1, 6):
            sq = pre_level(items, sq, level == 5)
            if level in (2, 4):
                run_hook()
        pre_solve(items)
        while hooks:
            run_hook()

    step(0, True, False)

    def fused(it, carry):
        step(it, True, True)
        return carry

    lax.fori_loop(1, steps, fused, 0)
    step(steps, False, True)


def _deltanet(dn3, ab3, abt4, conv_w, shift, alog, dtb, alogt, dtbt, norm_w):
    b, s, _ = dn3.shape
    nch = s // CHUNK
    return pl.pallas_call(
        _deltanet_kernel,
        grid=(b,),
        in_specs=[
            pl.BlockSpec((None, s, 4 * DN_WIDTH), lambda i: (i, 0, 0)),
            pl.BlockSpec((None, s, LANES), lambda i: (i, 0, 0)),
            pl.BlockSpec((None, nch, 8, CHUNK), lambda i: (i, 0, 0, 0)),
            _const_spec(conv_w.shape),
            _const_spec(shift.shape),
            _const_spec((1, LANES)),
            _const_spec((1, LANES)),
            _const_spec((8, 1)),
            _const_spec((8, 1)),
            _const_spec((1, DN_DIM)),
        ],
        out_specs=pl.BlockSpec((None, s, DN_WIDTH), lambda i: (i, 0, 0)),
        out_shape=jax.ShapeDtypeStruct((b, s, DN_WIDTH), BF16),
        scratch_shapes=[
            pltpu.VMEM((DN_HEADS, DN_DIM, DN_DIM), F32),
            pltpu.VMEM((s, DN_WIDTH), F32),
            pltpu.VMEM((s, DN_WIDTH), BF16),
            pltpu.VMEM((s, DN_WIDTH), BF16),
            pltpu.VMEM((s, DN_WIDTH), BF16),
            pltpu.VMEM((nch, DN_HEADS, CHUNK, CHUNK), BF16),
            pltpu.VMEM((nch, 8, LANES), F32),
        ],
        compiler_params=_params("arbitrary"),
        name="deltanet",
    )(dn3, ab3, abt4, conv_w, shift, alog, dtb, alogt, dtbt, norm_w)


def _stickbreak_kernel(q_ref, k_ref, v_ref, qw_ref, kw_ref, m2_ref, hsum_ref, o_ref,
                       qn_s, kn_s, vm_s, carry_s, acc_s):
    s, width = q_ref.shape
    nh = width // SB_DIM
    group = carry_s.shape[0]
    lane = lax.broadcasted_iota(jnp.int32, (1, width), 1)
    head_lanes = [jnp.logical_and(lane >= h * SB_DIM, lane < (h + 1) * SB_DIM) for h in range(nh)]

    def head_norm(r0, x_ref, w):
        x = x_ref[pl.ds(r0, SB_BLOCK), :].astype(F32)
        sq = x * x
        hi = sq.astype(BF16)
        lo = (sq - hi.astype(F32)).astype(BF16)
        ms = (_dot(hi, hsum_ref[...]) + _dot(lo, hsum_ref[...])) * (1.0 / SB_DIM)
        return x * lax.rsqrt(ms + EPS) * w

    def norm_block(i, carry):
        r0 = pl.multiple_of(i * SB_BLOCK, SB_BLOCK)
        qn = head_norm(r0, q_ref, qw_ref[...]) * (SB_DIM ** -0.5)
        v = v_ref[pl.ds(r0, SB_BLOCK), :]
        for h in range(nh):
            qn_s[h, pl.ds(r0, SB_BLOCK), :] = jnp.where(head_lanes[h], qn, 0.0).astype(BF16)
            vm_s[i, h * SB_BLOCK:(h + 1) * SB_BLOCK, :] = jnp.where(head_lanes[h], v, jnp.zeros_like(v))
        kn_s[pl.ds(r0, SB_BLOCK), :] = head_norm(r0, k_ref, kw_ref[...]).astype(BF16)
        return carry

    lax.fori_loop(0, s // SB_BLOCK, norm_block, 0)

    row = lax.broadcasted_iota(jnp.int32, (SB_BLOCK, SB_BLOCK), 0)
    col = lax.broadcasted_iota(jnp.int32, (SB_BLOCK, SB_BLOCK), 1)
    causal = col < row

    def step(rows, key_blocks, valid):
        tiles = [(g, h) for g in range(group) for h in range(nh)]
        k16 = [kn_s[pl.ds(pl.multiple_of(kb * SB_BLOCK, SB_BLOCK), SB_BLOCK), :] for kb in key_blocks]
        z = [_dot_nt(qn_s[h, pl.ds(rows[g], SB_BLOCK), :], k16[g]) for g, h in tiles]
        sp = [_softplus(x) for x in z]
        if valid is None:
            log_fail = [jnp.where(causal, -x, 0.0) for x in sp]
        else:
            log_fail = [-x for x in sp]
        cs = [_dot(x.astype(BF16), m2_ref[...]) for x in log_fail]
        p = []
        for i, (g, h) in enumerate(tiles):
            w = jnp.exp(z[i] - sp[i] + carry_s[g, h] + cs[i][:, :SB_BLOCK])
            p.append((jnp.where(causal, w, 0.0) if valid is None else w).astype(BF16))
        pv = [_dot(jnp.concatenate(p[g * nh:(g + 1) * nh], axis=1), vm_s[key_blocks[g]])
              for g in range(group)]
        for g in range(group):
            acc = acc_s[g] + pv[g]
            acc_s[g] = acc if valid is None else jnp.where(valid[g], acc, acc_s[g])
        for i, (g, h) in enumerate(tiles):
            carry = carry_s[g, h] + cs[i][:, SB_BLOCK:]
            carry_s[g, h] = carry if valid is None else jnp.where(valid[g], carry, carry_s[g, h])

    def q_group(qg, carry):
        blocks = [qg * group + g for g in range(group)]
        rows = [pl.multiple_of(qb * SB_BLOCK, SB_BLOCK) for qb in blocks]
        carry_s[...] = jnp.zeros(carry_s.shape, F32)
        acc_s[...] = jnp.zeros(acc_s.shape, F32)
        step(rows, blocks, None)

        def cond(st):
            d, alive = st
            return jnp.logical_and(d <= blocks[-1], alive)

        def body(st):
            d, _ = st
            alive = jnp.max(carry_s[...]) > SB_UNDERFLOW
            step(rows, [jnp.maximum(qb - d, 0) for qb in blocks], [qb >= d for qb in blocks])
            return d + 1, alive

        lax.while_loop(cond, body, (jnp.int32(1), jnp.bool_(True)))
        for g in range(group):
            o_ref[pl.ds(rows[g], SB_BLOCK), :] = acc_s[g].astype(BF16)
        return carry

    lax.fori_loop(0, s // (SB_BLOCK * group), q_group, 0)


def _stickbreak(sb3, qw, kw, m2, hsum):
    b, s, _ = sb3.shape
    width = hsum.shape[0]
    nh = width // SB_DIM
    parts = SB_WIDTH // width
    group = min(SB_GROUP, s // SB_BLOCK)
    return pl.pallas_call(
        _stickbreak_kernel,
        grid=(b, parts),
        in_specs=[
            pl.BlockSpec((None, s, width), lambda i, j: (i, 0, j)),
            pl.BlockSpec((None, s, width), lambda i, j: (i, 0, parts + j)),
            pl.BlockSpec((None, s, width), lambda i, j: (i, 0, 2 * parts + j)),
            _const_spec((1, width)),
            _const_spec((1, width)),
            _const_spec(m2.shape),
            _const_spec(hsum.shape),
        ],
        out_specs=pl.BlockSpec((None, s, width), lambda i, j: (i, 0, j)),
        out_shape=jax.ShapeDtypeStruct((b, s, SB_WIDTH), BF16),
        scratch_shapes=[
            pltpu.VMEM((nh, s, width), BF16),
            pltpu.VMEM((s, width), BF16),
            pltpu.VMEM((s // SB_BLOCK, nh * SB_BLOCK, width), BF16),
            pltpu.VMEM((group, nh, SB_BLOCK, SB_BLOCK), F32),
            pltpu.VMEM((group, SB_BLOCK, width), F32),
        ],
        compiler_params=_params("arbitrary", "arbitrary"),
        name="stickbreak",
    )(sb3, sb3, sb3, qw, kw, m2, hsum)


def _memkv_kernel(mem_ref, nw_ref, wk_ref, wv_ref, knw_ref, k_ref, v_ref):
    n = _rms(mem_ref[...], nw_ref[...]).astype(BF16)
    k = _dot(n, wk_ref[...])
    for h in range(X_HEADS):
        cs = slice(h * X_DIM, (h + 1) * X_DIM)
        k_ref[:, cs] = _rms(k[:, cs], knw_ref[...]).astype(BF16)
    v_ref[...] = _dot(n, wv_ref[...]).astype(BF16)


def _memkv(mem, norm_w, wk, wv, k_norm_w):
    b, m, d = mem.shape
    return pl.pallas_call(
        _memkv_kernel,
        grid=(b,),
        in_specs=[
            pl.BlockSpec((None, m, d), lambda i: (i, 0, 0)),
            _const_spec((1, d)),
            _const_spec(wk.shape),
            _const_spec(wv.shape),
            _const_spec((1, X_DIM)),
        ],
        out_specs=[
            pl.BlockSpec((None, m, d), lambda i: (i, 0, 0)),
            pl.BlockSpec((None, m, d), lambda i: (i, 0, 0)),
        ],
        out_shape=[jax.ShapeDtypeStruct((b, m, d), BF16)] * 2,
        compiler_params=_params("arbitrary"),
        name="memkv",
    )(mem, norm_w, wk, wv, k_norm_w)


def _mid_kernel(x_ref, ydn_ref, ysb_ref, wout_ref, n2w_ref, wq_ref, qnw_ref, km_ref, vm_ref, wo_ref,
                n3w_ref, wrh_ref, wrl_ref, rb_ref, upper_ref,
                x2_ref, h3_ref, idx_ref, gate_ref, rank_ref, cnt_ref, count_s):
    tm = x_ref.shape[0]
    first_step = jnp.logical_and(pl.program_id(0) == 0, pl.program_id(1) == 0)

    @pl.when(first_step)
    def _():
        count_s[...] = jnp.zeros(count_s.shape, F32)

    sub = upper_ref.shape[0]
    subs = range(tm // sub)
    rows = [slice(i * sub, (i + 1) * sub) for i in subs]
    heads = range(X_HEADS)
    cols = [slice(h * X_DIM, (h + 1) * X_DIM) for h in heads]
    x1 = [x_ref[r, :] + _dot(ydn_ref[r, :], wout_ref[0:DN_WIDTH, :])
          + _dot(ysb_ref[r, :], wout_ref[DN_WIDTH:, :]) for r in rows]
    n2 = [_rms(v, n2w_ref[...]).astype(BF16) for v in x1]
    q = [_dot(v, wq_ref[...]) for v in n2]
    qh = [[(_rms(q[i][:, c], qnw_ref[...]) * (X_DIM ** -0.5)).astype(BF16) for c in cols] for i in subs]
    sc = [[_dot_nt(qh[i][h], km_ref[:, cols[h]]) for h in heads] for i in subs]
    ex = [[jnp.exp(sc[i][h] - jnp.max(sc[i][h], axis=-1, keepdims=True)) for h in heads] for i in subs]
    pr = [[(ex[i][h] / jnp.sum(ex[i][h], axis=-1, keepdims=True)).astype(BF16) for h in heads] for i in subs]
    oh = [[_dot(pr[i][h], vm_ref[:, cols[h]]).astype(BF16) for h in heads] for i in subs]
    x2 = [x1[i] + _dot(jnp.concatenate(oh[i], axis=1), wo_ref[...]) for i in subs]
    h3 = [_rms(v, n3w_ref[...]) for v in x2]
    hi = [v.astype(BF16) for v in h3]
    lo = [(h3[i] - hi[i].astype(F32)).astype(BF16) for i in subs]
    logits = [_dot_nt(wrh_ref[...], hi[i]) + _dot_nt(wrh_ref[...], lo[i]) + _dot_nt(wrl_ref[...], hi[i])
              + rb_ref[...] for i in subs]
    eid = lax.broadcasted_iota(jnp.int32, (N_EXPERTS, sub), 0).astype(F32)
    count = count_s[...]
    for i in subs:
        x2_ref[rows[i], :] = x2[i]
        h3_ref[rows[i], :] = h3[i]
        vals, ids = [], []
        cur = logits[i]
        for _ in range(TOP_K):
            m = jnp.max(cur, axis=0, keepdims=True)
            j = jnp.min(jnp.where(cur == m, eid, float(N_EXPERTS)), axis=0, keepdims=True)
            vals.append(m)
            ids.append(j)
            cur = jnp.where(eid == j, -jnp.inf, cur)
        exps = [jnp.exp(v - vals[0]) for v in vals]
        denom = exps[0] + exps[1] + exps[2] + exps[3]
        onehot = jnp.zeros((N_EXPERTS, sub), F32)
        for j in ids:
            onehot = onehot + jnp.where(eid == j, 1.0, 0.0)
        before = count + _dot(onehot.astype(BF16), upper_ref[...])
        for k in range(TOP_K):
            idx_ref[k:k + 1, rows[i]] = ids[k].astype(jnp.int32)
            gate_ref[k:k + 1, rows[i]] = exps[k] / denom
            rank_ref[k:k + 1, rows[i]] = jnp.sum(jnp.where(eid == ids[k], before, 0.0), axis=0,
                                                 keepdims=True).astype(jnp.int32)
        count = count + jnp.sum(onehot, axis=1, keepdims=True)
    count_s[...] = count
    cnt_ref[...] = jnp.broadcast_to(count, cnt_ref.shape)


def _mid(x3, ydn3, ysb3, w_out, n2w, wq, qnw, k_mem, v_mem, wo, n3w, wr_hi, wr_lo, rb, upper, tm):
    b, s, d = x3.shape
    n = b * s
    nt = s // tm
    m = k_mem.shape[1]
    tok = lambda i, j: (0, i * nt + j)
    return pl.pallas_call(
        _mid_kernel,
        grid=(b, nt),
        in_specs=[
            pl.BlockSpec((None, tm, d), lambda i, j: (i, j, 0)),
            pl.BlockSpec((None, tm, DN_WIDTH), lambda i, j: (i, j, 0)),
            pl.BlockSpec((None, tm, SB_WIDTH), lambda i, j: (i, j, 0)),
            _const_spec(w_out.shape),
            _const_spec((1, d)),
            _const_spec(wq.shape),
            _const_spec((1, X_DIM)),
            pl.BlockSpec((None, m, d), lambda i, j: (i, 0, 0)),
            pl.BlockSpec((None, m, d), lambda i, j: (i, 0, 0)),
            _const_spec(wo.shape),
            _const_spec((1, d)),
            _const_spec(wr_hi.shape),
            _const_spec(wr_lo.shape),
            _const_spec((N_EXPERTS, 1)),
            _const_spec(upper.shape),
        ],
        out_specs=[
            pl.BlockSpec((None, tm, d), lambda i, j: (i, j, 0)),
            pl.BlockSpec((None, tm, d), lambda i, j: (i, j, 0)),
            pl.BlockSpec((TOP_K, tm), tok),
            pl.BlockSpec((TOP_K, tm), tok),
            pl.BlockSpec((TOP_K, tm), tok),
            _const_spec((N_EXPERTS, LANES)),
        ],
        out_shape=[
            jax.ShapeDtypeStruct((b, s, d), F32),
            jax.ShapeDtypeStruct((b, s, d), F32),
            jax.ShapeDtypeStruct((TOP_K, n), jnp.int32),
            jax.ShapeDtypeStruct((TOP_K, n), F32),
            jax.ShapeDtypeStruct((TOP_K, n), jnp.int32),
            jax.ShapeDtypeStruct((N_EXPERTS, LANES), F32),
        ],
        scratch_shapes=[pltpu.VMEM((N_EXPERTS, 1), F32)],
        compiler_params=_params("arbitrary", "arbitrary"),
        name="mid",
    )(x3, ydn3, ysb3, w_out, n2w, wq, qnw, k_mem, v_mem, wo, n3w, wr_hi, wr_lo, rb, upper)


def _row_copy(src_ref, src_row, dst_ref, dst_row, sem):
    return pltpu.make_async_copy(src_ref.at[pl.ds(src_row, 1), :], dst_ref.at[pl.ds(dst_row, 1), :], sem)


def _dispatch_kernel(pend_ref, padded_ref, dest_ref, h_hbm, xpad_ref, zero_s, hbuf, sem, zsem, fsem):
    tf = hbuf.shape[1]
    i = pl.program_id(0)
    last = pl.num_programs(0) - 1

    def fetch(tile, slot):
        start = pl.multiple_of(tile * tf, tf)
        return pltpu.make_async_copy(h_hbm.at[pl.ds(start, tf), :], hbuf.at[slot], fsem.at[slot])

    def wait_rows(step):
        for _ in range(TOP_K):
            pltpu.make_async_copy(hbuf.at[0], xpad_ref.at[pl.ds(0, tf), :], sem.at[step % 2]).wait()

    @pl.when(i == 0)
    def _():
        fetch(0, 0).start()

        @pl.when(last >= 1)
        def _():
            fetch(1, 1).start()

        zero_s[...] = jnp.zeros(zero_s.shape, F32)

        def last_block(e):
            start = pl.multiple_of(pend_ref[e] - EXPERT_BLOCK, EXPERT_BLOCK)
            return pltpu.make_async_copy(zero_s, xpad_ref.at[pl.ds(start, EXPERT_BLOCK), :], zsem)

        for e in range(N_EXPERTS):
            @pl.when(padded_ref[e] > 0)
            def _(e=e):
                last_block(e).start()
        for e in range(N_EXPERTS):
            @pl.when(padded_ref[e] > 0)
            def _(e=e):
                last_block(e).wait()

    slot = i % 3
    fetch(i, slot).wait()
    src = hbuf.at[slot]
    for k in range(TOP_K):
        for t in range(tf):
            _row_copy(src, t, xpad_ref, dest_ref[0, k * tf + t], sem.at[i % 2]).start(priority=t % 2)

    @pl.when(i >= 1)
    def _():
        wait_rows(i - 1)

    @pl.when(i + 2 <= last)
    def _():
        fetch(i + 2, (i + 2) % 3).start()

    @pl.when(i == last)
    def _():
        wait_rows(i)


def _dispatch(pad_ends, padded, dest_tiles, h2d, rows, tf):
    n, d = h2d.shape
    return pl.pallas_call(
        _dispatch_kernel,
        grid_spec=pltpu.PrefetchScalarGridSpec(
            num_scalar_prefetch=2,
            grid=(n // tf,),
            in_specs=[
                pl.BlockSpec((None, 1, TOP_K * tf), lambda i, *_: (i, 0, 0), memory_space=pltpu.SMEM),
                pl.BlockSpec(memory_space=pl.ANY),
            ],
            out_specs=pl.BlockSpec(memory_space=pl.ANY),
            scratch_shapes=[pltpu.VMEM((EXPERT_BLOCK, d), F32), pltpu.VMEM((3, tf, d), F32),
                            pltpu.SemaphoreType.DMA((2,)), pltpu.SemaphoreType.DMA(()),
                            pltpu.SemaphoreType.DMA((3,))],
        ),
        out_shape=jax.ShapeDtypeStruct((rows, d), F32),
        compiler_params=_params("arbitrary"),
        name="dispatch",
    )(pad_ends, padded, dest_tiles, h2d)


def _experts_kernel(be_ref, nb_ref, x_ref, wg_ref, bg_ref, wu_ref, bu_ref, wd_ref, bd_ref, y_ref,
                    wg_s, wu_s, wd_s):
    j = pl.program_id(0)
    used = j < nb_ref[0]
    new_expert = jnp.logical_or(j == 0, be_ref[j] != be_ref[jnp.maximum(j - 1, 0)])

    @pl.when(jnp.logical_and(used, new_expert))
    def _():
        wg_s[...] = wg_ref[...].astype(BF16)
        wu_s[...] = wu_ref[...].astype(BF16)
        wd_s[...] = wd_ref[...].astype(BF16)

    @pl.when(used)
    def _():
        x = x_ref[...].astype(BF16)
        gate = jnp.minimum(_dot(x, wg_s[...]) + bg_ref[...], SWIGLU_LIMIT)
        up = jnp.clip(_dot(x, wu_s[...]) + bu_ref[...], -SWIGLU_LIMIT, SWIGLU_LIMIT)
        act = (up + 1.0) * gate * _sigmoid(gate * SWIGLU_ALPHA)
        y_ref[...] = _dot(act.astype(BF16), wd_s[...]) + bd_ref[...]


def _experts(block_e, nb_used, x_pad, wg, bg, wu, bu, wd, bd):
    rows, d = x_pad.shape
    nblk = rows // EXPERT_BLOCK
    dff = wg.shape[2]
    row_blk = lambda j, be, nb: (jnp.minimum(j, nb[0] - 1), 0)
    w_blk = lambda j, be, nb: (be[j], 0, 0)
    return pl.pallas_call(
        _experts_kernel,
        grid_spec=pltpu.PrefetchScalarGridSpec(
            num_scalar_prefetch=2,
            grid=(nblk,),
            in_specs=[
                pl.BlockSpec((EXPERT_BLOCK, d), row_blk),
                pl.BlockSpec((None, d, dff), w_blk),
                pl.BlockSpec((None, 1, dff), w_blk),
                pl.BlockSpec((None, d, dff), w_blk),
                pl.BlockSpec((None, 1, dff), w_blk),
                pl.BlockSpec((None, dff, d), w_blk),
                pl.BlockSpec((None, 1, d), w_blk),
            ],
            out_specs=pl.BlockSpec((EXPERT_BLOCK, d), row_blk),
            scratch_shapes=[pltpu.VMEM((d, dff), BF16), pltpu.VMEM((d, dff), BF16),
                            pltpu.VMEM((dff, d), BF16)],
        ),
        out_shape=jax.ShapeDtypeStruct((rows, d), F32),
        compiler_params=pltpu.CompilerParams(dimension_semantics=("arbitrary",),
                                             vmem_limit_bytes=V7X_EXPERTS_VMEM_LIMIT),
        name="experts",
    )(block_e, nb_used, x_pad, wg, bg, wu, bu, wd, bd)


def _combine_kernel(dest_ref, dnext_ref, x2_ref, gate_ref, ypad_ref, o_ref, buf, sem):
    th = x2_ref.shape[0]
    i = pl.program_id(0)
    slot = i % 2

    def gather(dref, s):
        for k in range(TOP_K):
            for t in range(th):
                _row_copy(ypad_ref, dref[0, k * th + t], buf.at[s, k], t, sem.at[s]).start(priority=t % 2)

    @pl.when(i == 0)
    def _():
        gather(dest_ref, 0)

    @pl.when(i + 1 < pl.num_programs(0))
    def _():
        gather(dnext_ref, 1 - slot)

    for k in range(TOP_K):
        pltpu.make_async_copy(ypad_ref.at[pl.ds(0, th), :], buf.at[slot, k], sem.at[slot]).wait()
    g = gate_ref[...]
    out = x2_ref[...]
    for k in range(TOP_K):
        out = out + buf[slot, k] * g[:, k:k + 1]
    o_ref[...] = out


def _combine(dest_tiles, x2d, gates_nk, y_pad, th):
    n, d = x2d.shape
    nt = n // th
    return pl.pallas_call(
        _combine_kernel,
        grid=(nt,),
        in_specs=[
            pl.BlockSpec((None, 1, TOP_K * th), lambda i: (i, 0, 0), memory_space=pltpu.SMEM),
            pl.BlockSpec((None, 1, TOP_K * th), lambda i: (jnp.minimum(i + 1, nt - 1), 0, 0),
                         memory_space=pltpu.SMEM),
            pl.BlockSpec((th, d), lambda i: (i, 0)),
            pl.BlockSpec((th, TOP_K), lambda i: (i, 0)),
            pl.BlockSpec(memory_space=pl.ANY),
        ],
        out_specs=pl.BlockSpec((th, d), lambda i: (i, 0)),
        out_shape=jax.ShapeDtypeStruct((n, d), F32),
        scratch_shapes=[pltpu.VMEM((2, TOP_K, th, d), F32), pltpu.SemaphoreType.DMA((2,))],
        compiler_params=_params("arbitrary"),
        name="combine",
    )(dest_tiles, dest_tiles, x2d, gates_nk, y_pad)


def _tile_dest(dest, t):
    k, n = dest.shape
    return dest.reshape(k, n // t, t).transpose(1, 0, 2).reshape(n // t, 1, k * t)


def _layer(x, mem, norm1_w, w_in, conv_w, a_log, dt_bias, dn_norm_w, sb_q_norm_w, sb_k_norm_w, w_out,
           norm2_w, mem_norm_w, xq_w, xk_w, xv_w, xq_norm_w, xk_norm_w, xo_w, norm3_w, router_w,
           router_b, w_gate, b_gate, w_up, b_up, w_down, b_down):
    b, s, d = x.shape
    n = b * s
    tm_proj = min(512, s)
    tm_mid = min(512, s)
    sub_mid = min(256, s)
    t_moe = min(256, s)

    o_dn, o_ab, o_sb = 4 * DN_WIDTH, 4 * DN_WIDTH + 2 * DN_HEADS, 4 * DN_WIDTH + 2 * DN_HEADS
    w_dn = w_in[:, :o_dn].astype(BF16)
    w_ab_f = w_in[:, o_dn:o_ab]
    w_ab = jnp.pad(w_ab_f, ((0, 0), (0, LANES - 2 * DN_HEADS))).astype(BF16)
    w_abt = w_ab_f.T.astype(BF16)
    w_sb = w_in[:, o_sb:].astype(BF16)
    row = lambda v: v.reshape(1, -1).astype(F32)

    dn, sb, ab, abt = _in_proj(x.reshape(n, d), row(norm1_w), w_dn, w_sb, w_ab, w_abt, tm_proj)

    abt4 = abt.reshape(8, n // CHUNK, CHUNK).transpose(1, 0, 2).reshape(b, s // CHUNK, 8, CHUNK)
    pad_lane = lambda v: jnp.pad(v.astype(F32), (0, LANES - v.shape[0])).reshape(1, LANES)
    pad_col = lambda v: jnp.pad(v.astype(F32), (0, 8 - v.shape[0])).reshape(8, 1)
    tap = jnp.arange(CONV_WIDTH * CHUNK)
    shift = (jnp.arange(2 * CHUNK)[None, :] == (CHUNK - CONV_WIDTH + 1 + tap // CHUNK + tap % CHUNK)[:, None])
    y_dn = _deltanet(dn.reshape(b, s, -1), ab.reshape(b, s, LANES), abt4, conv_w.astype(F32),
                     shift.astype(BF16), pad_lane(a_log), pad_lane(dt_bias), pad_col(a_log), pad_col(dt_bias),
                     row(dn_norm_w))

    ii = jnp.arange(SB_BLOCK)
    m2 = jnp.concatenate([(ii[:, None] > ii[None, :]).astype(BF16),
                          jnp.ones((SB_BLOCK, SB_BLOCK), BF16)], axis=1)
    sb_heads = SB_LANES // SB_DIM
    hh = jnp.arange(SB_LANES) // SB_DIM
    hsum = (hh[:, None] == hh[None, :]).astype(BF16)
    y_sb = _stickbreak(sb.reshape(b, s, -1), row(jnp.tile(sb_q_norm_w, sb_heads)),
                       row(jnp.tile(sb_k_norm_w, sb_heads)), m2, hsum)

    k_mem, v_mem = _memkv(mem, row(mem_norm_w), xk_w.astype(BF16), xv_w.astype(BF16), row(xk_norm_w))

    wr_t = router_w.T.astype(F32)
    wr_hi = wr_t.astype(BF16)
    wr_lo = (wr_t - wr_hi.astype(F32)).astype(BF16)
    jj = jnp.arange(sub_mid)
    upper = (jj[:, None] < jj[None, :]).astype(BF16)
    x2, h3, idx, gates, rank, cnt = _mid(
        x, y_dn, y_sb, w_out.astype(BF16), row(norm2_w), xq_w.astype(BF16), row(xq_norm_w), k_mem, v_mem,
        xo_w.astype(BF16), row(norm3_w), wr_hi, wr_lo, router_b.reshape(N_EXPERTS, 1).astype(F32), upper,
        tm_mid)

    counts = cnt[:, 0].astype(jnp.int32)
    padded = (counts + EXPERT_BLOCK - 1) // EXPERT_BLOCK * EXPERT_BLOCK
    pad_ends = jnp.cumsum(padded)
    pad_starts = pad_ends - padded
    sel = idx[:, :, None] == jnp.arange(N_EXPERTS, dtype=jnp.int32)[None, None, :]
    dest = rank + jnp.sum(jnp.where(sel, pad_starts[None, None, :], 0), axis=-1)
    n_blocks = -(-n * TOP_K // EXPERT_BLOCK) + N_EXPERTS
    nb_used = (pad_ends[-1] // EXPERT_BLOCK).astype(jnp.int32)
    blk = jnp.minimum(jnp.arange(n_blocks, dtype=jnp.int32), nb_used - 1) * EXPERT_BLOCK
    block_e = jnp.minimum(jnp.sum(pad_ends[None, :] <= blk[:, None], axis=1), N_EXPERTS - 1).astype(jnp.int32)
    dest_tiles = _tile_dest(dest, t_moe)

    x_pad = _dispatch(pad_ends.astype(jnp.int32), padded.astype(jnp.int32), dest_tiles, h3.reshape(n, d),
                      n_blocks * EXPERT_BLOCK, t_moe)
    bias = lambda v: v.reshape(N_EXPERTS, 1, -1).astype(F32)
    y_pad = _experts(block_e, nb_used.reshape(1), x_pad, w_gate.astype(F32), bias(b_gate),
                     w_up.astype(F32), bias(b_up), w_down.astype(F32), bias(b_down))
    out = _combine(dest_tiles, x2.reshape(n, d), gates.T, y_pad, t_moe)
    return out.reshape(b, s, d)


def kernel(x, mem, norm1_w, w_in, conv_w, a_log, dt_bias, dn_norm_w, sb_q_norm_w, sb_k_norm_w, w_out,
           norm2_w, mem_norm_w, xq_w, xk_w, xv_w, xq_norm_w, xk_norm_w, xo_w, norm3_w, router_w,
           router_b, w_gate, b_gate, w_up, b_up, w_down, b_down):
    depth = w_in.shape[0]
    for l in range(depth):
        x = _layer(x, mem, norm1_w[l], w_in[l], conv_w[l], a_log[l], dt_bias[l], dn_norm_w[l],
                   sb_q_norm_w[l], sb_k_norm_w[l], w_out[l], norm2_w[l], mem_norm_w[l], xq_w[l], xk_w[l],
                   xv_w[l], xq_norm_w[l], xk_norm_w[l], xo_w[l], norm3_w[l], router_w[l], router_b[l],
                   w_gate[l], b_gate[l], w_up[l], b_up[l], w_down[l], b_down[l])
    return x
```

```python
import functools

import jax
import jax.numpy as jnp
from jax import lax
from jax.experimental import pallas as pl
from jax.experimental.pallas import tpu as pltpu

F32 = jnp.float32
BF16 = jnp.bfloat16

EPS = 1e-6
CHUNK = 64
DN_HEADS = 4
DN_DIM = 128
DN_WIDTH = DN_HEADS * DN_DIM
CONV_WIDTH = 4
DN_PAIR = 4
SB_HEADS = 8
SB_DIM = 64
SB_WIDTH = SB_HEADS * SB_DIM
SB_BLOCK = 128
SB_GROUP = 4
SB_LANES = 256
X_HEADS = 4
X_DIM = 256
N_EXPERTS = 32
TOP_K = 4
EXPERT_BLOCK = 512
SWIGLU_ALPHA = 1.702
SWIGLU_LIMIT = 7.0
LANES = 128
LOG2E = 1.4426950408889634
SB_UNDERFLOW_LOG2 = -127.0
V7X_VMEM_LIMIT = 48 * 1024 * 1024
V7X_EXPERTS_VMEM_LIMIT = 58 * 1024 * 1024


def _dot(a, b):
    return jnp.dot(a, b, preferred_element_type=F32)


def _dot_nt(a, b):
    return lax.dot_general(a, b, (((1,), (1,)), ((), ())), preferred_element_type=F32)


def _dot_tn(a, b):
    return lax.dot_general(a, b, (((0,), (0,)), ((), ())), preferred_element_type=F32)


def _split3(x):
    hi = x.astype(BF16)
    r = x - hi.astype(F32)
    mid = r.astype(BF16)
    return hi, mid, (r - mid.astype(F32)).astype(BF16)


def _softplus(x):
    return jnp.maximum(x, 0.0) + jnp.log(1.0 + jnp.exp(-jnp.abs(x)))


def _sigmoid(x):
    return 1.0 / (1.0 + jnp.exp(-x))


def _rms(x, w):
    return x * lax.rsqrt(jnp.mean(x * x, axis=-1, keepdims=True) + EPS) * w


def _params(*sem):
    return pltpu.CompilerParams(dimension_semantics=sem, vmem_limit_bytes=V7X_VMEM_LIMIT)


def _const_spec(shape):
    nd = len(shape)
    return pl.BlockSpec(shape, lambda *_: (0,) * nd)


def _in_proj_kernel(x_ref, nw_ref, wdn_ref, wsb_ref, wab_ref, wabt_ref,
                    dn_ref, sb_ref, ab_ref, abt_ref):
    n = _rms(x_ref[...], nw_ref[...]).astype(BF16)
    dn_ref[...] = _dot(n, wdn_ref[...]).astype(BF16)
    sb_ref[...] = _dot(n, wsb_ref[...]).astype(BF16)
    ab_ref[...] = _dot(n, wab_ref[...])
    abt_ref[...] = _dot_nt(wabt_ref[...], n)


def _in_proj(x2d, norm_w, w_dn, w_sb, w_ab, w_abt, tm):
    n, d = x2d.shape
    return pl.pallas_call(
        _in_proj_kernel,
        grid=(n // tm,),
        in_specs=[
            pl.BlockSpec((tm, d), lambda i: (i, 0)),
            _const_spec((1, d)),
            _const_spec(w_dn.shape),
            _const_spec(w_sb.shape),
            _const_spec(w_ab.shape),
            _const_spec(w_abt.shape),
        ],
        out_specs=[
            pl.BlockSpec((tm, w_dn.shape[1]), lambda i: (i, 0)),
            pl.BlockSpec((tm, w_sb.shape[1]), lambda i: (i, 0)),
            pl.BlockSpec((tm, LANES), lambda i: (i, 0)),
            pl.BlockSpec((8, tm), lambda i: (0, i)),
        ],
        out_shape=[
            jax.ShapeDtypeStruct((n, w_dn.shape[1]), BF16),
            jax.ShapeDtypeStruct((n, w_sb.shape[1]), BF16),
            jax.ShapeDtypeStruct((n, LANES), F32),
            jax.ShapeDtypeStruct((8, n), F32),
        ],
        compiler_params=_params("arbitrary"),
        name="in_proj",
    )(x2d, norm_w, w_dn, w_sb, w_ab, w_abt)


def _deltanet_kernel(dn_ref, ab_ref, abt_ref, convw_ref, shift_ref, alog_ref, dtb_ref, alogt_ref, dtbt_ref,
                     normw_ref, o_ref, state_s, u_s, w_s, kt_s, qg_s, attn_s, egl_s):
    s = dn_ref.shape[0]
    steps = s // (CHUNK * DN_PAIR)
    state_s[...] = jnp.zeros(state_s.shape, F32)

    def conv_silu(r0, c, part, h, l2):
        cs = slice(part * DN_WIDTH + h * LANES, part * DN_WIDTH + (h + 1) * LANES)
        prev0 = pl.multiple_of(jnp.maximum(r0 - CHUNK, 0), CHUNK)
        prev = dn_ref[pl.ds(prev0, CHUNK), cs]
        prev = jnp.where(c > 0, prev, jnp.zeros_like(prev))
        win = jnp.concatenate([prev, dn_ref[pl.ds(r0, CHUNK), cs]], axis=0)
        sh = _dot(shift_ref[...], win)
        w = convw_ref[:, cs]
        y = w[0:1, :] * sh[0:CHUNK, :]
        for i in range(1, CONV_WIDTH):
            y = y + w[i:i + 1, :] * sh[i * CHUNK:(i + 1) * CHUNK, :]
        y = y * _sigmoid(y)
        if l2:
            y = y * lax.rsqrt(jnp.sum(y * y, axis=-1, keepdims=True) + EPS)
        return y

    row = lax.broadcasted_iota(jnp.int32, (CHUNK, CHUNK), 0)
    col = lax.broadcasted_iota(jnp.int32, (CHUNK, CHUNK), 1)
    tri = row >= col
    strict = row > col
    tril16 = jnp.where(tri, 1.0, 0.0).astype(BF16)
    triu16 = jnp.where(row <= col, 1.0, 0.0).astype(BF16)
    neg_a_col = -jnp.exp(alog_ref[...])
    neg_a_row = -jnp.exp(alogt_ref[...])
    scale = DN_DIM ** -0.5
    heads = range(DN_HEADS)
    cols = [slice(h * LANES, (h + 1) * LANES) for h in heads]

    def pre_items(it):
        items = []
        for ci in range(DN_PAIR):
            c = it * DN_PAIR + ci
            r0 = pl.multiple_of(c * CHUNK, CHUNK)
            ab = ab_ref[pl.ds(r0, CHUNK), :]
            g_col = neg_a_col * _softplus(ab + dtb_ref[...])
            gc_col_all = sum(_dot(tril16, p) for p in _split3(g_col))
            beta_all = _sigmoid(ab)
            abt = abt_ref[c]
            g_row = neg_a_row * _softplus(abt + dtbt_ref[...])
            gc_row_all = sum(_dot(p, triu16) for p in _split3(g_row))
            for h in heads:
                q = conv_silu(r0, c, 0, h, True) * scale
                k = conv_silu(r0, c, 1, h, True)
                v = conv_silu(r0, c, 2, h, False)
                gcol = gc_col_all[:, h:h + 1]
                grow = gc_row_all[h:h + 1, :]
                beta = beta_all[:, DN_HEADS + h:DN_HEADS + h + 1]
                glast = gcol[CHUNK - 1:CHUNK, :]
                decay = jnp.where(tri, jnp.exp(jnp.where(tri, gcol - grow, 0.0)), 0.0)
                kb = k * beta
                kt_s[pl.ds(r0, CHUNK), cols[h]] = (k * jnp.exp(glast - gcol)).astype(BF16)
                qg_s[pl.ds(r0, CHUNK), cols[h]] = (q * jnp.exp(gcol)).astype(BF16)
                egl_s[c, h:h + 1, :] = jnp.broadcast_to(jnp.exp(glast), (1, LANES))
                items.append(dict(c=c, r0=r0, h=h, decay=decay, kb16=kb.astype(BF16),
                                  k16=k.astype(BF16), q16=q.astype(BF16),
                                  sol=jnp.concatenate([v * beta, kb * jnp.exp(gcol)], axis=1)))
        return items

    def pre_gram(items):
        kk = [_dot_nt(t["kb16"], t["k16"]) for t in items]
        qk = [_dot_nt(t["q16"], t["k16"]) for t in items]
        for t, kk_i, qk_i in zip(items, kk, qk):
            attn_s[t["c"], t["h"]] = jnp.where(tri, qk_i * t["decay"], 0.0).astype(BF16)
            t["tm"] = -jnp.where(strict, kk_i * t["decay"], 0.0)
            t["p16"] = t["tm"].astype(BF16)
        return [_dot(t["p16"], t["p16"]) for t in items]

    def pre_level(items, sq, last):
        for i, t in enumerate(items):
            t["pw"] = sq[i]
            t["p16"] = sq[i].astype(BF16)
        app = [_dot(t["p16"], t["tm"].astype(BF16)) for t in items]
        nxt = None if last else [_dot(t["p16"], t["p16"]) for t in items]
        for i, t in enumerate(items):
            t["tm"] = t["tm"] + t["pw"] + app[i]
        return nxt

    def pre_solve(items):
        corr = [_dot(t["tm"].astype(BF16), t["sol"].astype(BF16)) for t in items]
        for i, t in enumerate(items):
            sol = t["sol"] + corr[i]
            u_s[pl.ds(t["r0"], CHUNK), cols[t["h"]]] = sol[:, :DN_DIM]
            w_s[pl.ds(t["r0"], CHUNK), cols[t["h"]]] = sol[:, DN_DIM:].astype(BF16)

    def rec_read(c):
        r0 = pl.multiple_of(c * CHUNK, CHUNK)
        st = [state_s[h] for h in heads]
        st16 = [x.astype(BF16) for x in st]
        ws = [_dot(w_s[pl.ds(r0, CHUNK), cols[h]], st16[h]) for h in heads]
        qs = [_dot(qg_s[pl.ds(r0, CHUNK), cols[h]], st16[h]) for h in heads]
        return dict(c=c, r0=r0, st=st, ws=ws, qs=qs)

    def rec_update(rd):
        c, r0 = rd["c"], rd["r0"]
        vn16 = [(u_s[pl.ds(r0, CHUNK), cols[h]] - rd["ws"][h]).astype(BF16) for h in heads]
        av = [_dot(attn_s[c, h], vn16[h]) for h in heads]
        ks = [_dot_tn(kt_s[pl.ds(r0, CHUNK), cols[h]], vn16[h]) for h in heads]
        for h in heads:
            state_s[h] = rd["st"][h] * egl_s[c, h:h + 1, :] + ks[h]
            o = rd["qs"][h] + av[h]
            o = o * lax.rsqrt(jnp.mean(o * o, axis=-1, keepdims=True) + EPS)
            z = dn_ref[pl.ds(r0, CHUNK), 3 * DN_WIDTH + h * LANES:3 * DN_WIDTH + (h + 1) * LANES].astype(F32)
            o_ref[pl.ds(r0, CHUNK), cols[h]] = (o * normw_ref[...] * (z * _sigmoid(z))).astype(BF16)

    def step(it, with_pre, with_rec):
        hooks = []
        for ci in range(DN_PAIR if with_rec else 0):
            hooks += [("read", ci), ("update", ci)]
        pending = {}

        def run_hook():
            if hooks:
                kind, ci = hooks.pop(0)
                if kind == "read":
                    pending[ci] = rec_read((it - 1) * DN_PAIR + ci)
                else:
                    rec_update(pending.pop(ci))

        if not with_pre:
            while hooks:
                run_hook()
            return
        slots = [(i * 7) // max(len(hooks), 1) for i in range(len(hooks))]

        def boundary(b):
            for _ in range(slots.count(b)):
                run_hook()

        items = pre_items(it)
        boundary(0)
        sq = pre_gram(items)
        boundary(1)
        for level in range(1, 6):
            sq = pre_level(items, sq, level == 5)
            boundary(level + 1)
        pre_solve(items)
        while hooks:
            run_hook()

    step(0, True, False)

    def fused(it, carry):
        step(it, True, True)
        return carry

    lax.fori_loop(1, steps, fused, 0)
    step(steps, False, True)


def _deltanet(dn3, ab3, abt4, conv_w, shift, alog, dtb, alogt, dtbt, norm_w):
    b, s, _ = dn3.shape
    nch = s // CHUNK
    return pl.pallas_call(
        _deltanet_kernel,
        grid=(b,),
        in_specs=[
            pl.BlockSpec((None, s, 4 * DN_WIDTH), lambda i: (i, 0, 0)),
            pl.BlockSpec((None, s, LANES), lambda i: (i, 0, 0)),
            pl.BlockSpec((None, nch, 8, CHUNK), lambda i: (i, 0, 0, 0)),
            _const_spec(conv_w.shape),
            _const_spec(shift.shape),
            _const_spec((1, LANES)),
            _const_spec((1, LANES)),
            _const_spec((8, 1)),
            _const_spec((8, 1)),
            _const_spec((1, DN_DIM)),
        ],
        out_specs=pl.BlockSpec((None, s, DN_WIDTH), lambda i: (i, 0, 0)),
        out_shape=jax.ShapeDtypeStruct((b, s, DN_WIDTH), BF16),
        scratch_shapes=[
            pltpu.VMEM((DN_HEADS, DN_DIM, DN_DIM), F32),
            pltpu.VMEM((s, DN_WIDTH), F32),
            pltpu.VMEM((s, DN_WIDTH), BF16),
            pltpu.VMEM((s, DN_WIDTH), BF16),
            pltpu.VMEM((s, DN_WIDTH), BF16),
            pltpu.VMEM((nch, DN_HEADS, CHUNK, CHUNK), BF16),
            pltpu.VMEM((nch, 8, LANES), F32),
        ],
        compiler_params=_params("arbitrary"),
        name="deltanet",
    )(dn3, ab3, abt4, conv_w, shift, alog, dtb, alogt, dtbt, norm_w)


def _stickbreak_kernel(q_ref, k_ref, v_ref, qw_ref, kw_ref, m2_ref, hsum_ref, o_ref,
                       qn_s, kn_s, vm_s, carry_s, acc_s):
    s, width = q_ref.shape
    nh = width // SB_DIM
    group = carry_s.shape[0]
    lane = lax.broadcasted_iota(jnp.int32, (1, width), 1)
    head_lanes = [jnp.logical_and(lane >= h * SB_DIM, lane < (h + 1) * SB_DIM) for h in range(nh)]

    def head_norm(r0, x_ref, w):
        x = x_ref[pl.ds(r0, SB_BLOCK), :].astype(F32)
        sq = x * x
        hi = sq.astype(BF16)
        lo = (sq - hi.astype(F32)).astype(BF16)
        ms = (_dot(hi, hsum_ref[...]) + _dot(lo, hsum_ref[...])) * (1.0 / SB_DIM)
        return x * lax.rsqrt(ms + EPS) * w

    def norm_block(i, carry):
        r0 = pl.multiple_of(i * SB_BLOCK, SB_BLOCK)
        qn = head_norm(r0, q_ref, qw_ref[...]) * (SB_DIM ** -0.5 * LOG2E)
        v = v_ref[pl.ds(r0, SB_BLOCK), :]
        for h in range(nh):
            qn_s[h, pl.ds(r0, SB_BLOCK), :] = jnp.where(head_lanes[h], qn, 0.0).astype(BF16)
            vm_s[i, h * SB_BLOCK:(h + 1) * SB_BLOCK, :] = jnp.where(head_lanes[h], v, jnp.zeros_like(v))
        kn_s[pl.ds(r0, SB_BLOCK), :] = head_norm(r0, k_ref, kw_ref[...]).astype(BF16)
        return carry

    lax.fori_loop(0, s // SB_BLOCK, norm_block, 0)

    row = lax.broadcasted_iota(jnp.int32, (SB_BLOCK, SB_BLOCK), 0)
    col = lax.broadcasted_iota(jnp.int32, (SB_BLOCK, SB_BLOCK), 1)
    causal = col < row

    def step(rows, key_blocks, valid):
        tiles = [(g, h) for g in range(group) for h in range(nh)]
        k16 = [kn_s[pl.ds(pl.multiple_of(kb * SB_BLOCK, SB_BLOCK), SB_BLOCK), :] for kb in key_blocks]
        z = [_dot_nt(qn_s[h, pl.ds(rows[g], SB_BLOCK), :], k16[g]) for g, h in tiles]
        sp = [jnp.maximum(x, 0.0) + jnp.log(1.0 + jnp.exp2(-jnp.abs(x))) * LOG2E for x in z]
        fail = [jnp.where(causal, x, 0.0) for x in sp] if valid is None else sp
        cs = [_dot(x.astype(BF16), m2_ref[...]) for x in fail]
        p = []
        for i, (g, h) in enumerate(tiles):
            w = jnp.exp2(z[i] - sp[i] + carry_s[g, h] + cs[i][:, :SB_BLOCK])
            p.append((jnp.where(causal, w, 0.0) if valid is None else w).astype(BF16))
        pv = [_dot(jnp.concatenate(p[g * nh:(g + 1) * nh], axis=1), vm_s[key_blocks[g]])
              for g in range(group)]
        for g in range(group):
            acc = acc_s[g] + pv[g]
            acc_s[g] = acc if valid is None else jnp.where(valid[g], acc, acc_s[g])
        for i, (g, h) in enumerate(tiles):
            carry = carry_s[g, h] + cs[i][:, SB_BLOCK:]
            carry_s[g, h] = carry if valid is None else jnp.where(valid[g], carry, carry_s[g, h])

    def q_group(qg, carry):
        blocks = [qg * group + g for g in range(group)]
        rows = [pl.multiple_of(qb * SB_BLOCK, SB_BLOCK) for qb in blocks]
        carry_s[...] = jnp.zeros(carry_s.shape, F32)
        acc_s[...] = jnp.zeros(acc_s.shape, F32)
        step(rows, blocks, None)

        def cond(st):
            d, alive = st
            return jnp.logical_and(d <= blocks[-1], alive)

        def body(st):
            d, _ = st
            step(rows, [jnp.maximum(qb - d, 0) for qb in blocks], [qb >= d for qb in blocks])
            return d + 1, jnp.max(carry_s[...]) > SB_UNDERFLOW_LOG2

        lax.while_loop(cond, body, (jnp.int32(1), jnp.bool_(True)))
        for g in range(group):
            o_ref[pl.ds(rows[g], SB_BLOCK), :] = acc_s[g].astype(BF16)
        return carry

    lax.fori_loop(0, s // (SB_BLOCK * group), q_group, 0)


def _stickbreak(sb3, qw, kw, m2, hsum):
    b, s, _ = sb3.shape
    width = hsum.shape[0]
    nh = width // SB_DIM
    parts = SB_WIDTH // width
    group = min(SB_GROUP, s // SB_BLOCK)
    return pl.pallas_call(
        _stickbreak_kernel,
        grid=(b, parts),
        in_specs=[
            pl.BlockSpec((None, s, width), lambda i, j: (i, 0, j)),
            pl.BlockSpec((None, s, width), lambda i, j: (i, 0, parts + j)),
            pl.BlockSpec((None, s, width), lambda i, j: (i, 0, 2 * parts + j)),
            _const_spec((1, width)),
            _const_spec((1, width)),
            _const_spec(m2.shape),
            _const_spec(hsum.shape),
        ],
        out_specs=pl.BlockSpec((None, s, width), lambda i, j: (i, 0, j)),
        out_shape=jax.ShapeDtypeStruct((b, s, SB_WIDTH), BF16),
        scratch_shapes=[
            pltpu.VMEM((nh, s, width), BF16),
            pltpu.VMEM((s, width), BF16),
            pltpu.VMEM((s // SB_BLOCK, nh * SB_BLOCK, width), BF16),
            pltpu.VMEM((group, nh, SB_BLOCK, SB_BLOCK), F32),
            pltpu.VMEM((group, SB_BLOCK, width), F32),
        ],
        compiler_params=_params("arbitrary", "arbitrary"),
        name="stickbreak",
    )(sb3, sb3, sb3, qw, kw, m2, hsum)


def _memkv_kernel(mem_ref, nw_ref, wk_ref, wv_ref, knw_ref, k_ref, v_ref):
    n = _rms(mem_ref[...], nw_ref[...]).astype(BF16)
    k = _dot(n, wk_ref[...])
    for h in range(X_HEADS):
        cs = slice(h * X_DIM, (h + 1) * X_DIM)
        k_ref[:, cs] = _rms(k[:, cs], knw_ref[...]).astype(BF16)
    v_ref[...] = _dot(n, wv_ref[...]).astype(BF16)


def _memkv(mem, norm_w, wk, wv, k_norm_w):
    b, m, d = mem.shape
    return pl.pallas_call(
        _memkv_kernel,
        grid=(b,),
        in_specs=[
            pl.BlockSpec((None, m, d), lambda i: (i, 0, 0)),
            _const_spec((1, d)),
            _const_spec(wk.shape),
            _const_spec(wv.shape),
            _const_spec((1, X_DIM)),
        ],
        out_specs=[
            pl.BlockSpec((None, m, d), lambda i: (i, 0, 0)),
            pl.BlockSpec((None, m, d), lambda i: (i, 0, 0)),
        ],
        out_shape=[jax.ShapeDtypeStruct((b, m, d), BF16)] * 2,
        compiler_params=_params("arbitrary"),
        name="memkv",
    )(mem, norm_w, wk, wv, k_norm_w)


def _mid_kernel(x_ref, ydn_ref, ysb_ref, wout_ref, n2w_ref, wq_ref, qnw_ref, km_ref, vm_ref, wo_ref,
                n3w_ref, wrh_ref, wrl_ref, rb_ref, upper_ref,
                x2_ref, h3_ref, idx_ref, gate_ref, rank_ref, cnt_ref, count_s):
    tm = x_ref.shape[0]
    first_step = jnp.logical_and(pl.program_id(0) == 0, pl.program_id(1) == 0)

    @pl.when(first_step)
    def _():
        count_s[...] = jnp.zeros(count_s.shape, F32)

    sub = upper_ref.shape[0]
    subs = range(tm // sub)
    rows = [slice(i * sub, (i + 1) * sub) for i in subs]
    heads = range(X_HEADS)
    cols = [slice(h * X_DIM, (h + 1) * X_DIM) for h in heads]
    x1 = [x_ref[r, :] + _dot(ydn_ref[r, :], wout_ref[0:DN_WIDTH, :])
          + _dot(ysb_ref[r, :], wout_ref[DN_WIDTH:, :]) for r in rows]
    n2 = [_rms(v, n2w_ref[...]).astype(BF16) for v in x1]
    q = [_dot(v, wq_ref[...]) for v in n2]
    qh = [[(_rms(q[i][:, c], qnw_ref[...]) * (X_DIM ** -0.5)).astype(BF16) for c in cols] for i in subs]
    sc = [[_dot_nt(qh[i][h], km_ref[:, cols[h]]) for h in heads] for i in subs]
    ex = [[jnp.exp(sc[i][h] - jnp.max(sc[i][h], axis=-1, keepdims=True)) for h in heads] for i in subs]
    pr = [[(ex[i][h] / jnp.sum(ex[i][h], axis=-1, keepdims=True)).astype(BF16) for h in heads] for i in subs]
    oh = [[_dot(pr[i][h], vm_ref[:, cols[h]]).astype(BF16) for h in heads] for i in subs]
    x2 = [x1[i] + _dot(jnp.concatenate(oh[i], axis=1), wo_ref[...]) for i in subs]
    h3 = [_rms(v, n3w_ref[...]) for v in x2]
    hi = [v.astype(BF16) for v in h3]
    lo = [(h3[i] - hi[i].astype(F32)).astype(BF16) for i in subs]
    logits = [_dot_nt(wrh_ref[...], hi[i]) + _dot_nt(wrh_ref[...], lo[i]) + _dot_nt(wrl_ref[...], hi[i])
              + rb_ref[...] for i in subs]
    eid = lax.broadcasted_iota(jnp.int32, (N_EXPERTS, sub), 0).astype(F32)
    count = count_s[...]
    for i in subs:
        x2_ref[rows[i], :] = x2[i]
        h3_ref[rows[i], :] = h3[i]
        vals, ids = [], []
        cur = logits[i]
        for _ in range(TOP_K):
            m = jnp.max(cur, axis=0, keepdims=True)
            j = jnp.min(jnp.where(cur == m, eid, float(N_EXPERTS)), axis=0, keepdims=True)
            vals.append(m)
            ids.append(j)
            cur = jnp.where(eid == j, -jnp.inf, cur)
        exps = [jnp.exp(v - vals[0]) for v in vals]
        denom = exps[0] + exps[1] + exps[2] + exps[3]
        onehot = jnp.zeros((N_EXPERTS, sub), F32)
        for j in ids:
            onehot = onehot + jnp.where(eid == j, 1.0, 0.0)
        before = count + _dot(onehot.astype(BF16), upper_ref[...])
        for k in range(TOP_K):
            idx_ref[k:k + 1, rows[i]] = ids[k].astype(jnp.int32)
            gate_ref[k:k + 1, rows[i]] = exps[k] / denom
            rank_ref[k:k + 1, rows[i]] = jnp.sum(jnp.where(eid == ids[k], before, 0.0), axis=0,
                                                 keepdims=True).astype(jnp.int32)
        count = count + jnp.sum(onehot, axis=1, keepdims=True)
    count_s[...] = count
    cnt_ref[...] = jnp.broadcast_to(count, cnt_ref.shape)


def _mid(x3, ydn3, ysb3, w_out, n2w, wq, qnw, k_mem, v_mem, wo, n3w, wr_hi, wr_lo, rb, upper, tm):
    b, s, d = x3.shape
    n = b * s
    nt = s // tm
    m = k_mem.shape[1]
    tok = lambda i, j: (0, i * nt + j)
    return pl.pallas_call(
        _mid_kernel,
        grid=(b, nt),
        in_specs=[
            pl.BlockSpec((None, tm, d), lambda i, j: (i, j, 0)),
            pl.BlockSpec((None, tm, DN_WIDTH), lambda i, j: (i, j, 0)),
            pl.BlockSpec((None, tm, SB_WIDTH), lambda i, j: (i, j, 0)),
            _const_spec(w_out.shape),
            _const_spec((1, d)),
            _const_spec(wq.shape),
            _const_spec((1, X_DIM)),
            pl.BlockSpec((None, m, d), lambda i, j: (i, 0, 0)),
            pl.BlockSpec((None, m, d), lambda i, j: (i, 0, 0)),
            _const_spec(wo.shape),
            _const_spec((1, d)),
            _const_spec(wr_hi.shape),
            _const_spec(wr_lo.shape),
            _const_spec((N_EXPERTS, 1)),
            _const_spec(upper.shape),
        ],
        out_specs=[
            pl.BlockSpec((None, tm, d), lambda i, j: (i, j, 0)),
            pl.BlockSpec((None, tm, d), lambda i, j: (i, j, 0)),
            pl.BlockSpec((TOP_K, tm), tok),
            pl.BlockSpec((TOP_K, tm), tok),
            pl.BlockSpec((TOP_K, tm), tok),
            _const_spec((N_EXPERTS, LANES)),
        ],
        out_shape=[
            jax.ShapeDtypeStruct((b, s, d), F32),
            jax.ShapeDtypeStruct((b, s, d), F32),
            jax.ShapeDtypeStruct((TOP_K, n), jnp.int32),
            jax.ShapeDtypeStruct((TOP_K, n), F32),
            jax.ShapeDtypeStruct((TOP_K, n), jnp.int32),
            jax.ShapeDtypeStruct((N_EXPERTS, LANES), F32),
        ],
        scratch_shapes=[pltpu.VMEM((N_EXPERTS, 1), F32)],
        compiler_params=_params("arbitrary", "arbitrary"),
        name="mid",
    )(x3, ydn3, ysb3, w_out, n2w, wq, qnw, k_mem, v_mem, wo, n3w, wr_hi, wr_lo, rb, upper)


def _row_copy(src_ref, src_row, dst_ref, dst_row, sem):
    return pltpu.make_async_copy(src_ref.at[pl.ds(src_row, 1), :], dst_ref.at[pl.ds(dst_row, 1), :], sem)


def _dispatch_kernel(pend_ref, padded_ref, dest_ref, h_hbm, xpad_ref, zero_s, hbuf, sem, zsem, fsem):
    tf = hbuf.shape[1]
    i = pl.program_id(0)
    last = pl.num_programs(0) - 1

    def fetch(tile, slot):
        start = pl.multiple_of(tile * tf, tf)
        return pltpu.make_async_copy(h_hbm.at[pl.ds(start, tf), :], hbuf.at[slot], fsem.at[slot])

    def wait_rows(step):
        for _ in range(TOP_K):
            pltpu.make_async_copy(hbuf.at[0], xpad_ref.at[pl.ds(0, tf), :], sem.at[step % 2]).wait()

    @pl.when(i == 0)
    def _():
        fetch(0, 0).start()

        @pl.when(last >= 1)
        def _():
            fetch(1, 1).start()

        zero_s[...] = jnp.zeros(zero_s.shape, F32)

        def last_block(e):
            start = pl.multiple_of(pend_ref[e] - EXPERT_BLOCK, EXPERT_BLOCK)
            return pltpu.make_async_copy(zero_s, xpad_ref.at[pl.ds(start, EXPERT_BLOCK), :], zsem)

        for e in range(N_EXPERTS):
            @pl.when(padded_ref[e] > 0)
            def _(e=e):
                last_block(e).start()
        for e in range(N_EXPERTS):
            @pl.when(padded_ref[e] > 0)
            def _(e=e):
                last_block(e).wait()

    slot = i % 3
    fetch(i, slot).wait()
    src = hbuf.at[slot]
    for k in range(TOP_K):
        for t in range(tf):
            _row_copy(src, t, xpad_ref, dest_ref[0, k * tf + t], sem.at[i % 2]).start(priority=t % 2)

    @pl.when(i >= 1)
    def _():
        wait_rows(i - 1)

    @pl.when(i + 2 <= last)
    def _():
        fetch(i + 2, (i + 2) % 3).start()

    @pl.when(i == last)
    def _():
        wait_rows(i)


def _dispatch(pad_ends, padded, dest_tiles, h2d, rows, tf):
    n, d = h2d.shape
    return pl.pallas_call(
        _dispatch_kernel,
        grid_spec=pltpu.PrefetchScalarGridSpec(
            num_scalar_prefetch=2,
            grid=(n // tf,),
            in_specs=[
                pl.BlockSpec((None, 1, TOP_K * tf), lambda i, *_: (i, 0, 0), memory_space=pltpu.SMEM),
                pl.BlockSpec(memory_space=pl.ANY),
            ],
            out_specs=pl.BlockSpec(memory_space=pl.ANY),
            scratch_shapes=[pltpu.VMEM((EXPERT_BLOCK, d), F32), pltpu.VMEM((3, tf, d), F32),
                            pltpu.SemaphoreType.DMA((2,)), pltpu.SemaphoreType.DMA(()),
                            pltpu.SemaphoreType.DMA((3,))],
        ),
        out_shape=jax.ShapeDtypeStruct((rows, d), F32),
        compiler_params=_params("arbitrary"),
        name="dispatch",
    )(pad_ends, padded, dest_tiles, h2d)


def _experts_kernel(be_ref, nb_ref, x_ref, wg_ref, bg_ref, wu_ref, bu_ref, wd_ref, bd_ref, y_ref,
                    wg_s, wu_s, wd_s):
    j = pl.program_id(0)
    used = j < nb_ref[0]
    new_expert = jnp.logical_or(j == 0, be_ref[j] != be_ref[jnp.maximum(j - 1, 0)])

    @pl.when(jnp.logical_and(used, new_expert))
    def _():
        wg_s[...] = wg_ref[...].astype(BF16)
        wu_s[...] = wu_ref[...].astype(BF16)
        wd_s[...] = wd_ref[...].astype(BF16)

    @pl.when(used)
    def _():
        x = x_ref[...].astype(BF16)
        gate = jnp.minimum(_dot(x, wg_s[...]) + bg_ref[...], SWIGLU_LIMIT)
        up = jnp.clip(_dot(x, wu_s[...]) + bu_ref[...], -SWIGLU_LIMIT, SWIGLU_LIMIT)
        act = (up + 1.0) * gate * _sigmoid(gate * SWIGLU_ALPHA)
        y_ref[...] = _dot(act.astype(BF16), wd_s[...]) + bd_ref[...]


def _experts(block_e, nb_used, x_pad, wg, bg, wu, bu, wd, bd):
    rows, d = x_pad.shape
    nblk = rows // EXPERT_BLOCK
    dff = wg.shape[2]
    row_blk = lambda j, be, nb: (jnp.minimum(j, nb[0] - 1), 0)
    w_blk = lambda j, be, nb: (be[j], 0, 0)
    return pl.pallas_call(
        _experts_kernel,
        grid_spec=pltpu.PrefetchScalarGridSpec(
            num_scalar_prefetch=2,
            grid=(nblk,),
            in_specs=[
                pl.BlockSpec((EXPERT_BLOCK, d), row_blk),
                pl.BlockSpec((None, d, dff), w_blk),
                pl.BlockSpec((None, 1, dff), w_blk),
                pl.BlockSpec((None, d, dff), w_blk),
                pl.BlockSpec((None, 1, dff), w_blk),
                pl.BlockSpec((None, dff, d), w_blk),
                pl.BlockSpec((None, 1, d), w_blk),
            ],
            out_specs=pl.BlockSpec((EXPERT_BLOCK, d), row_blk),
            scratch_shapes=[pltpu.VMEM((d, dff), BF16), pltpu.VMEM((d, dff), BF16),
                            pltpu.VMEM((dff, d), BF16)],
        ),
        out_shape=jax.ShapeDtypeStruct((rows, d), F32),
        compiler_params=pltpu.CompilerParams(dimension_semantics=("arbitrary",),
                                             vmem_limit_bytes=V7X_EXPERTS_VMEM_LIMIT),
        name="experts",
    )(block_e, nb_used, x_pad, wg, bg, wu, bu, wd, bd)


def _combine_kernel(dest_ref, dnext_ref, x2_ref, gate_ref, ypad_ref, o_ref, buf, sem):
    th = x2_ref.shape[0]
    i = pl.program_id(0)
    slot = i % 2

    def gather(dref, s):
        for k in range(TOP_K):
            for t in range(th):
                _row_copy(ypad_ref, dref[0, k * th + t], buf.at[s, k], t, sem.at[s]).start(priority=t % 2)

    @pl.when(i == 0)
    def _():
        gather(dest_ref, 0)

    @pl.when(i + 1 < pl.num_programs(0))
    def _():
        gather(dnext_ref, 1 - slot)

    for k in range(TOP_K):
        pltpu.make_async_copy(ypad_ref.at[pl.ds(0, th), :], buf.at[slot, k], sem.at[slot]).wait()
    g = gate_ref[...]
    out = x2_ref[...]
    for k in range(TOP_K):
        out = out + buf[slot, k] * g[:, k:k + 1]
    o_ref[...] = out


def _combine(dest_tiles, x2d, gates_nk, y_pad, th):
    n, d = x2d.shape
    nt = n // th
    return pl.pallas_call(
        _combine_kernel,
        grid=(nt,),
        in_specs=[
            pl.BlockSpec((None, 1, TOP_K * th), lambda i: (i, 0, 0), memory_space=pltpu.SMEM),
            pl.BlockSpec((None, 1, TOP_K * th), lambda i: (jnp.minimum(i + 1, nt - 1), 0, 0),
                         memory_space=pltpu.SMEM),
            pl.BlockSpec((th, d), lambda i: (i, 0)),
            pl.BlockSpec((th, TOP_K), lambda i: (i, 0)),
            pl.BlockSpec(memory_space=pl.ANY),
        ],
        out_specs=pl.BlockSpec((th, d), lambda i: (i, 0)),
        out_shape=jax.ShapeDtypeStruct((n, d), F32),
        scratch_shapes=[pltpu.VMEM((2, TOP_K, th, d), F32), pltpu.SemaphoreType.DMA((2,))],
        compiler_params=_params("arbitrary"),
        name="combine",
    )(dest_tiles, dest_tiles, x2d, gates_nk, y_pad)


def _tile_dest(dest, t):
    k, n = dest.shape
    return dest.reshape(k, n // t, t).transpose(1, 0, 2).reshape(n // t, 1, k * t)


def _layer(x, mem, norm1_w, w_in, conv_w, a_log, dt_bias, dn_norm_w, sb_q_norm_w, sb_k_norm_w, w_out,
           norm2_w, mem_norm_w, xq_w, xk_w, xv_w, xq_norm_w, xk_norm_w, xo_w, norm3_w, router_w,
           router_b, w_gate, b_gate, w_up, b_up, w_down, b_down):
    b, s, d = x.shape
    n = b * s
    tm_proj = min(512, s)
    tm_mid = min(512, s)
    sub_mid = min(256, s)
    t_moe = min(256, s)

    o_dn, o_ab, o_sb = 4 * DN_WIDTH, 4 * DN_WIDTH + 2 * DN_HEADS, 4 * DN_WIDTH + 2 * DN_HEADS
    w_dn = w_in[:, :o_dn].astype(BF16)
    w_ab_f = w_in[:, o_dn:o_ab]
    w_ab = jnp.pad(w_ab_f, ((0, 0), (0, LANES - 2 * DN_HEADS))).astype(BF16)
    w_abt = w_ab_f.T.astype(BF16)
    w_sb = w_in[:, o_sb:].astype(BF16)
    row = lambda v: v.reshape(1, -1).astype(F32)

    dn, sb, ab, abt = _in_proj(x.reshape(n, d), row(norm1_w), w_dn, w_sb, w_ab, w_abt, tm_proj)

    abt4 = abt.reshape(8, n // CHUNK, CHUNK).transpose(1, 0, 2).reshape(b, s // CHUNK, 8, CHUNK)
    pad_lane = lambda v: jnp.pad(v.astype(F32), (0, LANES - v.shape[0])).reshape(1, LANES)
    pad_col = lambda v: jnp.pad(v.astype(F32), (0, 8 - v.shape[0])).reshape(8, 1)
    tap = jnp.arange(CONV_WIDTH * CHUNK)
    shift = (jnp.arange(2 * CHUNK)[None, :] == (CHUNK - CONV_WIDTH + 1 + tap // CHUNK + tap % CHUNK)[:, None])
    y_dn = _deltanet(dn.reshape(b, s, -1), ab.reshape(b, s, LANES), abt4, conv_w.astype(F32),
                     shift.astype(BF16), pad_lane(a_log), pad_lane(dt_bias), pad_col(a_log), pad_col(dt_bias),
                     row(dn_norm_w))

    ii = jnp.arange(SB_BLOCK)
    m2 = -jnp.concatenate([(ii[:, None] > ii[None, :]).astype(BF16),
                           jnp.ones((SB_BLOCK, SB_BLOCK), BF16)], axis=1)
    sb_heads = SB_LANES // SB_DIM
    hh = jnp.arange(SB_LANES) // SB_DIM
    hsum = (hh[:, None] == hh[None, :]).astype(BF16)
    y_sb = _stickbreak(sb.reshape(b, s, -1), row(jnp.tile(sb_q_norm_w, sb_heads)),
                       row(jnp.tile(sb_k_norm_w, sb_heads)), m2, hsum)

    k_mem, v_mem = _memkv(mem, row(mem_norm_w), xk_w.astype(BF16), xv_w.astype(BF16), row(xk_norm_w))

    wr_t = router_w.T.astype(F32)
    wr_hi = wr_t.astype(BF16)
    wr_lo = (wr_t - wr_hi.astype(F32)).astype(BF16)
    jj = jnp.arange(sub_mid)
    upper = (jj[:, None] < jj[None, :]).astype(BF16)
    x2, h3, idx, gates, rank, cnt = _mid(
        x, y_dn, y_sb, w_out.astype(BF16), row(norm2_w), xq_w.astype(BF16), row(xq_norm_w), k_mem, v_mem,
        xo_w.astype(BF16), row(norm3_w), wr_hi, wr_lo, router_b.reshape(N_EXPERTS, 1).astype(F32), upper,
        tm_mid)

    counts = cnt[:, 0].astype(jnp.int32)
    padded = (counts + EXPERT_BLOCK - 1) // EXPERT_BLOCK * EXPERT_BLOCK
    pad_ends = jnp.cumsum(padded)
    pad_starts = pad_ends - padded
    sel = idx[:, :, None] == jnp.arange(N_EXPERTS, dtype=jnp.int32)[None, None, :]
    dest = rank + jnp.sum(jnp.where(sel, pad_starts[None, None, :], 0), axis=-1)
    n_blocks = -(-n * TOP_K // EXPERT_BLOCK) + N_EXPERTS
    nb_used = (pad_ends[-1] // EXPERT_BLOCK).astype(jnp.int32)
    blk = jnp.minimum(jnp.arange(n_blocks, dtype=jnp.int32), nb_used - 1) * EXPERT_BLOCK
    block_e = jnp.minimum(jnp.sum(pad_ends[None, :] <= blk[:, None], axis=1), N_EXPERTS - 1).astype(jnp.int32)
    dest_tiles = _tile_dest(dest, t_moe)

    x_pad = _dispatch(pad_ends.astype(jnp.int32), padded.astype(jnp.int32), dest_tiles, h3.reshape(n, d),
                      n_blocks * EXPERT_BLOCK, t_moe)
    bias = lambda v: v.reshape(N_EXPERTS, 1, -1).astype(F32)
    y_pad = _experts(block_e, nb_used.reshape(1), x_pad, w_gate.astype(F32), bias(b_gate),
                     w_up.astype(F32), bias(b_up), w_down.astype(F32), bias(b_down))
    out = _combine(dest_tiles, x2.reshape(n, d), gates.T, y_pad, t_moe)
    return out.reshape(b, s, d)


def kernel(x, mem, norm1_w, w_in, conv_w, a_log, dt_bias, dn_norm_w, sb_q_norm_w, sb_k_norm_w, w_out,
           norm2_w, mem_norm_w, xq_w, xk_w, xv_w, xq_norm_w, xk_norm_w, xo_w, norm3_w, router_w,
           router_b, w_gate, b_gate, w_up, b_up, w_down, b_down):
    depth = w_in.shape[0]
    for l in range(depth):
        x = _layer(x, mem, norm1_w[l], w_in[l], conv_w[l], a_log[l], dt_bias[l], dn_norm_w[l],
                   sb_q_norm_w[l], sb_k_norm_w[l], w_out[l], norm2_w[l], mem_norm_w[l], xq_w[l], xk_w[l],
                   xv_w[l], xq_norm_w[l], xk_norm_w[l], xo_w[l], norm3_w[l], router_w[l], router_b[l],
                   w_gate[l], b_gate[l], w_up[l], b_up[l], w_down[l], b_down[l])
    return x
```

```python
import jax
import jax.numpy as jnp
from jax import lax
from jax.experimental import pallas as pl
from jax.experimental.pallas import tpu as pltpu

F32 = jnp.float32
BF16 = jnp.bfloat16

EPS = 1e-6
CHUNK = 64
DN_HEADS = 4
DN_DIM = 128
DN_WIDTH = DN_HEADS * DN_DIM
CONV_WIDTH = 4
DN_PAIR = 4
SB_HEADS = 8
SB_DIM = 64
SB_WIDTH = SB_HEADS * SB_DIM
SB_BLOCK = 128
SB_GROUP = 4
SB_LANES = 256
X_HEADS = 4
X_DIM = 256
N_EXPERTS = 32
TOP_K = 4
EXPERT_BLOCK = 512
SWIGLU_ALPHA = 1.702
SWIGLU_LIMIT = 7.0
LANES = 128
LOG2E = 1.4426950408889634
SB_UNDERFLOW_LOG2 = -127.0
V7X_VMEM_LIMIT = 48 * 1024 * 1024
V7X_EXPERTS_VMEM_LIMIT = 58 * 1024 * 1024


def _dot(a, b):
    return jnp.dot(a, b, preferred_element_type=F32)


def _dot_nt(a, b):
    return lax.dot_general(a, b, (((1,), (1,)), ((), ())), preferred_element_type=F32)


def _dot_tn(a, b):
    return lax.dot_general(a, b, (((0,), (0,)), ((), ())), preferred_element_type=F32)


def _split3(x):
    hi = x.astype(BF16)
    r = x - hi.astype(F32)
    mid = r.astype(BF16)
    return hi, mid, (r - mid.astype(F32)).astype(BF16)


def _softplus(x):
    return jnp.maximum(x, 0.0) + jnp.log(1.0 + jnp.exp(-jnp.abs(x)))


def _sigmoid(x):
    return 1.0 / (1.0 + jnp.exp(-x))


def _rms(x, w):
    return x * lax.rsqrt(jnp.mean(x * x, axis=-1, keepdims=True) + EPS) * w


def _params(*sem):
    return pltpu.CompilerParams(dimension_semantics=sem, vmem_limit_bytes=V7X_VMEM_LIMIT)


def _const_spec(shape):
    nd = len(shape)
    return pl.BlockSpec(shape, lambda *_: (0,) * nd)


def _in_proj_kernel(x_ref, nw_ref, wdn_ref, wsb_ref, wab_ref, wabt_ref,
                    dn_ref, sb_ref, ab_ref, abt_ref):
    n = _rms(x_ref[...], nw_ref[...]).astype(BF16)
    dn_ref[...] = _dot(n, wdn_ref[...]).astype(BF16)
    sb_ref[...] = _dot(n, wsb_ref[...]).astype(BF16)
    ab_ref[...] = _dot(n, wab_ref[...])
    abt_ref[...] = _dot_nt(wabt_ref[...], n)


def _in_proj(x2d, norm_w, w_dn, w_sb, w_ab, w_abt, tm):
    n, d = x2d.shape
    return pl.pallas_call(
        _in_proj_kernel,
        grid=(n // tm,),
        in_specs=[
            pl.BlockSpec((tm, d), lambda i: (i, 0)),
            _const_spec((1, d)),
            _const_spec(w_dn.shape),
            _const_spec(w_sb.shape),
            _const_spec(w_ab.shape),
            _const_spec(w_abt.shape),
        ],
        out_specs=[
            pl.BlockSpec((tm, w_dn.shape[1]), lambda i: (i, 0)),
            pl.BlockSpec((tm, w_sb.shape[1]), lambda i: (i, 0)),
            pl.BlockSpec((tm, LANES), lambda i: (i, 0)),
            pl.BlockSpec((8, tm), lambda i: (0, i)),
        ],
        out_shape=[
            jax.ShapeDtypeStruct((n, w_dn.shape[1]), BF16),
            jax.ShapeDtypeStruct((n, w_sb.shape[1]), BF16),
            jax.ShapeDtypeStruct((n, LANES), F32),
            jax.ShapeDtypeStruct((8, n), F32),
        ],
        compiler_params=_params("arbitrary"),
        name="in_proj",
    )(x2d, norm_w, w_dn, w_sb, w_ab, w_abt)


def _deltanet_kernel(dn_ref, ab_ref, abt_ref, convw_ref, shift_ref, alog_ref, dtb_ref, alogt_ref, dtbt_ref,
                     normw_ref, o_ref, state_s, u_s, w_s, kt_s, qg_s, attn_s, egl_s):
    s = dn_ref.shape[0]
    steps = s // (CHUNK * DN_PAIR)
    state_s[...] = jnp.zeros(state_s.shape, F32)

    def conv_silu(r0, c, part, h, l2):
        cs = slice(part * DN_WIDTH + h * LANES, part * DN_WIDTH + (h + 1) * LANES)
        prev0 = pl.multiple_of(jnp.maximum(r0 - CHUNK, 0), CHUNK)
        prev = dn_ref[pl.ds(prev0, CHUNK), cs]
        prev = jnp.where(c > 0, prev, jnp.zeros_like(prev))
        win = jnp.concatenate([prev, dn_ref[pl.ds(r0, CHUNK), cs]], axis=0)
        sh = _dot(shift_ref[...], win)
        w = convw_ref[:, cs]
        y = w[0:1, :] * sh[0:CHUNK, :]
        for i in range(1, CONV_WIDTH):
            y = y + w[i:i + 1, :] * sh[i * CHUNK:(i + 1) * CHUNK, :]
        y = y * _sigmoid(y)
        if l2:
            y = y * lax.rsqrt(jnp.sum(y * y, axis=-1, keepdims=True) + EPS)
        return y

    row = lax.broadcasted_iota(jnp.int32, (CHUNK, CHUNK), 0)
    col = lax.broadcasted_iota(jnp.int32, (CHUNK, CHUNK), 1)
    tri = row >= col
    strict = row > col
    tril16 = jnp.where(tri, 1.0, 0.0).astype(BF16)
    triu16 = jnp.where(row <= col, 1.0, 0.0).astype(BF16)
    neg_a_col = -jnp.exp(alog_ref[...])
    neg_a_row = -jnp.exp(alogt_ref[...])
    scale = DN_DIM ** -0.5
    heads = range(DN_HEADS)
    cols = [slice(h * LANES, (h + 1) * LANES) for h in heads]

    def pre_items(it):
        items = []
        for ci in range(DN_PAIR):
            c = it * DN_PAIR + ci
            r0 = pl.multiple_of(c * CHUNK, CHUNK)
            ab = ab_ref[pl.ds(r0, CHUNK), :]
            g_col = neg_a_col * _softplus(ab + dtb_ref[...])
            gc_col_all = sum(_dot(tril16, p) for p in _split3(g_col))
            beta_all = _sigmoid(ab)
            abt = abt_ref[c]
            g_row = neg_a_row * _softplus(abt + dtbt_ref[...])
            gc_row_all = sum(_dot(p, triu16) for p in _split3(g_row))
            for h in heads:
                q = conv_silu(r0, c, 0, h, True) * scale
                k = conv_silu(r0, c, 1, h, True)
                v = conv_silu(r0, c, 2, h, False)
                gcol = gc_col_all[:, h:h + 1]
                grow = gc_row_all[h:h + 1, :]
                beta = beta_all[:, DN_HEADS + h:DN_HEADS + h + 1]
                glast = gcol[CHUNK - 1:CHUNK, :]
                decay = jnp.where(tri, jnp.exp(jnp.where(tri, gcol - grow, 0.0)), 0.0)
                kb = k * beta
                kt_s[pl.ds(r0, CHUNK), cols[h]] = (k * jnp.exp(glast - gcol)).astype(BF16)
                qg_s[pl.ds(r0, CHUNK), cols[h]] = (q * jnp.exp(gcol)).astype(BF16)
                egl_s[c, h:h + 1, :] = jnp.broadcast_to(jnp.exp(glast), (1, LANES))
                items.append(dict(c=c, r0=r0, h=h, decay=decay, kb16=kb.astype(BF16),
                                  k16=k.astype(BF16), q16=q.astype(BF16),
                                  sol=jnp.concatenate([v * beta, kb * jnp.exp(gcol)], axis=1)))
        return items

    def pre_gram(items):
        kk = [_dot_nt(t["kb16"], t["k16"]) for t in items]
        qk = [_dot_nt(t["q16"], t["k16"]) for t in items]
        for t, kk_i, qk_i in zip(items, kk, qk):
            attn_s[t["c"], t["h"]] = jnp.where(tri, qk_i * t["decay"], 0.0).astype(BF16)
            t["tm"] = -jnp.where(strict, kk_i * t["decay"], 0.0)
            t["p16"] = t["tm"].astype(BF16)
        return [_dot(t["p16"], t["p16"]) for t in items]

    def pre_level(items, sq, last):
        for i, t in enumerate(items):
            t["pw"] = sq[i]
            t["p16"] = sq[i].astype(BF16)
        app = [_dot(t["p16"], t["tm"].astype(BF16)) for t in items]
        nxt = None if last else [_dot(t["p16"], t["p16"]) for t in items]
        for i, t in enumerate(items):
            t["tm"] = t["tm"] + t["pw"] + app[i]
        return nxt

    def pre_solve(items):
        corr = [_dot(t["tm"].astype(BF16), t["sol"].astype(BF16)) for t in items]
        for i, t in enumerate(items):
            sol = t["sol"] + corr[i]
            u_s[pl.ds(t["r0"], CHUNK), cols[t["h"]]] = sol[:, :DN_DIM]
            w_s[pl.ds(t["r0"], CHUNK), cols[t["h"]]] = sol[:, DN_DIM:].astype(BF16)

    def rec_read(c):
        r0 = pl.multiple_of(c * CHUNK, CHUNK)
        st = [state_s[h] for h in heads]
        st16 = [x.astype(BF16) for x in st]
        ws = [_dot(w_s[pl.ds(r0, CHUNK), cols[h]], st16[h]) for h in heads]
        qs = [_dot(qg_s[pl.ds(r0, CHUNK), cols[h]], st16[h]) for h in heads]
        return dict(c=c, r0=r0, st=st, ws=ws, qs=qs)

    def rec_update(rd):
        c, r0 = rd["c"], rd["r0"]
        vn16 = [(u_s[pl.ds(r0, CHUNK), cols[h]] - rd["ws"][h]).astype(BF16) for h in heads]
        av = [_dot(attn_s[c, h], vn16[h]) for h in heads]
        ks = [_dot_tn(kt_s[pl.ds(r0, CHUNK), cols[h]], vn16[h]) for h in heads]
        for h in heads:
            state_s[h] = rd["st"][h] * egl_s[c, h:h + 1, :] + ks[h]
            o = rd["qs"][h] + av[h]
            o = o * lax.rsqrt(jnp.mean(o * o, axis=-1, keepdims=True) + EPS)
            z = dn_ref[pl.ds(r0, CHUNK), 3 * DN_WIDTH + h * LANES:3 * DN_WIDTH + (h + 1) * LANES].astype(F32)
            o_ref[pl.ds(r0, CHUNK), cols[h]] = (o * normw_ref[...] * (z * _sigmoid(z))).astype(BF16)

    def step(it, with_pre, with_rec):
        hooks = []
        for ci in range(DN_PAIR if with_rec else 0):
            hooks += [("read", ci), ("update", ci)]
        pending = {}

        def run_hook():
            if hooks:
                kind, ci = hooks.pop(0)
                if kind == "read":
                    pending[ci] = rec_read((it - 1) * DN_PAIR + ci)
                else:
                    rec_update(pending.pop(ci))

        if not with_pre:
            while hooks:
                run_hook()
            return
        slots = [(i * 7) // max(len(hooks), 1) for i in range(len(hooks))]

        def boundary(b):
            for _ in range(slots.count(b)):
                run_hook()

        items = pre_items(it)
        boundary(0)
        sq = pre_gram(items)
        boundary(1)
        for level in range(1, 6):
            sq = pre_level(items, sq, level == 5)
            boundary(level + 1)
        pre_solve(items)
        while hooks:
            run_hook()

    step(0, True, False)

    def fused(it, carry):
        step(it, True, True)
        return carry

    lax.fori_loop(1, steps, fused, 0)
    step(steps, False, True)


def _deltanet(dn3, ab3, abt4, conv_w, shift, alog, dtb, alogt, dtbt, norm_w):
    b, s, _ = dn3.shape
    nch = s // CHUNK
    return pl.pallas_call(
        _deltanet_kernel,
        grid=(b,),
        in_specs=[
            pl.BlockSpec((None, s, 4 * DN_WIDTH), lambda i: (i, 0, 0)),
            pl.BlockSpec((None, s, LANES), lambda i: (i, 0, 0)),
            pl.BlockSpec((None, nch, 8, CHUNK), lambda i: (i, 0, 0, 0)),
            _const_spec(conv_w.shape),
            _const_spec(shift.shape),
            _const_spec((1, LANES)),
            _const_spec((1, LANES)),
            _const_spec((8, 1)),
            _const_spec((8, 1)),
            _const_spec((1, DN_DIM)),
        ],
        out_specs=pl.BlockSpec((None, s, DN_WIDTH), lambda i: (i, 0, 0)),
        out_shape=jax.ShapeDtypeStruct((b, s, DN_WIDTH), BF16),
        scratch_shapes=[
            pltpu.VMEM((DN_HEADS, DN_DIM, DN_DIM), F32),
            pltpu.VMEM((s, DN_WIDTH), F32),
            pltpu.VMEM((s, DN_WIDTH), BF16),
            pltpu.VMEM((s, DN_WIDTH), BF16),
            pltpu.VMEM((s, DN_WIDTH), BF16),
            pltpu.VMEM((nch, DN_HEADS, CHUNK, CHUNK), BF16),
            pltpu.VMEM((nch, 8, LANES), F32),
        ],
        compiler_params=_params("arbitrary"),
        name="deltanet",
    )(dn3, ab3, abt4, conv_w, shift, alog, dtb, alogt, dtbt, norm_w)


def _stickbreak_kernel(q_ref, k_ref, v_ref, qw_ref, kw_ref, m2_ref, hsum_ref, o_ref,
                       qn_s, kn_s, vm_s, carry_s, acc_s):
    s, width = q_ref.shape
    nh = width // SB_DIM
    group = carry_s.shape[0]
    lane = lax.broadcasted_iota(jnp.int32, (1, width), 1)
    head_lanes = [jnp.logical_and(lane >= h * SB_DIM, lane < (h + 1) * SB_DIM) for h in range(nh)]

    def head_norm(r0, x_ref, w):
        x = x_ref[pl.ds(r0, SB_BLOCK), :].astype(F32)
        sq = x * x
        hi = sq.astype(BF16)
        lo = (sq - hi.astype(F32)).astype(BF16)
        ms = (_dot(hi, hsum_ref[...]) + _dot(lo, hsum_ref[...])) * (1.0 / SB_DIM)
        return x * lax.rsqrt(ms + EPS) * w

    def norm_block(i, carry):
        r0 = pl.multiple_of(i * SB_BLOCK, SB_BLOCK)
        qn = head_norm(r0, q_ref, qw_ref[...]) * (SB_DIM ** -0.5 * LOG2E)
        v = v_ref[pl.ds(r0, SB_BLOCK), :]
        for h in range(nh):
            qn_s[h, pl.ds(r0, SB_BLOCK), :] = jnp.where(head_lanes[h], qn, 0.0).astype(BF16)
            vm_s[i, h * SB_BLOCK:(h + 1) * SB_BLOCK, :] = jnp.where(head_lanes[h], v, jnp.zeros_like(v))
        kn_s[pl.ds(r0, SB_BLOCK), :] = head_norm(r0, k_ref, kw_ref[...]).astype(BF16)
        return carry

    lax.fori_loop(0, s // SB_BLOCK, norm_block, 0, unroll=min(4, s // SB_BLOCK))

    row = lax.broadcasted_iota(jnp.int32, (SB_BLOCK, SB_BLOCK), 0)
    col = lax.broadcasted_iota(jnp.int32, (SB_BLOCK, SB_BLOCK), 1)
    causal = col < row

    def step(rows, key_blocks, valid):
        tiles = [(g, h) for g in range(group) for h in range(nh)]
        k16 = [kn_s[pl.ds(pl.multiple_of(kb * SB_BLOCK, SB_BLOCK), SB_BLOCK), :] for kb in key_blocks]
        z = [_dot_nt(qn_s[h, pl.ds(rows[g], SB_BLOCK), :], k16[g]) for g, h in tiles]
        sp = [jnp.maximum(x, 0.0) + jnp.log(1.0 + jnp.exp2(-jnp.abs(x))) * LOG2E for x in z]
        fail = [jnp.where(causal, x, 0.0) for x in sp] if valid is None else sp
        cs = [_dot(x.astype(BF16), m2_ref[...]) for x in fail]
        p = []
        for i, (g, h) in enumerate(tiles):
            w = jnp.exp2(z[i] - sp[i] + carry_s[g, h] + cs[i][:, :SB_BLOCK])
            p.append((jnp.where(causal, w, 0.0) if valid is None else w).astype(BF16))
        pv = [_dot(jnp.concatenate(p[g * nh:(g + 1) * nh], axis=1), vm_s[key_blocks[g]])
              for g in range(group)]
        for g in range(group):
            acc = acc_s[g] + pv[g]
            acc_s[g] = acc if valid is None else jnp.where(valid[g], acc, acc_s[g])
        for i, (g, h) in enumerate(tiles):
            carry = carry_s[g, h] + cs[i][:, SB_BLOCK:]
            carry_s[g, h] = carry if valid is None else jnp.where(valid[g], carry, carry_s[g, h])

    def q_group(qg, carry):
        blocks = [qg * group + g for g in range(group)]
        rows = [pl.multiple_of(qb * SB_BLOCK, SB_BLOCK) for qb in blocks]
        carry_s[...] = jnp.zeros(carry_s.shape, F32)
        acc_s[...] = jnp.zeros(acc_s.shape, F32)
        step(rows, blocks, None)

        def cond(st):
            d, alive = st
            return jnp.logical_and(d <= blocks[-1], alive)

        def body(st):
            d, _ = st
            step(rows, [jnp.maximum(qb - d, 0) for qb in blocks], [qb >= d for qb in blocks])
            return d + 1, jnp.max(carry_s[...]) > SB_UNDERFLOW_LOG2

        lax.while_loop(cond, body, (jnp.int32(1), jnp.bool_(True)))
        for g in range(group):
            o_ref[pl.ds(rows[g], SB_BLOCK), :] = acc_s[g].astype(BF16)
        return carry

    lax.fori_loop(0, s // (SB_BLOCK * group), q_group, 0)


def _stickbreak(sb3, qw, kw, m2, hsum):
    b, s, _ = sb3.shape
    width = hsum.shape[0]
    nh = width // SB_DIM
    parts = SB_WIDTH // width
    group = min(SB_GROUP, s // SB_BLOCK)
    return pl.pallas_call(
        _stickbreak_kernel,
        grid=(b, parts),
        in_specs=[
            pl.BlockSpec((None, s, width), lambda i, j: (i, 0, j)),
            pl.BlockSpec((None, s, width), lambda i, j: (i, 0, parts + j)),
            pl.BlockSpec((None, s, width), lambda i, j: (i, 0, 2 * parts + j)),
            _const_spec((1, width)),
            _const_spec((1, width)),
            _const_spec(m2.shape),
            _const_spec(hsum.shape),
        ],
        out_specs=pl.BlockSpec((None, s, width), lambda i, j: (i, 0, j)),
        out_shape=jax.ShapeDtypeStruct((b, s, SB_WIDTH), BF16),
        scratch_shapes=[
            pltpu.VMEM((nh, s, width), BF16),
            pltpu.VMEM((s, width), BF16),
            pltpu.VMEM((s // SB_BLOCK, nh * SB_BLOCK, width), BF16),
            pltpu.VMEM((group, nh, SB_BLOCK, SB_BLOCK), F32),
            pltpu.VMEM((group, SB_BLOCK, width), F32),
        ],
        compiler_params=_params("arbitrary", "arbitrary"),
        name="stickbreak",
    )(sb3, sb3, sb3, qw, kw, m2, hsum)


def _memkv_kernel(mem_ref, nw_ref, wk_ref, wv_ref, knw_ref, k_ref, v_ref):
    n = _rms(mem_ref[...], nw_ref[...]).astype(BF16)
    k = _dot(n, wk_ref[...])
    for h in range(X_HEADS):
        cs = slice(h * X_DIM, (h + 1) * X_DIM)
        k_ref[:, cs] = _rms(k[:, cs], knw_ref[...]).astype(BF16)
    v_ref[...] = _dot(n, wv_ref[...]).astype(BF16)


def _memkv(mem, norm_w, wk, wv, k_norm_w):
    b, m, d = mem.shape
    return pl.pallas_call(
        _memkv_kernel,
        grid=(b,),
        in_specs=[
            pl.BlockSpec((None, m, d), lambda i: (i, 0, 0)),
            _const_spec((1, d)),
            _const_spec(wk.shape),
            _const_spec(wv.shape),
            _const_spec((1, X_DIM)),
        ],
        out_specs=[
            pl.BlockSpec((None, m, d), lambda i: (i, 0, 0)),
            pl.BlockSpec((None, m, d), lambda i: (i, 0, 0)),
        ],
        out_shape=[jax.ShapeDtypeStruct((b, m, d), BF16)] * 2,
        compiler_params=_params("arbitrary"),
        name="memkv",
    )(mem, norm_w, wk, wv, k_norm_w)


def _mid_kernel(x_ref, ydn_ref, ysb_ref, wout_ref, n2w_ref, wq_ref, qnw_ref, km_ref, vm_ref, wo_ref,
                n3w_ref, wrh_ref, wrl_ref, rb_ref, upper_ref,
                x2_ref, h3_ref, idx_ref, gate_ref, rank_ref, cnt_ref, count_s):
    tm = x_ref.shape[0]
    first_step = jnp.logical_and(pl.program_id(0) == 0, pl.program_id(1) == 0)

    @pl.when(first_step)
    def _():
        count_s[...] = jnp.zeros(count_s.shape, F32)

    sub = upper_ref.shape[0]
    subs = range(tm // sub)
    rows = [slice(i * sub, (i + 1) * sub) for i in subs]
    heads = range(X_HEADS)
    cols = [slice(h * X_DIM, (h + 1) * X_DIM) for h in heads]
    x1 = [x_ref[r, :] + _dot(ydn_ref[r, :], wout_ref[0:DN_WIDTH, :])
          + _dot(ysb_ref[r, :], wout_ref[DN_WIDTH:, :]) for r in rows]
    n2 = [_rms(v, n2w_ref[...]).astype(BF16) for v in x1]
    q = [_dot(v, wq_ref[...]) for v in n2]
    qh = [[(_rms(q[i][:, c], qnw_ref[...]) * (X_DIM ** -0.5)).astype(BF16) for c in cols] for i in subs]
    sc = [[_dot_nt(qh[i][h], km_ref[:, cols[h]]) for h in heads] for i in subs]
    ex = [[jnp.exp(sc[i][h] - jnp.max(sc[i][h], axis=-1, keepdims=True)) for h in heads] for i in subs]
    pr = [[(ex[i][h] / jnp.sum(ex[i][h], axis=-1, keepdims=True)).astype(BF16) for h in heads] for i in subs]
    oh = [[_dot(pr[i][h], vm_ref[:, cols[h]]).astype(BF16) for h in heads] for i in subs]
    x2 = [x1[i] + _dot(jnp.concatenate(oh[i], axis=1), wo_ref[...]) for i in subs]
    h3 = [_rms(v, n3w_ref[...]) for v in x2]
    hi = [v.astype(BF16) for v in h3]
    lo = [(h3[i] - hi[i].astype(F32)).astype(BF16) for i in subs]
    logits = [_dot_nt(wrh_ref[...], hi[i]) + _dot_nt(wrh_ref[...], lo[i]) + _dot_nt(wrl_ref[...], hi[i])
              + rb_ref[...] for i in subs]
    eid = lax.broadcasted_iota(jnp.int32, (N_EXPERTS, sub), 0).astype(F32)
    count = count_s[...]
    for i in subs:
        x2_ref[rows[i], :] = x2[i]
        h3_ref[rows[i], :] = h3[i]
        vals, ids = [], []
        cur = logits[i]
        for _ in range(TOP_K):
            m = jnp.max(cur, axis=0, keepdims=True)
            j = jnp.min(jnp.where(cur == m, eid, float(N_EXPERTS)), axis=0, keepdims=True)
            vals.append(m)
            ids.append(j)
            cur = jnp.where(eid == j, -jnp.inf, cur)
        exps = [jnp.exp(v - vals[0]) for v in vals]
        denom = exps[0] + exps[1] + exps[2] + exps[3]
        onehot = jnp.zeros((N_EXPERTS, sub), F32)
        for j in ids:
            onehot = onehot + jnp.where(eid == j, 1.0, 0.0)
        before = count + _dot(onehot.astype(BF16), upper_ref[...])
        for k in range(TOP_K):
            idx_ref[k:k + 1, rows[i]] = ids[k].astype(jnp.int32)
            gate_ref[k:k + 1, rows[i]] = exps[k] / denom
            rank_ref[k:k + 1, rows[i]] = jnp.sum(jnp.where(eid == ids[k], before, 0.0), axis=0,
                                                 keepdims=True).astype(jnp.int32)
        count = count + jnp.sum(onehot, axis=1, keepdims=True)
    count_s[...] = count
    cnt_ref[...] = jnp.broadcast_to(count, cnt_ref.shape)


def _mid(x3, ydn3, ysb3, w_out, n2w, wq, qnw, k_mem, v_mem, wo, n3w, wr_hi, wr_lo, rb, upper, tm):
    b, s, d = x3.shape
    n = b * s
    nt = s // tm
    m = k_mem.shape[1]
    tok = lambda i, j: (0, i * nt + j)
    return pl.pallas_call(
        _mid_kernel,
        grid=(b, nt),
        in_specs=[
            pl.BlockSpec((None, tm, d), lambda i, j: (i, j, 0)),
            pl.BlockSpec((None, tm, DN_WIDTH), lambda i, j: (i, j, 0)),
            pl.BlockSpec((None, tm, SB_WIDTH), lambda i, j: (i, j, 0)),
            _const_spec(w_out.shape),
            _const_spec((1, d)),
            _const_spec(wq.shape),
            _const_spec((1, X_DIM)),
            pl.BlockSpec((None, m, d), lambda i, j: (i, 0, 0)),
            pl.BlockSpec((None, m, d), lambda i, j: (i, 0, 0)),
            _const_spec(wo.shape),
            _const_spec((1, d)),
            _const_spec(wr_hi.shape),
            _const_spec(wr_lo.shape),
            _const_spec((N_EXPERTS, 1)),
            _const_spec(upper.shape),
        ],
        out_specs=[
            pl.BlockSpec((None, tm, d), lambda i, j: (i, j, 0)),
            pl.BlockSpec((None, tm, d), lambda i, j: (i, j, 0)),
            pl.BlockSpec((TOP_K, tm), tok),
            pl.BlockSpec((TOP_K, tm), tok),
            pl.BlockSpec((TOP_K, tm), tok),
            _const_spec((N_EXPERTS, LANES)),
        ],
        out_shape=[
            jax.ShapeDtypeStruct((b, s, d), F32),
            jax.ShapeDtypeStruct((b, s, d), F32),
            jax.ShapeDtypeStruct((TOP_K, n), jnp.int32),
            jax.ShapeDtypeStruct((TOP_K, n), F32),
            jax.ShapeDtypeStruct((TOP_K, n), jnp.int32),
            jax.ShapeDtypeStruct((N_EXPERTS, LANES), F32),
        ],
        scratch_shapes=[pltpu.VMEM((N_EXPERTS, 1), F32)],
        compiler_params=pltpu.CompilerParams(dimension_semantics=("arbitrary", "arbitrary"),
                                             vmem_limit_bytes=V7X_EXPERTS_VMEM_LIMIT),
        name="mid",
    )(x3, ydn3, ysb3, w_out, n2w, wq, qnw, k_mem, v_mem, wo, n3w, wr_hi, wr_lo, rb, upper)


def _row_copy(src_ref, src_row, dst_ref, dst_row, sem):
    return pltpu.make_async_copy(src_ref.at[pl.ds(src_row, 1), :], dst_ref.at[pl.ds(dst_row, 1), :], sem)


def _dispatch_kernel(pend_ref, padded_ref, dest_ref, h_hbm, xpad_ref, zero_s, hbuf, sem, zsem, fsem):
    tf = hbuf.shape[1]
    i = pl.program_id(0)
    last = pl.num_programs(0) - 1

    def fetch(tile, slot):
        start = pl.multiple_of(tile * tf, tf)
        return pltpu.make_async_copy(h_hbm.at[pl.ds(start, tf), :], hbuf.at[slot], fsem.at[slot])

    def wait_rows(step):
        for _ in range(TOP_K):
            pltpu.make_async_copy(hbuf.at[0], xpad_ref.at[pl.ds(0, tf), :], sem.at[step % 2]).wait()

    @pl.when(i == 0)
    def _():
        fetch(0, 0).start()

        @pl.when(last >= 1)
        def _():
            fetch(1, 1).start()

        zero_s[...] = jnp.zeros(zero_s.shape, F32)

        def last_block(e):
            start = pl.multiple_of(pend_ref[e] - EXPERT_BLOCK, EXPERT_BLOCK)
            return pltpu.make_async_copy(zero_s, xpad_ref.at[pl.ds(start, EXPERT_BLOCK), :], zsem)

        for e in range(N_EXPERTS):
            @pl.when(padded_ref[e] > 0)
            def _(e=e):
                last_block(e).start()
        for e in range(N_EXPERTS):
            @pl.when(padded_ref[e] > 0)
            def _(e=e):
                last_block(e).wait()

    slot = i % 3
    fetch(i, slot).wait()
    src = hbuf.at[slot]
    for k in range(TOP_K):
        for t in range(tf):
            _row_copy(src, t, xpad_ref, dest_ref[0, k * tf + t], sem.at[i % 2]).start(priority=t % 2)

    @pl.when(i >= 1)
    def _():
        wait_rows(i - 1)

    @pl.when(i + 2 <= last)
    def _():
        fetch(i + 2, (i + 2) % 3).start()

    @pl.when(i == last)
    def _():
        wait_rows(i)


def _dispatch(pad_ends, padded, dest_tiles, h2d, rows, tf):
    n, d = h2d.shape
    return pl.pallas_call(
        _dispatch_kernel,
        grid_spec=pltpu.PrefetchScalarGridSpec(
            num_scalar_prefetch=2,
            grid=(n // tf,),
            in_specs=[
                pl.BlockSpec((None, 1, TOP_K * tf), lambda i, *_: (i, 0, 0), memory_space=pltpu.SMEM),
                pl.BlockSpec(memory_space=pl.ANY),
            ],
            out_specs=pl.BlockSpec(memory_space=pl.ANY),
            scratch_shapes=[pltpu.VMEM((EXPERT_BLOCK, d), F32), pltpu.VMEM((3, tf, d), F32),
                            pltpu.SemaphoreType.DMA((2,)), pltpu.SemaphoreType.DMA(()),
                            pltpu.SemaphoreType.DMA((3,))],
        ),
        out_shape=jax.ShapeDtypeStruct((rows, d), F32),
        compiler_params=_params("arbitrary"),
        name="dispatch",
    )(pad_ends, padded, dest_tiles, h2d)


def _experts_kernel(be_ref, nb_ref, x_ref, wg_ref, bg_ref, wu_ref, bu_ref, wd_ref, bd_ref, y_ref,
                    wg_s, wu_s, wd_s):
    j = pl.program_id(0)
    used = j < nb_ref[0]
    new_expert = jnp.logical_or(j == 0, be_ref[j] != be_ref[jnp.maximum(j - 1, 0)])

    @pl.when(jnp.logical_and(used, new_expert))
    def _():
        wg_s[...] = wg_ref[...].astype(BF16)
        wu_s[...] = wu_ref[...].astype(BF16)
        wd_s[...] = wd_ref[...].astype(BF16)

    @pl.when(used)
    def _():
        x = x_ref[...].astype(BF16)
        gate = jnp.minimum(_dot(x, wg_s[...]) + bg_ref[...], SWIGLU_LIMIT)
        up = jnp.clip(_dot(x, wu_s[...]) + bu_ref[...], -SWIGLU_LIMIT, SWIGLU_LIMIT)
        act = (up + 1.0) * gate * _sigmoid(gate * SWIGLU_ALPHA)
        y_ref[...] = _dot(act.astype(BF16), wd_s[...]) + bd_ref[...]


def _experts(block_e, nb_used, x_pad, wg, bg, wu, bu, wd, bd):
    rows, d = x_pad.shape
    nblk = rows // EXPERT_BLOCK
    dff = wg.shape[2]
    row_blk = lambda j, be, nb: (jnp.minimum(j, nb[0] - 1), 0)
    w_blk = lambda j, be, nb: (be[j], 0, 0)
    return pl.pallas_call(
        _experts_kernel,
        grid_spec=pltpu.PrefetchScalarGridSpec(
            num_scalar_prefetch=2,
            grid=(nblk,),
            in_specs=[
                pl.BlockSpec((EXPERT_BLOCK, d), row_blk),
                pl.BlockSpec((None, d, dff), w_blk),
                pl.BlockSpec((None, 1, dff), w_blk),
                pl.BlockSpec((None, d, dff), w_blk),
                pl.BlockSpec((None, 1, dff), w_blk),
                pl.BlockSpec((None, dff, d), w_blk),
                pl.BlockSpec((None, 1, d), w_blk),
            ],
            out_specs=pl.BlockSpec((EXPERT_BLOCK, d), row_blk),
            scratch_shapes=[pltpu.VMEM((d, dff), BF16), pltpu.VMEM((d, dff), BF16),
                            pltpu.VMEM((dff, d), BF16)],
        ),
        out_shape=jax.ShapeDtypeStruct((rows, d), F32),
        compiler_params=pltpu.CompilerParams(dimension_semantics=("arbitrary",),
                                             vmem_limit_bytes=V7X_EXPERTS_VMEM_LIMIT),
        name="experts",
    )(block_e, nb_used, x_pad, wg, bg, wu, bu, wd, bd)


def _combine_kernel(dest_ref, dnext_ref, x2_ref, gate_ref, ypad_ref, o_ref, buf, sem):
    th = x2_ref.shape[0]
    i = pl.program_id(0)
    slot = i % 2

    def gather(dref, s):
        for k in range(TOP_K):
            for t in range(th):
                _row_copy(ypad_ref, dref[0, k * th + t], buf.at[s, k], t, sem.at[s]).start(priority=t % 2)

    @pl.when(i == 0)
    def _():
        gather(dest_ref, 0)

    @pl.when(i + 1 < pl.num_programs(0))
    def _():
        gather(dnext_ref, 1 - slot)

    for k in range(TOP_K):
        pltpu.make_async_copy(ypad_ref.at[pl.ds(0, th), :], buf.at[slot, k], sem.at[slot]).wait()
    g = gate_ref[...]
    out = x2_ref[...]
    for k in range(TOP_K):
        out = out + buf[slot, k] * g[:, k:k + 1]
    o_ref[...] = out


def _combine(dest_tiles, x2d, gates_nk, y_pad, th):
    n, d = x2d.shape
    nt = n // th
    return pl.pallas_call(
        _combine_kernel,
        grid=(nt,),
        in_specs=[
            pl.BlockSpec((None, 1, TOP_K * th), lambda i: (i, 0, 0), memory_space=pltpu.SMEM),
            pl.BlockSpec((None, 1, TOP_K * th), lambda i: (jnp.minimum(i + 1, nt - 1), 0, 0),
                         memory_space=pltpu.SMEM),
            pl.BlockSpec((th, d), lambda i: (i, 0)),
            pl.BlockSpec((th, TOP_K), lambda i: (i, 0)),
            pl.BlockSpec(memory_space=pl.ANY),
        ],
        out_specs=pl.BlockSpec((th, d), lambda i: (i, 0)),
        out_shape=jax.ShapeDtypeStruct((n, d), F32),
        scratch_shapes=[pltpu.VMEM((2, TOP_K, th, d), F32), pltpu.SemaphoreType.DMA((2,))],
        compiler_params=_params("arbitrary"),
        name="combine",
    )(dest_tiles, dest_tiles, x2d, gates_nk, y_pad)


def _tile_dest(dest, t):
    k, n = dest.shape
    return dest.reshape(k, n // t, t).transpose(1, 0, 2).reshape(n // t, 1, k * t)


def _layer(x, mem, norm1_w, w_in, conv_w, a_log, dt_bias, dn_norm_w, sb_q_norm_w, sb_k_norm_w, w_out,
           norm2_w, mem_norm_w, xq_w, xk_w, xv_w, xq_norm_w, xk_norm_w, xo_w, norm3_w, router_w,
           router_b, w_gate, b_gate, w_up, b_up, w_down, b_down):
    b, s, d = x.shape
    n = b * s
    tm_proj = min(512, s)
    tm_mid = min(1024, s)
    sub_mid = min(256, s)
    t_moe = min(256, s)

    o_dn, o_ab, o_sb = 4 * DN_WIDTH, 4 * DN_WIDTH + 2 * DN_HEADS, 4 * DN_WIDTH + 2 * DN_HEADS
    w_dn = w_in[:, :o_dn].astype(BF16)
    w_ab_f = w_in[:, o_dn:o_ab]
    w_ab = jnp.pad(w_ab_f, ((0, 0), (0, LANES - 2 * DN_HEADS))).astype(BF16)
    w_abt = w_ab_f.T.astype(BF16)
    w_sb = w_in[:, o_sb:].astype(BF16)
    row = lambda v: v.reshape(1, -1).astype(F32)

    dn, sb, ab, abt = _in_proj(x.reshape(n, d), row(norm1_w), w_dn, w_sb, w_ab, w_abt, tm_proj)

    abt4 = abt.reshape(8, n // CHUNK, CHUNK).transpose(1, 0, 2).reshape(b, s // CHUNK, 8, CHUNK)
    pad_lane = lambda v: jnp.pad(v.astype(F32), (0, LANES - v.shape[0])).reshape(1, LANES)
    pad_col = lambda v: jnp.pad(v.astype(F32), (0, 8 - v.shape[0])).reshape(8, 1)
    tap = jnp.arange(CONV_WIDTH * CHUNK)
    shift = (jnp.arange(2 * CHUNK)[None, :] == (CHUNK - CONV_WIDTH + 1 + tap // CHUNK + tap % CHUNK)[:, None])
    y_dn = _deltanet(dn.reshape(b, s, -1), ab.reshape(b, s, LANES), abt4, conv_w.astype(F32),
                     shift.astype(BF16), pad_lane(a_log), pad_lane(dt_bias), pad_col(a_log), pad_col(dt_bias),
                     row(dn_norm_w))

    ii = jnp.arange(SB_BLOCK)
    m2 = -jnp.concatenate([(ii[:, None] > ii[None, :]).astype(BF16),
                           jnp.ones((SB_BLOCK, SB_BLOCK), BF16)], axis=1)
    sb_heads = SB_LANES // SB_DIM
    hh = jnp.arange(SB_LANES) // SB_DIM
    hsum = (hh[:, None] == hh[None, :]).astype(BF16)
    y_sb = _stickbreak(sb.reshape(b, s, -1), row(jnp.tile(sb_q_norm_w, sb_heads)),
                       row(jnp.tile(sb_k_norm_w, sb_heads)), m2, hsum)

    k_mem, v_mem = _memkv(mem, row(mem_norm_w), xk_w.astype(BF16), xv_w.astype(BF16), row(xk_norm_w))

    wr_t = router_w.T.astype(F32)
    wr_hi = wr_t.astype(BF16)
    wr_lo = (wr_t - wr_hi.astype(F32)).astype(BF16)
    jj = jnp.arange(sub_mid)
    upper = (jj[:, None] < jj[None, :]).astype(BF16)
    x2, h3, idx, gates, rank, cnt = _mid(
        x, y_dn, y_sb, w_out.astype(BF16), row(norm2_w), xq_w.astype(BF16), row(xq_norm_w), k_mem, v_mem,
        xo_w.astype(BF16), row(norm3_w), wr_hi, wr_lo, router_b.reshape(N_EXPERTS, 1).astype(F32), upper,
        tm_mid)

    counts = cnt[:, 0].astype(jnp.int32)
    padded = (counts + EXPERT_BLOCK - 1) // EXPERT_BLOCK * EXPERT_BLOCK
    pad_ends = jnp.cumsum(padded)
    pad_starts = pad_ends - padded
    sel = idx[:, :, None] == jnp.arange(N_EXPERTS, dtype=jnp.int32)[None, None, :]
    dest = rank + jnp.sum(jnp.where(sel, pad_starts[None, None, :], 0), axis=-1)
    n_blocks = -(-n * TOP_K // EXPERT_BLOCK) + N_EXPERTS
    nb_used = (pad_ends[-1] // EXPERT_BLOCK).astype(jnp.int32)
    blk = jnp.minimum(jnp.arange(n_blocks, dtype=jnp.int32), nb_used - 1) * EXPERT_BLOCK
    block_e = jnp.minimum(jnp.sum(pad_ends[None, :] <= blk[:, None], axis=1), N_EXPERTS - 1).astype(jnp.int32)
    dest_tiles = _tile_dest(dest, t_moe)

    x_pad = _dispatch(pad_ends.astype(jnp.int32), padded.astype(jnp.int32), dest_tiles, h3.reshape(n, d),
                      n_blocks * EXPERT_BLOCK, t_moe)
    bias = lambda v: v.reshape(N_EXPERTS, 1, -1).astype(F32)
    y_pad = _experts(block_e, nb_used.reshape(1), x_pad, w_gate.astype(F32), bias(b_gate),
                     w_up.astype(F32), bias(b_up), w_down.astype(F32), bias(b_down))
    out = _combine(dest_tiles, x2.reshape(n, d), gates.T, y_pad, t_moe)
    return out.reshape(b, s, d)


def kernel(x, mem, norm1_w, w_in, conv_w, a_log, dt_bias, dn_norm_w, sb_q_norm_w, sb_k_norm_w, w_out,
           norm2_w, mem_norm_w, xq_w, xk_w, xv_w, xq_norm_w, xk_norm_w, xo_w, norm3_w, router_w,
           router_b, w_gate, b_gate, w_up, b_up, w_down, b_down):
    depth = w_in.shape[0]
    for l in range(depth):
        x = _layer(x, mem, norm1_w[l], w_in[l], conv_w[l], a_log[l], dt_bias[l], dn_norm_w[l],
                   sb_q_norm_w[l], sb_k_norm_w[l], w_out[l], norm2_w[l], mem_norm_w[l], xq_w[l], xk_w[l],
                   xv_w[l], xq_norm_w[l], xk_norm_w[l], xo_w[l], norm3_w[l], router_w[l], router_b[l],
                   w_gate[l], b_gate[l], w_up[l], b_up[l], w_down[l], b_down[l])
    return x
```

```python
import jax
import jax.numpy as jnp
from jax import lax
from jax.experimental import pallas as pl
from jax.experimental.pallas import tpu as pltpu

F32 = jnp.float32
BF16 = jnp.bfloat16

EPS = 1e-6
CHUNK = 64
DN_HEADS = 4
DN_DIM = 128
DN_WIDTH = DN_HEADS * DN_DIM
CONV_WIDTH = 4
DN_PAIR = 4
SB_HEADS = 8
SB_DIM = 64
SB_WIDTH = SB_HEADS * SB_DIM
SB_BLOCK = 128
SB_GROUP = 4
SB_LANES = 256
X_HEADS = 4
X_DIM = 256
N_EXPERTS = 32
TOP_K = 4
EXPERT_BLOCK = 512
SWIGLU_ALPHA = 1.702
SWIGLU_LIMIT = 7.0
LANES = 128
LOG2E = 1.4426950408889634
SB_UNDERFLOW_LOG2 = -127.0
V7X_VMEM_LIMIT = 48 * 1024 * 1024
V7X_EXPERTS_VMEM_LIMIT = 58 * 1024 * 1024


def _dot(a, b):
    return jnp.dot(a, b, preferred_element_type=F32)


def _dot_nt(a, b):
    return lax.dot_general(a, b, (((1,), (1,)), ((), ())), preferred_element_type=F32)


def _dot_tn(a, b):
    return lax.dot_general(a, b, (((0,), (0,)), ((), ())), preferred_element_type=F32)


def _split3(x):
    hi = x.astype(BF16)
    r = x - hi.astype(F32)
    mid = r.astype(BF16)
    return hi, mid, (r - mid.astype(F32)).astype(BF16)


def _softplus(x):
    return jnp.maximum(x, 0.0) + jnp.log(1.0 + jnp.exp(-jnp.abs(x)))


def _sigmoid(x):
    return 1.0 / (1.0 + jnp.exp(-x))


def _pack_rows(x):
    c = x.shape[1] // 2
    lo = lax.bitcast_convert_type(x[:, :c].astype(BF16).astype(F32), jnp.uint32) >> 16
    hi = lax.bitcast_convert_type(x[:, c:].astype(BF16).astype(F32), jnp.uint32) & jnp.uint32(0xFFFF0000)
    return lo | hi


def _unpack_rows(w):
    lo = lax.bitcast_convert_type(w << 16, F32)
    hi = lax.bitcast_convert_type(w & jnp.uint32(0xFFFF0000), F32)
    return jnp.concatenate([lo, hi], axis=1)


def _rms(x, w):
    return x * lax.rsqrt(jnp.mean(x * x, axis=-1, keepdims=True) + EPS) * w


def _params(*sem):
    return pltpu.CompilerParams(dimension_semantics=sem, vmem_limit_bytes=V7X_VMEM_LIMIT)


def _const_spec(shape):
    nd = len(shape)
    return pl.BlockSpec(shape, lambda *_: (0,) * nd)


def _in_proj_kernel(x_ref, nw_ref, wdn_ref, wsb_ref, wab_ref, wabt_ref,
                    dn_ref, sb_ref, ab_ref, abt_ref):
    n = _rms(x_ref[...], nw_ref[...]).astype(BF16)
    dn_ref[...] = _dot(n, wdn_ref[...]).astype(BF16)
    sb_ref[...] = _dot(n, wsb_ref[...]).astype(BF16)
    ab_ref[...] = _dot(n, wab_ref[...])
    abt_ref[...] = _dot_nt(wabt_ref[...], n)


def _in_proj(x2d, norm_w, w_dn, w_sb, w_ab, w_abt, tm):
    n, d = x2d.shape
    return pl.pallas_call(
        _in_proj_kernel,
        grid=(n // tm,),
        in_specs=[
            pl.BlockSpec((tm, d), lambda i: (i, 0)),
            _const_spec((1, d)),
            _const_spec(w_dn.shape),
            _const_spec(w_sb.shape),
            _const_spec(w_ab.shape),
            _const_spec(w_abt.shape),
        ],
        out_specs=[
            pl.BlockSpec((tm, w_dn.shape[1]), lambda i: (i, 0)),
            pl.BlockSpec((tm, w_sb.shape[1]), lambda i: (i, 0)),
            pl.BlockSpec((tm, LANES), lambda i: (i, 0)),
            pl.BlockSpec((8, tm), lambda i: (0, i)),
        ],
        out_shape=[
            jax.ShapeDtypeStruct((n, w_dn.shape[1]), BF16),
            jax.ShapeDtypeStruct((n, w_sb.shape[1]), BF16),
            jax.ShapeDtypeStruct((n, LANES), F32),
            jax.ShapeDtypeStruct((8, n), F32),
        ],
        compiler_params=_params("arbitrary"),
        name="in_proj",
    )(x2d, norm_w, w_dn, w_sb, w_ab, w_abt)


def _deltanet_kernel(dn_ref, ab_ref, abt_ref, convw_ref, shift_ref, alog_ref, dtb_ref, alogt_ref, dtbt_ref,
                     normw_ref, o_ref, state_s, u_s, w_s, kt_s, qg_s, attn_s, egl_s):
    s = dn_ref.shape[0]
    steps = s // (CHUNK * DN_PAIR)
    state_s[...] = jnp.zeros(state_s.shape, F32)

    def conv_silu(r0, c, part, h, l2):
        cs = slice(part * DN_WIDTH + h * LANES, part * DN_WIDTH + (h + 1) * LANES)
        prev0 = pl.multiple_of(jnp.maximum(r0 - CHUNK, 0), CHUNK)
        prev = dn_ref[pl.ds(prev0, CHUNK), cs]
        prev = jnp.where(c > 0, prev, jnp.zeros_like(prev))
        win = jnp.concatenate([prev, dn_ref[pl.ds(r0, CHUNK), cs]], axis=0)
        sh = _dot(shift_ref[...], win)
        w = convw_ref[:, cs]
        y = w[0:1, :] * sh[0:CHUNK, :]
        for i in range(1, CONV_WIDTH):
            y = y + w[i:i + 1, :] * sh[i * CHUNK:(i + 1) * CHUNK, :]
        y = y * _sigmoid(y)
        if l2:
            y = y * lax.rsqrt(jnp.sum(y * y, axis=-1, keepdims=True) + EPS)
        return y

    row = lax.broadcasted_iota(jnp.int32, (CHUNK, CHUNK), 0)
    col = lax.broadcasted_iota(jnp.int32, (CHUNK, CHUNK), 1)
    tri = row >= col
    strict = row > col
    tril16 = jnp.where(tri, 1.0, 0.0).astype(BF16)
    triu16 = jnp.where(row <= col, 1.0, 0.0).astype(BF16)
    neg_a_col = -jnp.exp(alog_ref[...])
    neg_a_row = -jnp.exp(alogt_ref[...])
    scale = DN_DIM ** -0.5
    heads = range(DN_HEADS)
    cols = [slice(h * LANES, (h + 1) * LANES) for h in heads]

    def pre_items(it):
        items = []
        for ci in range(DN_PAIR):
            c = it * DN_PAIR + ci
            r0 = pl.multiple_of(c * CHUNK, CHUNK)
            ab = ab_ref[pl.ds(r0, CHUNK), :]
            g_col = neg_a_col * _softplus(ab + dtb_ref[...])
            gc_col_all = sum(_dot(tril16, p) for p in _split3(g_col))
            beta_all = _sigmoid(ab)
            abt = abt_ref[c]
            g_row = neg_a_row * _softplus(abt + dtbt_ref[...])
            gc_row_all = sum(_dot(p, triu16) for p in _split3(g_row))
            for h in heads:
                q = conv_silu(r0, c, 0, h, True) * scale
                k = conv_silu(r0, c, 1, h, True)
                v = conv_silu(r0, c, 2, h, False)
                gcol = gc_col_all[:, h:h + 1]
                grow = gc_row_all[h:h + 1, :]
                beta = beta_all[:, DN_HEADS + h:DN_HEADS + h + 1]
                glast = gcol[CHUNK - 1:CHUNK, :]
                decay = jnp.where(tri, jnp.exp(jnp.where(tri, gcol - grow, 0.0)), 0.0)
                kb = k * beta
                kt_s[pl.ds(r0, CHUNK), cols[h]] = (k * jnp.exp(glast - gcol)).astype(BF16)
                qg_s[pl.ds(r0, CHUNK), cols[h]] = (q * jnp.exp(gcol)).astype(BF16)
                egl_s[c, h:h + 1, :] = jnp.broadcast_to(jnp.exp(glast), (1, LANES))
                items.append(dict(c=c, r0=r0, h=h, decay=decay, kb16=kb.astype(BF16),
                                  k16=k.astype(BF16), q16=q.astype(BF16),
                                  sol=jnp.concatenate([v * beta, kb * jnp.exp(gcol)], axis=1)))
        return items

    def pre_gram(items):
        kk = [_dot_nt(t["kb16"], t["k16"]) for t in items]
        qk = [_dot_nt(t["q16"], t["k16"]) for t in items]
        for t, kk_i, qk_i in zip(items, kk, qk):
            attn_s[t["c"], t["h"]] = jnp.where(tri, qk_i * t["decay"], 0.0).astype(BF16)
            t["tm"] = -jnp.where(strict, kk_i * t["decay"], 0.0)
            t["p16"] = t["tm"].astype(BF16)
        return [_dot(t["p16"], t["p16"]) for t in items]

    def pre_level(items, sq, last):
        for i, t in enumerate(items):
            t["pw"] = sq[i]
            t["p16"] = sq[i].astype(BF16)
        app = [_dot(t["p16"], t["tm"].astype(BF16)) for t in items]
        nxt = None if last else [_dot(t["p16"], t["p16"]) for t in items]
        for i, t in enumerate(items):
            t["tm"] = t["tm"] + t["pw"] + app[i]
        return nxt

    def pre_solve(items):
        corr = [_dot(t["tm"].astype(BF16), t["sol"].astype(BF16)) for t in items]
        for i, t in enumerate(items):
            sol = t["sol"] + corr[i]
            u_s[pl.ds(t["r0"], CHUNK), cols[t["h"]]] = sol[:, :DN_DIM]
            w_s[pl.ds(t["r0"], CHUNK), cols[t["h"]]] = sol[:, DN_DIM:].astype(BF16)

    def rec_read(c):
        r0 = pl.multiple_of(c * CHUNK, CHUNK)
        st = [state_s[h] for h in heads]
        st16 = [x.astype(BF16) for x in st]
        ws = [_dot(w_s[pl.ds(r0, CHUNK), cols[h]], st16[h]) for h in heads]
        qs = [_dot(qg_s[pl.ds(r0, CHUNK), cols[h]], st16[h]) for h in heads]
        return dict(c=c, r0=r0, st=st, ws=ws, qs=qs)

    def rec_update(rd):
        c, r0 = rd["c"], rd["r0"]
        vn16 = [(u_s[pl.ds(r0, CHUNK), cols[h]] - rd["ws"][h]).astype(BF16) for h in heads]
        av = [_dot(attn_s[c, h], vn16[h]) for h in heads]
        ks = [_dot_tn(kt_s[pl.ds(r0, CHUNK), cols[h]], vn16[h]) for h in heads]
        for h in heads:
            state_s[h] = rd["st"][h] * egl_s[c, h:h + 1, :] + ks[h]
            o = rd["qs"][h] + av[h]
            o = o * lax.rsqrt(jnp.mean(o * o, axis=-1, keepdims=True) + EPS)
            z = dn_ref[pl.ds(r0, CHUNK), 3 * DN_WIDTH + h * LANES:3 * DN_WIDTH + (h + 1) * LANES].astype(F32)
            o_ref[pl.ds(r0, CHUNK), cols[h]] = (o * normw_ref[...] * (z * _sigmoid(z))).astype(BF16)

    def step(it, with_pre, with_rec):
        hooks = []
        for ci in range(DN_PAIR if with_rec else 0):
            hooks += [("read", ci), ("update", ci)]
        pending = {}

        def run_hook():
            if hooks:
                kind, ci = hooks.pop(0)
                if kind == "read":
                    pending[ci] = rec_read((it - 1) * DN_PAIR + ci)
                else:
                    rec_update(pending.pop(ci))

        if not with_pre:
            while hooks:
                run_hook()
            return
        slots = [(i * 7) // max(len(hooks), 1) for i in range(len(hooks))]

        def boundary(b):
            for _ in range(slots.count(b)):
                run_hook()

        items = pre_items(it)
        boundary(0)
        sq = pre_gram(items)
        boundary(1)
        for level in range(1, 6):
            sq = pre_level(items, sq, level == 5)
            boundary(level + 1)
        pre_solve(items)
        while hooks:
            run_hook()

    step(0, True, False)

    def fused(it, carry):
        step(it, True, True)
        return carry

    lax.fori_loop(1, steps, fused, 0)
    step(steps, False, True)


def _deltanet(dn3, ab3, abt4, conv_w, shift, alog, dtb, alogt, dtbt, norm_w):
    b, s, _ = dn3.shape
    nch = s // CHUNK
    return pl.pallas_call(
        _deltanet_kernel,
        grid=(b,),
        in_specs=[
            pl.BlockSpec((None, s, 4 * DN_WIDTH), lambda i: (i, 0, 0)),
            pl.BlockSpec((None, s, LANES), lambda i: (i, 0, 0)),
            pl.BlockSpec((None, nch, 8, CHUNK), lambda i: (i, 0, 0, 0)),
            _const_spec(conv_w.shape),
            _const_spec(shift.shape),
            _const_spec((1, LANES)),
            _const_spec((1, LANES)),
            _const_spec((8, 1)),
            _const_spec((8, 1)),
            _const_spec((1, DN_DIM)),
        ],
        out_specs=pl.BlockSpec((None, s, DN_WIDTH), lambda i: (i, 0, 0)),
        out_shape=jax.ShapeDtypeStruct((b, s, DN_WIDTH), BF16),
        scratch_shapes=[
            pltpu.VMEM((DN_HEADS, DN_DIM, DN_DIM), F32),
            pltpu.VMEM((s, DN_WIDTH), F32),
            pltpu.VMEM((s, DN_WIDTH), BF16),
            pltpu.VMEM((s, DN_WIDTH), BF16),
            pltpu.VMEM((s, DN_WIDTH), BF16),
            pltpu.VMEM((nch, DN_HEADS, CHUNK, CHUNK), BF16),
            pltpu.VMEM((nch, 8, LANES), F32),
        ],
        compiler_params=_params("arbitrary"),
        name="deltanet",
    )(dn3, ab3, abt4, conv_w, shift, alog, dtb, alogt, dtbt, norm_w)


def _stickbreak_kernel(q_ref, k_ref, v_ref, qw_ref, kw_ref, m2_ref, hsum_ref, o_ref,
                       qn_s, kn_s, vm_s, carry_s, acc_s):
    s, width = q_ref.shape
    nh = width // SB_DIM
    group = carry_s.shape[0]
    lane = lax.broadcasted_iota(jnp.int32, (1, width), 1)
    head_lanes = [jnp.logical_and(lane >= h * SB_DIM, lane < (h + 1) * SB_DIM) for h in range(nh)]

    def head_norm(r0, x_ref, w):
        x = x_ref[pl.ds(r0, SB_BLOCK), :].astype(F32)
        sq = x * x
        hi = sq.astype(BF16)
        lo = (sq - hi.astype(F32)).astype(BF16)
        ms = (_dot(hi, hsum_ref[...]) + _dot(lo, hsum_ref[...])) * (1.0 / SB_DIM)
        return x * lax.rsqrt(ms + EPS) * w

    def norm_block(i, carry):
        r0 = pl.multiple_of(i * SB_BLOCK, SB_BLOCK)
        qn = head_norm(r0, q_ref, qw_ref[...]) * (SB_DIM ** -0.5 * LOG2E)
        v = v_ref[pl.ds(r0, SB_BLOCK), :]
        for h in range(nh):
            qn_s[h, pl.ds(r0, SB_BLOCK), :] = jnp.where(head_lanes[h], qn, 0.0).astype(BF16)
            vm_s[i, h * SB_BLOCK:(h + 1) * SB_BLOCK, :] = jnp.where(head_lanes[h], v, jnp.zeros_like(v))
        kn_s[pl.ds(r0, SB_BLOCK), :] = head_norm(r0, k_ref, kw_ref[...]).astype(BF16)
        return carry

    lax.fori_loop(0, s // SB_BLOCK, norm_block, 0, unroll=min(4, s // SB_BLOCK))

    row = lax.broadcasted_iota(jnp.int32, (SB_BLOCK, SB_BLOCK), 0)
    col = lax.broadcasted_iota(jnp.int32, (SB_BLOCK, SB_BLOCK), 1)
    causal = col < row

    def step(rows, key_blocks, valid):
        tiles = [(g, h) for g in range(group) for h in range(nh)]
        k16 = [kn_s[pl.ds(pl.multiple_of(kb * SB_BLOCK, SB_BLOCK), SB_BLOCK), :] for kb in key_blocks]
        z = [_dot_nt(qn_s[h, pl.ds(rows[g], SB_BLOCK), :], k16[g]) for g, h in tiles]
        sp = [jnp.maximum(x, 0.0) + jnp.log(1.0 + jnp.exp2(-jnp.abs(x))) * LOG2E for x in z]
        fail = [jnp.where(causal, x, 0.0) for x in sp] if valid is None else sp
        cs = [_dot(x.astype(BF16), m2_ref[...]) for x in fail]
        p = []
        for i, (g, h) in enumerate(tiles):
            w = jnp.exp2(z[i] - sp[i] + carry_s[g, h] + cs[i][:, :SB_BLOCK])
            p.append((jnp.where(causal, w, 0.0) if valid is None else w).astype(BF16))
        pv = [_dot(jnp.concatenate(p[g * nh:(g + 1) * nh], axis=1), vm_s[key_blocks[g]])
              for g in range(group)]
        for g in range(group):
            acc = acc_s[g] + pv[g]
            acc_s[g] = acc if valid is None else jnp.where(valid[g], acc, acc_s[g])
        for i, (g, h) in enumerate(tiles):
            carry = carry_s[g, h] + cs[i][:, SB_BLOCK:]
            carry_s[g, h] = carry if valid is None else jnp.where(valid[g], carry, carry_s[g, h])

    def q_group(qg, carry):
        blocks = [qg * group + g for g in range(group)]
        rows = [pl.multiple_of(qb * SB_BLOCK, SB_BLOCK) for qb in blocks]
        carry_s[...] = jnp.zeros(carry_s.shape, F32)
        acc_s[...] = jnp.zeros(acc_s.shape, F32)
        step(rows, blocks, None)

        def cond(st):
            d, alive = st
            return jnp.logical_and(d <= blocks[-1], alive)

        def body(st):
            d, _ = st
            step(rows, [jnp.maximum(qb - d, 0) for qb in blocks], [qb >= d for qb in blocks])
            return d + 1, jnp.max(carry_s[...]) > SB_UNDERFLOW_LOG2

        lax.while_loop(cond, body, (jnp.int32(1), jnp.bool_(True)))
        for g in range(group):
            o_ref[pl.ds(rows[g], SB_BLOCK), :] = acc_s[g].astype(BF16)
        return carry

    lax.fori_loop(0, s // (SB_BLOCK * group), q_group, 0)


def _stickbreak(sb3, qw, kw, m2, hsum):
    b, s, _ = sb3.shape
    width = hsum.shape[0]
    nh = width // SB_DIM
    parts = SB_WIDTH // width
    group = min(SB_GROUP, s // SB_BLOCK)
    return pl.pallas_call(
        _stickbreak_kernel,
        grid=(b, parts),
        in_specs=[
            pl.BlockSpec((None, s, width), lambda i, j: (i, 0, j)),
            pl.BlockSpec((None, s, width), lambda i, j: (i, 0, parts + j)),
            pl.BlockSpec((None, s, width), lambda i, j: (i, 0, 2 * parts + j)),
            _const_spec((1, width)),
            _const_spec((1, width)),
            _const_spec(m2.shape),
            _const_spec(hsum.shape),
        ],
        out_specs=pl.BlockSpec((None, s, width), lambda i, j: (i, 0, j)),
        out_shape=jax.ShapeDtypeStruct((b, s, SB_WIDTH), BF16),
        scratch_shapes=[
            pltpu.VMEM((nh, s, width), BF16),
            pltpu.VMEM((s, width), BF16),
            pltpu.VMEM((s // SB_BLOCK, nh * SB_BLOCK, width), BF16),
            pltpu.VMEM((group, nh, SB_BLOCK, SB_BLOCK), F32),
            pltpu.VMEM((group, SB_BLOCK, width), F32),
        ],
        compiler_params=_params("arbitrary", "arbitrary"),
        name="stickbreak",
    )(sb3, sb3, sb3, qw, kw, m2, hsum)


def _memkv_kernel(mem_ref, nw_ref, wk_ref, wv_ref, knw_ref, k_ref, v_ref):
    n = _rms(mem_ref[...], nw_ref[...]).astype(BF16)
    k = _dot(n, wk_ref[...])
    for h in range(X_HEADS):
        cs = slice(h * X_DIM, (h + 1) * X_DIM)
        k_ref[:, cs] = _rms(k[:, cs], knw_ref[...]).astype(BF16)
    v_ref[...] = _dot(n, wv_ref[...]).astype(BF16)


def _memkv(mem, norm_w, wk, wv, k_norm_w):
    b, m, d = mem.shape
    return pl.pallas_call(
        _memkv_kernel,
        grid=(b,),
        in_specs=[
            pl.BlockSpec((None, m, d), lambda i: (i, 0, 0)),
            _const_spec((1, d)),
            _const_spec(wk.shape),
            _const_spec(wv.shape),
            _const_spec((1, X_DIM)),
        ],
        out_specs=[
            pl.BlockSpec((None, m, d), lambda i: (i, 0, 0)),
            pl.BlockSpec((None, m, d), lambda i: (i, 0, 0)),
        ],
        out_shape=[jax.ShapeDtypeStruct((b, m, d), BF16)] * 2,
        compiler_params=_params("arbitrary"),
        name="memkv",
    )(mem, norm_w, wk, wv, k_norm_w)


def _mid_kernel(x_ref, ydn_ref, ysb_ref, wout_ref, n2w_ref, wq_ref, qnw_ref, km_ref, vm_ref, wo_ref,
                n3w_ref, wrh_ref, wrl_ref, rb_ref, upper_ref,
                x2_ref, h3_ref, idx_ref, gate_ref, rank_ref, cnt_ref, count_s):
    tm = x_ref.shape[0]
    first_step = jnp.logical_and(pl.program_id(0) == 0, pl.program_id(1) == 0)

    @pl.when(first_step)
    def _():
        count_s[...] = jnp.zeros(count_s.shape, F32)

    sub = upper_ref.shape[0]
    subs = range(tm // sub)
    rows = [slice(i * sub, (i + 1) * sub) for i in subs]
    heads = range(X_HEADS)
    cols = [slice(h * X_DIM, (h + 1) * X_DIM) for h in heads]
    x1 = [x_ref[r, :] + _dot(ydn_ref[r, :], wout_ref[0:DN_WIDTH, :])
          + _dot(ysb_ref[r, :], wout_ref[DN_WIDTH:, :]) for r in rows]
    n2 = [_rms(v, n2w_ref[...]).astype(BF16) for v in x1]
    q = [_dot(v, wq_ref[...]) for v in n2]
    qh = [[(_rms(q[i][:, c], qnw_ref[...]) * (X_DIM ** -0.5)).astype(BF16) for c in cols] for i in subs]
    sc = [[_dot_nt(qh[i][h], km_ref[:, cols[h]]) for h in heads] for i in subs]
    ex = [[jnp.exp(sc[i][h] - jnp.max(sc[i][h], axis=-1, keepdims=True)) for h in heads] for i in subs]
    pr = [[(ex[i][h] / jnp.sum(ex[i][h], axis=-1, keepdims=True)).astype(BF16) for h in heads] for i in subs]
    oh = [[_dot(pr[i][h], vm_ref[:, cols[h]]).astype(BF16) for h in heads] for i in subs]
    x2 = [x1[i] + _dot(jnp.concatenate(oh[i], axis=1), wo_ref[...]) for i in subs]
    h3 = [_rms(v, n3w_ref[...]) for v in x2]
    hi = [v.astype(BF16) for v in h3]
    lo = [(h3[i] - hi[i].astype(F32)).astype(BF16) for i in subs]
    logits = [_dot_nt(wrh_ref[...], hi[i]) + _dot_nt(wrh_ref[...], lo[i]) + _dot_nt(wrl_ref[...], hi[i])
              + rb_ref[...] for i in subs]
    eid = lax.broadcasted_iota(jnp.int32, (N_EXPERTS, sub), 0).astype(F32)
    count = count_s[...]
    for i in subs:
        x2_ref[rows[i], :] = x2[i]
        h3_ref[rows[i], :] = _pack_rows(h3[i])
        vals, ids = [], []
        cur = logits[i]
        for _ in range(TOP_K):
            m = jnp.max(cur, axis=0, keepdims=True)
            j = jnp.min(jnp.where(cur == m, eid, float(N_EXPERTS)), axis=0, keepdims=True)
            vals.append(m)
            ids.append(j)
            cur = jnp.where(eid == j, -jnp.inf, cur)
        exps = [jnp.exp(v - vals[0]) for v in vals]
        denom = exps[0] + exps[1] + exps[2] + exps[3]
        onehot = jnp.zeros((N_EXPERTS, sub), F32)
        for j in ids:
            onehot = onehot + jnp.where(eid == j, 1.0, 0.0)
        before = count + _dot(onehot.astype(BF16), upper_ref[...])
        for k in range(TOP_K):
            idx_ref[k:k + 1, rows[i]] = ids[k].astype(jnp.int32)
            gate_ref[k:k + 1, rows[i]] = exps[k] / denom
            rank_ref[k:k + 1, rows[i]] = jnp.sum(jnp.where(eid == ids[k], before, 0.0), axis=0,
                                                 keepdims=True).astype(jnp.int32)
        count = count + jnp.sum(onehot, axis=1, keepdims=True)
    count_s[...] = count
    cnt_ref[...] = jnp.broadcast_to(count, cnt_ref.shape)


def _mid(x3, ydn3, ysb3, w_out, n2w, wq, qnw, k_mem, v_mem, wo, n3w, wr_hi, wr_lo, rb, upper, tm):
    b, s, d = x3.shape
    n = b * s
    nt = s // tm
    m = k_mem.shape[1]
    tok = lambda i, j: (0, i * nt + j)
    return pl.pallas_call(
        _mid_kernel,
        grid=(b, nt),
        in_specs=[
            pl.BlockSpec((None, tm, d), lambda i, j: (i, j, 0)),
            pl.BlockSpec((None, tm, DN_WIDTH), lambda i, j: (i, j, 0)),
            pl.BlockSpec((None, tm, SB_WIDTH), lambda i, j: (i, j, 0)),
            _const_spec(w_out.shape),
            _const_spec((1, d)),
            _const_spec(wq.shape),
            _const_spec((1, X_DIM)),
            pl.BlockSpec((None, m, d), lambda i, j: (i, 0, 0)),
            pl.BlockSpec((None, m, d), lambda i, j: (i, 0, 0)),
            _const_spec(wo.shape),
            _const_spec((1, d)),
            _const_spec(wr_hi.shape),
            _const_spec(wr_lo.shape),
            _const_spec((N_EXPERTS, 1)),
            _const_spec(upper.shape),
        ],
        out_specs=[
            pl.BlockSpec((None, tm, d), lambda i, j: (i, j, 0)),
            pl.BlockSpec((None, tm, d // 2), lambda i, j: (i, j, 0)),
            pl.BlockSpec((TOP_K, tm), tok),
            pl.BlockSpec((TOP_K, tm), tok),
            pl.BlockSpec((TOP_K, tm), tok),
            _const_spec((N_EXPERTS, LANES)),
        ],
        out_shape=[
            jax.ShapeDtypeStruct((b, s, d), F32),
            jax.ShapeDtypeStruct((b, s, d // 2), jnp.uint32),
            jax.ShapeDtypeStruct((TOP_K, n), jnp.int32),
            jax.ShapeDtypeStruct((TOP_K, n), F32),
            jax.ShapeDtypeStruct((TOP_K, n), jnp.int32),
            jax.ShapeDtypeStruct((N_EXPERTS, LANES), F32),
        ],
        scratch_shapes=[pltpu.VMEM((N_EXPERTS, 1), F32)],
        compiler_params=pltpu.CompilerParams(dimension_semantics=("arbitrary", "arbitrary"),
                                             vmem_limit_bytes=V7X_EXPERTS_VMEM_LIMIT),
        name="mid",
    )(x3, ydn3, ysb3, w_out, n2w, wq, qnw, k_mem, v_mem, wo, n3w, wr_hi, wr_lo, rb, upper)


def _row_copy(src_ref, src_row, dst_ref, dst_row, sem):
    return pltpu.make_async_copy(src_ref.at[pl.ds(src_row, 1), :], dst_ref.at[pl.ds(dst_row, 1), :], sem)


def _dispatch_kernel(pend_ref, padded_ref, dest_ref, h_hbm, xpad_ref, zero_s, hbuf, sem, zsem, fsem):
    tf = hbuf.shape[1]
    i = pl.program_id(0)
    last = pl.num_programs(0) - 1

    def fetch(tile, slot):
        start = pl.multiple_of(tile * tf, tf)
        return pltpu.make_async_copy(h_hbm.at[pl.ds(start, tf), :], hbuf.at[slot], fsem.at[slot])

    def wait_rows(step):
        for _ in range(TOP_K):
            pltpu.make_async_copy(hbuf.at[0], xpad_ref.at[pl.ds(0, tf), :], sem.at[step % 2]).wait()

    @pl.when(i == 0)
    def _():
        fetch(0, 0).start()

        @pl.when(last >= 1)
        def _():
            fetch(1, 1).start()

        zero_s[...] = jnp.zeros(zero_s.shape, zero_s.dtype)

        def last_block(e):
            start = pl.multiple_of(pend_ref[e] - EXPERT_BLOCK, EXPERT_BLOCK)
            return pltpu.make_async_copy(zero_s, xpad_ref.at[pl.ds(start, EXPERT_BLOCK), :], zsem)

        for e in range(N_EXPERTS):
            @pl.when(padded_ref[e] > 0)
            def _(e=e):
                last_block(e).start()
        for e in range(N_EXPERTS):
            @pl.when(padded_ref[e] > 0)
            def _(e=e):
                last_block(e).wait()

    slot = i % 3
    fetch(i, slot).wait()
    src = hbuf.at[slot]
    for k in range(TOP_K):
        for t in range(tf):
            _row_copy(src, t, xpad_ref, dest_ref[0, k * tf + t], sem.at[i % 2]).start(priority=t % 2)

    @pl.when(i >= 1)
    def _():
        wait_rows(i - 1)

    @pl.when(i + 2 <= last)
    def _():
        fetch(i + 2, (i + 2) % 3).start()

    @pl.when(i == last)
    def _():
        wait_rows(i)


def _dispatch(pad_ends, padded, dest_tiles, h2d, rows, tf):
    n, d = h2d.shape
    return pl.pallas_call(
        _dispatch_kernel,
        grid_spec=pltpu.PrefetchScalarGridSpec(
            num_scalar_prefetch=2,
            grid=(n // tf,),
            in_specs=[
                pl.BlockSpec((None, 1, TOP_K * tf), lambda i, *_: (i, 0, 0), memory_space=pltpu.SMEM),
                pl.BlockSpec(memory_space=pl.ANY),
            ],
            out_specs=pl.BlockSpec(memory_space=pl.ANY),
            scratch_shapes=[pltpu.VMEM((EXPERT_BLOCK, d), h2d.dtype), pltpu.VMEM((3, tf, d), h2d.dtype),
                            pltpu.SemaphoreType.DMA((2,)), pltpu.SemaphoreType.DMA(()),
                            pltpu.SemaphoreType.DMA((3,))],
        ),
        out_shape=jax.ShapeDtypeStruct((rows, d), h2d.dtype),
        compiler_params=_params("arbitrary"),
        name="dispatch",
    )(pad_ends, padded, dest_tiles, h2d)


def _experts_kernel(be_ref, nb_ref, x_ref, wg_ref, bg_ref, wu_ref, bu_ref, wd_ref, bd_ref, y_ref,
                    wg_s, wu_s, wd_s):
    j = pl.program_id(0)
    used = j < nb_ref[0]
    new_expert = jnp.logical_or(j == 0, be_ref[j] != be_ref[jnp.maximum(j - 1, 0)])

    @pl.when(jnp.logical_and(used, new_expert))
    def _():
        wg_s[...] = wg_ref[...].astype(BF16)
        wu_s[...] = wu_ref[...].astype(BF16)
        wd_s[...] = wd_ref[...].astype(BF16)

    @pl.when(used)
    def _():
        x = _unpack_rows(x_ref[...]).astype(BF16)
        gate = jnp.minimum(_dot(x, wg_s[...]) + bg_ref[...], SWIGLU_LIMIT)
        up = jnp.clip(_dot(x, wu_s[...]) + bu_ref[...], -SWIGLU_LIMIT, SWIGLU_LIMIT)
        act = (up + 1.0) * gate * _sigmoid(gate * SWIGLU_ALPHA)
        y_ref[...] = _pack_rows(_dot(act.astype(BF16), wd_s[...]) + bd_ref[...])


def _experts(block_e, nb_used, x_pad, wg, bg, wu, bu, wd, bd):
    rows, dp = x_pad.shape
    nblk = rows // EXPERT_BLOCK
    d, dff = wg.shape[1:]
    row_blk = lambda j, be, nb: (jnp.minimum(j, nb[0] - 1), 0)
    w_blk = lambda j, be, nb: (be[j], 0, 0)
    return pl.pallas_call(
        _experts_kernel,
        grid_spec=pltpu.PrefetchScalarGridSpec(
            num_scalar_prefetch=2,
            grid=(nblk,),
            in_specs=[
                pl.BlockSpec((EXPERT_BLOCK, dp), row_blk),
                pl.BlockSpec((None, d, dff), w_blk),
                pl.BlockSpec((None, 1, dff), w_blk),
                pl.BlockSpec((None, d, dff), w_blk),
                pl.BlockSpec((None, 1, dff), w_blk),
                pl.BlockSpec((None, dff, d), w_blk),
                pl.BlockSpec((None, 1, d), w_blk),
            ],
            out_specs=pl.BlockSpec((EXPERT_BLOCK, dp), row_blk),
            scratch_shapes=[pltpu.VMEM((d, dff), BF16), pltpu.VMEM((d, dff), BF16),
                            pltpu.VMEM((dff, d), BF16)],
        ),
        out_shape=jax.ShapeDtypeStruct((rows, dp), jnp.uint32),
        compiler_params=pltpu.CompilerParams(dimension_semantics=("arbitrary",),
                                             vmem_limit_bytes=V7X_EXPERTS_VMEM_LIMIT),
        name="experts",
    )(block_e, nb_used, x_pad, wg, bg, wu, bu, wd, bd)


def _combine_kernel(dest_ref, dnext_ref, x2_ref, gate_ref, ypad_ref, o_ref, buf, sem):
    th = x2_ref.shape[0]
    i = pl.program_id(0)
    slot = i % 2

    def gather(dref, s):
        for k in range(TOP_K):
            for t in range(th):
                _row_copy(ypad_ref, dref[0, k * th + t], buf.at[s, k], t, sem.at[s]).start(priority=t % 2)

    @pl.when(i == 0)
    def _():
        gather(dest_ref, 0)

    @pl.when(i + 1 < pl.num_programs(0))
    def _():
        gather(dnext_ref, 1 - slot)

    for k in range(TOP_K):
        pltpu.make_async_copy(ypad_ref.at[pl.ds(0, th), :], buf.at[slot, k], sem.at[slot]).wait()
    g = gate_ref[...]
    out = x2_ref[...]
    for k in range(TOP_K):
        out = out + _unpack_rows(buf[slot, k]) * g[:, k:k + 1]
    o_ref[...] = out


def _combine(dest_tiles, x2d, gates_nk, y_pad, th):
    n, d = x2d.shape
    nt = n // th
    return pl.pallas_call(
        _combine_kernel,
        grid=(nt,),
        in_specs=[
            pl.BlockSpec((None, 1, TOP_K * th), lambda i: (i, 0, 0), memory_space=pltpu.SMEM),
            pl.BlockSpec((None, 1, TOP_K * th), lambda i: (jnp.minimum(i + 1, nt - 1), 0, 0),
                         memory_space=pltpu.SMEM),
            pl.BlockSpec((th, d), lambda i: (i, 0)),
            pl.BlockSpec((th, TOP_K), lambda i: (i, 0)),
            pl.BlockSpec(memory_space=pl.ANY),
        ],
        out_specs=pl.BlockSpec((th, d), lambda i: (i, 0)),
        out_shape=jax.ShapeDtypeStruct((n, d), F32),
        scratch_shapes=[pltpu.VMEM((2, TOP_K, th, y_pad.shape[1]), y_pad.dtype),
                        pltpu.SemaphoreType.DMA((2,))],
        compiler_params=_params("arbitrary"),
        name="combine",
    )(dest_tiles, dest_tiles, x2d, gates_nk, y_pad)


def _tile_dest(dest, t):
    k, n = dest.shape
    return dest.reshape(k, n // t, t).transpose(1, 0, 2).reshape(n // t, 1, k * t)


def _layer(x, mem, norm1_w, w_in, conv_w, a_log, dt_bias, dn_norm_w, sb_q_norm_w, sb_k_norm_w, w_out,
           norm2_w, mem_norm_w, xq_w, xk_w, xv_w, xq_norm_w, xk_norm_w, xo_w, norm3_w, router_w,
           router_b, w_gate, b_gate, w_up, b_up, w_down, b_down):
    b, s, d = x.shape
    n = b * s
    tm_proj = min(512, s)
    tm_mid = min(1024, s)
    sub_mid = min(256, s)
    t_moe = min(256, s)

    o_dn, o_ab, o_sb = 4 * DN_WIDTH, 4 * DN_WIDTH + 2 * DN_HEADS, 4 * DN_WIDTH + 2 * DN_HEADS
    w_dn = w_in[:, :o_dn].astype(BF16)
    w_ab_f = w_in[:, o_dn:o_ab]
    w_ab = jnp.pad(w_ab_f, ((0, 0), (0, LANES - 2 * DN_HEADS))).astype(BF16)
    w_abt = w_ab_f.T.astype(BF16)
    w_sb = w_in[:, o_sb:].astype(BF16)
    row = lambda v: v.reshape(1, -1).astype(F32)

    dn, sb, ab, abt = _in_proj(x.reshape(n, d), row(norm1_w), w_dn, w_sb, w_ab, w_abt, tm_proj)

    abt4 = abt.reshape(8, n // CHUNK, CHUNK).transpose(1, 0, 2).reshape(b, s // CHUNK, 8, CHUNK)
    pad_lane = lambda v: jnp.pad(v.astype(F32), (0, LANES - v.shape[0])).reshape(1, LANES)
    pad_col = lambda v: jnp.pad(v.astype(F32), (0, 8 - v.shape[0])).reshape(8, 1)
    tap = jnp.arange(CONV_WIDTH * CHUNK)
    shift = (jnp.arange(2 * CHUNK)[None, :] == (CHUNK - CONV_WIDTH + 1 + tap // CHUNK + tap % CHUNK)[:, None])
    y_dn = _deltanet(dn.reshape(b, s, -1), ab.reshape(b, s, LANES), abt4, conv_w.astype(F32),
                     shift.astype(BF16), pad_lane(a_log), pad_lane(dt_bias), pad_col(a_log), pad_col(dt_bias),
                     row(dn_norm_w))

    ii = jnp.arange(SB_BLOCK)
    m2 = -jnp.concatenate([(ii[:, None] > ii[None, :]).astype(BF16),
                           jnp.ones((SB_BLOCK, SB_BLOCK), BF16)], axis=1)
    sb_heads = SB_LANES // SB_DIM
    hh = jnp.arange(SB_LANES) // SB_DIM
    hsum = (hh[:, None] == hh[None, :]).astype(BF16)
    y_sb = _stickbreak(sb.reshape(b, s, -1), row(jnp.tile(sb_q_norm_w, sb_heads)),
                       row(jnp.tile(sb_k_norm_w, sb_heads)), m2, hsum)

    k_mem, v_mem = _memkv(mem, row(mem_norm_w), xk_w.astype(BF16), xv_w.astype(BF16), row(xk_norm_w))

    wr_t = router_w.T.astype(F32)
    wr_hi = wr_t.astype(BF16)
    wr_lo = (wr_t - wr_hi.astype(F32)).astype(BF16)
    jj = jnp.arange(sub_mid)
    upper = (jj[:, None] < jj[None, :]).astype(BF16)
    x2, h3, idx, gates, rank, cnt = _mid(
        x, y_dn, y_sb, w_out.astype(BF16), row(norm2_w), xq_w.astype(BF16), row(xq_norm_w), k_mem, v_mem,
        xo_w.astype(BF16), row(norm3_w), wr_hi, wr_lo, router_b.reshape(N_EXPERTS, 1).astype(F32), upper,
        tm_mid)

    counts = cnt[:, 0].astype(jnp.int32)
    padded = (counts + EXPERT_BLOCK - 1) // EXPERT_BLOCK * EXPERT_BLOCK
    pad_ends = jnp.cumsum(padded)
    pad_starts = pad_ends - padded
    sel = idx[:, :, None] == jnp.arange(N_EXPERTS, dtype=jnp.int32)[None, None, :]
    dest = rank + jnp.sum(jnp.where(sel, pad_starts[None, None, :], 0), axis=-1)
    n_blocks = -(-n * TOP_K // EXPERT_BLOCK) + N_EXPERTS
    nb_used = (pad_ends[-1] // EXPERT_BLOCK).astype(jnp.int32)
    blk = jnp.minimum(jnp.arange(n_blocks, dtype=jnp.int32), nb_used - 1) * EXPERT_BLOCK
    block_e = jnp.minimum(jnp.sum(pad_ends[None, :] <= blk[:, None], axis=1), N_EXPERTS - 1).astype(jnp.int32)
    dest_tiles = _tile_dest(dest, t_moe)

    x_pad = _dispatch(pad_ends.astype(jnp.int32), padded.astype(jnp.int32), dest_tiles, h3.reshape(n, d // 2),
                      n_blocks * EXPERT_BLOCK, t_moe)
    bias = lambda v: v.reshape(N_EXPERTS, 1, -1).astype(F32)
    y_pad = _experts(block_e, nb_used.reshape(1), x_pad, w_gate.astype(F32), bias(b_gate),
                     w_up.astype(F32), bias(b_up), w_down.astype(F32), bias(b_down))
    out = _combine(dest_tiles, x2.reshape(n, d), gates.T, y_pad, t_moe)
    return out.reshape(b, s, d)


def kernel(x, mem, norm1_w, w_in, conv_w, a_log, dt_bias, dn_norm_w, sb_q_norm_w, sb_k_norm_w, w_out,
           norm2_w, mem_norm_w, xq_w, xk_w, xv_w, xq_norm_w, xk_norm_w, xo_w, norm3_w, router_w,
           router_b, w_gate, b_gate, w_up, b_up, w_down, b_down):
    depth = w_in.shape[0]
    for l in range(depth):
        x = _layer(x, mem, norm1_w[l], w_in[l], conv_w[l], a_log[l], dt_bias[l], dn_norm_w[l],
                   sb_q_norm_w[l], sb_k_norm_w[l], w_out[l], norm2_w[l], mem_norm_w[l], xq_w[l], xk_w[l],
                   xv_w[l], xq_norm_w[l], xk_norm_w[l], xo_w[l], norm3_w[l], router_w[l], router_b[l],
                   w_gate[l], b_gate[l], w_up[l], b_up[l], w_down[l], b_down[l])
    return x
```

```python
import jax
import jax.numpy as jnp
from jax import lax
from jax.experimental import pallas as pl
from jax.experimental.pallas import tpu as pltpu

F32 = jnp.float32
BF16 = jnp.bfloat16

EPS = 1e-6
CHUNK = 64
DN_HEADS = 4
DN_DIM = 128
DN_WIDTH = DN_HEADS * DN_DIM
CONV_WIDTH = 4
DN_PAIR = 4
SB_HEADS = 8
SB_DIM = 64
SB_WIDTH = SB_HEADS * SB_DIM
SB_BLOCK = 128
SB_GROUP = 4
SB_LANES = 256
X_HEADS = 4
X_DIM = 256
N_EXPERTS = 32
TOP_K = 4
EXPERT_BLOCK = 512
SWIGLU_ALPHA = 1.702
SWIGLU_LIMIT = 7.0
LANES = 128
LOG2E = 1.4426950408889634
SB_UNDERFLOW_LOG2 = -127.0
V7X_VMEM_LIMIT = 48 * 1024 * 1024
V7X_EXPERTS_VMEM_LIMIT = 58 * 1024 * 1024


def _dot(a, b):
    return jnp.dot(a, b, preferred_element_type=F32)


def _dot_nt(a, b):
    return lax.dot_general(a, b, (((1,), (1,)), ((), ())), preferred_element_type=F32)


def _dot_tn(a, b):
    return lax.dot_general(a, b, (((0,), (0,)), ((), ())), preferred_element_type=F32)


def _split3(x):
    hi = x.astype(BF16)
    r = x - hi.astype(F32)
    mid = r.astype(BF16)
    return hi, mid, (r - mid.astype(F32)).astype(BF16)


def _softplus(x):
    return jnp.maximum(x, 0.0) + jnp.log(1.0 + jnp.exp(-jnp.abs(x)))


def _sigmoid(x):
    return 1.0 / (1.0 + jnp.exp(-x))


def _rms(x, w):
    return x * lax.rsqrt(jnp.mean(x * x, axis=-1, keepdims=True) + EPS) * w


def _params(*sem):
    return pltpu.CompilerParams(dimension_semantics=sem, vmem_limit_bytes=V7X_VMEM_LIMIT)


def _const_spec(shape):
    nd = len(shape)
    return pl.BlockSpec(shape, lambda *_: (0,) * nd)


def _in_proj_kernel(x_ref, nw_ref, wdn_ref, wsb_ref, wab_ref, wabt_ref,
                    dn_ref, sb_ref, ab_ref, abt_ref):
    n = _rms(x_ref[...], nw_ref[...]).astype(BF16)
    dn_ref[...] = _dot(n, wdn_ref[...]).astype(BF16)
    sb_ref[...] = _dot(n, wsb_ref[...]).astype(BF16)
    ab_ref[...] = _dot(n, wab_ref[...])
    abt_ref[...] = _dot_nt(wabt_ref[...], n)


def _in_proj(x2d, norm_w, w_dn, w_sb, w_ab, w_abt, tm):
    n, d = x2d.shape
    return pl.pallas_call(
        _in_proj_kernel,
        grid=(n // tm,),
        in_specs=[
            pl.BlockSpec((tm, d), lambda i: (i, 0)),
            _const_spec((1, d)),
            _const_spec(w_dn.shape),
            _const_spec(w_sb.shape),
            _const_spec(w_ab.shape),
            _const_spec(w_abt.shape),
        ],
        out_specs=[
            pl.BlockSpec((tm, w_dn.shape[1]), lambda i: (i, 0)),
            pl.BlockSpec((tm, w_sb.shape[1]), lambda i: (i, 0)),
            pl.BlockSpec((tm, LANES), lambda i: (i, 0)),
            pl.BlockSpec((8, tm), lambda i: (0, i)),
        ],
        out_shape=[
            jax.ShapeDtypeStruct((n, w_dn.shape[1]), BF16),
            jax.ShapeDtypeStruct((n, w_sb.shape[1]), BF16),
            jax.ShapeDtypeStruct((n, LANES), F32),
            jax.ShapeDtypeStruct((8, n), F32),
        ],
        compiler_params=_params("arbitrary"),
        name="in_proj",
    )(x2d, norm_w, w_dn, w_sb, w_ab, w_abt)


def _deltanet_kernel(dn_ref, ab_ref, abt_ref, convw_ref, shift_ref, alog_ref, dtb_ref, alogt_ref, dtbt_ref,
                     normw_ref, o_ref, state_s, u_s, w_s, kt_s, qg_s, attn_s, egl_s):
    s = dn_ref.shape[0]
    steps = s // (CHUNK * DN_PAIR)
    state_s[...] = jnp.zeros(state_s.shape, F32)

    def conv_silu(r0, c, part, h, l2):
        cs = slice(part * DN_WIDTH + h * LANES, part * DN_WIDTH + (h + 1) * LANES)
        prev0 = pl.multiple_of(jnp.maximum(r0 - CHUNK, 0), CHUNK)
        prev = dn_ref[pl.ds(prev0, CHUNK), cs]
        prev = jnp.where(c > 0, prev, jnp.zeros_like(prev))
        win = jnp.concatenate([prev, dn_ref[pl.ds(r0, CHUNK), cs]], axis=0)
        sh = _dot(shift_ref[...], win)
        w = convw_ref[:, cs]
        y = w[0:1, :] * sh[0:CHUNK, :]
        for i in range(1, CONV_WIDTH):
            y = y + w[i:i + 1, :] * sh[i * CHUNK:(i + 1) * CHUNK, :]
        y = y * _sigmoid(y)
        if l2:
            y = y * lax.rsqrt(jnp.sum(y * y, axis=-1, keepdims=True) + EPS)
        return y

    row = lax.broadcasted_iota(jnp.int32, (CHUNK, CHUNK), 0)
    col = lax.broadcasted_iota(jnp.int32, (CHUNK, CHUNK), 1)
    tri = row >= col
    strict = row > col
    tril16 = jnp.where(tri, 1.0, 0.0).astype(BF16)
    triu16 = jnp.where(row <= col, 1.0, 0.0).astype(BF16)
    neg_a_col = -jnp.exp(alog_ref[...])
    neg_a_row = -jnp.exp(alogt_ref[...])
    scale = DN_DIM ** -0.5
    heads = range(DN_HEADS)
    cols = [slice(h * LANES, (h + 1) * LANES) for h in heads]

    def pre_items(it):
        items = []
        for ci in range(DN_PAIR):
            c = it * DN_PAIR + ci
            r0 = pl.multiple_of(c * CHUNK, CHUNK)
            ab = ab_ref[pl.ds(r0, CHUNK), :]
            g_col = neg_a_col * _softplus(ab + dtb_ref[...])
            gc_col_all = sum(_dot(tril16, p) for p in _split3(g_col))
            beta_all = _sigmoid(ab)
            abt = abt_ref[c]
            g_row = neg_a_row * _softplus(abt + dtbt_ref[...])
            gc_row_all = sum(_dot(p, triu16) for p in _split3(g_row))
            for h in heads:
                q = conv_silu(r0, c, 0, h, True) * scale
                k = conv_silu(r0, c, 1, h, True)
                v = conv_silu(r0, c, 2, h, False)
                gcol = gc_col_all[:, h:h + 1]
                grow = gc_row_all[h:h + 1, :]
                beta = beta_all[:, DN_HEADS + h:DN_HEADS + h + 1]
                glast = gcol[CHUNK - 1:CHUNK, :]
                decay = jnp.where(tri, jnp.exp(jnp.where(tri, gcol - grow, 0.0)), 0.0)
                kb = k * beta
                kt_s[pl.ds(r0, CHUNK), cols[h]] = (k * jnp.exp(glast - gcol)).astype(BF16)
                qg_s[pl.ds(r0, CHUNK), cols[h]] = (q * jnp.exp(gcol)).astype(BF16)
                egl_s[c, h:h + 1, :] = jnp.broadcast_to(jnp.exp(glast), (1, LANES))
                items.append(dict(c=c, r0=r0, h=h, decay=decay, kb16=kb.astype(BF16),
                                  k16=k.astype(BF16), q16=q.astype(BF16),
                                  sol=jnp.concatenate([v * beta, kb * jnp.exp(gcol)], axis=1)))
        return items

    def pre_gram(items):
        kk = [_dot_nt(t["kb16"], t["k16"]) for t in items]
        qk = [_dot_nt(t["q16"], t["k16"]) for t in items]
        for t, kk_i, qk_i in zip(items, kk, qk):
            attn_s[t["c"], t["h"]] = jnp.where(tri, qk_i * t["decay"], 0.0).astype(BF16)
            t["tm"] = -jnp.where(strict, kk_i * t["decay"], 0.0)
            t["p16"] = t["tm"].astype(BF16)
        return [_dot(t["p16"], t["p16"]) for t in items]

    def pre_level(items, sq, last):
        for i, t in enumerate(items):
            t["pw"] = sq[i]
            t["p16"] = sq[i].astype(BF16)
        app = [_dot(t["p16"], t["tm"].astype(BF16)) for t in items]
        nxt = None if last else [_dot(t["p16"], t["p16"]) for t in items]
        for i, t in enumerate(items):
            t["tm"] = t["tm"] + t["pw"] + app[i]
        return nxt

    def pre_solve(items):
        corr = [_dot(t["tm"].astype(BF16), t["sol"].astype(BF16)) for t in items]
        for i, t in enumerate(items):
            sol = t["sol"] + corr[i]
            u_s[pl.ds(t["r0"], CHUNK), cols[t["h"]]] = sol[:, :DN_DIM]
            w_s[pl.ds(t["r0"], CHUNK), cols[t["h"]]] = sol[:, DN_DIM:].astype(BF16)

    def rec_read(c):
        r0 = pl.multiple_of(c * CHUNK, CHUNK)
        st = [state_s[h] for h in heads]
        st16 = [x.astype(BF16) for x in st]
        ws = [_dot(w_s[pl.ds(r0, CHUNK), cols[h]], st16[h]) for h in heads]
        qs = [_dot(qg_s[pl.ds(r0, CHUNK), cols[h]], st16[h]) for h in heads]
        return dict(c=c, r0=r0, st=st, ws=ws, qs=qs)

    def rec_update(rd):
        c, r0 = rd["c"], rd["r0"]
        vn16 = [(u_s[pl.ds(r0, CHUNK), cols[h]] - rd["ws"][h]).astype(BF16) for h in heads]
        av = [_dot(attn_s[c, h], vn16[h]) for h in heads]
        ks = [_dot_tn(kt_s[pl.ds(r0, CHUNK), cols[h]], vn16[h]) for h in heads]
        for h in heads:
            state_s[h] = rd["st"][h] * egl_s[c, h:h + 1, :] + ks[h]
            o = rd["qs"][h] + av[h]
            o = o * lax.rsqrt(jnp.mean(o * o, axis=-1, keepdims=True) + EPS)
            z = dn_ref[pl.ds(r0, CHUNK), 3 * DN_WIDTH + h * LANES:3 * DN_WIDTH + (h + 1) * LANES].astype(F32)
            o_ref[pl.ds(r0, CHUNK), cols[h]] = (o * normw_ref[...] * (z * _sigmoid(z))).astype(BF16)

    def step(it, with_pre, with_rec):
        hooks = []
        for ci in range(DN_PAIR if with_rec else 0):
            hooks += [("read", ci), ("update", ci)]
        pending = {}

        def run_hook():
            if hooks:
                kind, ci = hooks.pop(0)
                if kind == "read":
                    pending[ci] = rec_read((it - 1) * DN_PAIR + ci)
                else:
                    rec_update(pending.pop(ci))

        if not with_pre:
            while hooks:
                run_hook()
            return
        slots = [(i * 7) // max(len(hooks), 1) for i in range(len(hooks))]

        def boundary(b):
            for _ in range(slots.count(b)):
                run_hook()

        items = pre_items(it)
        boundary(0)
        sq = pre_gram(items)
        boundary(1)
        for level in range(1, 6):
            sq = pre_level(items, sq, level == 5)
            boundary(level + 1)
        pre_solve(items)
        while hooks:
            run_hook()

    step(0, True, False)

    def fused(it, carry):
        step(it, True, True)
        return carry

    lax.fori_loop(1, steps, fused, 0)
    step(steps, False, True)


def _deltanet(dn3, ab3, abt4, conv_w, shift, alog, dtb, alogt, dtbt, norm_w):
    b, s, _ = dn3.shape
    nch = s // CHUNK
    return pl.pallas_call(
        _deltanet_kernel,
        grid=(b,),
        in_specs=[
            pl.BlockSpec((None, s, 4 * DN_WIDTH), lambda i: (i, 0, 0)),
            pl.BlockSpec((None, s, LANES), lambda i: (i, 0, 0)),
            pl.BlockSpec((None, nch, 8, CHUNK), lambda i: (i, 0, 0, 0)),
            _const_spec(conv_w.shape),
            _const_spec(shift.shape),
            _const_spec((1, LANES)),
            _const_spec((1, LANES)),
            _const_spec((8, 1)),
            _const_spec((8, 1)),
            _const_spec((1, DN_DIM)),
        ],
        out_specs=pl.BlockSpec((None, s, DN_WIDTH), lambda i: (i, 0, 0)),
        out_shape=jax.ShapeDtypeStruct((b, s, DN_WIDTH), BF16),
        scratch_shapes=[
            pltpu.VMEM((DN_HEADS, DN_DIM, DN_DIM), F32),
            pltpu.VMEM((s, DN_WIDTH), F32),
            pltpu.VMEM((s, DN_WIDTH), BF16),
            pltpu.VMEM((s, DN_WIDTH), BF16),
            pltpu.VMEM((s, DN_WIDTH), BF16),
            pltpu.VMEM((nch, DN_HEADS, CHUNK, CHUNK), BF16),
            pltpu.VMEM((nch, 8, LANES), F32),
        ],
        compiler_params=_params("arbitrary"),
        name="deltanet",
    )(dn3, ab3, abt4, conv_w, shift, alog, dtb, alogt, dtbt, norm_w)


def _stickbreak_kernel(q_ref, k_ref, v_ref, qw_ref, kw_ref, m2_ref, hsum_ref, o_ref,
                       qn_s, kn_s, vm_s, carry_s, acc_s):
    s, width = q_ref.shape
    nh = width // SB_DIM
    group = carry_s.shape[0]
    lane = lax.broadcasted_iota(jnp.int32, (1, width), 1)
    head_lanes = [jnp.logical_and(lane >= h * SB_DIM, lane < (h + 1) * SB_DIM) for h in range(nh)]

    def head_norm(r0, x_ref, w):
        x = x_ref[pl.ds(r0, SB_BLOCK), :].astype(F32)
        sq = x * x
        hi = sq.astype(BF16)
        lo = (sq - hi.astype(F32)).astype(BF16)
        ms = (_dot(hi, hsum_ref[...]) + _dot(lo, hsum_ref[...])) * (1.0 / SB_DIM)
        return x * lax.rsqrt(ms + EPS) * w

    def norm_block(i, carry):
        r0 = pl.multiple_of(i * SB_BLOCK, SB_BLOCK)
        qn = head_norm(r0, q_ref, qw_ref[...]) * (SB_DIM ** -0.5 * LOG2E)
        v = v_ref[pl.ds(r0, SB_BLOCK), :]
        for h in range(nh):
            qn_s[h, pl.ds(r0, SB_BLOCK), :] = jnp.where(head_lanes[h], qn, 0.0).astype(BF16)
            vm_s[i, h * SB_BLOCK:(h + 1) * SB_BLOCK, :] = jnp.where(head_lanes[h], v, jnp.zeros_like(v))
        kn_s[pl.ds(r0, SB_BLOCK), :] = head_norm(r0, k_ref, kw_ref[...]).astype(BF16)
        return carry

    lax.fori_loop(0, s // SB_BLOCK, norm_block, 0, unroll=min(4, s // SB_BLOCK))

    row = lax.broadcasted_iota(jnp.int32, (SB_BLOCK, SB_BLOCK), 0)
    col = lax.broadcasted_iota(jnp.int32, (SB_BLOCK, SB_BLOCK), 1)
    causal = col < row

    def step(rows, key_blocks, valid):
        tiles = [(g, h) for g in range(group) for h in range(nh)]
        k16 = [kn_s[pl.ds(pl.multiple_of(kb * SB_BLOCK, SB_BLOCK), SB_BLOCK), :] for kb in key_blocks]
        z = [_dot_nt(qn_s[h, pl.ds(rows[g], SB_BLOCK), :], k16[g]) for g, h in tiles]
        sp = [jnp.maximum(x, 0.0) + jnp.log(1.0 + jnp.exp2(-jnp.abs(x))) * LOG2E for x in z]
        fail = [jnp.where(causal, x, 0.0) for x in sp] if valid is None else sp
        cs = [_dot(x.astype(BF16), m2_ref[...]) for x in fail]
        p = []
        for i, (g, h) in enumerate(tiles):
            w = jnp.exp2(z[i] - sp[i] + carry_s[g, h] + cs[i][:, :SB_BLOCK])
            p.append((jnp.where(causal, w, 0.0) if valid is None else w).astype(BF16))
        pv = [_dot(jnp.concatenate(p[g * nh:(g + 1) * nh], axis=1), vm_s[key_blocks[g]])
              for g in range(group)]
        for g in range(group):
            acc = acc_s[g] + pv[g]
            acc_s[g] = acc if valid is None else jnp.where(valid[g], acc, acc_s[g])
        for i, (g, h) in enumerate(tiles):
            carry = carry_s[g, h] + cs[i][:, SB_BLOCK:]
            carry_s[g, h] = carry if valid is None else jnp.where(valid[g], carry, carry_s[g, h])

    def q_group(qg, carry):
        blocks = [qg * group + g for g in range(group)]
        rows = [pl.multiple_of(qb * SB_BLOCK, SB_BLOCK) for qb in blocks]
        carry_s[...] = jnp.zeros(carry_s.shape, F32)
        acc_s[...] = jnp.zeros(acc_s.shape, F32)
        step(rows, blocks, None)

        def cond(st):
            d, alive = st
            return jnp.logical_and(d <= blocks[-1], alive)

        def body(st):
            d, _ = st
            step(rows, [jnp.maximum(qb - d, 0) for qb in blocks], [qb >= d for qb in blocks])
            return d + 1, jnp.max(carry_s[...]) > SB_UNDERFLOW_LOG2

        lax.while_loop(cond, body, (jnp.int32(1), jnp.bool_(True)))
        for g in range(group):
            o_ref[pl.ds(rows[g], SB_BLOCK), :] = acc_s[g].astype(BF16)
        return carry

    lax.fori_loop(0, s // (SB_BLOCK * group), q_group, 0)


def _stickbreak(sb3, qw, kw, m2, hsum):
    b, s, _ = sb3.shape
    width = hsum.shape[0]
    nh = width // SB_DIM
    parts = SB_WIDTH // width
    group = min(SB_GROUP, s // SB_BLOCK)
    return pl.pallas_call(
        _stickbreak_kernel,
        grid=(b, parts),
        in_specs=[
            pl.BlockSpec((None, s, width), lambda i, j: (i, 0, j)),
            pl.BlockSpec((None, s, width), lambda i, j: (i, 0, parts + j)),
            pl.BlockSpec((None, s, width), lambda i, j: (i, 0, 2 * parts + j)),
            _const_spec((1, width)),
            _const_spec((1, width)),
            _const_spec(m2.shape),
            _const_spec(hsum.shape),
        ],
        out_specs=pl.BlockSpec((None, s, width), lambda i, j: (i, 0, j)),
        out_shape=jax.ShapeDtypeStruct((b, s, SB_WIDTH), BF16),
        scratch_shapes=[
            pltpu.VMEM((nh, s, width), BF16),
            pltpu.VMEM((s, width), BF16),
            pltpu.VMEM((s // SB_BLOCK, nh * SB_BLOCK, width), BF16),
            pltpu.VMEM((group, nh, SB_BLOCK, SB_BLOCK), F32),
            pltpu.VMEM((group, SB_BLOCK, width), F32),
        ],
        compiler_params=_params("arbitrary", "arbitrary"),
        name="stickbreak",
    )(sb3, sb3, sb3, qw, kw, m2, hsum)


def _memkv_kernel(mem_ref, nw_ref, wk_ref, wv_ref, knw_ref, k_ref, v_ref):
    n = _rms(mem_ref[...], nw_ref[...]).astype(BF16)
    k = _dot(n, wk_ref[...])
    for h in range(X_HEADS):
        cs = slice(h * X_DIM, (h + 1) * X_DIM)
        k_ref[:, cs] = _rms(k[:, cs], knw_ref[...]).astype(BF16)
    v_ref[...] = _dot(n, wv_ref[...]).astype(BF16)


def _memkv(mem, norm_w, wk, wv, k_norm_w):
    b, m, d = mem.shape
    return pl.pallas_call(
        _memkv_kernel,
        grid=(b,),
        in_specs=[
            pl.BlockSpec((None, m, d), lambda i: (i, 0, 0)),
            _const_spec((1, d)),
            _const_spec(wk.shape),
            _const_spec(wv.shape),
            _const_spec((1, X_DIM)),
        ],
        out_specs=[
            pl.BlockSpec((None, m, d), lambda i: (i, 0, 0)),
            pl.BlockSpec((None, m, d), lambda i: (i, 0, 0)),
        ],
        out_shape=[jax.ShapeDtypeStruct((b, m, d), BF16)] * 2,
        compiler_params=_params("arbitrary"),
        name="memkv",
    )(mem, norm_w, wk, wv, k_norm_w)


def _mid_kernel(x_ref, ydn_ref, ysb_ref, wout_ref, n2w_ref, wq_ref, qnw_ref, km_ref, vm_ref, wo_ref,
                n3w_ref, wrh_ref, wrl_ref, rb_ref, upper_ref,
                x2_ref, h3_ref, idx_ref, gate_ref, rank_ref, cnt_ref, count_s):
    tm = x_ref.shape[0]
    first_step = jnp.logical_and(pl.program_id(0) == 0, pl.program_id(1) == 0)

    @pl.when(first_step)
    def _():
        count_s[...] = jnp.zeros(count_s.shape, F32)

    sub = upper_ref.shape[0]
    subs = range(tm // sub)
    rows = [slice(i * sub, (i + 1) * sub) for i in subs]
    heads = range(X_HEADS)
    cols = [slice(h * X_DIM, (h + 1) * X_DIM) for h in heads]
    x1 = [x_ref[r, :] + _dot(ydn_ref[r, :], wout_ref[0:DN_WIDTH, :])
          + _dot(ysb_ref[r, :], wout_ref[DN_WIDTH:, :]) for r in rows]
    n2 = [_rms(v, n2w_ref[...]).astype(BF16) for v in x1]
    q = [_dot(v, wq_ref[...]) for v in n2]
    qh = [[(_rms(q[i][:, c], qnw_ref[...]) * (X_DIM ** -0.5)).astype(BF16) for c in cols] for i in subs]
    sc = [[_dot_nt(qh[i][h], km_ref[:, cols[h]]) for h in heads] for i in subs]
    ex = [[jnp.exp(sc[i][h] - jnp.max(sc[i][h], axis=-1, keepdims=True)) for h in heads] for i in subs]
    pr = [[(ex[i][h] / jnp.sum(ex[i][h], axis=-1, keepdims=True)).astype(BF16) for h in heads] for i in subs]
    oh = [[_dot(pr[i][h], vm_ref[:, cols[h]]).astype(BF16) for h in heads] for i in subs]
    x2 = [x1[i] + _dot(jnp.concatenate(oh[i], axis=1), wo_ref[...]) for i in subs]
    h3 = [_rms(v, n3w_ref[...]) for v in x2]
    hi = [v.astype(BF16) for v in h3]
    lo = [(h3[i] - hi[i].astype(F32)).astype(BF16) for i in subs]
    logits = [_dot_nt(wrh_ref[...], hi[i]) + _dot_nt(wrh_ref[...], lo[i]) + _dot_nt(wrl_ref[...], hi[i])
              + rb_ref[...] for i in subs]
    eid = lax.broadcasted_iota(jnp.int32, (N_EXPERTS, sub), 0).astype(F32)
    count = count_s[...]
    for i in subs:
        x2_ref[rows[i], :] = x2[i]
        h3_ref[rows[i], :] = h3[i]
        vals, ids = [], []
        cur = logits[i]
        for _ in range(TOP_K):
            m = jnp.max(cur, axis=0, keepdims=True)
            j = jnp.min(jnp.where(cur == m, eid, float(N_EXPERTS)), axis=0, keepdims=True)
            vals.append(m)
            ids.append(j)
            cur = jnp.where(eid == j, -jnp.inf, cur)
        exps = [jnp.exp(v - vals[0]) for v in vals]
        denom = exps[0] + exps[1] + exps[2] + exps[3]
        onehot = jnp.zeros((N_EXPERTS, sub), F32)
        for j in ids:
            onehot = onehot + jnp.where(eid == j, 1.0, 0.0)
        before = count + _dot(onehot.astype(BF16), upper_ref[...])
        for k in range(TOP_K):
            idx_ref[k:k + 1, rows[i]] = ids[k].astype(jnp.int32)
            gate_ref[k:k + 1, rows[i]] = exps[k] / denom
            rank_ref[k:k + 1, rows[i]] = jnp.sum(jnp.where(eid == ids[k], before, 0.0), axis=0,
                                                 keepdims=True).astype(jnp.int32)
        count = count + jnp.sum(onehot, axis=1, keepdims=True)
    count_s[...] = count
    cnt_ref[...] = jnp.broadcast_to(count, cnt_ref.shape)


def _mid(x3, ydn3, ysb3, w_out, n2w, wq, qnw, k_mem, v_mem, wo, n3w, wr_hi, wr_lo, rb, upper, tm):
    b, s, d = x3.shape
    n = b * s
    nt = s // tm
    m = k_mem.shape[1]
    tok = lambda i, j: (0, i * nt + j)
    return pl.pallas_call(
        _mid_kernel,
        grid=(b, nt),
        in_specs=[
            pl.BlockSpec((None, tm, d), lambda i, j: (i, j, 0)),
            pl.BlockSpec((None, tm, DN_WIDTH), lambda i, j: (i, j, 0)),
            pl.BlockSpec((None, tm, SB_WIDTH), lambda i, j: (i, j, 0)),
            _const_spec(w_out.shape),
            _const_spec((1, d)),
            _const_spec(wq.shape),
            _const_spec((1, X_DIM)),
            pl.BlockSpec((None, m, d), lambda i, j: (i, 0, 0)),
            pl.BlockSpec((None, m, d), lambda i, j: (i, 0, 0)),
            _const_spec(wo.shape),
            _const_spec((1, d)),
            _const_spec(wr_hi.shape),
            _const_spec(wr_lo.shape),
            _const_spec((N_EXPERTS, 1)),
            _const_spec(upper.shape),
        ],
        out_specs=[
            pl.BlockSpec((None, tm, d), lambda i, j: (i, j, 0)),
            pl.BlockSpec((None, tm, d), lambda i, j: (i, j, 0)),
            pl.BlockSpec((TOP_K, tm), tok),
            pl.BlockSpec((TOP_K, tm), tok),
            pl.BlockSpec((TOP_K, tm), tok),
            _const_spec((N_EXPERTS, LANES)),
        ],
        out_shape=[
            jax.ShapeDtypeStruct((b, s, d), F32),
            jax.ShapeDtypeStruct((b, s, d), F32),
            jax.ShapeDtypeStruct((TOP_K, n), jnp.int32),
            jax.ShapeDtypeStruct((TOP_K, n), F32),
            jax.ShapeDtypeStruct((TOP_K, n), jnp.int32),
            jax.ShapeDtypeStruct((N_EXPERTS, LANES), F32),
        ],
        scratch_shapes=[pltpu.VMEM((N_EXPERTS, 1), F32)],
        compiler_params=pltpu.CompilerParams(dimension_semantics=("arbitrary", "arbitrary"),
                                             vmem_limit_bytes=V7X_EXPERTS_VMEM_LIMIT),
        name="mid",
    )(x3, ydn3, ysb3, w_out, n2w, wq, qnw, k_mem, v_mem, wo, n3w, wr_hi, wr_lo, rb, upper)


def _row_copy(src_ref, src_row, dst_ref, dst_row, sem):
    return pltpu.make_async_copy(src_ref.at[pl.ds(src_row, 1), :], dst_ref.at[pl.ds(dst_row, 1), :], sem)


def _dispatch_kernel(pend_ref, padded_ref, dest_ref, h_hbm, xpad_ref, zero_s, hbuf, sem, zsem, fsem):
    tf = hbuf.shape[1]
    i = pl.program_id(0)
    last = pl.num_programs(0) - 1

    def fetch(tile, slot):
        start = pl.multiple_of(tile * tf, tf)
        return pltpu.make_async_copy(h_hbm.at[pl.ds(start, tf), :], hbuf.at[slot], fsem.at[slot])

    def wait_rows(step):
        for _ in range(TOP_K):
            pltpu.make_async_copy(hbuf.at[0], xpad_ref.at[pl.ds(0, tf), :], sem.at[step % 2]).wait()

    @pl.when(i == 0)
    def _():
        fetch(0, 0).start()

        @pl.when(last >= 1)
        def _():
            fetch(1, 1).start()

        zero_s[...] = jnp.zeros(zero_s.shape, F32)

        def last_block(e):
            start = pl.multiple_of(pend_ref[e] - EXPERT_BLOCK, EXPERT_BLOCK)
            return pltpu.make_async_copy(zero_s, xpad_ref.at[pl.ds(start, EXPERT_BLOCK), :], zsem)

        for e in range(N_EXPERTS):
            @pl.when(padded_ref[e] > 0)
            def _(e=e):
                last_block(e).start()
        for e in range(N_EXPERTS):
            @pl.when(padded_ref[e] > 0)
            def _(e=e):
                last_block(e).wait()

    slot = i % 3
    fetch(i, slot).wait()
    src = hbuf.at[slot]
    for k in range(TOP_K):
        for t in range(tf):
            _row_copy(src, t, xpad_ref, dest_ref[0, k * tf + t], sem.at[i % 2]).start(priority=t % 2)

    @pl.when(i >= 1)
    def _():
        wait_rows(i - 1)

    @pl.when(i + 2 <= last)
    def _():
        fetch(i + 2, (i + 2) % 3).start()

    @pl.when(i == last)
    def _():
        wait_rows(i)


def _dispatch(pad_ends, padded, dest_tiles, h2d, rows, tf):
    n, d = h2d.shape
    return pl.pallas_call(
        _dispatch_kernel,
        grid_spec=pltpu.PrefetchScalarGridSpec(
            num_scalar_prefetch=2,
            grid=(n // tf,),
            in_specs=[
                pl.BlockSpec((None, 1, TOP_K * tf), lambda i, *_: (i, 0, 0), memory_space=pltpu.SMEM),
                pl.BlockSpec(memory_space=pl.ANY),
            ],
            out_specs=pl.BlockSpec(memory_space=pl.ANY),
            scratch_shapes=[pltpu.VMEM((EXPERT_BLOCK, d), F32), pltpu.VMEM((3, tf, d), F32),
                            pltpu.SemaphoreType.DMA((2,)), pltpu.SemaphoreType.DMA(()),
                            pltpu.SemaphoreType.DMA((3,))],
        ),
        out_shape=jax.ShapeDtypeStruct((rows, d), F32),
        compiler_params=_params("arbitrary"),
        name="dispatch",
    )(pad_ends, padded, dest_tiles, h2d)


def _experts_kernel(be_ref, nb_ref, first_ref, ord_ref, nxt_ref, x_ref, wg_hbm, bg_ref, wu_hbm, bu_ref,
                    wd_hbm, bd_ref, y_ref, wf_s, wb_s, wsem):
    j = pl.program_id(0)
    used = j < nb_ref[0]
    slot = ord_ref[j] % 2

    def fetch(e, s):
        return [pltpu.make_async_copy(w.at[e], wf_s.at[s, i], wsem.at[s])
                for i, w in enumerate((wg_hbm, wu_hbm, wd_hbm))]

    @pl.when(j == 0)
    def _():
        for c in fetch(be_ref[0], 0):
            c.start()

    @pl.when(jnp.logical_and(used, first_ref[j] == 1))
    def _():
        for c in fetch(be_ref[j], slot):
            c.wait()

        @pl.when(nxt_ref[j] >= 0)
        def _():
            for c in fetch(nxt_ref[j], 1 - slot):
                c.start()

        for i in range(3):
            wb_s[i] = wf_s[slot, i].astype(BF16)

    @pl.when(used)
    def _():
        x = x_ref[...].astype(BF16)
        gate = jnp.minimum(_dot(x, wb_s[0]) + bg_ref[...], SWIGLU_LIMIT)
        up = jnp.clip(_dot(x, wb_s[1]) + bu_ref[...], -SWIGLU_LIMIT, SWIGLU_LIMIT)
        act = (up + 1.0) * gate * _sigmoid(gate * SWIGLU_ALPHA)
        y_ref[...] = _dot(act.astype(BF16), wb_s[2]) + bd_ref[...]


def _experts(block_e, nb_used, first, ordinal, nxt, x_pad, wg, bg, wu, bu, wd, bd):
    rows, d = x_pad.shape
    nblk = rows // EXPERT_BLOCK
    dff = wg.shape[2]
    assert d == dff, "the staging buffers hold all three weight matrices in one (d, dff) shape"
    row_blk = lambda j, be, nb, *_: (jnp.minimum(j, nb[0] - 1), 0)
    b_blk = lambda j, be, *_: (be[j], 0, 0)
    hbm = pl.BlockSpec(memory_space=pl.ANY)
    return pl.pallas_call(
        _experts_kernel,
        grid_spec=pltpu.PrefetchScalarGridSpec(
            num_scalar_prefetch=5,
            grid=(nblk,),
            in_specs=[
                pl.BlockSpec((EXPERT_BLOCK, d), row_blk),
                hbm,
                pl.BlockSpec((None, 1, dff), b_blk),
                hbm,
                pl.BlockSpec((None, 1, dff), b_blk),
                hbm,
                pl.BlockSpec((None, 1, d), b_blk),
            ],
            out_specs=pl.BlockSpec((EXPERT_BLOCK, d), row_blk),
            scratch_shapes=[pltpu.VMEM((2, 3, d, dff), F32), pltpu.VMEM((3, d, dff), BF16),
                            pltpu.SemaphoreType.DMA((2,))],
        ),
        out_shape=jax.ShapeDtypeStruct((rows, d), F32),
        compiler_params=pltpu.CompilerParams(dimension_semantics=("arbitrary",),
                                             vmem_limit_bytes=V7X_EXPERTS_VMEM_LIMIT),
        name="experts",
    )(block_e, nb_used, first, ordinal, nxt, x_pad, wg, bg, wu, bu, wd, bd)


def _combine_kernel(dest_ref, dnext_ref, x2_ref, gate_ref, ypad_ref, o_ref, buf, sem):
    th = x2_ref.shape[0]
    i = pl.program_id(0)
    slot = i % 2

    def gather(dref, s):
        for k in range(TOP_K):
            for t in range(th):
                _row_copy(ypad_ref, dref[0, k * th + t], buf.at[s, k], t, sem.at[s]).start(priority=t % 2)

    @pl.when(i == 0)
    def _():
        gather(dest_ref, 0)

    @pl.when(i + 1 < pl.num_programs(0))
    def _():
        gather(dnext_ref, 1 - slot)

    for k in range(TOP_K):
        pltpu.make_async_copy(ypad_ref.at[pl.ds(0, th), :], buf.at[slot, k], sem.at[slot]).wait()
    g = gate_ref[...]
    out = x2_ref[...]
    for k in range(TOP_K):
        out = out + buf[slot, k] * g[:, k:k + 1]
    o_ref[...] = out


def _combine(dest_tiles, x2d, gates_nk, y_pad, th):
    n, d = x2d.shape
    nt = n // th
    return pl.pallas_call(
        _combine_kernel,
        grid=(nt,),
        in_specs=[
            pl.BlockSpec((None, 1, TOP_K * th), lambda i: (i, 0, 0), memory_space=pltpu.SMEM),
            pl.BlockSpec((None, 1, TOP_K * th), lambda i: (jnp.minimum(i + 1, nt - 1), 0, 0),
                         memory_space=pltpu.SMEM),
            pl.BlockSpec((th, d), lambda i: (i, 0)),
            pl.BlockSpec((th, TOP_K), lambda i: (i, 0)),
            pl.BlockSpec(memory_space=pl.ANY),
        ],
        out_specs=pl.BlockSpec((th, d), lambda i: (i, 0)),
        out_shape=jax.ShapeDtypeStruct((n, d), F32),
        scratch_shapes=[pltpu.VMEM((2, TOP_K, th, d), F32), pltpu.SemaphoreType.DMA((2,))],
        compiler_params=_params("arbitrary"),
        name="combine",
    )(dest_tiles, dest_tiles, x2d, gates_nk, y_pad)


def _tile_dest(dest, t):
    k, n = dest.shape
    return dest.reshape(k, n // t, t).transpose(1, 0, 2).reshape(n // t, 1, k * t)


def _layer(x, mem, norm1_w, w_in, conv_w, a_log, dt_bias, dn_norm_w, sb_q_norm_w, sb_k_norm_w, w_out,
           norm2_w, mem_norm_w, xq_w, xk_w, xv_w, xq_norm_w, xk_norm_w, xo_w, norm3_w, router_w,
           router_b, w_gate, b_gate, w_up, b_up, w_down, b_down):
    b, s, d = x.shape
    n = b * s
    tm_proj = min(512, s)
    tm_mid = min(1024, s)
    sub_mid = min(256, s)
    t_moe = min(256, s)

    o_dn, o_ab, o_sb = 4 * DN_WIDTH, 4 * DN_WIDTH + 2 * DN_HEADS, 4 * DN_WIDTH + 2 * DN_HEADS
    w_dn = w_in[:, :o_dn].astype(BF16)
    w_ab_f = w_in[:, o_dn:o_ab]
    w_ab = jnp.pad(w_ab_f, ((0, 0), (0, LANES - 2 * DN_HEADS))).astype(BF16)
    w_abt = w_ab_f.T.astype(BF16)
    w_sb = w_in[:, o_sb:].astype(BF16)
    row = lambda v: v.reshape(1, -1).astype(F32)

    dn, sb, ab, abt = _in_proj(x.reshape(n, d), row(norm1_w), w_dn, w_sb, w_ab, w_abt, tm_proj)

    abt4 = abt.reshape(8, n // CHUNK, CHUNK).transpose(1, 0, 2).reshape(b, s // CHUNK, 8, CHUNK)
    pad_lane = lambda v: jnp.pad(v.astype(F32), (0, LANES - v.shape[0])).reshape(1, LANES)
    pad_col = lambda v: jnp.pad(v.astype(F32), (0, 8 - v.shape[0])).reshape(8, 1)
    tap = jnp.arange(CONV_WIDTH * CHUNK)
    shift = (jnp.arange(2 * CHUNK)[None, :] == (CHUNK - CONV_WIDTH + 1 + tap // CHUNK + tap % CHUNK)[:, None])
    y_dn = _deltanet(dn.reshape(b, s, -1), ab.reshape(b, s, LANES), abt4, conv_w.astype(F32),
                     shift.astype(BF16), pad_lane(a_log), pad_lane(dt_bias), pad_col(a_log), pad_col(dt_bias),
                     row(dn_norm_w))

    ii = jnp.arange(SB_BLOCK)
    m2 = -jnp.concatenate([(ii[:, None] > ii[None, :]).astype(BF16),
                           jnp.ones((SB_BLOCK, SB_BLOCK), BF16)], axis=1)
    sb_heads = SB_LANES // SB_DIM
    hh = jnp.arange(SB_LANES) // SB_DIM
    hsum = (hh[:, None] == hh[None, :]).astype(BF16)
    y_sb = _stickbreak(sb.reshape(b, s, -1), row(jnp.tile(sb_q_norm_w, sb_heads)),
                       row(jnp.tile(sb_k_norm_w, sb_heads)), m2, hsum)

    k_mem, v_mem = _memkv(mem, row(mem_norm_w), xk_w.astype(BF16), xv_w.astype(BF16), row(xk_norm_w))

    wr_t = router_w.T.astype(F32)
    wr_hi = wr_t.astype(BF16)
    wr_lo = (wr_t - wr_hi.astype(F32)).astype(BF16)
    jj = jnp.arange(sub_mid)
    upper = (jj[:, None] < jj[None, :]).astype(BF16)
    x2, h3, idx, gates, rank, cnt = _mid(
        x, y_dn, y_sb, w_out.astype(BF16), row(norm2_w), xq_w.astype(BF16), row(xq_norm_w), k_mem, v_mem,
        xo_w.astype(BF16), row(norm3_w), wr_hi, wr_lo, router_b.reshape(N_EXPERTS, 1).astype(F32), upper,
        tm_mid)

    counts = cnt[:, 0].astype(jnp.int32)
    padded = (counts + EXPERT_BLOCK - 1) // EXPERT_BLOCK * EXPERT_BLOCK
    pad_ends = jnp.cumsum(padded)
    pad_starts = pad_ends - padded
    sel = idx[:, :, None] == jnp.arange(N_EXPERTS, dtype=jnp.int32)[None, None, :]
    dest = rank + jnp.sum(jnp.where(sel, pad_starts[None, None, :], 0), axis=-1)
    n_blocks = -(-n * TOP_K // EXPERT_BLOCK) + N_EXPERTS
    nb_used = (pad_ends[-1] // EXPERT_BLOCK).astype(jnp.int32)
    blk = jnp.minimum(jnp.arange(n_blocks, dtype=jnp.int32), nb_used - 1) * EXPERT_BLOCK
    block_e = jnp.minimum(jnp.sum(pad_ends[None, :] <= blk[:, None], axis=1), N_EXPERTS - 1).astype(jnp.int32)
    dest_tiles = _tile_dest(dest, t_moe)

    x_pad = _dispatch(pad_ends.astype(jnp.int32), padded.astype(jnp.int32), dest_tiles, h3.reshape(n, d),
                      n_blocks * EXPERT_BLOCK, t_moe)
    bias = lambda v: v.reshape(N_EXPERTS, 1, -1).astype(F32)
    jb = jnp.arange(n_blocks, dtype=jnp.int32)
    first = jnp.logical_and(jb < nb_used, jnp.logical_or(jb == 0, block_e != jnp.roll(block_e, 1)))
    ordinal = (jnp.cumsum(first) - 1).astype(jnp.int32)
    later_first = jnp.roll(jnp.where(first, jb, n_blocks), -1).at[-1].set(n_blocks)
    nxt_pos = lax.cummin(later_first, axis=0, reverse=True)
    nxt = jnp.where(nxt_pos < n_blocks, block_e[jnp.minimum(nxt_pos, n_blocks - 1)], -1).astype(jnp.int32)
    y_pad = _experts(block_e, nb_used.reshape(1), first.astype(jnp.int32), ordinal, nxt, x_pad,
                     w_gate.astype(F32), bias(b_gate), w_up.astype(F32), bias(b_up), w_down.astype(F32),
                     bias(b_down))
    out = _combine(dest_tiles, x2.reshape(n, d), gates.T, y_pad, t_moe)
    return out.reshape(b, s, d)


def kernel(x, mem, norm1_w, w_in, conv_w, a_log, dt_bias, dn_norm_w, sb_q_norm_w, sb_k_norm_w, w_out,
           norm2_w, mem_norm_w, xq_w, xk_w, xv_w, xq_norm_w, xk_norm_w, xo_w, norm3_w, router_w,
           router_b, w_gate, b_gate, w_up, b_up, w_down, b_down):
    depth = w_in.shape[0]
    for l in range(depth):
        x = _layer(x, mem, norm1_w[l], w_in[l], conv_w[l], a_log[l], dt_bias[l], dn_norm_w[l],
                   sb_q_norm_w[l], sb_k_norm_w[l], w_out[l], norm2_w[l], mem_norm_w[l], xq_w[l], xk_w[l],
                   xv_w[l], xq_norm_w[l], xk_norm_w[l], xo_w[l], norm3_w[l], router_w[l], router_b[l],
                   w_gate[l], b_gate[l], w_up[l], b_up[l], w_down[l], b_down[l])
    return x
```

```python
import jax
import jax.numpy as jnp
from jax import lax
from jax.experimental import pallas as pl
from jax.experimental.pallas import tpu as pltpu

F32 = jnp.float32
BF16 = jnp.bfloat16

EPS = 1e-6
CHUNK = 64
DN_HEADS = 4
DN_DIM = 128
DN_WIDTH = DN_HEADS * DN_DIM
CONV_WIDTH = 4
DN_PAIR = 4
SB_HEADS = 8
SB_DIM = 64
SB_WIDTH = SB_HEADS * SB_DIM
SB_BLOCK = 128
SB_GROUP = 4
SB_LANES = 256
X_HEADS = 4
X_DIM = 256
N_EXPERTS = 32
TOP_K = 4
EXPERT_BLOCK = 512
SWIGLU_ALPHA = 1.702
SWIGLU_LIMIT = 7.0
LANES = 128
LOG2E = 1.4426950408889634
SB_UNDERFLOW_LOG2 = -127.0
V7X_VMEM_LIMIT = 48 * 1024 * 1024
V7X_EXPERTS_VMEM_LIMIT = 58 * 1024 * 1024


def _dot(a, b):
    return jnp.dot(a, b, preferred_element_type=F32)


def _dot_nt(a, b):
    return lax.dot_general(a, b, (((1,), (1,)), ((), ())), preferred_element_type=F32)


def _dot_tn(a, b):
    return lax.dot_general(a, b, (((0,), (0,)), ((), ())), preferred_element_type=F32)


def _split3(x):
    hi = x.astype(BF16)
    r = x - hi.astype(F32)
    mid = r.astype(BF16)
    return hi, mid, (r - mid.astype(F32)).astype(BF16)


def _softplus(x):
    return jnp.maximum(x, 0.0) + jnp.log(1.0 + jnp.exp(-jnp.abs(x)))


def _sigmoid(x):
    return 1.0 / (1.0 + jnp.exp(-x))


def _rms(x, w):
    return x * lax.rsqrt(jnp.mean(x * x, axis=-1, keepdims=True) + EPS) * w


def _params(*sem):
    return pltpu.CompilerParams(dimension_semantics=sem, vmem_limit_bytes=V7X_VMEM_LIMIT)


def _const_spec(shape):
    nd = len(shape)
    return pl.BlockSpec(shape, lambda *_: (0,) * nd)


def _in_proj_kernel(x_ref, nw_ref, wdn_ref, wsb_ref, wab_ref, wabt_ref,
                    dn_ref, sb_ref, ab_ref, abt_ref):
    n = _rms(x_ref[...], nw_ref[...]).astype(BF16)
    dn_ref[...] = _dot(n, wdn_ref[...]).astype(BF16)
    sb_ref[...] = _dot(n, wsb_ref[...]).astype(BF16)
    ab_ref[...] = _dot(n, wab_ref[...])
    abt_ref[...] = _dot_nt(wabt_ref[...], n)


def _in_proj(x2d, norm_w, w_dn, w_sb, w_ab, w_abt, tm):
    n, d = x2d.shape
    return pl.pallas_call(
        _in_proj_kernel,
        grid=(n // tm,),
        in_specs=[
            pl.BlockSpec((tm, d), lambda i: (i, 0)),
            _const_spec((1, d)),
            _const_spec(w_dn.shape),
            _const_spec(w_sb.shape),
            _const_spec(w_ab.shape),
            _const_spec(w_abt.shape),
        ],
        out_specs=[
            pl.BlockSpec((tm, w_dn.shape[1]), lambda i: (i, 0)),
            pl.BlockSpec((tm, w_sb.shape[1]), lambda i: (i, 0)),
            pl.BlockSpec((tm, LANES), lambda i: (i, 0)),
            pl.BlockSpec((8, tm), lambda i: (0, i)),
        ],
        out_shape=[
            jax.ShapeDtypeStruct((n, w_dn.shape[1]), BF16),
            jax.ShapeDtypeStruct((n, w_sb.shape[1]), BF16),
            jax.ShapeDtypeStruct((n, LANES), F32),
            jax.ShapeDtypeStruct((8, n), F32),
        ],
        compiler_params=_params("arbitrary"),
        name="in_proj",
    )(x2d, norm_w, w_dn, w_sb, w_ab, w_abt)


def _deltanet_kernel(dn_ref, ab_ref, abt_ref, convw_ref, shift_ref, alog_ref, dtb_ref, alogt_ref, dtbt_ref,
                     normw_ref, o_ref, state_s, u_s, w_s, kt_s, qg_s, attn_s, egl_s):
    s = dn_ref.shape[0]
    steps = s // (CHUNK * DN_PAIR)
    state_s[...] = jnp.zeros(state_s.shape, F32)

    def conv_silu(r0, c, part, h, l2):
        cs = slice(part * DN_WIDTH + h * LANES, part * DN_WIDTH + (h + 1) * LANES)
        prev0 = pl.multiple_of(jnp.maximum(r0 - CHUNK, 0), CHUNK)
        prev = dn_ref[pl.ds(prev0, CHUNK), cs]
        prev = jnp.where(c > 0, prev, jnp.zeros_like(prev))
        win = jnp.concatenate([prev, dn_ref[pl.ds(r0, CHUNK), cs]], axis=0)
        sh = _dot(shift_ref[...], win)
        w = convw_ref[:, cs]
        y = w[0:1, :] * sh[0:CHUNK, :]
        for i in range(1, CONV_WIDTH):
            y = y + w[i:i + 1, :] * sh[i * CHUNK:(i + 1) * CHUNK, :]
        y = y * _sigmoid(y)
        if l2:
            y = y * lax.rsqrt(jnp.sum(y * y, axis=-1, keepdims=True) + EPS)
        return y

    row = lax.broadcasted_iota(jnp.int32, (CHUNK, CHUNK), 0)
    col = lax.broadcasted_iota(jnp.int32, (CHUNK, CHUNK), 1)
    tri = row >= col
    strict = row > col
    tril16 = jnp.where(tri, 1.0, 0.0).astype(BF16)
    triu16 = jnp.where(row <= col, 1.0, 0.0).astype(BF16)
    neg_a_col = -jnp.exp(alog_ref[...])
    neg_a_row = -jnp.exp(alogt_ref[...])
    scale = DN_DIM ** -0.5
    heads = range(DN_HEADS)
    cols = [slice(h * LANES, (h + 1) * LANES) for h in heads]

    def pre_items(it):
        items = []
        for ci in range(DN_PAIR):
            c = it * DN_PAIR + ci
            r0 = pl.multiple_of(c * CHUNK, CHUNK)
            ab = ab_ref[pl.ds(r0, CHUNK), :]
            g_col = neg_a_col * _softplus(ab + dtb_ref[...])
            gc_col_all = sum(_dot(tril16, p) for p in _split3(g_col))
            beta_all = _sigmoid(ab)
            abt = abt_ref[c]
            g_row = neg_a_row * _softplus(abt + dtbt_ref[...])
            gc_row_all = sum(_dot(p, triu16) for p in _split3(g_row))
            for h in heads:
                q = conv_silu(r0, c, 0, h, True) * scale
                k = conv_silu(r0, c, 1, h, True)
                v = conv_silu(r0, c, 2, h, False)
                gcol = gc_col_all[:, h:h + 1]
                grow = gc_row_all[h:h + 1, :]
                beta = beta_all[:, DN_HEADS + h:DN_HEADS + h + 1]
                glast = gcol[CHUNK - 1:CHUNK, :]
                decay = jnp.where(tri, jnp.exp(jnp.where(tri, gcol - grow, 0.0)), 0.0)
                kb = k * beta
                kt_s[pl.ds(r0, CHUNK), cols[h]] = (k * jnp.exp(glast - gcol)).astype(BF16)
                qg_s[pl.ds(r0, CHUNK), cols[h]] = (q * jnp.exp(gcol)).astype(BF16)
                egl_s[c, h:h + 1, :] = jnp.broadcast_to(jnp.exp(glast), (1, LANES))
                items.append(dict(c=c, r0=r0, h=h, decay=decay, kb16=kb.astype(BF16),
                                  k16=k.astype(BF16), q16=q.astype(BF16),
                                  sol=jnp.concatenate([v * beta, kb * jnp.exp(gcol)], axis=1)))
        return items

    def pre_gram(items):
        kk = [_dot_nt(t["kb16"], t["k16"]) for t in items]
        qk = [_dot_nt(t["q16"], t["k16"]) for t in items]
        for t, kk_i, qk_i in zip(items, kk, qk):
            attn_s[t["c"], t["h"]] = jnp.where(tri, qk_i * t["decay"], 0.0).astype(BF16)
            t["tm"] = -jnp.where(strict, kk_i * t["decay"], 0.0)
            t["p16"] = t["tm"].astype(BF16)
        return [_dot(t["p16"], t["p16"]) for t in items]

    def pre_level(items, sq, last):
        for i, t in enumerate(items):
            t["pw"] = sq[i]
            t["p16"] = sq[i].astype(BF16)
        app = [_dot(t["p16"], t["tm"].astype(BF16)) for t in items]
        nxt = None if last else [_dot(t["p16"], t["p16"]) for t in items]
        for i, t in enumerate(items):
            t["tm"] = t["tm"] + t["pw"] + app[i]
        return nxt

    def pre_solve(items):
        corr = [_dot(t["tm"].astype(BF16), t["sol"].astype(BF16)) for t in items]
        for i, t in enumerate(items):
            sol = t["sol"] + corr[i]
            u_s[pl.ds(t["r0"], CHUNK), cols[t["h"]]] = sol[:, :DN_DIM]
            w_s[pl.ds(t["r0"], CHUNK), cols[t["h"]]] = sol[:, DN_DIM:].astype(BF16)

    def rec_read(c):
        r0 = pl.multiple_of(c * CHUNK, CHUNK)
        st = [state_s[h] for h in heads]
        st16 = [x.astype(BF16) for x in st]
        ws = [_dot(w_s[pl.ds(r0, CHUNK), cols[h]], st16[h]) for h in heads]
        qs = [_dot(qg_s[pl.ds(r0, CHUNK), cols[h]], st16[h]) for h in heads]
        return dict(c=c, r0=r0, st=st, ws=ws, qs=qs)

    def rec_update(rd):
        c, r0 = rd["c"], rd["r0"]
        vn16 = [(u_s[pl.ds(r0, CHUNK), cols[h]] - rd["ws"][h]).astype(BF16) for h in heads]
        av = [_dot(attn_s[c, h], vn16[h]) for h in heads]
        ks = [_dot_tn(kt_s[pl.ds(r0, CHUNK), cols[h]], vn16[h]) for h in heads]
        for h in heads:
            state_s[h] = rd["st"][h] * egl_s[c, h:h + 1, :] + ks[h]
            o = rd["qs"][h] + av[h]
            o = o * lax.rsqrt(jnp.mean(o * o, axis=-1, keepdims=True) + EPS)
            z = dn_ref[pl.ds(r0, CHUNK), 3 * DN_WIDTH + h * LANES:3 * DN_WIDTH + (h + 1) * LANES].astype(F32)
            o_ref[pl.ds(r0, CHUNK), cols[h]] = (o * normw_ref[...] * (z * _sigmoid(z))).astype(BF16)

    def step(it, with_pre, with_rec):
        hooks = []
        for ci in range(DN_PAIR if with_rec else 0):
            hooks += [("read", ci), ("update", ci)]
        pending = {}

        def run_hook():
            if hooks:
                kind, ci = hooks.pop(0)
                if kind == "read":
                    pending[ci] = rec_read((it - 1) * DN_PAIR + ci)
                else:
                    rec_update(pending.pop(ci))

        if not with_pre:
            while hooks:
                run_hook()
            return
        slots = [(i * 7) // max(len(hooks), 1) for i in range(len(hooks))]

        def boundary(b):
            for _ in range(slots.count(b)):
                run_hook()

        items = pre_items(it)
        boundary(0)
        sq = pre_gram(items)
        boundary(1)
        for level in range(1, 6):
            sq = pre_level(items, sq, level == 5)
            boundary(level + 1)
        pre_solve(items)
        while hooks:
            run_hook()

    step(0, True, False)

    def fused(it, carry):
        step(it, True, True)
        return carry

    lax.fori_loop(1, steps, fused, 0)
    step(steps, False, True)


def _deltanet(dn3, ab3, abt4, conv_w, shift, alog, dtb, alogt, dtbt, norm_w):
    b, s, _ = dn3.shape
    nch = s // CHUNK
    return pl.pallas_call(
        _deltanet_kernel,
        grid=(b,),
        in_specs=[
            pl.BlockSpec((None, s, 4 * DN_WIDTH), lambda i: (i, 0, 0)),
            pl.BlockSpec((None, s, LANES), lambda i: (i, 0, 0)),
            pl.BlockSpec((None, nch, 8, CHUNK), lambda i: (i, 0, 0, 0)),
            _const_spec(conv_w.shape),
            _const_spec(shift.shape),
            _const_spec((1, LANES)),
            _const_spec((1, LANES)),
            _const_spec((8, 1)),
            _const_spec((8, 1)),
            _const_spec((1, DN_DIM)),
        ],
        out_specs=pl.BlockSpec((None, s, DN_WIDTH), lambda i: (i, 0, 0)),
        out_shape=jax.ShapeDtypeStruct((b, s, DN_WIDTH), BF16),
        scratch_shapes=[
            pltpu.VMEM((DN_HEADS, DN_DIM, DN_DIM), F32),
            pltpu.VMEM((s, DN_WIDTH), F32),
            pltpu.VMEM((s, DN_WIDTH), BF16),
            pltpu.VMEM((s, DN_WIDTH), BF16),
            pltpu.VMEM((s, DN_WIDTH), BF16),
            pltpu.VMEM((nch, DN_HEADS, CHUNK, CHUNK), BF16),
            pltpu.VMEM((nch, 8, LANES), F32),
        ],
        compiler_params=_params("arbitrary"),
        name="deltanet",
    )(dn3, ab3, abt4, conv_w, shift, alog, dtb, alogt, dtbt, norm_w)


def _stickbreak_kernel(q_ref, k_ref, v_ref, qw_ref, kw_ref, m2_ref, hsum_ref, o_ref,
                       qn_s, kn_s, vm_s, carry_s, acc_s):
    s, width = q_ref.shape
    nh = width // SB_DIM
    group = carry_s.shape[0]
    lane = lax.broadcasted_iota(jnp.int32, (1, width), 1)
    head_lanes = [jnp.logical_and(lane >= h * SB_DIM, lane < (h + 1) * SB_DIM) for h in range(nh)]

    def head_norm(r0, x_ref, w):
        x = x_ref[pl.ds(r0, SB_BLOCK), :].astype(F32)
        sq = x * x
        hi = sq.astype(BF16)
        lo = (sq - hi.astype(F32)).astype(BF16)
        ms = (_dot(hi, hsum_ref[...]) + _dot(lo, hsum_ref[...])) * (1.0 / SB_DIM)
        return x * lax.rsqrt(ms + EPS) * w

    def norm_block(i, carry):
        r0 = pl.multiple_of(i * SB_BLOCK, SB_BLOCK)
        qn = head_norm(r0, q_ref, qw_ref[...]) * (SB_DIM ** -0.5 * LOG2E)
        v = v_ref[pl.ds(r0, SB_BLOCK), :]
        for h in range(nh):
            qn_s[h, pl.ds(r0, SB_BLOCK), :] = jnp.where(head_lanes[h], qn, 0.0).astype(BF16)
            vm_s[i, h * SB_BLOCK:(h + 1) * SB_BLOCK, :] = jnp.where(head_lanes[h], v, jnp.zeros_like(v))
        kn_s[pl.ds(r0, SB_BLOCK), :] = head_norm(r0, k_ref, kw_ref[...]).astype(BF16)
        return carry

    lax.fori_loop(0, s // SB_BLOCK, norm_block, 0, unroll=min(4, s // SB_BLOCK))

    row = lax.broadcasted_iota(jnp.int32, (SB_BLOCK, SB_BLOCK), 0)
    col = lax.broadcasted_iota(jnp.int32, (SB_BLOCK, SB_BLOCK), 1)
    causal = col < row

    def step(rows, key_blocks, valid):
        tiles = [(g, h) for g in range(group) for h in range(nh)]
        k16 = [kn_s[pl.ds(pl.multiple_of(kb * SB_BLOCK, SB_BLOCK), SB_BLOCK), :] for kb in key_blocks]
        z = [_dot_nt(qn_s[h, pl.ds(rows[g], SB_BLOCK), :], k16[g]) for g, h in tiles]
        sp = [jnp.maximum(x, 0.0) + jnp.log(1.0 + jnp.exp2(-jnp.abs(x))) * LOG2E for x in z]
        fail = [jnp.where(causal, x, 0.0) for x in sp] if valid is None else sp
        cs = [_dot(x.astype(BF16), m2_ref[...]) for x in fail]
        p = []
        for i, (g, h) in enumerate(tiles):
            w = jnp.exp2(z[i] - sp[i] + carry_s[g, h] + cs[i][:, :SB_BLOCK])
            p.append((jnp.where(causal, w, 0.0) if valid is None else w).astype(BF16))
        pv = [_dot(jnp.concatenate(p[g * nh:(g + 1) * nh], axis=1), vm_s[key_blocks[g]])
              for g in range(group)]
        for g in range(group):
            acc = acc_s[g] + pv[g]
            acc_s[g] = acc if valid is None else jnp.where(valid[g], acc, acc_s[g])
        for i, (g, h) in enumerate(tiles):
            carry = carry_s[g, h] + cs[i][:, SB_BLOCK:]
            carry_s[g, h] = carry if valid is None else jnp.where(valid[g], carry, carry_s[g, h])

    def q_group(qg, carry):
        blocks = [qg * group + g for g in range(group)]
        rows = [pl.multiple_of(qb * SB_BLOCK, SB_BLOCK) for qb in blocks]
        carry_s[...] = jnp.zeros(carry_s.shape, F32)
        acc_s[...] = jnp.zeros(acc_s.shape, F32)
        step(rows, blocks, None)

        def cond(st):
            d, alive = st
            return jnp.logical_and(d <= blocks[-1], alive)

        def body(st):
            d, _ = st
            step(rows, [jnp.maximum(qb - d, 0) for qb in blocks], [qb >= d for qb in blocks])
            return d + 1, jnp.max(carry_s[...]) > SB_UNDERFLOW_LOG2

        lax.while_loop(cond, body, (jnp.int32(1), jnp.bool_(True)))
        for g in range(group):
            o_ref[pl.ds(rows[g], SB_BLOCK), :] = acc_s[g].astype(BF16)
        return carry

    lax.fori_loop(0, s // (SB_BLOCK * group), q_group, 0)


def _stickbreak(sb3, qw, kw, m2, hsum):
    b, s, _ = sb3.shape
    width = hsum.shape[0]
    nh = width // SB_DIM
    parts = SB_WIDTH // width
    group = min(SB_GROUP, s // SB_BLOCK)
    return pl.pallas_call(
        _stickbreak_kernel,
        grid=(b, parts),
        in_specs=[
            pl.BlockSpec((None, s, width), lambda i, j: (i, 0, j)),
            pl.BlockSpec((None, s, width), lambda i, j: (i, 0, parts + j)),
            pl.BlockSpec((None, s, width), lambda i, j: (i, 0, 2 * parts + j)),
            _const_spec((1, width)),
            _const_spec((1, width)),
            _const_spec(m2.shape),
            _const_spec(hsum.shape),
        ],
        out_specs=pl.BlockSpec((None, s, width), lambda i, j: (i, 0, j)),
        out_shape=jax.ShapeDtypeStruct((b, s, SB_WIDTH), BF16),
        scratch_shapes=[
            pltpu.VMEM((nh, s, width), BF16),
            pltpu.VMEM((s, width), BF16),
            pltpu.VMEM((s // SB_BLOCK, nh * SB_BLOCK, width), BF16),
            pltpu.VMEM((group, nh, SB_BLOCK, SB_BLOCK), F32),
            pltpu.VMEM((group, SB_BLOCK, width), F32),
        ],
        compiler_params=_params("arbitrary", "arbitrary"),
        name="stickbreak",
    )(sb3, sb3, sb3, qw, kw, m2, hsum)


def _memkv_kernel(mem_ref, nw_ref, wk_ref, wv_ref, knw_ref, k_ref, v_ref):
    n = _rms(mem_ref[...], nw_ref[...]).astype(BF16)
    k = _dot(n, wk_ref[...])
    for h in range(X_HEADS):
        cs = slice(h * X_DIM, (h + 1) * X_DIM)
        k_ref[:, cs] = _rms(k[:, cs], knw_ref[...]).astype(BF16)
    v_ref[...] = _dot(n, wv_ref[...]).astype(BF16)


def _memkv(mem, norm_w, wk, wv, k_norm_w):
    b, m, d = mem.shape
    return pl.pallas_call(
        _memkv_kernel,
        grid=(b,),
        in_specs=[
            pl.BlockSpec((None, m, d), lambda i: (i, 0, 0)),
            _const_spec((1, d)),
            _const_spec(wk.shape),
            _const_spec(wv.shape),
            _const_spec((1, X_DIM)),
        ],
        out_specs=[
            pl.BlockSpec((None, m, d), lambda i: (i, 0, 0)),
            pl.BlockSpec((None, m, d), lambda i: (i, 0, 0)),
        ],
        out_shape=[jax.ShapeDtypeStruct((b, m, d), BF16)] * 2,
        compiler_params=_params("arbitrary"),
        name="memkv",
    )(mem, norm_w, wk, wv, k_norm_w)


def _mid_kernel(x_ref, ydn_ref, ysb_ref, wout_ref, n2w_ref, wq_ref, qnw_ref, km_ref, vm_ref, wo_ref,
                n3w_ref, wrh_ref, wrl_ref, rb_ref, upper_ref,
                x2_ref, h3_ref, idx_ref, gate_ref, rank_ref, cnt_ref, count_s):
    tm = x_ref.shape[0]
    first_step = jnp.logical_and(pl.program_id(0) == 0, pl.program_id(1) == 0)

    @pl.when(first_step)
    def _():
        count_s[...] = jnp.zeros(count_s.shape, F32)

    sub = upper_ref.shape[0]
    subs = range(tm // sub)
    rows = [slice(i * sub, (i + 1) * sub) for i in subs]
    heads = range(X_HEADS)
    cols = [slice(h * X_DIM, (h + 1) * X_DIM) for h in heads]
    x1 = [x_ref[r, :] + _dot(ydn_ref[r, :], wout_ref[0:DN_WIDTH, :])
          + _dot(ysb_ref[r, :], wout_ref[DN_WIDTH:, :]) for r in rows]
    n2 = [_rms(v, n2w_ref[...]).astype(BF16) for v in x1]
    q = [_dot(v, wq_ref[...]) for v in n2]
    qh = [[(_rms(q[i][:, c], qnw_ref[...]) * (X_DIM ** -0.5)).astype(BF16) for c in cols] for i in subs]
    sc = [[_dot_nt(qh[i][h], km_ref[:, cols[h]]) for h in heads] for i in subs]
    ex = [[jnp.exp(sc[i][h] - jnp.max(sc[i][h], axis=-1, keepdims=True)) for h in heads] for i in subs]
    pr = [[(ex[i][h] / jnp.sum(ex[i][h], axis=-1, keepdims=True)).astype(BF16) for h in heads] for i in subs]
    oh = [[_dot(pr[i][h], vm_ref[:, cols[h]]).astype(BF16) for h in heads] for i in subs]
    x2 = [x1[i] + _dot(jnp.concatenate(oh[i], axis=1), wo_ref[...]) for i in subs]
    h3 = [_rms(v, n3w_ref[...]) for v in x2]
    hi = [v.astype(BF16) for v in h3]
    lo = [(h3[i] - hi[i].astype(F32)).astype(BF16) for i in subs]
    logits = [_dot_nt(wrh_ref[...], hi[i]) + _dot_nt(wrh_ref[...], lo[i]) + _dot_nt(wrl_ref[...], hi[i])
              + rb_ref[...] for i in subs]
    eid = lax.broadcasted_iota(jnp.int32, (N_EXPERTS, sub), 0).astype(F32)
    count = count_s[...]
    for i in subs:
        x2_ref[rows[i], :] = x2[i]
        h3_ref[rows[i], :] = h3[i]
        vals, ids = [], []
        cur = logits[i]
        for _ in range(TOP_K):
            m = jnp.max(cur, axis=0, keepdims=True)
            j = jnp.min(jnp.where(cur == m, eid, float(N_EXPERTS)), axis=0, keepdims=True)
            vals.append(m)
            ids.append(j)
            cur = jnp.where(eid == j, -jnp.inf, cur)
        exps = [jnp.exp(v - vals[0]) for v in vals]
        denom = exps[0] + exps[1] + exps[2] + exps[3]
        onehot = jnp.zeros((N_EXPERTS, sub), F32)
        for j in ids:
            onehot = onehot + jnp.where(eid == j, 1.0, 0.0)
        before = count + _dot(onehot.astype(BF16), upper_ref[...])
        for k in range(TOP_K):
            idx_ref[k:k + 1, rows[i]] = ids[k].astype(jnp.int32)
            gate_ref[k:k + 1, rows[i]] = exps[k] / denom
            rank_ref[k:k + 1, rows[i]] = jnp.sum(jnp.where(eid == ids[k], before, 0.0), axis=0,
                                                 keepdims=True).astype(jnp.int32)
        count = count + jnp.sum(onehot, axis=1, keepdims=True)
    count_s[...] = count
    cnt_ref[...] = jnp.broadcast_to(count, cnt_ref.shape)


def _mid(x3, ydn3, ysb3, w_out, n2w, wq, qnw, k_mem, v_mem, wo, n3w, wr_hi, wr_lo, rb, upper, tm):
    b, s, d = x3.shape
    n = b * s
    nt = s // tm
    m = k_mem.shape[1]
    tok = lambda i, j: (0, i * nt + j)
    return pl.pallas_call(
        _mid_kernel,
        grid=(b, nt),
        in_specs=[
            pl.BlockSpec((None, tm, d), lambda i, j: (i, j, 0)),
            pl.BlockSpec((None, tm, DN_WIDTH), lambda i, j: (i, j, 0)),
            pl.BlockSpec((None, tm, SB_WIDTH), lambda i, j: (i, j, 0)),
            _const_spec(w_out.shape),
            _const_spec((1, d)),
            _const_spec(wq.shape),
            _const_spec((1, X_DIM)),
            pl.BlockSpec((None, m, d), lambda i, j: (i, 0, 0)),
            pl.BlockSpec((None, m, d), lambda i, j: (i, 0, 0)),
            _const_spec(wo.shape),
            _const_spec((1, d)),
            _const_spec(wr_hi.shape),
            _const_spec(wr_lo.shape),
            _const_spec((N_EXPERTS, 1)),
            _const_spec(upper.shape),
        ],
        out_specs=[
            pl.BlockSpec((None, tm, d), lambda i, j: (i, j, 0)),
            pl.BlockSpec((None, tm, d), lambda i, j: (i, j, 0)),
            pl.BlockSpec((TOP_K, tm), tok),
            pl.BlockSpec((TOP_K, tm), tok),
            pl.BlockSpec((TOP_K, tm), tok),
            _const_spec((N_EXPERTS, LANES)),
        ],
        out_shape=[
            jax.ShapeDtypeStruct((b, s, d), F32),
            jax.ShapeDtypeStruct((b, s, d), F32),
            jax.ShapeDtypeStruct((TOP_K, n), jnp.int32),
            jax.ShapeDtypeStruct((TOP_K, n), F32),
            jax.ShapeDtypeStruct((TOP_K, n), jnp.int32),
            jax.ShapeDtypeStruct((N_EXPERTS, LANES), F32),
        ],
        scratch_shapes=[pltpu.VMEM((N_EXPERTS, 1), F32)],
        compiler_params=pltpu.CompilerParams(dimension_semantics=("arbitrary", "arbitrary"),
                                             vmem_limit_bytes=V7X_EXPERTS_VMEM_LIMIT),
        name="mid",
    )(x3, ydn3, ysb3, w_out, n2w, wq, qnw, k_mem, v_mem, wo, n3w, wr_hi, wr_lo, rb, upper)


def _row_copy(src_ref, src_row, dst_ref, dst_row, sem):
    return pltpu.make_async_copy(src_ref.at[pl.ds(src_row, 1), :], dst_ref.at[pl.ds(dst_row, 1), :], sem)


def _dispatch_kernel(pend_ref, padded_ref, dest_ref, h_hbm, xpad_ref, zero_s, hbuf, sem, zsem, fsem):
    tf = hbuf.shape[1]
    i = pl.program_id(0)
    last = pl.num_programs(0) - 1

    def fetch(tile, slot):
        start = pl.multiple_of(tile * tf, tf)
        return pltpu.make_async_copy(h_hbm.at[pl.ds(start, tf), :], hbuf.at[slot], fsem.at[slot])

    def wait_rows(step):
        for _ in range(TOP_K):
            pltpu.make_async_copy(hbuf.at[0], xpad_ref.at[pl.ds(0, tf), :], sem.at[step % 2]).wait()

    @pl.when(i == 0)
    def _():
        fetch(0, 0).start()

        @pl.when(last >= 1)
        def _():
            fetch(1, 1).start()

        zero_s[...] = jnp.zeros(zero_s.shape, F32)

        def last_block(e):
            start = pl.multiple_of(pend_ref[e] - EXPERT_BLOCK, EXPERT_BLOCK)
            return pltpu.make_async_copy(zero_s, xpad_ref.at[pl.ds(start, EXPERT_BLOCK), :], zsem)

        def spare_block(t):
            start = pl.multiple_of(pend_ref[N_EXPERTS - 1] + t * EXPERT_BLOCK, EXPERT_BLOCK)
            return pltpu.make_async_copy(zero_s, xpad_ref.at[pl.ds(start, EXPERT_BLOCK), :], zsem)

        def spare_exists(t):
            return pend_ref[N_EXPERTS - 1] + (t + 1) * EXPERT_BLOCK <= xpad_ref.shape[0]

        for e in range(N_EXPERTS):
            @pl.when(padded_ref[e] > 0)
            def _(e=e):
                last_block(e).start()

            @pl.when(spare_exists(e))
            def _(e=e):
                spare_block(e).start()
        for e in range(N_EXPERTS):
            @pl.when(padded_ref[e] > 0)
            def _(e=e):
                last_block(e).wait()

            @pl.when(spare_exists(e))
            def _(e=e):
                spare_block(e).wait()

    slot = i % 3
    fetch(i, slot).wait()
    src = hbuf.at[slot]
    for k in range(TOP_K):
        for t in range(tf):
            _row_copy(src, t, xpad_ref, dest_ref[0, k * tf + t], sem.at[i % 2]).start(priority=t % 2)

    @pl.when(i >= 1)
    def _():
        wait_rows(i - 1)

    @pl.when(i + 2 <= last)
    def _():
        fetch(i + 2, (i + 2) % 3).start()

    @pl.when(i == last)
    def _():
        wait_rows(i)


def _dispatch(pad_ends, padded, dest_tiles, h2d, rows, tf):
    n, d = h2d.shape
    return pl.pallas_call(
        _dispatch_kernel,
        grid_spec=pltpu.PrefetchScalarGridSpec(
            num_scalar_prefetch=2,
            grid=(n // tf,),
            in_specs=[
                pl.BlockSpec((None, 1, TOP_K * tf), lambda i, *_: (i, 0, 0), memory_space=pltpu.SMEM),
                pl.BlockSpec(memory_space=pl.ANY),
            ],
            out_specs=pl.BlockSpec(memory_space=pl.ANY),
            scratch_shapes=[pltpu.VMEM((EXPERT_BLOCK, d), F32), pltpu.VMEM((3, tf, d), F32),
                            pltpu.SemaphoreType.DMA((2,)), pltpu.SemaphoreType.DMA(()),
                            pltpu.SemaphoreType.DMA((3,))],
        ),
        out_shape=jax.ShapeDtypeStruct((rows, d), F32),
        compiler_params=_params("arbitrary"),
        name="dispatch",
    )(pad_ends, padded, dest_tiles, h2d)


def _experts_kernel(be_ref, nb_ref, first_ref, ord_ref, nxt_ref, x_ref, wg_hbm, bg_ref, wu_hbm, bu_ref,
                    wd_hbm, bd_ref, y_ref, wf_s, wb_s, wsem):
    j = pl.program_id(0)
    used = j < nb_ref[0]
    slot = ord_ref[j] % 2

    def fetch(e, s):
        return [pltpu.make_async_copy(w.at[e], wf_s.at[s, i], wsem.at[s])
                for i, w in enumerate((wg_hbm, wu_hbm, wd_hbm))]

    @pl.when(j == 0)
    def _():
        for c in fetch(be_ref[0], 0):
            c.start()

    @pl.when(jnp.logical_and(used, first_ref[j] == 1))
    def _():
        for c in fetch(be_ref[j], slot):
            c.wait()

        @pl.when(nxt_ref[j] >= 0)
        def _():
            for c in fetch(nxt_ref[j], 1 - slot):
                c.start()

        for i in range(3):
            wb_s[i] = wf_s[slot, i].astype(BF16)

    @pl.when(used)
    def _():
        x = x_ref[...].astype(BF16)
        gate = jnp.minimum(_dot(x, wb_s[0]) + bg_ref[...], SWIGLU_LIMIT)
        up = jnp.clip(_dot(x, wb_s[1]) + bu_ref[...], -SWIGLU_LIMIT, SWIGLU_LIMIT)
        act = (up + 1.0) * gate * _sigmoid(gate * SWIGLU_ALPHA)
        y_ref[...] = _dot(act.astype(BF16), wb_s[2]) + bd_ref[...]

    @pl.when(jnp.logical_not(used))
    def _():
        y_ref[...] = jnp.zeros(y_ref.shape, F32)


def _experts(block_e, nb_used, first, ordinal, nxt, x_pad, wg, bg, wu, bu, wd, bd):
    rows, d = x_pad.shape
    nblk = rows // EXPERT_BLOCK
    dff = wg.shape[2]
    assert d == dff, "the staging buffers hold all three weight matrices in one (d, dff) shape"
    row_blk = lambda j, be, nb, *_: (jnp.minimum(j, nb[0] - 1), 0)
    b_blk = lambda j, be, *_: (be[j], 0, 0)
    hbm = pl.BlockSpec(memory_space=pl.ANY)
    return pl.pallas_call(
        _experts_kernel,
        grid_spec=pltpu.PrefetchScalarGridSpec(
            num_scalar_prefetch=5,
            grid=(nblk,),
            in_specs=[
                pl.BlockSpec((EXPERT_BLOCK, d), row_blk),
                hbm,
                pl.BlockSpec((None, 1, dff), b_blk),
                hbm,
                pl.BlockSpec((None, 1, dff), b_blk),
                hbm,
                pl.BlockSpec((None, 1, d), b_blk),
            ],
            out_specs=pl.BlockSpec((EXPERT_BLOCK, d), lambda j, *_: (j, 0)),
            scratch_shapes=[pltpu.VMEM((2, 3, d, dff), F32), pltpu.VMEM((3, d, dff), BF16),
                            pltpu.SemaphoreType.DMA((2,))],
        ),
        out_shape=jax.ShapeDtypeStruct((rows, d), F32),
        compiler_params=pltpu.CompilerParams(dimension_semantics=("arbitrary",),
                                             vmem_limit_bytes=V7X_EXPERTS_VMEM_LIMIT),
        name="experts",
    )(block_e, nb_used, first, ordinal, nxt, x_pad, wg, bg, wu, bu, wd, bd)


def _combine_kernel(dest_ref, dnext_ref, x2_ref, gate_ref, ypad_ref, o_ref, buf, sem):
    th = x2_ref.shape[0]
    i = pl.program_id(0)
    slot = i % 2

    def gather(dref, s):
        for k in range(TOP_K):
            for t in range(th):
                _row_copy(ypad_ref, dref[0, k * th + t], buf.at[s, k], t, sem.at[s]).start(priority=t % 2)

    @pl.when(i == 0)
    def _():
        gather(dest_ref, 0)

    @pl.when(i + 1 < pl.num_programs(0))
    def _():
        gather(dnext_ref, 1 - slot)

    for k in range(TOP_K):
        pltpu.make_async_copy(ypad_ref.at[pl.ds(0, th), :], buf.at[slot, k], sem.at[slot]).wait()
    g = gate_ref[...]
    out = x2_ref[...]
    for k in range(TOP_K):
        out = out + buf[slot, k] * g[:, k:k + 1]
    o_ref[...] = out


def _combine(dest_tiles, x2d, gates_nk, y_pad, th):
    n, d = x2d.shape
    nt = n // th
    return pl.pallas_call(
        _combine_kernel,
        grid=(nt,),
        in_specs=[
            pl.BlockSpec((None, 1, TOP_K * th), lambda i: (i, 0, 0), memory_space=pltpu.SMEM),
            pl.BlockSpec((None, 1, TOP_K * th), lambda i: (jnp.minimum(i + 1, nt - 1), 0, 0),
                         memory_space=pltpu.SMEM),
            pl.BlockSpec((th, d), lambda i: (i, 0)),
            pl.BlockSpec((th, TOP_K), lambda i: (i, 0)),
            pl.BlockSpec(memory_space=pl.ANY),
        ],
        out_specs=pl.BlockSpec((th, d), lambda i: (i, 0)),
        out_shape=jax.ShapeDtypeStruct((n, d), F32),
        scratch_shapes=[pltpu.VMEM((2, TOP_K, th, d), F32), pltpu.SemaphoreType.DMA((2,))],
        compiler_params=_params("arbitrary"),
        name="combine",
    )(dest_tiles, dest_tiles, x2d, gates_nk, y_pad)


def _tile_dest(dest, t):
    k, n = dest.shape
    return dest.reshape(k, n // t, t).transpose(1, 0, 2).reshape(n // t, 1, k * t)


def _layer(x, mem, norm1_w, w_in, conv_w, a_log, dt_bias, dn_norm_w, sb_q_norm_w, sb_k_norm_w, w_out,
           norm2_w, mem_norm_w, xq_w, xk_w, xv_w, xq_norm_w, xk_norm_w, xo_w, norm3_w, router_w,
           router_b, w_gate, b_gate, w_up, b_up, w_down, b_down):
    b, s, d = x.shape
    n = b * s
    tm_proj = min(512, s)
    tm_mid = min(1024, s)
    sub_mid = min(256, s)
    t_moe = min(256, s)

    o_dn, o_ab, o_sb = 4 * DN_WIDTH, 4 * DN_WIDTH + 2 * DN_HEADS, 4 * DN_WIDTH + 2 * DN_HEADS
    w_dn = w_in[:, :o_dn].astype(BF16)
    w_ab_f = w_in[:, o_dn:o_ab]
    w_ab = jnp.pad(w_ab_f, ((0, 0), (0, LANES - 2 * DN_HEADS))).astype(BF16)
    w_abt = w_ab_f.T.astype(BF16)
    w_sb = w_in[:, o_sb:].astype(BF16)
    row = lambda v: v.reshape(1, -1).astype(F32)

    dn, sb, ab, abt = _in_proj(x.reshape(n, d), row(norm1_w), w_dn, w_sb, w_ab, w_abt, tm_proj)

    abt4 = abt.reshape(8, n // CHUNK, CHUNK).transpose(1, 0, 2).reshape(b, s // CHUNK, 8, CHUNK)
    pad_lane = lambda v: jnp.pad(v.astype(F32), (0, LANES - v.shape[0])).reshape(1, LANES)
    pad_col = lambda v: jnp.pad(v.astype(F32), (0, 8 - v.shape[0])).reshape(8, 1)
    tap = jnp.arange(CONV_WIDTH * CHUNK)
    shift = (jnp.arange(2 * CHUNK)[None, :] == (CHUNK - CONV_WIDTH + 1 + tap // CHUNK + tap % CHUNK)[:, None])
    y_dn = _deltanet(dn.reshape(b, s, -1), ab.reshape(b, s, LANES), abt4, conv_w.astype(F32),
                     shift.astype(BF16), pad_lane(a_log), pad_lane(dt_bias), pad_col(a_log), pad_col(dt_bias),
                     row(dn_norm_w))

    ii = jnp.arange(SB_BLOCK)
    m2 = -jnp.concatenate([(ii[:, None] > ii[None, :]).astype(BF16),
                           jnp.ones((SB_BLOCK, SB_BLOCK), BF16)], axis=1)
    sb_heads = SB_LANES // SB_DIM
    hh = jnp.arange(SB_LANES) // SB_DIM
    hsum = (hh[:, None] == hh[None, :]).astype(BF16)
    y_sb = _stickbreak(sb.reshape(b, s, -1), row(jnp.tile(sb_q_norm_w, sb_heads)),
                       row(jnp.tile(sb_k_norm_w, sb_heads)), m2, hsum)

    k_mem, v_mem = _memkv(mem, row(mem_norm_w), xk_w.astype(BF16), xv_w.astype(BF16), row(xk_norm_w))

    wr_t = router_w.T.astype(F32)
    wr_hi = wr_t.astype(BF16)
    wr_lo = (wr_t - wr_hi.astype(F32)).astype(BF16)
    jj = jnp.arange(sub_mid)
    upper = (jj[:, None] < jj[None, :]).astype(BF16)
    x2, h3, idx, gates, rank, cnt = _mid(
        x, y_dn, y_sb, w_out.astype(BF16), row(norm2_w), xq_w.astype(BF16), row(xq_norm_w), k_mem, v_mem,
        xo_w.astype(BF16), row(norm3_w), wr_hi, wr_lo, router_b.reshape(N_EXPERTS, 1).astype(F32), upper,
        tm_mid)

    counts = cnt[:, 0].astype(jnp.int32)
    padded = (counts + EXPERT_BLOCK - 1) // EXPERT_BLOCK * EXPERT_BLOCK
    pad_ends = jnp.cumsum(padded)
    pad_starts = pad_ends - padded
    sel = idx[:, :, None] == jnp.arange(N_EXPERTS, dtype=jnp.int32)[None, None, :]
    dest = rank + jnp.sum(jnp.where(sel, pad_starts[None, None, :], 0), axis=-1)
    n_blocks = -(-n * TOP_K // EXPERT_BLOCK) + N_EXPERTS
    nb_used = (pad_ends[-1] // EXPERT_BLOCK).astype(jnp.int32)
    blk = jnp.minimum(jnp.arange(n_blocks, dtype=jnp.int32), nb_used - 1) * EXPERT_BLOCK
    block_e = jnp.minimum(jnp.sum(pad_ends[None, :] <= blk[:, None], axis=1), N_EXPERTS - 1).astype(jnp.int32)
    dest_tiles = _tile_dest(dest, t_moe)

    x_pad = _dispatch(pad_ends.astype(jnp.int32), padded.astype(jnp.int32), dest_tiles, h3.reshape(n, d),
                      n_blocks * EXPERT_BLOCK, t_moe)
    bias = lambda v: v.reshape(N_EXPERTS, 1, -1).astype(F32)
    jb = jnp.arange(n_blocks, dtype=jnp.int32)
    first = jnp.logical_and(jb < nb_used, jnp.logical_or(jb == 0, block_e != jnp.roll(block_e, 1)))
    ordinal = (jnp.cumsum(first) - 1).astype(jnp.int32)
    later_first = jnp.roll(jnp.where(first, jb, n_blocks), -1).at[-1].set(n_blocks)
    nxt_pos = lax.cummin(later_first, axis=0, reverse=True)
    nxt = jnp.where(nxt_pos < n_blocks, block_e[jnp.minimum(nxt_pos, n_blocks - 1)], -1).astype(jnp.int32)
    y_pad = _experts(block_e, nb_used.reshape(1), first.astype(jnp.int32), ordinal, nxt, x_pad,
                     w_gate.astype(F32), bias(b_gate), w_up.astype(F32), bias(b_up), w_down.astype(F32),
                     bias(b_down))
    out = _combine(dest_tiles, x2.reshape(n, d), gates.T, y_pad, t_moe)
    return out.reshape(b, s, d)


def kernel(x, mem, norm1_w, w_in, conv_w, a_log, dt_bias, dn_norm_w, sb_q_norm_w, sb_k_norm_w, w_out,
           norm2_w, mem_norm_w, xq_w, xk_w, xv_w, xq_norm_w, xk_norm_w, xo_w, norm3_w, router_w,
           router_b, w_gate, b_gate, w_up, b_up, w_down, b_down):
    depth = w_in.shape[0]
    for l in range(depth):
        x = _layer(x, mem, norm1_w[l], w_in[l], conv_w[l], a_log[l], dt_bias[l], dn_norm_w[l],
                   sb_q_norm_w[l], sb_k_norm_w[l], w_out[l], norm2_w[l], mem_norm_w[l], xq_w[l], xk_w[l],
                   xv_w[l], xq_norm_w[l], xk_norm_w[l], xo_w[l], norm3_w[l], router_w[l], router_b[l],
                   w_gate[l], b_gate[l], w_up[l], b_up[l], w_down[l], b_down[l])
    return x
```

```python
import jax
import jax.numpy as jnp
from jax import lax
from jax.experimental import pallas as pl
from jax.experimental.pallas import tpu as pltpu
from jax.experimental.pallas import tpu_sc as plsc

F32 = jnp.float32
BF16 = jnp.bfloat16

EPS = 1e-6
CHUNK = 64
DN_HEADS = 4
DN_DIM = 128
DN_WIDTH = DN_HEADS * DN_DIM
CONV_WIDTH = 4
DN_PAIR = 4
SB_HEADS = 8
SB_DIM = 64
SB_WIDTH = SB_HEADS * SB_DIM
SB_BLOCK = 128
SB_GROUP = 4
SB_LANES = 256
X_HEADS = 4
X_DIM = 256
N_EXPERTS = 32
TOP_K = 4
EXPERT_BLOCK = 512
SWIGLU_ALPHA = 1.702
SWIGLU_LIMIT = 7.0
LANES = 128
LOG2E = 1.4426950408889634
SB_UNDERFLOW_LOG2 = -127.0
V7X_VMEM_LIMIT = 48 * 1024 * 1024
V7X_EXPERTS_VMEM_LIMIT = 58 * 1024 * 1024


def _dot(a, b):
    return jnp.dot(a, b, preferred_element_type=F32)


def _dot_nt(a, b):
    return lax.dot_general(a, b, (((1,), (1,)), ((), ())), preferred_element_type=F32)


def _dot_tn(a, b):
    return lax.dot_general(a, b, (((0,), (0,)), ((), ())), preferred_element_type=F32)


def _split3(x):
    hi = x.astype(BF16)
    r = x - hi.astype(F32)
    mid = r.astype(BF16)
    return hi, mid, (r - mid.astype(F32)).astype(BF16)


def _softplus(x):
    return jnp.maximum(x, 0.0) + jnp.log(1.0 + jnp.exp(-jnp.abs(x)))


def _sigmoid(x):
    return 1.0 / (1.0 + jnp.exp(-x))


def _rms(x, w):
    return x * lax.rsqrt(jnp.mean(x * x, axis=-1, keepdims=True) + EPS) * w


def _params(*sem):
    return pltpu.CompilerParams(dimension_semantics=sem, vmem_limit_bytes=V7X_VMEM_LIMIT)


def _const_spec(shape):
    nd = len(shape)
    return pl.BlockSpec(shape, lambda *_: (0,) * nd)


def _in_proj_kernel(x_ref, nw_ref, wdn_ref, wsb_ref, wab_ref, wabt_ref,
                    dn_ref, sb_ref, ab_ref, abt_ref):
    n = _rms(x_ref[...], nw_ref[...]).astype(BF16)
    dn_ref[...] = _dot(n, wdn_ref[...]).astype(BF16)
    sb_ref[...] = _dot(n, wsb_ref[...]).astype(BF16)
    ab_ref[...] = _dot(n, wab_ref[...])
    abt_ref[...] = _dot_nt(wabt_ref[...], n)


def _in_proj(x2d, norm_w, w_dn, w_sb, w_ab, w_abt, tm):
    n, d = x2d.shape
    return pl.pallas_call(
        _in_proj_kernel,
        grid=(n // tm,),
        in_specs=[
            pl.BlockSpec((tm, d), lambda i: (i, 0)),
            _const_spec((1, d)),
            _const_spec(w_dn.shape),
            _const_spec(w_sb.shape),
            _const_spec(w_ab.shape),
            _const_spec(w_abt.shape),
        ],
        out_specs=[
            pl.BlockSpec((tm, w_dn.shape[1]), lambda i: (i, 0)),
            pl.BlockSpec((tm, w_sb.shape[1]), lambda i: (i, 0)),
            pl.BlockSpec((tm, LANES), lambda i: (i, 0)),
            pl.BlockSpec((8, tm), lambda i: (0, i)),
        ],
        out_shape=[
            jax.ShapeDtypeStruct((n, w_dn.shape[1]), BF16),
            jax.ShapeDtypeStruct((n, w_sb.shape[1]), BF16),
            jax.ShapeDtypeStruct((n, LANES), F32),
            jax.ShapeDtypeStruct((8, n), F32),
        ],
        compiler_params=_params("arbitrary"),
        name="in_proj",
    )(x2d, norm_w, w_dn, w_sb, w_ab, w_abt)


def _deltanet_kernel(dn_ref, ab_ref, abt_ref, convw_ref, shift_ref, alog_ref, dtb_ref, alogt_ref, dtbt_ref,
                     normw_ref, o_ref, state_s, u_s, w_s, kt_s, qg_s, attn_s, egl_s):
    s = dn_ref.shape[0]
    steps = s // (CHUNK * DN_PAIR)
    state_s[...] = jnp.zeros(state_s.shape, F32)

    def conv_silu(r0, c, part, h, l2):
        cs = slice(part * DN_WIDTH + h * LANES, part * DN_WIDTH + (h + 1) * LANES)
        prev0 = pl.multiple_of(jnp.maximum(r0 - CHUNK, 0), CHUNK)
        prev = dn_ref[pl.ds(prev0, CHUNK), cs]
        prev = jnp.where(c > 0, prev, jnp.zeros_like(prev))
        win = jnp.concatenate([prev, dn_ref[pl.ds(r0, CHUNK), cs]], axis=0)
        sh = _dot(shift_ref[...], win)
        w = convw_ref[:, cs]
        y = w[0:1, :] * sh[0:CHUNK, :]
        for i in range(1, CONV_WIDTH):
            y = y + w[i:i + 1, :] * sh[i * CHUNK:(i + 1) * CHUNK, :]
        y = y * _sigmoid(y)
        if l2:
            y = y * lax.rsqrt(jnp.sum(y * y, axis=-1, keepdims=True) + EPS)
        return y

    row = lax.broadcasted_iota(jnp.int32, (CHUNK, CHUNK), 0)
    col = lax.broadcasted_iota(jnp.int32, (CHUNK, CHUNK), 1)
    tri = row >= col
    strict = row > col
    tril16 = jnp.where(tri, 1.0, 0.0).astype(BF16)
    triu16 = jnp.where(row <= col, 1.0, 0.0).astype(BF16)
    neg_a_col = -jnp.exp(alog_ref[...])
    neg_a_row = -jnp.exp(alogt_ref[...])
    scale = DN_DIM ** -0.5
    heads = range(DN_HEADS)
    cols = [slice(h * LANES, (h + 1) * LANES) for h in heads]

    def pre_items(it):
        items = []
        for ci in range(DN_PAIR):
            c = it * DN_PAIR + ci
            r0 = pl.multiple_of(c * CHUNK, CHUNK)
            ab = ab_ref[pl.ds(r0, CHUNK), :]
            g_col = neg_a_col * _softplus(ab + dtb_ref[...])
            gc_col_all = sum(_dot(tril16, p) for p in _split3(g_col))
            beta_all = _sigmoid(ab)
            abt = abt_ref[c]
            g_row = neg_a_row * _softplus(abt + dtbt_ref[...])
            gc_row_all = sum(_dot(p, triu16) for p in _split3(g_row))
            for h in heads:
                q = conv_silu(r0, c, 0, h, True) * scale
                k = conv_silu(r0, c, 1, h, True)
                v = conv_silu(r0, c, 2, h, False)
                gcol = gc_col_all[:, h:h + 1]
                grow = gc_row_all[h:h + 1, :]
                beta = beta_all[:, DN_HEADS + h:DN_HEADS + h + 1]
                glast = gcol[CHUNK - 1:CHUNK, :]
                decay = jnp.where(tri, jnp.exp(jnp.where(tri, gcol - grow, 0.0)), 0.0)
                kb = k * beta
                kt_s[pl.ds(r0, CHUNK), cols[h]] = (k * jnp.exp(glast - gcol)).astype(BF16)
                qg_s[pl.ds(r0, CHUNK), cols[h]] = (q * jnp.exp(gcol)).astype(BF16)
                egl_s[c, h:h + 1, :] = jnp.broadcast_to(jnp.exp(glast), (1, LANES))
                items.append(dict(c=c, r0=r0, h=h, decay=decay, kb16=kb.astype(BF16),
                                  k16=k.astype(BF16), q16=q.astype(BF16),
                                  sol=jnp.concatenate([v * beta, kb * jnp.exp(gcol)], axis=1)))
        return items

    def pre_gram(items):
        kk = [_dot_nt(t["kb16"], t["k16"]) for t in items]
        qk = [_dot_nt(t["q16"], t["k16"]) for t in items]
        for t, kk_i, qk_i in zip(items, kk, qk):
            attn_s[t["c"], t["h"]] = jnp.where(tri, qk_i * t["decay"], 0.0).astype(BF16)
            t["tm"] = -jnp.where(strict, kk_i * t["decay"], 0.0)
            t["p16"] = t["tm"].astype(BF16)
        return [_dot(t["p16"], t["p16"]) for t in items]

    def pre_level(items, sq, last):
        for i, t in enumerate(items):
            t["pw"] = sq[i]
            t["p16"] = sq[i].astype(BF16)
        app = [_dot(t["p16"], t["tm"].astype(BF16)) for t in items]
        nxt = None if last else [_dot(t["p16"], t["p16"]) for t in items]
        for i, t in enumerate(items):
            t["tm"] = t["tm"] + t["pw"] + app[i]
        return nxt

    def pre_solve(items):
        corr = [_dot(t["tm"].astype(BF16), t["sol"].astype(BF16)) for t in items]
        for i, t in enumerate(items):
            sol = t["sol"] + corr[i]
            u_s[pl.ds(t["r0"], CHUNK), cols[t["h"]]] = sol[:, :DN_DIM]
            w_s[pl.ds(t["r0"], CHUNK), cols[t["h"]]] = sol[:, DN_DIM:].astype(BF16)

    def rec_read(c):
        r0 = pl.multiple_of(c * CHUNK, CHUNK)
        st = [state_s[h] for h in heads]
        st16 = [x.astype(BF16) for x in st]
        ws = [_dot(w_s[pl.ds(r0, CHUNK), cols[h]], st16[h]) for h in heads]
        qs = [_dot(qg_s[pl.ds(r0, CHUNK), cols[h]], st16[h]) for h in heads]
        return dict(c=c, r0=r0, st=st, ws=ws, qs=qs)

    def rec_update(rd):
        c, r0 = rd["c"], rd["r0"]
        vn16 = [(u_s[pl.ds(r0, CHUNK), cols[h]] - rd["ws"][h]).astype(BF16) for h in heads]
        av = [_dot(attn_s[c, h], vn16[h]) for h in heads]
        ks = [_dot_tn(kt_s[pl.ds(r0, CHUNK), cols[h]], vn16[h]) for h in heads]
        for h in heads:
            state_s[h] = rd["st"][h] * egl_s[c, h:h + 1, :] + ks[h]
            o = rd["qs"][h] + av[h]
            o = o * lax.rsqrt(jnp.mean(o * o, axis=-1, keepdims=True) + EPS)
            z = dn_ref[pl.ds(r0, CHUNK), 3 * DN_WIDTH + h * LANES:3 * DN_WIDTH + (h + 1) * LANES].astype(F32)
            o_ref[pl.ds(r0, CHUNK), cols[h]] = (o * normw_ref[...] * (z * _sigmoid(z))).astype(BF16)

    def step(it, with_pre, with_rec):
        hooks = []
        for ci in range(DN_PAIR if with_rec else 0):
            hooks += [("read", ci), ("update", ci)]
        pending = {}

        def run_hook():
            if hooks:
                kind, ci = hooks.pop(0)
                if kind == "read":
                    pending[ci] = rec_read((it - 1) * DN_PAIR + ci)
                else:
                    rec_update(pending.pop(ci))

        if not with_pre:
            while hooks:
                run_hook()
            return
        slots = [(i * 7) // max(len(hooks), 1) for i in range(len(hooks))]

        def boundary(b):
            for _ in range(slots.count(b)):
                run_hook()

        items = pre_items(it)
        boundary(0)
        sq = pre_gram(items)
        boundary(1)
        for level in range(1, 6):
            sq = pre_level(items, sq, level == 5)
            boundary(level + 1)
        pre_solve(items)
        while hooks:
            run_hook()

    step(0, True, False)

    def fused(it, carry):
        step(it, True, True)
        return carry

    lax.fori_loop(1, steps, fused, 0)
    step(steps, False, True)


def _deltanet(dn3, ab3, abt4, conv_w, shift, alog, dtb, alogt, dtbt, norm_w):
    b, s, _ = dn3.shape
    nch = s // CHUNK
    return pl.pallas_call(
        _deltanet_kernel,
        grid=(b,),
        in_specs=[
            pl.BlockSpec((None, s, 4 * DN_WIDTH), lambda i: (i, 0, 0)),
            pl.BlockSpec((None, s, LANES), lambda i: (i, 0, 0)),
            pl.BlockSpec((None, nch, 8, CHUNK), lambda i: (i, 0, 0, 0)),
            _const_spec(conv_w.shape),
            _const_spec(shift.shape),
            _const_spec((1, LANES)),
            _const_spec((1, LANES)),
            _const_spec((8, 1)),
            _const_spec((8, 1)),
            _const_spec((1, DN_DIM)),
        ],
        out_specs=pl.BlockSpec((None, s, DN_WIDTH), lambda i: (i, 0, 0)),
        out_shape=jax.ShapeDtypeStruct((b, s, DN_WIDTH), BF16),
        scratch_shapes=[
            pltpu.VMEM((DN_HEADS, DN_DIM, DN_DIM), F32),
            pltpu.VMEM((s, DN_WIDTH), F32),
            pltpu.VMEM((s, DN_WIDTH), BF16),
            pltpu.VMEM((s, DN_WIDTH), BF16),
            pltpu.VMEM((s, DN_WIDTH), BF16),
            pltpu.VMEM((nch, DN_HEADS, CHUNK, CHUNK), BF16),
            pltpu.VMEM((nch, 8, LANES), F32),
        ],
        compiler_params=_params("arbitrary"),
        name="deltanet",
    )(dn3, ab3, abt4, conv_w, shift, alog, dtb, alogt, dtbt, norm_w)


def _stickbreak_kernel(q_ref, k_ref, v_ref, qw_ref, kw_ref, m2_ref, hsum_ref, o_ref,
                       qn_s, kn_s, vm_s, carry_s, acc_s):
    s, width = q_ref.shape
    nh = width // SB_DIM
    group = carry_s.shape[0]
    lane = lax.broadcasted_iota(jnp.int32, (1, width), 1)
    head_lanes = [jnp.logical_and(lane >= h * SB_DIM, lane < (h + 1) * SB_DIM) for h in range(nh)]

    def head_norm(r0, x_ref, w):
        x = x_ref[pl.ds(r0, SB_BLOCK), :].astype(F32)
        sq = x * x
        hi = sq.astype(BF16)
        lo = (sq - hi.astype(F32)).astype(BF16)
        ms = (_dot(hi, hsum_ref[...]) + _dot(lo, hsum_ref[...])) * (1.0 / SB_DIM)
        return x * lax.rsqrt(ms + EPS) * w

    def norm_block(i, carry):
        r0 = pl.multiple_of(i * SB_BLOCK, SB_BLOCK)
        qn = head_norm(r0, q_ref, qw_ref[...]) * (SB_DIM ** -0.5 * LOG2E)
        v = v_ref[pl.ds(r0, SB_BLOCK), :]
        for h in range(nh):
            qn_s[h, pl.ds(r0, SB_BLOCK), :] = jnp.where(head_lanes[h], qn, 0.0).astype(BF16)
            vm_s[i, h * SB_BLOCK:(h + 1) * SB_BLOCK, :] = jnp.where(head_lanes[h], v, jnp.zeros_like(v))
        kn_s[pl.ds(r0, SB_BLOCK), :] = head_norm(r0, k_ref, kw_ref[...]).astype(BF16)
        return carry

    lax.fori_loop(0, s // SB_BLOCK, norm_block, 0, unroll=min(4, s // SB_BLOCK))

    row = lax.broadcasted_iota(jnp.int32, (SB_BLOCK, SB_BLOCK), 0)
    col = lax.broadcasted_iota(jnp.int32, (SB_BLOCK, SB_BLOCK), 1)
    causal = col < row

    def step(rows, key_blocks, valid):
        tiles = [(g, h) for g in range(group) for h in range(nh)]
        k16 = [kn_s[pl.ds(pl.multiple_of(kb * SB_BLOCK, SB_BLOCK), SB_BLOCK), :] for kb in key_blocks]
        z = [_dot_nt(qn_s[h, pl.ds(rows[g], SB_BLOCK), :], k16[g]) for g, h in tiles]
        sp = [jnp.maximum(x, 0.0) + jnp.log(1.0 + jnp.exp2(-jnp.abs(x))) * LOG2E for x in z]
        fail = [jnp.where(causal, x, 0.0) for x in sp] if valid is None else sp
        cs = [_dot(x.astype(BF16), m2_ref[...]) for x in fail]
        p = []
        for i, (g, h) in enumerate(tiles):
            w = jnp.exp2(z[i] - sp[i] + carry_s[g, h] + cs[i][:, :SB_BLOCK])
            p.append((jnp.where(causal, w, 0.0) if valid is None else w).astype(BF16))
        pv = [_dot(jnp.concatenate(p[g * nh:(g + 1) * nh], axis=1), vm_s[key_blocks[g]])
              for g in range(group)]
        for g in range(group):
            acc = acc_s[g] + pv[g]
            acc_s[g] = acc if valid is None else jnp.where(valid[g], acc, acc_s[g])
        for i, (g, h) in enumerate(tiles):
            carry = carry_s[g, h] + cs[i][:, SB_BLOCK:]
            carry_s[g, h] = carry if valid is None else jnp.where(valid[g], carry, carry_s[g, h])

    def q_group(qg, carry):
        blocks = [qg * group + g for g in range(group)]
        rows = [pl.multiple_of(qb * SB_BLOCK, SB_BLOCK) for qb in blocks]
        carry_s[...] = jnp.zeros(carry_s.shape, F32)
        acc_s[...] = jnp.zeros(acc_s.shape, F32)
        step(rows, blocks, None)

        def cond(st):
            d, alive = st
            return jnp.logical_and(d <= blocks[-1], alive)

        def body(st):
            d, _ = st
            step(rows, [jnp.maximum(qb - d, 0) for qb in blocks], [qb >= d for qb in blocks])
            return d + 1, jnp.max(carry_s[...]) > SB_UNDERFLOW_LOG2

        lax.while_loop(cond, body, (jnp.int32(1), jnp.bool_(True)))
        for g in range(group):
            o_ref[pl.ds(rows[g], SB_BLOCK), :] = acc_s[g].astype(BF16)
        return carry

    lax.fori_loop(0, s // (SB_BLOCK * group), q_group, 0)


def _stickbreak(sb3, qw, kw, m2, hsum):
    b, s, _ = sb3.shape
    width = hsum.shape[0]
    nh = width // SB_DIM
    parts = SB_WIDTH // width
    group = min(SB_GROUP, s // SB_BLOCK)
    return pl.pallas_call(
        _stickbreak_kernel,
        grid=(b, parts),
        in_specs=[
            pl.BlockSpec((None, s, width), lambda i, j: (i, 0, j)),
            pl.BlockSpec((None, s, width), lambda i, j: (i, 0, parts + j)),
            pl.BlockSpec((None, s, width), lambda i, j: (i, 0, 2 * parts + j)),
            _const_spec((1, width)),
            _const_spec((1, width)),
            _const_spec(m2.shape),
            _const_spec(hsum.shape),
        ],
        out_specs=pl.BlockSpec((None, s, width), lambda i, j: (i, 0, j)),
        out_shape=jax.ShapeDtypeStruct((b, s, SB_WIDTH), BF16),
        scratch_shapes=[
            pltpu.VMEM((nh, s, width), BF16),
            pltpu.VMEM((s, width), BF16),
            pltpu.VMEM((s // SB_BLOCK, nh * SB_BLOCK, width), BF16),
            pltpu.VMEM((group, nh, SB_BLOCK, SB_BLOCK), F32),
            pltpu.VMEM((group, SB_BLOCK, width), F32),
        ],
        compiler_params=_params("arbitrary", "arbitrary"),
        name="stickbreak",
    )(sb3, sb3, sb3, qw, kw, m2, hsum)


def _memkv_kernel(mem_ref, nw_ref, wk_ref, wv_ref, knw_ref, k_ref, v_ref):
    n = _rms(mem_ref[...], nw_ref[...]).astype(BF16)
    k = _dot(n, wk_ref[...])
    for h in range(X_HEADS):
        cs = slice(h * X_DIM, (h + 1) * X_DIM)
        k_ref[:, cs] = _rms(k[:, cs], knw_ref[...]).astype(BF16)
    v_ref[...] = _dot(n, wv_ref[...]).astype(BF16)


def _memkv(mem, norm_w, wk, wv, k_norm_w):
    b, m, d = mem.shape
    return pl.pallas_call(
        _memkv_kernel,
        grid=(b,),
        in_specs=[
            pl.BlockSpec((None, m, d), lambda i: (i, 0, 0)),
            _const_spec((1, d)),
            _const_spec(wk.shape),
            _const_spec(wv.shape),
            _const_spec((1, X_DIM)),
        ],
        out_specs=[
            pl.BlockSpec((None, m, d), lambda i: (i, 0, 0)),
            pl.BlockSpec((None, m, d), lambda i: (i, 0, 0)),
        ],
        out_shape=[jax.ShapeDtypeStruct((b, m, d), BF16)] * 2,
        compiler_params=_params("arbitrary"),
        name="memkv",
    )(mem, norm_w, wk, wv, k_norm_w)


def _mid_kernel(x_ref, ydn_ref, ysb_ref, wout_ref, n2w_ref, wq_ref, qnw_ref, km_ref, vm_ref, wo_ref,
                n3w_ref, wrh_ref, wrl_ref, rb_ref, upper_ref,
                x2_ref, h3_ref, idx_ref, gate_ref, rank_ref, cnt_ref, count_s):
    tm = x_ref.shape[0]
    first_step = jnp.logical_and(pl.program_id(0) == 0, pl.program_id(1) == 0)

    @pl.when(first_step)
    def _():
        count_s[...] = jnp.zeros(count_s.shape, F32)

    sub = upper_ref.shape[0]
    subs = range(tm // sub)
    rows = [slice(i * sub, (i + 1) * sub) for i in subs]
    heads = range(X_HEADS)
    cols = [slice(h * X_DIM, (h + 1) * X_DIM) for h in heads]
    x1 = [x_ref[r, :] + _dot(ydn_ref[r, :], wout_ref[0:DN_WIDTH, :])
          + _dot(ysb_ref[r, :], wout_ref[DN_WIDTH:, :]) for r in rows]
    n2 = [_rms(v, n2w_ref[...]).astype(BF16) for v in x1]
    q = [_dot(v, wq_ref[...]) for v in n2]
    qh = [[(_rms(q[i][:, c], qnw_ref[...]) * (X_DIM ** -0.5)).astype(BF16) for c in cols] for i in subs]
    sc = [[_dot_nt(qh[i][h], km_ref[:, cols[h]]) for h in heads] for i in subs]
    ex = [[jnp.exp(sc[i][h] - jnp.max(sc[i][h], axis=-1, keepdims=True)) for h in heads] for i in subs]
    pr = [[(ex[i][h] / jnp.sum(ex[i][h], axis=-1, keepdims=True)).astype(BF16) for h in heads] for i in subs]
    oh = [[_dot(pr[i][h], vm_ref[:, cols[h]]).astype(BF16) for h in heads] for i in subs]
    x2 = [x1[i] + _dot(jnp.concatenate(oh[i], axis=1), wo_ref[...]) for i in subs]
    h3 = [_rms(v, n3w_ref[...]) for v in x2]
    hi = [v.astype(BF16) for v in h3]
    lo = [(h3[i] - hi[i].astype(F32)).astype(BF16) for i in subs]
    logits = [_dot_nt(wrh_ref[...], hi[i]) + _dot_nt(wrh_ref[...], lo[i]) + _dot_nt(wrl_ref[...], hi[i])
              + rb_ref[...] for i in subs]
    eid = lax.broadcasted_iota(jnp.int32, (N_EXPERTS, sub), 0).astype(F32)
    count = count_s[...]
    for i in subs:
        x2_ref[rows[i], :] = x2[i]
        h3_ref[rows[i], :] = h3[i]
        vals, ids = [], []
        cur = logits[i]
        for _ in range(TOP_K):
            m = jnp.max(cur, axis=0, keepdims=True)
            j = jnp.min(jnp.where(cur == m, eid, float(N_EXPERTS)), axis=0, keepdims=True)
            vals.append(m)
            ids.append(j)
            cur = jnp.where(eid == j, -jnp.inf, cur)
        exps = [jnp.exp(v - vals[0]) for v in vals]
        denom = exps[0] + exps[1] + exps[2] + exps[3]
        onehot = jnp.zeros((N_EXPERTS, sub), F32)
        for j in ids:
            onehot = onehot + jnp.where(eid == j, 1.0, 0.0)
        before = count + _dot(onehot.astype(BF16), upper_ref[...])
        for k in range(TOP_K):
            idx_ref[k:k + 1, rows[i]] = ids[k].astype(jnp.int32)
            gate_ref[k:k + 1, rows[i]] = exps[k] / denom
            rank_ref[k:k + 1, rows[i]] = jnp.sum(jnp.where(eid == ids[k], before, 0.0), axis=0,
                                                 keepdims=True).astype(jnp.int32)
        count = count + jnp.sum(onehot, axis=1, keepdims=True)
    count_s[...] = count
    cnt_ref[...] = jnp.broadcast_to(count, cnt_ref.shape)


def _mid(x3, ydn3, ysb3, w_out, n2w, wq, qnw, k_mem, v_mem, wo, n3w, wr_hi, wr_lo, rb, upper, tm):
    b, s, d = x3.shape
    n = b * s
    nt = s // tm
    m = k_mem.shape[1]
    tok = lambda i, j: (0, i * nt + j)
    return pl.pallas_call(
        _mid_kernel,
        grid=(b, nt),
        in_specs=[
            pl.BlockSpec((None, tm, d), lambda i, j: (i, j, 0)),
            pl.BlockSpec((None, tm, DN_WIDTH), lambda i, j: (i, j, 0)),
            pl.BlockSpec((None, tm, SB_WIDTH), lambda i, j: (i, j, 0)),
            _const_spec(w_out.shape),
            _const_spec((1, d)),
            _const_spec(wq.shape),
            _const_spec((1, X_DIM)),
            pl.BlockSpec((None, m, d), lambda i, j: (i, 0, 0)),
            pl.BlockSpec((None, m, d), lambda i, j: (i, 0, 0)),
            _const_spec(wo.shape),
            _const_spec((1, d)),
            _const_spec(wr_hi.shape),
            _const_spec(wr_lo.shape),
            _const_spec((N_EXPERTS, 1)),
            _const_spec(upper.shape),
        ],
        out_specs=[
            pl.BlockSpec((None, tm, d), lambda i, j: (i, j, 0)),
            pl.BlockSpec((None, tm, d), lambda i, j: (i, j, 0)),
            pl.BlockSpec((TOP_K, tm), tok),
            pl.BlockSpec((TOP_K, tm), tok),
            pl.BlockSpec((TOP_K, tm), tok),
            _const_spec((N_EXPERTS, LANES)),
        ],
        out_shape=[
            jax.ShapeDtypeStruct((b, s, d), F32),
            jax.ShapeDtypeStruct((b, s, d), F32),
            jax.ShapeDtypeStruct((TOP_K, n), jnp.int32),
            jax.ShapeDtypeStruct((TOP_K, n), F32),
            jax.ShapeDtypeStruct((TOP_K, n), jnp.int32),
            jax.ShapeDtypeStruct((N_EXPERTS, LANES), F32),
        ],
        scratch_shapes=[pltpu.VMEM((N_EXPERTS, 1), F32)],
        compiler_params=pltpu.CompilerParams(dimension_semantics=("arbitrary", "arbitrary"),
                                             vmem_limit_bytes=V7X_EXPERTS_VMEM_LIMIT),
        name="mid",
    )(x3, ydn3, ysb3, w_out, n2w, wq, qnw, k_mem, v_mem, wo, n3w, wr_hi, wr_lo, rb, upper)


def _row_copy(src_ref, src_row, dst_ref, dst_row, sem):
    return pltpu.make_async_copy(src_ref.at[pl.ds(src_row, 1), :], dst_ref.at[pl.ds(dst_row, 1), :], sem)


def _dispatch_kernel(pend_ref, padded_ref, dest_ref, h_hbm, xpad_ref, zero_s, hbuf, sem, zsem, fsem):
    tf = hbuf.shape[1]
    i = pl.program_id(0)
    last = pl.num_programs(0) - 1

    def fetch(tile, slot):
        start = pl.multiple_of(tile * tf, tf)
        return pltpu.make_async_copy(h_hbm.at[pl.ds(start, tf), :], hbuf.at[slot], fsem.at[slot])

    def wait_rows(step):
        for _ in range(TOP_K):
            pltpu.make_async_copy(hbuf.at[0], xpad_ref.at[pl.ds(0, tf), :], sem.at[step % 2]).wait()

    @pl.when(i == 0)
    def _():
        fetch(0, 0).start()

        @pl.when(last >= 1)
        def _():
            fetch(1, 1).start()

        zero_s[...] = jnp.zeros(zero_s.shape, F32)

        def last_block(e):
            start = pl.multiple_of(pend_ref[e] - EXPERT_BLOCK, EXPERT_BLOCK)
            return pltpu.make_async_copy(zero_s, xpad_ref.at[pl.ds(start, EXPERT_BLOCK), :], zsem)

        def spare_block(t):
            start = pl.multiple_of(pend_ref[N_EXPERTS - 1] + t * EXPERT_BLOCK, EXPERT_BLOCK)
            return pltpu.make_async_copy(zero_s, xpad_ref.at[pl.ds(start, EXPERT_BLOCK), :], zsem)

        def spare_exists(t):
            return pend_ref[N_EXPERTS - 1] + (t + 1) * EXPERT_BLOCK <= xpad_ref.shape[0]

        for e in range(N_EXPERTS):
            @pl.when(padded_ref[e] > 0)
            def _(e=e):
                last_block(e).start()

            @pl.when(spare_exists(e))
            def _(e=e):
                spare_block(e).start()
        for e in range(N_EXPERTS):
            @pl.when(padded_ref[e] > 0)
            def _(e=e):
                last_block(e).wait()

            @pl.when(spare_exists(e))
            def _(e=e):
                spare_block(e).wait()

    slot = i % 3
    fetch(i, slot).wait()
    src = hbuf.at[slot]
    for k in range(TOP_K):
        for t in range(tf):
            _row_copy(src, t, xpad_ref, dest_ref[0, k * tf + t], sem.at[i % 2]).start(priority=t % 2)

    @pl.when(i >= 1)
    def _():
        wait_rows(i - 1)

    @pl.when(i + 2 <= last)
    def _():
        fetch(i + 2, (i + 2) % 3).start()

    @pl.when(i == last)
    def _():
        wait_rows(i)


def _dispatch(pad_ends, padded, dest_tiles, h2d, rows, tf):
    n, d = h2d.shape
    return pl.pallas_call(
        _dispatch_kernel,
        grid_spec=pltpu.PrefetchScalarGridSpec(
            num_scalar_prefetch=2,
            grid=(n // tf,),
            in_specs=[
                pl.BlockSpec((None, 1, TOP_K * tf), lambda i, *_: (i, 0, 0), memory_space=pltpu.SMEM),
                pl.BlockSpec(memory_space=pl.ANY),
            ],
            out_specs=pl.BlockSpec(memory_space=pl.ANY),
            scratch_shapes=[pltpu.VMEM((EXPERT_BLOCK, d), F32), pltpu.VMEM((3, tf, d), F32),
                            pltpu.SemaphoreType.DMA((2,)), pltpu.SemaphoreType.DMA(()),
                            pltpu.SemaphoreType.DMA((3,))],
        ),
        out_shape=jax.ShapeDtypeStruct((rows, d), F32),
        compiler_params=_params("arbitrary"),
        name="dispatch",
    )(pad_ends, padded, dest_tiles, h2d)


def _experts_kernel(be_ref, nb_ref, first_ref, ord_ref, nxt_ref, x_ref, wg_hbm, bg_ref, wu_hbm, bu_ref,
                    wd_hbm, bd_ref, y_ref, wf_s, wb_s, wsem):
    j = pl.program_id(0)
    used = j < nb_ref[0]
    slot = ord_ref[j] % 2

    def fetch(e, s):
        return [pltpu.make_async_copy(w.at[e], wf_s.at[s, i], wsem.at[s])
                for i, w in enumerate((wg_hbm, wu_hbm, wd_hbm))]

    @pl.when(j == 0)
    def _():
        for c in fetch(be_ref[0], 0):
            c.start()

    @pl.when(jnp.logical_and(used, first_ref[j] == 1))
    def _():
        for c in fetch(be_ref[j], slot):
            c.wait()

        @pl.when(nxt_ref[j] >= 0)
        def _():
            for c in fetch(nxt_ref[j], 1 - slot):
                c.start()

        for i in range(3):
            wb_s[i] = wf_s[slot, i].astype(BF16)

    @pl.when(used)
    def _():
        x = x_ref[...].astype(BF16)
        gate = jnp.minimum(_dot(x, wb_s[0]) + bg_ref[...], SWIGLU_LIMIT)
        up = jnp.clip(_dot(x, wb_s[1]) + bu_ref[...], -SWIGLU_LIMIT, SWIGLU_LIMIT)
        act = (up + 1.0) * gate * _sigmoid(gate * SWIGLU_ALPHA)
        y_ref[...] = _dot(act.astype(BF16), wb_s[2]) + bd_ref[...]

    @pl.when(jnp.logical_not(used))
    def _():
        y_ref[...] = jnp.zeros(y_ref.shape, F32)


def _experts(block_e, nb_used, first, ordinal, nxt, x_pad, wg, bg, wu, bu, wd, bd):
    rows, d = x_pad.shape
    nblk = rows // EXPERT_BLOCK
    dff = wg.shape[2]
    assert d == dff, "the staging buffers hold all three weight matrices in one (d, dff) shape"
    row_blk = lambda j, be, nb, *_: (jnp.minimum(j, nb[0] - 1), 0)
    b_blk = lambda j, be, *_: (be[j], 0, 0)
    hbm = pl.BlockSpec(memory_space=pl.ANY)
    return pl.pallas_call(
        _experts_kernel,
        grid_spec=pltpu.PrefetchScalarGridSpec(
            num_scalar_prefetch=5,
            grid=(nblk,),
            in_specs=[
                pl.BlockSpec((EXPERT_BLOCK, d), row_blk),
                hbm,
                pl.BlockSpec((None, 1, dff), b_blk),
                hbm,
                pl.BlockSpec((None, 1, dff), b_blk),
                hbm,
                pl.BlockSpec((None, 1, d), b_blk),
            ],
            out_specs=pl.BlockSpec((EXPERT_BLOCK, d), lambda j, *_: (j, 0)),
            scratch_shapes=[pltpu.VMEM((2, 3, d, dff), F32), pltpu.VMEM((3, d, dff), BF16),
                            pltpu.SemaphoreType.DMA((2,))],
        ),
        out_shape=jax.ShapeDtypeStruct((rows, d), F32),
        compiler_params=pltpu.CompilerParams(dimension_semantics=("arbitrary",),
                                             vmem_limit_bytes=V7X_EXPERTS_VMEM_LIMIT),
        name="experts",
    )(block_e, nb_used, first, ordinal, nxt, x_pad, wg, bg, wu, bu, wd, bd)


def _combine_kernel(dest_ref, dnext_ref, x2_ref, gate_ref, ypad_ref, o_ref, buf, sem):
    th = x2_ref.shape[0]
    i = pl.program_id(0)
    slot = i % 2

    def gather(dref, s):
        for k in range(TOP_K):
            for t in range(th):
                _row_copy(ypad_ref, dref[0, k * th + t], buf.at[s, k], t, sem.at[s]).start(priority=t % 2)

    @pl.when(i == 0)
    def _():
        gather(dest_ref, 0)

    @pl.when(i + 1 < pl.num_programs(0))
    def _():
        gather(dnext_ref, 1 - slot)

    for k in range(TOP_K):
        pltpu.make_async_copy(ypad_ref.at[pl.ds(0, th), :], buf.at[slot, k], sem.at[slot]).wait()
    g = gate_ref[...]
    out = x2_ref[...]
    for k in range(TOP_K):
        out = out + buf[slot, k] * g[:, k:k + 1]
    o_ref[...] = out


def _combine(dest_tiles, x2d, gates_nk, y_pad, th):
    n, d = x2d.shape
    nt = n // th
    return pl.pallas_call(
        _combine_kernel,
        grid=(nt,),
        in_specs=[
            pl.BlockSpec((None, 1, TOP_K * th), lambda i: (i, 0, 0), memory_space=pltpu.SMEM),
            pl.BlockSpec((None, 1, TOP_K * th), lambda i: (jnp.minimum(i + 1, nt - 1), 0, 0),
                         memory_space=pltpu.SMEM),
            pl.BlockSpec((th, d), lambda i: (i, 0)),
            pl.BlockSpec((th, TOP_K), lambda i: (i, 0)),
            pl.BlockSpec(memory_space=pl.ANY),
        ],
        out_specs=pl.BlockSpec((th, d), lambda i: (i, 0)),
        out_shape=jax.ShapeDtypeStruct((n, d), F32),
        scratch_shapes=[pltpu.VMEM((2, TOP_K, th, d), F32), pltpu.SemaphoreType.DMA((2,))],
        compiler_params=_params("arbitrary"),
        name="combine",
    )(dest_tiles, dest_tiles, x2d, gates_nk, y_pad)


SC_CORES = 2
SC_SUBCORES = 16
SC_CHUNK = 64


def _sc_gather(table, idx):
    nidx = idx.shape[0]
    d = table.shape[1]
    workers = SC_CORES * SC_SUBCORES
    per_worker = nidx // workers
    mesh = plsc.VectorSubcoreMesh(core_axis_name="c", subcore_axis_name="s")

    def body(table_hbm, idx_hbm, out_hbm, idx_v, rows_v, sem):
        wid = lax.axis_index("s") * SC_CORES + lax.axis_index("c")
        base = wid * per_worker

        @pl.loop(0, per_worker // SC_CHUNK)
        def _(c):
            off = pl.multiple_of(base + c * SC_CHUNK, SC_CHUNK)
            pltpu.sync_copy(idx_hbm.at[pl.ds(off, SC_CHUNK)], idx_v)
            pltpu.async_copy(table_hbm.at[idx_v], rows_v, sem).wait()
            pltpu.sync_copy(rows_v, out_hbm.at[pl.ds(off, SC_CHUNK)])

    return pl.kernel(
        body,
        out_type=jax.ShapeDtypeStruct((nidx, d), table.dtype),
        mesh=mesh,
        scratch_types=[pltpu.VMEM((SC_CHUNK,), jnp.int32), pltpu.VMEM((SC_CHUNK, d), table.dtype),
                       pltpu.SemaphoreType.DMA],
        name="sc_gather",
    )(table, idx)


def _combine_dense_kernel(x2_ref, gate_ref, yg_ref, o_ref):
    g = gate_ref[...]
    out = x2_ref[...]
    for k in range(TOP_K):
        out = out + yg_ref[k] * g[:, k:k + 1]
    o_ref[...] = out


def _combine_dense(x2d, gates_nk, yg, th):
    n, d = x2d.shape
    return pl.pallas_call(
        _combine_dense_kernel,
        grid=(n // th,),
        in_specs=[
            pl.BlockSpec((th, d), lambda i: (i, 0)),
            pl.BlockSpec((th, TOP_K), lambda i: (i, 0)),
            pl.BlockSpec((TOP_K, th, d), lambda i: (0, i, 0)),
        ],
        out_specs=pl.BlockSpec((th, d), lambda i: (i, 0)),
        out_shape=jax.ShapeDtypeStruct((n, d), F32),
        compiler_params=_params("arbitrary"),
        name="combine_dense",
    )(x2d, gates_nk, yg)


def _tile_dest(dest, t):
    k, n = dest.shape
    return dest.reshape(k, n // t, t).transpose(1, 0, 2).reshape(n // t, 1, k * t)


def _layer(x, mem, norm1_w, w_in, conv_w, a_log, dt_bias, dn_norm_w, sb_q_norm_w, sb_k_norm_w, w_out,
           norm2_w, mem_norm_w, xq_w, xk_w, xv_w, xq_norm_w, xk_norm_w, xo_w, norm3_w, router_w,
           router_b, w_gate, b_gate, w_up, b_up, w_down, b_down):
    b, s, d = x.shape
    n = b * s
    tm_proj = min(512, s)
    tm_mid = min(1024, s)
    sub_mid = min(256, s)
    t_moe = min(256, s)

    o_dn, o_ab, o_sb = 4 * DN_WIDTH, 4 * DN_WIDTH + 2 * DN_HEADS, 4 * DN_WIDTH + 2 * DN_HEADS
    w_dn = w_in[:, :o_dn].astype(BF16)
    w_ab_f = w_in[:, o_dn:o_ab]
    w_ab = jnp.pad(w_ab_f, ((0, 0), (0, LANES - 2 * DN_HEADS))).astype(BF16)
    w_abt = w_ab_f.T.astype(BF16)
    w_sb = w_in[:, o_sb:].astype(BF16)
    row = lambda v: v.reshape(1, -1).astype(F32)

    dn, sb, ab, abt = _in_proj(x.reshape(n, d), row(norm1_w), w_dn, w_sb, w_ab, w_abt, tm_proj)

    abt4 = abt.reshape(8, n // CHUNK, CHUNK).transpose(1, 0, 2).reshape(b, s // CHUNK, 8, CHUNK)
    pad_lane = lambda v: jnp.pad(v.astype(F32), (0, LANES - v.shape[0])).reshape(1, LANES)
    pad_col = lambda v: jnp.pad(v.astype(F32), (0, 8 - v.shape[0])).reshape(8, 1)
    tap = jnp.arange(CONV_WIDTH * CHUNK)
    shift = (jnp.arange(2 * CHUNK)[None, :] == (CHUNK - CONV_WIDTH + 1 + tap // CHUNK + tap % CHUNK)[:, None])
    y_dn = _deltanet(dn.reshape(b, s, -1), ab.reshape(b, s, LANES), abt4, conv_w.astype(F32),
                     shift.astype(BF16), pad_lane(a_log), pad_lane(dt_bias), pad_col(a_log), pad_col(dt_bias),
                     row(dn_norm_w))

    ii = jnp.arange(SB_BLOCK)
    m2 = -jnp.concatenate([(ii[:, None] > ii[None, :]).astype(BF16),
                           jnp.ones((SB_BLOCK, SB_BLOCK), BF16)], axis=1)
    sb_heads = SB_LANES // SB_DIM
    hh = jnp.arange(SB_LANES) // SB_DIM
    hsum = (hh[:, None] == hh[None, :]).astype(BF16)
    y_sb = _stickbreak(sb.reshape(b, s, -1), row(jnp.tile(sb_q_norm_w, sb_heads)),
                       row(jnp.tile(sb_k_norm_w, sb_heads)), m2, hsum)

    k_mem, v_mem = _memkv(mem, row(mem_norm_w), xk_w.astype(BF16), xv_w.astype(BF16), row(xk_norm_w))

    wr_t = router_w.T.astype(F32)
    wr_hi = wr_t.astype(BF16)
    wr_lo = (wr_t - wr_hi.astype(F32)).astype(BF16)
    jj = jnp.arange(sub_mid)
    upper = (jj[:, None] < jj[None, :]).astype(BF16)
    x2, h3, idx, gates, rank, cnt = _mid(
        x, y_dn, y_sb, w_out.astype(BF16), row(norm2_w), xq_w.astype(BF16), row(xq_norm_w), k_mem, v_mem,
        xo_w.astype(BF16), row(norm3_w), wr_hi, wr_lo, router_b.reshape(N_EXPERTS, 1).astype(F32), upper,
        tm_mid)

    counts = cnt[:, 0].astype(jnp.int32)
    padded = (counts + EXPERT_BLOCK - 1) // EXPERT_BLOCK * EXPERT_BLOCK
    pad_ends = jnp.cumsum(padded)
    pad_starts = pad_ends - padded
    sel = idx[:, :, None] == jnp.arange(N_EXPERTS, dtype=jnp.int32)[None, None, :]
    dest = rank + jnp.sum(jnp.where(sel, pad_starts[None, None, :], 0), axis=-1)
    n_blocks = -(-n * TOP_K // EXPERT_BLOCK) + N_EXPERTS
    nb_used = (pad_ends[-1] // EXPERT_BLOCK).astype(jnp.int32)
    blk = jnp.minimum(jnp.arange(n_blocks, dtype=jnp.int32), nb_used - 1) * EXPERT_BLOCK
    block_e = jnp.minimum(jnp.sum(pad_ends[None, :] <= blk[:, None], axis=1), N_EXPERTS - 1).astype(jnp.int32)
    dest_tiles = _tile_dest(dest, t_moe)

    x_pad = _dispatch(pad_ends.astype(jnp.int32), padded.astype(jnp.int32), dest_tiles, h3.reshape(n, d),
                      n_blocks * EXPERT_BLOCK, t_moe)
    bias = lambda v: v.reshape(N_EXPERTS, 1, -1).astype(F32)
    jb = jnp.arange(n_blocks, dtype=jnp.int32)
    first = jnp.logical_and(jb < nb_used, jnp.logical_or(jb == 0, block_e != jnp.roll(block_e, 1)))
    ordinal = (jnp.cumsum(first) - 1).astype(jnp.int32)
    later_first = jnp.roll(jnp.where(first, jb, n_blocks), -1).at[-1].set(n_blocks)
    nxt_pos = lax.cummin(later_first, axis=0, reverse=True)
    nxt = jnp.where(nxt_pos < n_blocks, block_e[jnp.minimum(nxt_pos, n_blocks - 1)], -1).astype(jnp.int32)
    y_pad = _experts(block_e, nb_used.reshape(1), first.astype(jnp.int32), ordinal, nxt, x_pad,
                     w_gate.astype(F32), bias(b_gate), w_up.astype(F32), bias(b_up), w_down.astype(F32),
                     bias(b_down))
    yg = _sc_gather(y_pad, dest.reshape(-1).astype(jnp.int32)).reshape(TOP_K, n, d)
    out = _combine_dense(x2.reshape(n, d), gates.T, yg, t_moe)
    return out.reshape(b, s, d)


def kernel(x, mem, norm1_w, w_in, conv_w, a_log, dt_bias, dn_norm_w, sb_q_norm_w, sb_k_norm_w, w_out,
           norm2_w, mem_norm_w, xq_w, xk_w, xv_w, xq_norm_w, xk_norm_w, xo_w, norm3_w, router_w,
           router_b, w_gate, b_gate, w_up, b_up, w_down, b_down):
    depth = w_in.shape[0]
    for l in range(depth):
        x = _layer(x, mem, norm1_w[l], w_in[l], conv_w[l], a_log[l], dt_bias[l], dn_norm_w[l],
                   sb_q_norm_w[l], sb_k_norm_w[l], w_out[l], norm2_w[l], mem_norm_w[l], xq_w[l], xk_w[l],
                   xv_w[l], xq_norm_w[l], xk_norm_w[l], xo_w[l], norm3_w[l], router_w[l], router_b[l],
                   w_gate[l], b_gate[l], w_up[l], b_up[l], w_down[l], b_down[l])
    return x
```

```python
import jax
import jax.numpy as jnp
from jax import lax
from jax.experimental import pallas as pl
from jax.experimental.pallas import tpu as pltpu
from jax.experimental.pallas import tpu_sc as plsc

F32 = jnp.float32
BF16 = jnp.bfloat16

EPS = 1e-6
CHUNK = 64
DN_HEADS = 4
DN_DIM = 128
DN_WIDTH = DN_HEADS * DN_DIM
CONV_WIDTH = 4
DN_PAIR = 4
SB_HEADS = 8
SB_DIM = 64
SB_WIDTH = SB_HEADS * SB_DIM
SB_BLOCK = 128
SB_GROUP = 4
SB_LANES = 256
X_HEADS = 4
X_DIM = 256
N_EXPERTS = 32
TOP_K = 4
EXPERT_BLOCK = 512
SWIGLU_ALPHA = 1.702
SWIGLU_LIMIT = 7.0
LANES = 128
LOG2E = 1.4426950408889634
SB_UNDERFLOW_LOG2 = -127.0
V7X_VMEM_LIMIT = 48 * 1024 * 1024
V7X_EXPERTS_VMEM_LIMIT = 58 * 1024 * 1024


def _dot(a, b):
    return jnp.dot(a, b, preferred_element_type=F32)


def _dot_nt(a, b):
    return lax.dot_general(a, b, (((1,), (1,)), ((), ())), preferred_element_type=F32)


def _dot_tn(a, b):
    return lax.dot_general(a, b, (((0,), (0,)), ((), ())), preferred_element_type=F32)


def _split3(x):
    hi = x.astype(BF16)
    r = x - hi.astype(F32)
    mid = r.astype(BF16)
    return hi, mid, (r - mid.astype(F32)).astype(BF16)


def _softplus(x):
    return jnp.maximum(x, 0.0) + jnp.log(1.0 + jnp.exp(-jnp.abs(x)))


def _sigmoid(x):
    return 1.0 / (1.0 + jnp.exp(-x))


def _rms(x, w):
    return x * lax.rsqrt(jnp.mean(x * x, axis=-1, keepdims=True) + EPS) * w


def _params(*sem):
    return pltpu.CompilerParams(dimension_semantics=sem, vmem_limit_bytes=V7X_VMEM_LIMIT)


def _const_spec(shape):
    nd = len(shape)
    return pl.BlockSpec(shape, lambda *_: (0,) * nd)


def _in_proj_kernel(x_ref, nw_ref, wdn_ref, wsb_ref, wab_ref, wabt_ref,
                    dn_ref, sb_ref, ab_ref, abt_ref):
    n = _rms(x_ref[...], nw_ref[...]).astype(BF16)
    dn_ref[...] = _dot(n, wdn_ref[...]).astype(BF16)
    sb_ref[...] = _dot(n, wsb_ref[...]).astype(BF16)
    ab_ref[...] = _dot(n, wab_ref[...])
    abt_ref[...] = _dot_nt(wabt_ref[...], n)


def _in_proj(x2d, norm_w, w_dn, w_sb, w_ab, w_abt, tm):
    n, d = x2d.shape
    return pl.pallas_call(
        _in_proj_kernel,
        grid=(n // tm,),
        in_specs=[
            pl.BlockSpec((tm, d), lambda i: (i, 0)),
            _const_spec((1, d)),
            _const_spec(w_dn.shape),
            _const_spec(w_sb.shape),
            _const_spec(w_ab.shape),
            _const_spec(w_abt.shape),
        ],
        out_specs=[
            pl.BlockSpec((tm, w_dn.shape[1]), lambda i: (i, 0)),
            pl.BlockSpec((tm, w_sb.shape[1]), lambda i: (i, 0)),
            pl.BlockSpec((tm, LANES), lambda i: (i, 0)),
            pl.BlockSpec((8, tm), lambda i: (0, i)),
        ],
        out_shape=[
            jax.ShapeDtypeStruct((n, w_dn.shape[1]), BF16),
            jax.ShapeDtypeStruct((n, w_sb.shape[1]), BF16),
            jax.ShapeDtypeStruct((n, LANES), F32),
            jax.ShapeDtypeStruct((8, n), F32),
        ],
        compiler_params=_params("arbitrary"),
        name="in_proj",
    )(x2d, norm_w, w_dn, w_sb, w_ab, w_abt)


def _deltanet_kernel(dn_ref, ab_ref, abt_ref, convw_ref, shift_ref, alog_ref, dtb_ref, alogt_ref, dtbt_ref,
                     normw_ref, o_ref, state_s, u_s, w_s, kt_s, qg_s, attn_s, egl_s):
    s = dn_ref.shape[0]
    steps = s // (CHUNK * DN_PAIR)
    state_s[...] = jnp.zeros(state_s.shape, F32)

    def conv_silu(r0, c, part, h, l2):
        cs = slice(part * DN_WIDTH + h * LANES, part * DN_WIDTH + (h + 1) * LANES)
        prev0 = pl.multiple_of(jnp.maximum(r0 - CHUNK, 0), CHUNK)
        prev = dn_ref[pl.ds(prev0, CHUNK), cs]
        prev = jnp.where(c > 0, prev, jnp.zeros_like(prev))
        win = jnp.concatenate([prev, dn_ref[pl.ds(r0, CHUNK), cs]], axis=0)
        sh = _dot(shift_ref[...], win)
        w = convw_ref[:, cs]
        y = w[0:1, :] * sh[0:CHUNK, :]
        for i in range(1, CONV_WIDTH):
            y = y + w[i:i + 1, :] * sh[i * CHUNK:(i + 1) * CHUNK, :]
        y = y * _sigmoid(y)
        if l2:
            y = y * lax.rsqrt(jnp.sum(y * y, axis=-1, keepdims=True) + EPS)
        return y

    row = lax.broadcasted_iota(jnp.int32, (CHUNK, CHUNK), 0)
    col = lax.broadcasted_iota(jnp.int32, (CHUNK, CHUNK), 1)
    tri = row >= col
    strict = row > col
    tril16 = jnp.where(tri, 1.0, 0.0).astype(BF16)
    triu16 = jnp.where(row <= col, 1.0, 0.0).astype(BF16)
    neg_a_col = -jnp.exp(alog_ref[...])
    neg_a_row = -jnp.exp(alogt_ref[...])
    scale = DN_DIM ** -0.5
    heads = range(DN_HEADS)
    cols = [slice(h * LANES, (h + 1) * LANES) for h in heads]

    def pre_items(it):
        items = []
        for ci in range(DN_PAIR):
            c = it * DN_PAIR + ci
            r0 = pl.multiple_of(c * CHUNK, CHUNK)
            ab = ab_ref[pl.ds(r0, CHUNK), :]
            g_col = neg_a_col * _softplus(ab + dtb_ref[...])
            gc_col_all = sum(_dot(tril16, p) for p in _split3(g_col))
            beta_all = _sigmoid(ab)
            abt = abt_ref[c]
            g_row = neg_a_row * _softplus(abt + dtbt_ref[...])
            gc_row_all = sum(_dot(p, triu16) for p in _split3(g_row))
            for h in heads:
                q = conv_silu(r0, c, 0, h, True) * scale
                k = conv_silu(r0, c, 1, h, True)
                v = conv_silu(r0, c, 2, h, False)
                gcol = gc_col_all[:, h:h + 1]
                grow = gc_row_all[h:h + 1, :]
                beta = beta_all[:, DN_HEADS + h:DN_HEADS + h + 1]
                glast = gcol[CHUNK - 1:CHUNK, :]
                decay = jnp.where(tri, jnp.exp(jnp.where(tri, gcol - grow, 0.0)), 0.0)
                kb = k * beta
                kt_s[pl.ds(r0, CHUNK), cols[h]] = (k * jnp.exp(glast - gcol)).astype(BF16)
                qg_s[pl.ds(r0, CHUNK), cols[h]] = (q * jnp.exp(gcol)).astype(BF16)
                egl_s[c, h:h + 1, :] = jnp.broadcast_to(jnp.exp(glast), (1, LANES))
                items.append(dict(c=c, r0=r0, h=h, decay=decay, kb16=kb.astype(BF16),
                                  k16=k.astype(BF16), q16=q.astype(BF16),
                                  sol=jnp.concatenate([v * beta, kb * jnp.exp(gcol)], axis=1)))
        return items

    def pre_gram(items):
        kk = [_dot_nt(t["kb16"], t["k16"]) for t in items]
        qk = [_dot_nt(t["q16"], t["k16"]) for t in items]
        for t, kk_i, qk_i in zip(items, kk, qk):
            attn_s[t["c"], t["h"]] = jnp.where(tri, qk_i * t["decay"], 0.0).astype(BF16)
            t["tm"] = -jnp.where(strict, kk_i * t["decay"], 0.0)
            t["p16"] = t["tm"].astype(BF16)
        return [_dot(t["p16"], t["p16"]) for t in items]

    def pre_level(items, sq, last):
        for i, t in enumerate(items):
            t["pw"] = sq[i]
            t["p16"] = sq[i].astype(BF16)
        app = [_dot(t["p16"], t["tm"].astype(BF16)) for t in items]
        nxt = None if last else [_dot(t["p16"], t["p16"]) for t in items]
        for i, t in enumerate(items):
            t["tm"] = t["tm"] + t["pw"] + app[i]
        return nxt

    def pre_solve(items):
        corr = [_dot(t["tm"].astype(BF16), t["sol"].astype(BF16)) for t in items]
        for i, t in enumerate(items):
            sol = t["sol"] + corr[i]
            u_s[pl.ds(t["r0"], CHUNK), cols[t["h"]]] = sol[:, :DN_DIM]
            w_s[pl.ds(t["r0"], CHUNK), cols[t["h"]]] = sol[:, DN_DIM:].astype(BF16)

    def rec_read(c):
        r0 = pl.multiple_of(c * CHUNK, CHUNK)
        st = [state_s[h] for h in heads]
        st16 = [x.astype(BF16) for x in st]
        ws = [_dot(w_s[pl.ds(r0, CHUNK), cols[h]], st16[h]) for h in heads]
        qs = [_dot(qg_s[pl.ds(r0, CHUNK), cols[h]], st16[h]) for h in heads]
        return dict(c=c, r0=r0, st=st, ws=ws, qs=qs)

    def rec_update(rd):
        c, r0 = rd["c"], rd["r0"]
        vn16 = [(u_s[pl.ds(r0, CHUNK), cols[h]] - rd["ws"][h]).astype(BF16) for h in heads]
        av = [_dot(attn_s[c, h], vn16[h]) for h in heads]
        ks = [_dot_tn(kt_s[pl.ds(r0, CHUNK), cols[h]], vn16[h]) for h in heads]
        for h in heads:
            state_s[h] = rd["st"][h] * egl_s[c, h:h + 1, :] + ks[h]
            o = rd["qs"][h] + av[h]
            o = o * lax.rsqrt(jnp.mean(o * o, axis=-1, keepdims=True) + EPS)
            z = dn_ref[pl.ds(r0, CHUNK), 3 * DN_WIDTH + h * LANES:3 * DN_WIDTH + (h + 1) * LANES].astype(F32)
            o_ref[pl.ds(r0, CHUNK), cols[h]] = (o * normw_ref[...] * (z * _sigmoid(z))).astype(BF16)

    def step(it, with_pre, with_rec):
        hooks = []
        for ci in range(DN_PAIR if with_rec else 0):
            hooks += [("read", ci), ("update", ci)]
        pending = {}

        def run_hook():
            if hooks:
                kind, ci = hooks.pop(0)
                if kind == "read":
                    pending[ci] = rec_read((it - 1) * DN_PAIR + ci)
                else:
                    rec_update(pending.pop(ci))

        if not with_pre:
            while hooks:
                run_hook()
            return
        slots = [(i * 7) // max(len(hooks), 1) for i in range(len(hooks))]

        def boundary(b):
            for _ in range(slots.count(b)):
                run_hook()

        items = pre_items(it)
        boundary(0)
        sq = pre_gram(items)
        boundary(1)
        for level in range(1, 6):
            sq = pre_level(items, sq, level == 5)
            boundary(level + 1)
        pre_solve(items)
        while hooks:
            run_hook()

    step(0, True, False)

    def fused(it, carry):
        step(it, True, True)
        return carry

    lax.fori_loop(1, steps, fused, 0)
    step(steps, False, True)


def _deltanet(dn3, ab3, abt4, conv_w, shift, alog, dtb, alogt, dtbt, norm_w):
    b, s, _ = dn3.shape
    nch = s // CHUNK
    return pl.pallas_call(
        _deltanet_kernel,
        grid=(b,),
        in_specs=[
            pl.BlockSpec((None, s, 4 * DN_WIDTH), lambda i: (i, 0, 0)),
            pl.BlockSpec((None, s, LANES), lambda i: (i, 0, 0)),
            pl.BlockSpec((None, nch, 8, CHUNK), lambda i: (i, 0, 0, 0)),
            _const_spec(conv_w.shape),
            _const_spec(shift.shape),
            _const_spec((1, LANES)),
            _const_spec((1, LANES)),
            _const_spec((8, 1)),
            _const_spec((8, 1)),
            _const_spec((1, DN_DIM)),
        ],
        out_specs=pl.BlockSpec((None, s, DN_WIDTH), lambda i: (i, 0, 0)),
        out_shape=jax.ShapeDtypeStruct((b, s, DN_WIDTH), BF16),
        scratch_shapes=[
            pltpu.VMEM((DN_HEADS, DN_DIM, DN_DIM), F32),
            pltpu.VMEM((s, DN_WIDTH), F32),
            pltpu.VMEM((s, DN_WIDTH), BF16),
            pltpu.VMEM((s, DN_WIDTH), BF16),
            pltpu.VMEM((s, DN_WIDTH), BF16),
            pltpu.VMEM((nch, DN_HEADS, CHUNK, CHUNK), BF16),
            pltpu.VMEM((nch, 8, LANES), F32),
        ],
        compiler_params=_params("arbitrary"),
        name="deltanet",
    )(dn3, ab3, abt4, conv_w, shift, alog, dtb, alogt, dtbt, norm_w)


def _stickbreak_kernel(q_ref, k_ref, v_ref, qw_ref, kw_ref, m2_ref, hsum_ref, o_ref,
                       qn_s, kn_s, vm_s, carry_s, acc_s):
    s, width = q_ref.shape
    nh = width // SB_DIM
    group = carry_s.shape[0]
    lane = lax.broadcasted_iota(jnp.int32, (1, width), 1)
    head_lanes = [jnp.logical_and(lane >= h * SB_DIM, lane < (h + 1) * SB_DIM) for h in range(nh)]

    def head_norm(r0, x_ref, w):
        x = x_ref[pl.ds(r0, SB_BLOCK), :].astype(F32)
        sq = x * x
        hi = sq.astype(BF16)
        lo = (sq - hi.astype(F32)).astype(BF16)
        ms = (_dot(hi, hsum_ref[...]) + _dot(lo, hsum_ref[...])) * (1.0 / SB_DIM)
        return x * lax.rsqrt(ms + EPS) * w

    def norm_block(i, carry):
        r0 = pl.multiple_of(i * SB_BLOCK, SB_BLOCK)
        qn = head_norm(r0, q_ref, qw_ref[...]) * (SB_DIM ** -0.5 * LOG2E)
        v = v_ref[pl.ds(r0, SB_BLOCK), :]
        for h in range(nh):
            qn_s[h, pl.ds(r0, SB_BLOCK), :] = jnp.where(head_lanes[h], qn, 0.0).astype(BF16)
            vm_s[i, h * SB_BLOCK:(h + 1) * SB_BLOCK, :] = jnp.where(head_lanes[h], v, jnp.zeros_like(v))
        kn_s[pl.ds(r0, SB_BLOCK), :] = head_norm(r0, k_ref, kw_ref[...]).astype(BF16)
        return carry

    lax.fori_loop(0, s // SB_BLOCK, norm_block, 0, unroll=min(4, s // SB_BLOCK))

    row = lax.broadcasted_iota(jnp.int32, (SB_BLOCK, SB_BLOCK), 0)
    col = lax.broadcasted_iota(jnp.int32, (SB_BLOCK, SB_BLOCK), 1)
    causal = col < row

    def step(rows, key_blocks, valid):
        tiles = [(g, h) for g in range(group) for h in range(nh)]
        k16 = [kn_s[pl.ds(pl.multiple_of(kb * SB_BLOCK, SB_BLOCK), SB_BLOCK), :] for kb in key_blocks]
        z = [_dot_nt(qn_s[h, pl.ds(rows[g], SB_BLOCK), :], k16[g]) for g, h in tiles]
        sp = [jnp.maximum(x, 0.0) + jnp.log(1.0 + jnp.exp2(-jnp.abs(x))) * LOG2E for x in z]
        fail = [jnp.where(causal, x, 0.0) for x in sp] if valid is None else sp
        cs = [_dot(x.astype(BF16), m2_ref[...]) for x in fail]
        p = []
        for i, (g, h) in enumerate(tiles):
            w = jnp.exp2(z[i] - sp[i] + carry_s[g, h] + cs[i][:, :SB_BLOCK])
            p.append((jnp.where(causal, w, 0.0) if valid is None else w).astype(BF16))
        pv = [_dot(jnp.concatenate(p[g * nh:(g + 1) * nh], axis=1), vm_s[key_blocks[g]])
              for g in range(group)]
        for g in range(group):
            acc = acc_s[g] + pv[g]
            acc_s[g] = acc if valid is None else jnp.where(valid[g], acc, acc_s[g])
        for i, (g, h) in enumerate(tiles):
            carry = carry_s[g, h] + cs[i][:, SB_BLOCK:]
            carry_s[g, h] = carry if valid is None else jnp.where(valid[g], carry, carry_s[g, h])

    def q_group(qg, carry):
        blocks = [qg * group + g for g in range(group)]
        rows = [pl.multiple_of(qb * SB_BLOCK, SB_BLOCK) for qb in blocks]
        carry_s[...] = jnp.zeros(carry_s.shape, F32)
        acc_s[...] = jnp.zeros(acc_s.shape, F32)
        step(rows, blocks, None)

        def cond(st):
            d, alive = st
            return jnp.logical_and(d <= blocks[-1], alive)

        def body(st):
            d, _ = st
            step(rows, [jnp.maximum(qb - d, 0) for qb in blocks], [qb >= d for qb in blocks])
            return d + 1, jnp.max(carry_s[...]) > SB_UNDERFLOW_LOG2

        lax.while_loop(cond, body, (jnp.int32(1), jnp.bool_(True)))
        for g in range(group):
            o_ref[pl.ds(rows[g], SB_BLOCK), :] = acc_s[g].astype(BF16)
        return carry

    lax.fori_loop(0, s // (SB_BLOCK * group), q_group, 0)


def _stickbreak(sb3, qw, kw, m2, hsum):
    b, s, _ = sb3.shape
    width = hsum.shape[0]
    nh = width // SB_DIM
    parts = SB_WIDTH // width
    group = min(SB_GROUP, s // SB_BLOCK)
    return pl.pallas_call(
        _stickbreak_kernel,
        grid=(b, parts),
        in_specs=[
            pl.BlockSpec((None, s, width), lambda i, j: (i, 0, j)),
            pl.BlockSpec((None, s, width), lambda i, j: (i, 0, parts + j)),
            pl.BlockSpec((None, s, width), lambda i, j: (i, 0, 2 * parts + j)),
            _const_spec((1, width)),
            _const_spec((1, width)),
            _const_spec(m2.shape),
            _const_spec(hsum.shape),
        ],
        out_specs=pl.BlockSpec((None, s, width), lambda i, j: (i, 0, j)),
        out_shape=jax.ShapeDtypeStruct((b, s, SB_WIDTH), BF16),
        scratch_shapes=[
            pltpu.VMEM((nh, s, width), BF16),
            pltpu.VMEM((s, width), BF16),
            pltpu.VMEM((s // SB_BLOCK, nh * SB_BLOCK, width), BF16),
            pltpu.VMEM((group, nh, SB_BLOCK, SB_BLOCK), F32),
            pltpu.VMEM((group, SB_BLOCK, width), F32),
        ],
        compiler_params=_params("arbitrary", "arbitrary"),
        name="stickbreak",
    )(sb3, sb3, sb3, qw, kw, m2, hsum)


def _memkv_kernel(mem_ref, nw_ref, wk_ref, wv_ref, knw_ref, k_ref, v_ref):
    n = _rms(mem_ref[...], nw_ref[...]).astype(BF16)
    k = _dot(n, wk_ref[...])
    for h in range(X_HEADS):
        cs = slice(h * X_DIM, (h + 1) * X_DIM)
        k_ref[:, cs] = _rms(k[:, cs], knw_ref[...]).astype(BF16)
    v_ref[...] = _dot(n, wv_ref[...]).astype(BF16)


def _memkv(mem, norm_w, wk, wv, k_norm_w):
    b, m, d = mem.shape
    return pl.pallas_call(
        _memkv_kernel,
        grid=(b,),
        in_specs=[
            pl.BlockSpec((None, m, d), lambda i: (i, 0, 0)),
            _const_spec((1, d)),
            _const_spec(wk.shape),
            _const_spec(wv.shape),
            _const_spec((1, X_DIM)),
        ],
        out_specs=[
            pl.BlockSpec((None, m, d), lambda i: (i, 0, 0)),
            pl.BlockSpec((None, m, d), lambda i: (i, 0, 0)),
        ],
        out_shape=[jax.ShapeDtypeStruct((b, m, d), BF16)] * 2,
        compiler_params=_params("arbitrary"),
        name="memkv",
    )(mem, norm_w, wk, wv, k_norm_w)


def _mid_kernel(x_ref, ydn_ref, ysb_ref, wout_ref, n2w_ref, wq_ref, qnw_ref, km_ref, vm_ref, wo_ref,
                n3w_ref, wrh_ref, wrl_ref, rb_ref, upper_ref,
                x2_ref, h3_ref, idx_ref, gate_ref, rank_ref, cnt_ref, count_s):
    tm = x_ref.shape[0]
    first_step = jnp.logical_and(pl.program_id(0) == 0, pl.program_id(1) == 0)

    @pl.when(first_step)
    def _():
        count_s[...] = jnp.zeros(count_s.shape, F32)

    sub = upper_ref.shape[0]
    subs = range(tm // sub)
    rows = [slice(i * sub, (i + 1) * sub) for i in subs]
    heads = range(X_HEADS)
    cols = [slice(h * X_DIM, (h + 1) * X_DIM) for h in heads]
    x1 = [x_ref[r, :] + _dot(ydn_ref[r, :], wout_ref[0:DN_WIDTH, :])
          + _dot(ysb_ref[r, :], wout_ref[DN_WIDTH:, :]) for r in rows]
    n2 = [_rms(v, n2w_ref[...]).astype(BF16) for v in x1]
    q = [_dot(v, wq_ref[...]) for v in n2]
    qh = [[(_rms(q[i][:, c], qnw_ref[...]) * (X_DIM ** -0.5)).astype(BF16) for c in cols] for i in subs]
    sc = [[_dot_nt(qh[i][h], km_ref[:, cols[h]]) for h in heads] for i in subs]
    ex = [[jnp.exp(sc[i][h] - jnp.max(sc[i][h], axis=-1, keepdims=True)) for h in heads] for i in subs]
    pr = [[(ex[i][h] / jnp.sum(ex[i][h], axis=-1, keepdims=True)).astype(BF16) for h in heads] for i in subs]
    oh = [[_dot(pr[i][h], vm_ref[:, cols[h]]).astype(BF16) for h in heads] for i in subs]
    x2 = [x1[i] + _dot(jnp.concatenate(oh[i], axis=1), wo_ref[...]) for i in subs]
    h3 = [_rms(v, n3w_ref[...]) for v in x2]
    hi = [v.astype(BF16) for v in h3]
    lo = [(h3[i] - hi[i].astype(F32)).astype(BF16) for i in subs]
    logits = [_dot_nt(wrh_ref[...], hi[i]) + _dot_nt(wrh_ref[...], lo[i]) + _dot_nt(wrl_ref[...], hi[i])
              + rb_ref[...] for i in subs]
    eid = lax.broadcasted_iota(jnp.int32, (N_EXPERTS, sub), 0).astype(F32)
    count = count_s[...]
    for i in subs:
        x2_ref[rows[i], :] = x2[i]
        h3_ref[rows[i], :] = h3[i]
        vals, ids = [], []
        cur = logits[i]
        for _ in range(TOP_K):
            m = jnp.max(cur, axis=0, keepdims=True)
            j = jnp.min(jnp.where(cur == m, eid, float(N_EXPERTS)), axis=0, keepdims=True)
            vals.append(m)
            ids.append(j)
            cur = jnp.where(eid == j, -jnp.inf, cur)
        exps = [jnp.exp(v - vals[0]) for v in vals]
        denom = exps[0] + exps[1] + exps[2] + exps[3]
        onehot = jnp.zeros((N_EXPERTS, sub), F32)
        for j in ids:
            onehot = onehot + jnp.where(eid == j, 1.0, 0.0)
        before = count + _dot(onehot.astype(BF16), upper_ref[...])
        for k in range(TOP_K):
            idx_ref[k:k + 1, rows[i]] = ids[k].astype(jnp.int32)
            gate_ref[k:k + 1, rows[i]] = exps[k] / denom
            rank_ref[k:k + 1, rows[i]] = jnp.sum(jnp.where(eid == ids[k], before, 0.0), axis=0,
                                                 keepdims=True).astype(jnp.int32)
        count = count + jnp.sum(onehot, axis=1, keepdims=True)
    count_s[...] = count
    cnt_ref[...] = jnp.broadcast_to(count, cnt_ref.shape)


def _mid(x3, ydn3, ysb3, w_out, n2w, wq, qnw, k_mem, v_mem, wo, n3w, wr_hi, wr_lo, rb, upper, tm):
    b, s, d = x3.shape
    n = b * s
    nt = s // tm
    m = k_mem.shape[1]
    tok = lambda i, j: (0, i * nt + j)
    return pl.pallas_call(
        _mid_kernel,
        grid=(b, nt),
        in_specs=[
            pl.BlockSpec((None, tm, d), lambda i, j: (i, j, 0)),
            pl.BlockSpec((None, tm, DN_WIDTH), lambda i, j: (i, j, 0)),
            pl.BlockSpec((None, tm, SB_WIDTH), lambda i, j: (i, j, 0)),
            _const_spec(w_out.shape),
            _const_spec((1, d)),
            _const_spec(wq.shape),
            _const_spec((1, X_DIM)),
            pl.BlockSpec((None, m, d), lambda i, j: (i, 0, 0)),
            pl.BlockSpec((None, m, d), lambda i, j: (i, 0, 0)),
            _const_spec(wo.shape),
            _const_spec((1, d)),
            _const_spec(wr_hi.shape),
            _const_spec(wr_lo.shape),
            _const_spec((N_EXPERTS, 1)),
            _const_spec(upper.shape),
        ],
        out_specs=[
            pl.BlockSpec((None, tm, d), lambda i, j: (i, j, 0)),
            pl.BlockSpec((None, tm, d), lambda i, j: (i, j, 0)),
            pl.BlockSpec((TOP_K, tm), tok),
            pl.BlockSpec((TOP_K, tm), tok),
            pl.BlockSpec((TOP_K, tm), tok),
            _const_spec((N_EXPERTS, LANES)),
        ],
        out_shape=[
            jax.ShapeDtypeStruct((b, s, d), F32),
            jax.ShapeDtypeStruct((b, s, d), F32),
            jax.ShapeDtypeStruct((TOP_K, n), jnp.int32),
            jax.ShapeDtypeStruct((TOP_K, n), F32),
            jax.ShapeDtypeStruct((TOP_K, n), jnp.int32),
            jax.ShapeDtypeStruct((N_EXPERTS, LANES), F32),
        ],
        scratch_shapes=[pltpu.VMEM((N_EXPERTS, 1), F32)],
        compiler_params=pltpu.CompilerParams(dimension_semantics=("arbitrary", "arbitrary"),
                                             vmem_limit_bytes=V7X_EXPERTS_VMEM_LIMIT),
        name="mid",
    )(x3, ydn3, ysb3, w_out, n2w, wq, qnw, k_mem, v_mem, wo, n3w, wr_hi, wr_lo, rb, upper)


def _row_copy(src_ref, src_row, dst_ref, dst_row, sem):
    return pltpu.make_async_copy(src_ref.at[pl.ds(src_row, 1), :], dst_ref.at[pl.ds(dst_row, 1), :], sem)


def _dispatch_kernel(pend_ref, padded_ref, dest_ref, h_hbm, xpad_ref, zero_s, hbuf, sem, zsem, fsem):
    tf = hbuf.shape[1]
    i = pl.program_id(0)
    last = pl.num_programs(0) - 1

    def fetch(tile, slot):
        start = pl.multiple_of(tile * tf, tf)
        return pltpu.make_async_copy(h_hbm.at[pl.ds(start, tf), :], hbuf.at[slot], fsem.at[slot])

    def wait_rows(step):
        for _ in range(TOP_K):
            pltpu.make_async_copy(hbuf.at[0], xpad_ref.at[pl.ds(0, tf), :], sem.at[step % 2]).wait()

    @pl.when(i == 0)
    def _():
        fetch(0, 0).start()

        @pl.when(last >= 1)
        def _():
            fetch(1, 1).start()

        zero_s[...] = jnp.zeros(zero_s.shape, F32)

        def last_block(e):
            start = pl.multiple_of(pend_ref[e] - EXPERT_BLOCK, EXPERT_BLOCK)
            return pltpu.make_async_copy(zero_s, xpad_ref.at[pl.ds(start, EXPERT_BLOCK), :], zsem)

        def spare_block(t):
            start = pl.multiple_of(pend_ref[N_EXPERTS - 1] + t * EXPERT_BLOCK, EXPERT_BLOCK)
            return pltpu.make_async_copy(zero_s, xpad_ref.at[pl.ds(start, EXPERT_BLOCK), :], zsem)

        def spare_exists(t):
            return pend_ref[N_EXPERTS - 1] + (t + 1) * EXPERT_BLOCK <= xpad_ref.shape[0]

        for e in range(N_EXPERTS):
            @pl.when(padded_ref[e] > 0)
            def _(e=e):
                last_block(e).start()

            @pl.when(spare_exists(e))
            def _(e=e):
                spare_block(e).start()
        for e in range(N_EXPERTS):
            @pl.when(padded_ref[e] > 0)
            def _(e=e):
                last_block(e).wait()

            @pl.when(spare_exists(e))
            def _(e=e):
                spare_block(e).wait()

    slot = i % 3
    fetch(i, slot).wait()
    src = hbuf.at[slot]
    for k in range(TOP_K):
        for t in range(tf):
            _row_copy(src, t, xpad_ref, dest_ref[0, k * tf + t], sem.at[i % 2]).start(priority=t % 2)

    @pl.when(i >= 1)
    def _():
        wait_rows(i - 1)

    @pl.when(i + 2 <= last)
    def _():
        fetch(i + 2, (i + 2) % 3).start()

    @pl.when(i == last)
    def _():
        wait_rows(i)


def _dispatch(pad_ends, padded, dest_tiles, h2d, rows, tf):
    n, d = h2d.shape
    return pl.pallas_call(
        _dispatch_kernel,
        grid_spec=pltpu.PrefetchScalarGridSpec(
            num_scalar_prefetch=2,
            grid=(n // tf,),
            in_specs=[
                pl.BlockSpec((None, 1, TOP_K * tf), lambda i, *_: (i, 0, 0), memory_space=pltpu.SMEM),
                pl.BlockSpec(memory_space=pl.ANY),
            ],
            out_specs=pl.BlockSpec(memory_space=pl.ANY),
            scratch_shapes=[pltpu.VMEM((EXPERT_BLOCK, d), F32), pltpu.VMEM((3, tf, d), F32),
                            pltpu.SemaphoreType.DMA((2,)), pltpu.SemaphoreType.DMA(()),
                            pltpu.SemaphoreType.DMA((3,))],
        ),
        out_shape=jax.ShapeDtypeStruct((rows, d), F32),
        compiler_params=_params("arbitrary"),
        name="dispatch",
    )(pad_ends, padded, dest_tiles, h2d)


def _experts_kernel(be_ref, nb_ref, first_ref, ord_ref, nxt_ref, x_ref, wg_hbm, bg_ref, wu_hbm, bu_ref,
                    wd_hbm, bd_ref, y_ref, wf_s, wb_s, wsem):
    j = pl.program_id(0)
    used = j < nb_ref[0]
    slot = ord_ref[j] % 2

    def fetch(e, s):
        return [pltpu.make_async_copy(w.at[e], wf_s.at[s, i], wsem.at[s])
                for i, w in enumerate((wg_hbm, wu_hbm, wd_hbm))]

    @pl.when(j == 0)
    def _():
        for c in fetch(be_ref[0], 0):
            c.start()

    @pl.when(jnp.logical_and(used, first_ref[j] == 1))
    def _():
        for c in fetch(be_ref[j], slot):
            c.wait()

        @pl.when(nxt_ref[j] >= 0)
        def _():
            for c in fetch(nxt_ref[j], 1 - slot):
                c.start()

        for i in range(3):
            wb_s[i] = wf_s[slot, i].astype(BF16)

    @pl.when(used)
    def _():
        x = x_ref[...].astype(BF16)
        gate = jnp.minimum(_dot(x, wb_s[0]) + bg_ref[...], SWIGLU_LIMIT)
        up = jnp.clip(_dot(x, wb_s[1]) + bu_ref[...], -SWIGLU_LIMIT, SWIGLU_LIMIT)
        act = (up + 1.0) * gate * _sigmoid(gate * SWIGLU_ALPHA)
        y_ref[...] = _dot(act.astype(BF16), wb_s[2]) + bd_ref[...]

    @pl.when(jnp.logical_not(used))
    def _():
        y_ref[...] = jnp.zeros(y_ref.shape, F32)


def _experts(block_e, nb_used, first, ordinal, nxt, x_pad, wg, bg, wu, bu, wd, bd):
    rows, d = x_pad.shape
    nblk = rows // EXPERT_BLOCK
    dff = wg.shape[2]
    assert d == dff, "the staging buffers hold all three weight matrices in one (d, dff) shape"
    row_blk = lambda j, be, nb, *_: (jnp.minimum(j, nb[0] - 1), 0)
    b_blk = lambda j, be, *_: (be[j], 0, 0)
    hbm = pl.BlockSpec(memory_space=pl.ANY)
    return pl.pallas_call(
        _experts_kernel,
        grid_spec=pltpu.PrefetchScalarGridSpec(
            num_scalar_prefetch=5,
            grid=(nblk,),
            in_specs=[
                pl.BlockSpec((EXPERT_BLOCK, d), row_blk),
                hbm,
                pl.BlockSpec((None, 1, dff), b_blk),
                hbm,
                pl.BlockSpec((None, 1, dff), b_blk),
                hbm,
                pl.BlockSpec((None, 1, d), b_blk),
            ],
            out_specs=pl.BlockSpec((EXPERT_BLOCK, d), lambda j, *_: (j, 0)),
            scratch_shapes=[pltpu.VMEM((2, 3, d, dff), F32), pltpu.VMEM((3, d, dff), BF16),
                            pltpu.SemaphoreType.DMA((2,))],
        ),
        out_shape=jax.ShapeDtypeStruct((rows, d), F32),
        compiler_params=pltpu.CompilerParams(dimension_semantics=("arbitrary",),
                                             vmem_limit_bytes=V7X_EXPERTS_VMEM_LIMIT),
        name="experts",
    )(block_e, nb_used, first, ordinal, nxt, x_pad, wg, bg, wu, bu, wd, bd)


def _combine_kernel(dest_ref, dnext_ref, x2_ref, gate_ref, ypad_ref, o_ref, buf, sem):
    th = x2_ref.shape[0]
    i = pl.program_id(0)
    slot = i % 2

    def gather(dref, s):
        for k in range(TOP_K):
            for t in range(th):
                _row_copy(ypad_ref, dref[0, k * th + t], buf.at[s, k], t, sem.at[s]).start(priority=t % 2)

    @pl.when(i == 0)
    def _():
        gather(dest_ref, 0)

    @pl.when(i + 1 < pl.num_programs(0))
    def _():
        gather(dnext_ref, 1 - slot)

    for k in range(TOP_K):
        pltpu.make_async_copy(ypad_ref.at[pl.ds(0, th), :], buf.at[slot, k], sem.at[slot]).wait()
    g = gate_ref[...]
    out = x2_ref[...]
    for k in range(TOP_K):
        out = out + buf[slot, k] * g[:, k:k + 1]
    o_ref[...] = out


def _combine(dest_tiles, x2d, gates_nk, y_pad, th):
    n, d = x2d.shape
    nt = n // th
    return pl.pallas_call(
        _combine_kernel,
        grid=(nt,),
        in_specs=[
            pl.BlockSpec((None, 1, TOP_K * th), lambda i: (i, 0, 0), memory_space=pltpu.SMEM),
            pl.BlockSpec((None, 1, TOP_K * th), lambda i: (jnp.minimum(i + 1, nt - 1), 0, 0),
                         memory_space=pltpu.SMEM),
            pl.BlockSpec((th, d), lambda i: (i, 0)),
            pl.BlockSpec((th, TOP_K), lambda i: (i, 0)),
            pl.BlockSpec(memory_space=pl.ANY),
        ],
        out_specs=pl.BlockSpec((th, d), lambda i: (i, 0)),
        out_shape=jax.ShapeDtypeStruct((n, d), F32),
        scratch_shapes=[pltpu.VMEM((2, TOP_K, th, d), F32), pltpu.SemaphoreType.DMA((2,))],
        compiler_params=_params("arbitrary"),
        name="combine",
    )(dest_tiles, dest_tiles, x2d, gates_nk, y_pad)


COMBINE_PARTS = 4
SC_CORES = 2
SC_SUBCORES = 16
SC_CHUNK = 32


def _sc_gather(table, idx):
    nidx = idx.shape[0]
    d = table.shape[1]
    workers = SC_CORES * SC_SUBCORES
    per_worker = nidx // workers
    mesh = plsc.VectorSubcoreMesh(core_axis_name="c", subcore_axis_name="s")

    def body(table_hbm, idx_hbm, out_hbm, idx_a, idx_b, rows_a, rows_b, sem_a, sem_b, wsem_a, wsem_b):
        wid = lax.axis_index("s") * SC_CORES + lax.axis_index("c")
        base = wid * per_worker

        @pl.loop(0, per_worker // (2 * SC_CHUNK))
        def _(c):
            off_a = pl.multiple_of(base + c * (2 * SC_CHUNK), SC_CHUNK)
            off_b = pl.multiple_of(off_a + SC_CHUNK, SC_CHUNK)
            pltpu.sync_copy(idx_hbm.at[pl.ds(off_a, SC_CHUNK)], idx_a)
            pltpu.sync_copy(idx_hbm.at[pl.ds(off_b, SC_CHUNK)], idx_b)
            ga = pltpu.async_copy(table_hbm.at[idx_a], rows_a, sem_a)
            gb = pltpu.async_copy(table_hbm.at[idx_b], rows_b, sem_b)
            ga.wait()
            wa = pltpu.async_copy(rows_a, out_hbm.at[pl.ds(off_a, SC_CHUNK)], wsem_a)
            gb.wait()
            wb = pltpu.async_copy(rows_b, out_hbm.at[pl.ds(off_b, SC_CHUNK)], wsem_b)
            wa.wait()
            wb.wait()

    return pl.kernel(
        body,
        out_type=jax.ShapeDtypeStruct((nidx, d), table.dtype),
        mesh=mesh,
        scratch_types=[pltpu.VMEM((SC_CHUNK,), jnp.int32), pltpu.VMEM((SC_CHUNK,), jnp.int32),
                       pltpu.VMEM((SC_CHUNK, d), table.dtype), pltpu.VMEM((SC_CHUNK, d), table.dtype),
                       pltpu.SemaphoreType.DMA, pltpu.SemaphoreType.DMA, pltpu.SemaphoreType.DMA,
                       pltpu.SemaphoreType.DMA],
        name="sc_gather",
    )(table, idx)


def _combine_dense_kernel(x2_ref, gate_ref, yg_ref, o_ref):
    g = gate_ref[...]
    out = x2_ref[...]
    for k in range(TOP_K):
        out = out + yg_ref[k] * g[:, k:k + 1]
    o_ref[...] = out


def _combine_dense(x2d, gates_nk, yg, th, part, parts):
    n, d = x2d.shape
    steps = n // th // parts
    first = part * steps
    return pl.pallas_call(
        _combine_dense_kernel,
        grid=(steps,),
        in_specs=[
            pl.BlockSpec((th, d), lambda i: (first + i, 0)),
            pl.BlockSpec((th, TOP_K), lambda i: (first + i, 0)),
            pl.BlockSpec((TOP_K, th, d), lambda i: (0, i, 0)),
        ],
        out_specs=pl.BlockSpec((th, d), lambda i: (first + i, 0)),
        out_shape=jax.ShapeDtypeStruct((n, d), F32),
        input_output_aliases={0: 0},
        compiler_params=_params("arbitrary"),
        name="combine_dense",
    )(x2d, gates_nk, yg)


def _tile_dest(dest, t):
    k, n = dest.shape
    return dest.reshape(k, n // t, t).transpose(1, 0, 2).reshape(n // t, 1, k * t)


def _layer(x, mem, norm1_w, w_in, conv_w, a_log, dt_bias, dn_norm_w, sb_q_norm_w, sb_k_norm_w, w_out,
           norm2_w, mem_norm_w, xq_w, xk_w, xv_w, xq_norm_w, xk_norm_w, xo_w, norm3_w, router_w,
           router_b, w_gate, b_gate, w_up, b_up, w_down, b_down):
    b, s, d = x.shape
    n = b * s
    tm_proj = min(512, s)
    tm_mid = min(1024, s)
    sub_mid = min(256, s)
    t_moe = min(256, s)

    o_dn, o_ab, o_sb = 4 * DN_WIDTH, 4 * DN_WIDTH + 2 * DN_HEADS, 4 * DN_WIDTH + 2 * DN_HEADS
    w_dn = w_in[:, :o_dn].astype(BF16)
    w_ab_f = w_in[:, o_dn:o_ab]
    w_ab = jnp.pad(w_ab_f, ((0, 0), (0, LANES - 2 * DN_HEADS))).astype(BF16)
    w_abt = w_ab_f.T.astype(BF16)
    w_sb = w_in[:, o_sb:].astype(BF16)
    row = lambda v: v.reshape(1, -1).astype(F32)

    dn, sb, ab, abt = _in_proj(x.reshape(n, d), row(norm1_w), w_dn, w_sb, w_ab, w_abt, tm_proj)

    abt4 = abt.reshape(8, n // CHUNK, CHUNK).transpose(1, 0, 2).reshape(b, s // CHUNK, 8, CHUNK)
    pad_lane = lambda v: jnp.pad(v.astype(F32), (0, LANES - v.shape[0])).reshape(1, LANES)
    pad_col = lambda v: jnp.pad(v.astype(F32), (0, 8 - v.shape[0])).reshape(8, 1)
    tap = jnp.arange(CONV_WIDTH * CHUNK)
    shift = (jnp.arange(2 * CHUNK)[None, :] == (CHUNK - CONV_WIDTH + 1 + tap // CHUNK + tap % CHUNK)[:, None])
    y_dn = _deltanet(dn.reshape(b, s, -1), ab.reshape(b, s, LANES), abt4, conv_w.astype(F32),
                     shift.astype(BF16), pad_lane(a_log), pad_lane(dt_bias), pad_col(a_log), pad_col(dt_bias),
                     row(dn_norm_w))

    ii = jnp.arange(SB_BLOCK)
    m2 = -jnp.concatenate([(ii[:, None] > ii[None, :]).astype(BF16),
                           jnp.ones((SB_BLOCK, SB_BLOCK), BF16)], axis=1)
    sb_heads = SB_LANES // SB_DIM
    hh = jnp.arange(SB_LANES) // SB_DIM
    hsum = (hh[:, None] == hh[None, :]).astype(BF16)
    y_sb = _stickbreak(sb.reshape(b, s, -1), row(jnp.tile(sb_q_norm_w, sb_heads)),
                       row(jnp.tile(sb_k_norm_w, sb_heads)), m2, hsum)

    k_mem, v_mem = _memkv(mem, row(mem_norm_w), xk_w.astype(BF16), xv_w.astype(BF16), row(xk_norm_w))

    wr_t = router_w.T.astype(F32)
    wr_hi = wr_t.astype(BF16)
    wr_lo = (wr_t - wr_hi.astype(F32)).astype(BF16)
    jj = jnp.arange(sub_mid)
    upper = (jj[:, None] < jj[None, :]).astype(BF16)
    x2, h3, idx, gates, rank, cnt = _mid(
        x, y_dn, y_sb, w_out.astype(BF16), row(norm2_w), xq_w.astype(BF16), row(xq_norm_w), k_mem, v_mem,
        xo_w.astype(BF16), row(norm3_w), wr_hi, wr_lo, router_b.reshape(N_EXPERTS, 1).astype(F32), upper,
        tm_mid)

    counts = cnt[:, 0].astype(jnp.int32)
    padded = (counts + EXPERT_BLOCK - 1) // EXPERT_BLOCK * EXPERT_BLOCK
    pad_ends = jnp.cumsum(padded)
    pad_starts = pad_ends - padded
    sel = idx[:, :, None] == jnp.arange(N_EXPERTS, dtype=jnp.int32)[None, None, :]
    dest = rank + jnp.sum(jnp.where(sel, pad_starts[None, None, :], 0), axis=-1)
    n_blocks = -(-n * TOP_K // EXPERT_BLOCK) + N_EXPERTS
    nb_used = (pad_ends[-1] // EXPERT_BLOCK).astype(jnp.int32)
    blk = jnp.minimum(jnp.arange(n_blocks, dtype=jnp.int32), nb_used - 1) * EXPERT_BLOCK
    block_e = jnp.minimum(jnp.sum(pad_ends[None, :] <= blk[:, None], axis=1), N_EXPERTS - 1).astype(jnp.int32)
    dest_tiles = _tile_dest(dest, t_moe)

    x_pad = _dispatch(pad_ends.astype(jnp.int32), padded.astype(jnp.int32), dest_tiles, h3.reshape(n, d),
                      n_blocks * EXPERT_BLOCK, t_moe)
    bias = lambda v: v.reshape(N_EXPERTS, 1, -1).astype(F32)
    jb = jnp.arange(n_blocks, dtype=jnp.int32)
    first = jnp.logical_and(jb < nb_used, jnp.logical_or(jb == 0, block_e != jnp.roll(block_e, 1)))
    ordinal = (jnp.cumsum(first) - 1).astype(jnp.int32)
    later_first = jnp.roll(jnp.where(first, jb, n_blocks), -1).at[-1].set(n_blocks)
    nxt_pos = lax.cummin(later_first, axis=0, reverse=True)
    nxt = jnp.where(nxt_pos < n_blocks, block_e[jnp.minimum(nxt_pos, n_blocks - 1)], -1).astype(jnp.int32)
    y_pad = _experts(block_e, nb_used.reshape(1), first.astype(jnp.int32), ordinal, nxt, x_pad,
                     w_gate.astype(F32), bias(b_gate), w_up.astype(F32), bias(b_up), w_down.astype(F32),
                     bias(b_down))
    out = x2.reshape(n, d)
    gates_nk = gates.T
    npart = n // COMBINE_PARTS
    for part in range(COMBINE_PARTS):
        idx_part = dest[:, part * npart:(part + 1) * npart].reshape(-1).astype(jnp.int32)
        yg = _sc_gather(y_pad, idx_part).reshape(TOP_K, npart, d)
        out = _combine_dense(out, gates_nk, yg, t_moe, part, COMBINE_PARTS)
    return out.reshape(b, s, d)


def kernel(x, mem, norm1_w, w_in, conv_w, a_log, dt_bias, dn_norm_w, sb_q_norm_w, sb_k_norm_w, w_out,
           norm2_w, mem_norm_w, xq_w, xk_w, xv_w, xq_norm_w, xk_norm_w, xo_w, norm3_w, router_w,
           router_b, w_gate, b_gate, w_up, b_up, w_down, b_down):
    depth = w_in.shape[0]
    for l in range(depth):
        x = _layer(x, mem, norm1_w[l], w_in[l], conv_w[l], a_log[l], dt_bias[l], dn_norm_w[l],
                   sb_q_norm_w[l], sb_k_norm_w[l], w_out[l], norm2_w[l], mem_norm_w[l], xq_w[l], xk_w[l],
                   xv_w[l], xq_norm_w[l], xk_norm_w[l], xo_w[l], norm3_w[l], router_w[l], router_b[l],
                   w_gate[l], b_gate[l], w_up[l], b_up[l], w_down[l], b_down[l])
    return x
```
